```python
import math
import jax, jax.numpy as jnp
from jax import lax
import numpy as np

D_MODEL = 1024
BATCH = 1
SEQ = 16384
DEPTH = 2
DEC_BATCH = 128
DEC_SEQ = 8
PAST_LEN = 16384
PAGE_SIZE = 128

A_HEADS = 8
A_KV_HEADS = 2
A_GROUP = A_HEADS // A_KV_HEADS
A_HEAD_DIM = 64
WINDOW = 128
ATTN_BLOCK = 128
B_HEADS = 4
B_KEY_DIM = 128
B_VAL_DIM = 128
RET_CHUNK = 128
S5_GROUP = 16
S5_GROUPS = D_MODEL // S5_GROUP
S5_STATE = 64
SSM_CHUNK = 128
D_FF = ((8 * D_MODEL + 3 * 256 - 1) // (3 * 256)) * 256
IN_WIDTHS = (A_HEADS * A_HEAD_DIM, A_KV_HEADS * A_HEAD_DIM, A_KV_HEADS * A_HEAD_DIM,
             B_HEADS * B_KEY_DIM, B_HEADS * B_KEY_DIM, B_HEADS * B_VAL_DIM, B_HEADS * B_VAL_DIM)
IN_WIDTH = sum(IN_WIDTHS)
MIX_WIDTH = A_HEADS * A_HEAD_DIM + B_HEADS * B_VAL_DIM
N_EVEN = (DEPTH + 1) // 2
N_ODD = DEPTH // 2
EPS = 1e-6
NEG_INF = -1e30

kernel_name = 'hybrid_swa_sink_retention_s5_decoder_step'


def rmsnorm(x, g):
    xf = x.astype(jnp.float32)
    y = xf * lax.rsqrt(jnp.mean(xf * xf, axis=-1, keepdims=True) + EPS)
    return (y * g.astype(jnp.float32)).astype(x.dtype)


def alibi_slopes():
    h = jnp.arange(1, A_HEADS + 1, dtype=jnp.float32)
    return jnp.exp2(-8.0 * h / A_HEADS).reshape(A_KV_HEADS, A_GROUP)


def split_cols(proj):
    idx, acc = [], 0
    for w in IN_WIDTHS[:-1]:
        acc += w
        idx.append(acc)
    return jnp.split(proj, idx, axis=-1)


def sink_attention(q, k, v, dist, valid, sinks):
    s = jnp.einsum('...qkgd,...skd->...kgqs', q.astype(jnp.float32), k.astype(jnp.float32)) * (A_HEAD_DIM ** -0.5)
    s = s - alibi_slopes()[:, :, None, None] * dist[..., None, None, :, :].astype(jnp.float32)
    s = jnp.where(valid[..., None, None, :, :], s, NEG_INF)
    sink = jnp.broadcast_to(sinks.astype(jnp.float32).reshape(A_KV_HEADS, A_GROUP)[:, :, None, None],
                            s.shape[:-1] + (1,))
    p = jax.nn.softmax(jnp.concatenate([s, sink], axis=-1), axis=-1)[..., :-1]
    return jnp.einsum('...kgqs,...skd->...qkgd', p, v.astype(jnp.float32))


def swa_prompt(q, k, v, sinks):
    bsz, L = q.shape[:2]
    nb = L // ATTN_BLOCK
    qb = q.reshape(bsz, nb, ATTN_BLOCK, A_KV_HEADS, A_GROUP, A_HEAD_DIM)

    def with_prev(t):
        tb = t.reshape(bsz, nb, ATTN_BLOCK, A_KV_HEADS, A_HEAD_DIM)
        prev = jnp.pad(tb, ((0, 0), (1, 0), (0, 0), (0, 0), (0, 0)))[:, :-1]
        return jnp.concatenate([prev, tb], axis=2)

    kb, vb = with_prev(k), with_prev(v)
    blk = jnp.arange(nb)[:, None] * ATTN_BLOCK
    q_pos = blk + jnp.arange(ATTN_BLOCK)[None, :]
    k_pos = blk - ATTN_BLOCK + jnp.arange(2 * ATTN_BLOCK)[None, :]
    dist = q_pos[:, :, None] - k_pos[:, None, :]
    valid = (dist >= 0) & (dist < WINDOW) & (k_pos[:, None, :] >= 0)
    o = sink_attention(qb, kb, vb, dist, valid, sinks)
    return o.reshape(bsz, L, A_HEADS * A_HEAD_DIM)


def swa_sample(q, k, v, cache_k, cache_v, sinks):
    bsz, L = q.shape[:2]
    W = cache_k.shape[1]
    k_all = jnp.concatenate([cache_k.astype(k.dtype), k], axis=1)
    v_all = jnp.concatenate([cache_v.astype(v.dtype), v], axis=1)
    q_pos = PAST_LEN + jnp.arange(L)
    k_pos = PAST_LEN - W + jnp.arange(W + L)
    dist = q_pos[:, None] - k_pos[None, :]
    valid = (dist >= 0) & (dist < WINDOW)
    o = sink_attention(q, k_all, v_all, dist, valid, sinks)
    return o.reshape(bsz, L, A_HEADS * A_HEAD_DIM), k_all[:, L:], v_all[:, L:]


def retention(q, k, v, S0):
    bsz, L = q.shape[:2]
    blk = min(RET_CHUNK, L)
    nc = L // blk
    log_g = jnp.log1p(-jnp.exp2(-5.0 - jnp.arange(B_HEADS, dtype=jnp.float32)))
    idx = jnp.arange(blk, dtype=jnp.float32)
    diff = idx[:, None] - idx[None, :]
    decay_in = jnp.where(diff >= 0, jnp.exp(log_g[:, None, None] * jnp.maximum(diff, 0.0)), 0.0)
    decay_q = jnp.exp(log_g[None, :] * (idx[:, None] + 1.0))
    decay_k = jnp.exp(log_g[None, :] * (blk - 1.0 - idx[:, None]))
    decay_c = jnp.exp(log_g * blk)

    def chunks(t):
        return t.astype(jnp.float32).reshape(bsz, nc, blk, B_HEADS, t.shape[-1]).transpose(1, 0, 2, 3, 4)

    def step(S, xs):
        qc, kc, vc = xs
        inner = jnp.einsum('bihd,bjhd->bhij', qc, kc) * decay_in
        o = (jnp.einsum('bhij,bjhe->bihe', inner, vc)
             + jnp.einsum('bihd,bhde->bihe', qc, S) * decay_q[None, :, :, None])
        S = S * decay_c[None, :, None, None] + jnp.einsum('bjhd,bjhe->bhde', kc * decay_k[None, :, :, None], vc)
        return S, o

    S, o = lax.scan(step, S0.astype(jnp.float32), (chunks(q), chunks(k), chunks(v)))
    return o.transpose(1, 0, 2, 3, 4).reshape(bsz, L, B_HEADS, B_VAL_DIM), S


def s5_scan(u, h_re, h_im, A_re, A_im, log_dt, B_re, B_im, C_re, C_im, D_skip):
    f32 = jnp.float32
    bsz, L, _ = u.shape
    A_re, A_im = A_re.astype(f32), A_im.astype(f32)
    dt = jnp.exp(log_dt.astype(f32))[:, None]
    mag = jnp.exp(A_re * dt)
    lam_re, lam_im = mag * jnp.cos(A_im * dt), mag * jnp.sin(A_im * dt)
    den = A_re * A_re + A_im * A_im
    n_re, n_im = lam_re - 1.0, lam_im
    f_re = (n_re * A_re + n_im * A_im) / den
    f_im = (n_im * A_re - n_re * A_im) / den
    Br, Bi = B_re.astype(f32), B_im.astype(f32)
    bb_re = f_re[..., None] * Br - f_im[..., None] * Bi
    bb_im = f_re[..., None] * Bi + f_im[..., None] * Br
    Cr, Ci = C_re.astype(f32), C_im.astype(f32)
    blk = min(SSM_CHUNK, L)
    nc = L // blk
    uf = u.astype(f32)
    uc = uf.reshape(bsz, nc, blk, S5_GROUPS, S5_GROUP).transpose(1, 0, 2, 3, 4)

    def combine(e1, e2):
        a1r, a1i, b1r, b1i = e1
        a2r, a2i, b2r, b2i = e2
        return (a2r * a1r - a2i * a1i, a2r * a1i + a2i * a1r,
                a2r * b1r - a2i * b1i + b2r, a2r * b1i + a2i * b1r + b2i)

    def step(carry, x):
        hr, hi = carry
        bu_re = jnp.einsum('bcgi,gpi->bcgp', x, bb_re)
        bu_im = jnp.einsum('bcgi,gpi->bcgp', x, bb_im)
        bu_re = bu_re.at[:, 0].add(lam_re * hr - lam_im * hi)
        bu_im = bu_im.at[:, 0].add(lam_re * hi + lam_im * hr)
        a_re = jnp.broadcast_to(lam_re, bu_re.shape)
        a_im = jnp.broadcast_to(lam_im, bu_im.shape)
        _, _, hs_re, hs_im = lax.associative_scan(combine, (a_re, a_im, bu_re, bu_im), axis=1)
        y = jnp.einsum('bcgp,gip->bcgi', hs_re, Cr) - jnp.einsum('bcgp,gip->bcgi', hs_im, Ci)
        return (hs_re[:, -1], hs_im[:, -1]), y

    (hr, hi), ys = lax.scan(step, (h_re.astype(f32), h_im.astype(f32)), uc)
    y = ys.transpose(1, 0, 2, 3, 4).reshape(bsz, L, D_MODEL) + D_skip.astype(f32) * uf
    return y, hr, hi


def mix_even(h, w_in, q_gain, k_gain, sinks, ret_gain, w_out, cache_k, cache_v, ret_state):
    bsz, L, _ = h.shape
    q_a, k_a, v_a, q_b, k_b, v_b, g_b = split_cols(h @ w_in)
    q_a = rmsnorm(q_a.reshape(bsz, L, A_KV_HEADS, A_GROUP, A_HEAD_DIM), q_gain)
    k_a = rmsnorm(k_a.reshape(bsz, L, A_KV_HEADS, A_HEAD_DIM), k_gain)
    v_a = v_a.reshape(bsz, L, A_KV_HEADS, A_HEAD_DIM)
    if cache_k is None:
        o_a = swa_prompt(q_a, k_a, v_a, sinks)
        keep = min(WINDOW, L)
        new_k, new_v = k_a[:, L - keep:], v_a[:, L - keep:]
        ret_state = jnp.zeros((bsz, B_HEADS, B_KEY_DIM, B_VAL_DIM), jnp.float32)
    else:
        o_a, new_k, new_v = swa_sample(q_a, k_a, v_a, cache_k, cache_v, sinks)
    o_b, new_s = retention(q_b.reshape(bsz, L, B_HEADS, B_KEY_DIM),
                           k_b.reshape(bsz, L, B_HEADS, B_KEY_DIM) * (B_KEY_DIM ** -0.5),
                           v_b.reshape(bsz, L, B_HEADS, B_VAL_DIM), ret_state)
    mu = jnp.mean(o_b, axis=-1, keepdims=True)
    var = jnp.mean(jnp.square(o_b - mu), axis=-1, keepdims=True)
    o_b = (o_b - mu) * lax.rsqrt(var + EPS) * ret_gain.astype(jnp.float32).reshape(B_HEADS, B_VAL_DIM)
    o_b = o_b.reshape(bsz, L, B_HEADS * B_VAL_DIM).astype(h.dtype) * jax.nn.silu(g_b)
    o = jnp.concatenate([o_a.astype(h.dtype), o_b], axis=-1) @ w_out
    return o, new_k, new_v, new_s


def mix_odd(h, A_re, A_im, log_dt, B_re, B_im, C_re, C_im, D_skip, glu_a, glu_b, s_re, s_im):
    bsz = h.shape[0]
    if s_re is None:
        s_re = jnp.zeros((bsz, S5_GROUPS, S5_STATE), jnp.float32)
        s_im = jnp.zeros((bsz, S5_GROUPS, S5_STATE), jnp.float32)
    y, n_re, n_im = s5_scan(h, s_re, s_im, A_re, A_im, log_dt, B_re, B_im, C_re, C_im, D_skip)
    y = jax.nn.gelu(y).astype(h.dtype)
    return (y @ glu_a) * jax.nn.sigmoid(y @ glu_b), n_re, n_im


def swiglu(h, wg, wu, wd):
    return (jax.nn.silu(h @ wg) * (h @ wu)) @ wd


def trunk(x, c, cache_k, cache_v, ret, s5r, s5i, p):
    ks, vs, rets, res, ims = [], [], [], [], []
    for layer in range(DEPTH):
        i = layer // 2
        mod = jax.nn.silu(c) @ p['ada_w'][layer] + p['ada_b'][layer]
        sh1, sc1, g1, sh2, sc2, g2 = [m[:, None, :] for m in jnp.split(mod, 6, axis=-1)]
        h = rmsnorm(x, p['norm_mix'][layer]) * (1.0 + sc1) + sh1
        if layer % 2 == 0:
            out, nk, nv, ns = mix_even(
                h, p['even_w_in'][i], p['even_q_gain'][i], p['even_k_gain'][i], p['even_sinks'][i],
                p['even_ret_gain'][i], p['even_w_out'][i],
                None if cache_k is None else cache_k[i], None if cache_v is None else cache_v[i],
                None if ret is None else ret[i])
            ks.append(nk); vs.append(nv); rets.append(ns)
        else:
            out, nr, ni = mix_odd(
                h, p['odd_A_re'][i], p['odd_A_im'][i], p['odd_log_dt'][i], p['odd_B_re'][i], p['odd_B_im'][i],
                p['odd_C_re'][i], p['odd_C_im'][i], p['odd_D'][i], p['odd_glu_a'][i], p['odd_glu_b'][i],
                None if s5r is None else s5r[i], None if s5i is None else s5i[i])
            res.append(nr); ims.append(ni)
        x = x + g1 * out
        h = rmsnorm(x, p['norm_ffn'][layer]) * (1.0 + sc2) + sh2
        x = x + g2 * swiglu(h, p['ffn_wg'][layer], p['ffn_wu'][layer], p['ffn_wd'][layer])
    return x, jnp.stack(ks), jnp.stack(vs), jnp.stack(rets), jnp.stack(res), jnp.stack(ims)


def setup_inputs(seed: int = 0) -> dict:
    key = jax.random.key(seed)
    k = jax.random.split(key, 40)

    def nrm(kk, shape, scale=1.0):
        return scale * jax.random.normal(kk, shape, jnp.float32)

    W = min(WINDOW, PAST_LEN)
    n = jnp.arange(S5_STATE, dtype=jnp.float32)
    return {
        'x_prompt': nrm(k[0], (BATCH, SEQ, D_MODEL)),
        'x_sample': nrm(k[1], (DEC_BATCH, DEC_SEQ, D_MODEL)),
        'cache_win_k': nrm(k[2], (N_EVEN, DEC_BATCH, W, A_KV_HEADS, A_HEAD_DIM)),
        'cache_win_v': nrm(k[3], (N_EVEN, DEC_BATCH, W, A_KV_HEADS, A_HEAD_DIM)),
        'state_ret': nrm(k[4], (N_EVEN, DEC_BATCH, B_HEADS, B_KEY_DIM, B_VAL_DIM)),
        'state_s5_re': nrm(k[5], (N_ODD, DEC_BATCH, S5_GROUPS, S5_STATE), 0.2),
        'state_s5_im': nrm(k[6], (N_ODD, DEC_BATCH, S5_GROUPS, S5_STATE), 0.2),
        'c_prompt': nrm(k[7], (BATCH, D_MODEL)),
        'c_sample': nrm(k[8], (DEC_BATCH, D_MODEL)),
        'ada_w': nrm(k[9], (DEPTH, D_MODEL, 6 * D_MODEL), 0.5 * D_MODEL ** -0.5),
        'ada_b': nrm(k[10], (DEPTH, 6 * D_MODEL), 0.05),
        'norm_mix': 1.0 + nrm(k[11], (DEPTH, D_MODEL), 0.02),
        'norm_ffn': 1.0 + nrm(k[12], (DEPTH, D_MODEL), 0.02),
        'ffn_wg': nrm(k[13], (DEPTH, D_MODEL, D_FF), D_MODEL ** -0.5),
        'ffn_wu': nrm(k[14], (DEPTH, D_MODEL, D_FF), D_MODEL ** -0.5),
        'ffn_wd': nrm(k[15], (DEPTH, D_FF, D_MODEL), D_FF ** -0.5),
        'even_w_in': nrm(k[16], (N_EVEN, D_MODEL, IN_WIDTH), D_MODEL ** -0.5),
        'even_q_gain': 1.0 + nrm(k[17], (N_EVEN, A_HEAD_DIM), 0.02),
        'even_k_gain': 1.0 + nrm(k[18], (N_EVEN, A_HEAD_DIM), 0.02),
        'even_sinks': nrm(k[19], (N_EVEN, A_HEADS), 0.5),
        'even_ret_gain': 1.0 + nrm(k[20], (N_EVEN, B_HEADS * B_VAL_DIM), 0.02),
        'even_w_out': nrm(k[21], (N_EVEN, MIX_WIDTH, D_MODEL), MIX_WIDTH ** -0.5),
        'odd_A_re': -0.5 + nrm(k[22], (N_ODD, S5_GROUPS, S5_STATE), 0.01),
        'odd_A_im': math.pi * n + nrm(k[23], (N_ODD, S5_GROUPS, S5_STATE), 0.01),
        'odd_log_dt': jax.random.uniform(k[24], (N_ODD, S5_GROUPS), jnp.float32, math.log(0.001), math.log(0.1)),
        'odd_B_re': nrm(k[25], (N_ODD, S5_GROUPS, S5_STATE, S5_GROUP), (2.0 * S5_GROUP) ** -0.5),
        'odd_B_im': nrm(k[26], (N_ODD, S5_GROUPS, S5_STATE, S5_GROUP), (2.0 * S5_GROUP) ** -0.5),
        'odd_C_re': nrm(k[27], (N_ODD, S5_GROUPS, S5_GROUP, S5_STATE), S5_STATE ** -0.5),
        'odd_C_im': nrm(k[28], (N_ODD, S5_GROUPS, S5_GROUP, S5_STATE), S5_STATE ** -0.5),
        'odd_D': nrm(k[29], (N_ODD, D_MODEL), 0.5),
        'odd_glu_a': nrm(k[30], (N_ODD, D_MODEL, D_MODEL), D_MODEL ** -0.5),
        'odd_glu_b': nrm(k[31], (N_ODD, D_MODEL, D_MODEL), D_MODEL ** -0.5),
    }


def reference(x_prompt, x_sample, cache_win_k, cache_win_v, state_ret, state_s5_re, state_s5_im,
              c_prompt, c_sample, ada_w, ada_b, norm_mix, norm_ffn, ffn_wg, ffn_wu, ffn_wd,
              even_w_in, even_q_gain, even_k_gain, even_sinks, even_ret_gain, even_w_out,
              odd_A_re, odd_A_im, odd_log_dt, odd_B_re, odd_B_im, odd_C_re, odd_C_im, odd_D,
              odd_glu_a, odd_glu_b):
    p = dict(ada_w=ada_w, ada_b=ada_b, norm_mix=norm_mix, norm_ffn=norm_ffn,
             ffn_wg=ffn_wg, ffn_wu=ffn_wu, ffn_wd=ffn_wd,
             even_w_in=even_w_in, even_q_gain=even_q_gain, even_k_gain=even_k_gain,
             even_sinks=even_sinks, even_ret_gain=even_ret_gain, even_w_out=even_w_out,
             odd_A_re=odd_A_re, odd_A_im=odd_A_im, odd_log_dt=odd_log_dt,
             odd_B_re=odd_B_re, odd_B_im=odd_B_im, odd_C_re=odd_C_re, odd_C_im=odd_C_im,
             odd_D=odd_D, odd_glu_a=odd_glu_a, odd_glu_b=odd_glu_b)
    y_prompt, p_k, p_v, p_ret, p_re, p_im = trunk(x_prompt, c_prompt, None, None, None, None, None, p)
    y_sample, s_k, s_v, s_ret, s_re, s_im = trunk(x_sample, c_sample, cache_win_k, cache_win_v,
                                                  state_ret, state_s5_re, state_s5_im, p)
    return (y_prompt, y_sample, p_k, p_v, p_ret, p_re, p_im, s_k, s_v, s_ret, s_re, s_im)
```

```python
import functools
import math

import jax
import jax.numpy as jnp
from jax import lax
from jax.experimental import pallas as pl
from jax.experimental.pallas import tpu as pltpu

F32 = jnp.float32
BF16 = jnp.bfloat16

EPS = 1e-6
NEG_INF = -1e30
ROWS = 8

A_HEADS, A_KV, A_GROUP, A_HD = 8, 2, 4, 64
WINDOW = 128
B_HEADS, B_KD, B_VD = 4, 128, 128
S5_GROUP, S5_STATE = 16, 64
S5_OCT = 8

QA_W, KA_W, VA_W = A_HEADS * A_HD, A_KV * A_HD, A_KV * A_HD
QB_W, KB_W, VB_W, GB_W = B_HEADS * B_KD, B_HEADS * B_KD, B_HEADS * B_VD, B_HEADS * B_VD
OFF_QA = 0
OFF_KA = OFF_QA + QA_W
OFF_VA = OFF_KA + KA_W
OFF_QB = OFF_VA + VA_W
OFF_KB = OFF_QB + QB_W
OFF_VB = OFF_KB + KB_W
OFF_GB = OFF_VB + VB_W

VMEM_LIMIT = 56 * 1024 * 1024


def _ret_log_gamma(h):
    return math.log1p(-(2.0 ** (-5.0 - h)))


def _alibi_slope(h):
    return 2.0 ** (-8.0 * (h + 1) / A_HEADS)


def _const_spec(shape):
    nd = len(shape)
    return pl.BlockSpec(shape, lambda i, _n=nd: (0,) * _n, pipeline_mode=pl.Buffered(1))


def _params():
    return pltpu.CompilerParams(dimension_semantics=("arbitrary",), vmem_limit_bytes=VMEM_LIMIT)


def _dot(a, b):
    return jnp.dot(a, b, preferred_element_type=F32)


def _dot_nt(a, b):
    return lax.dot_general(a, b, (((1,), (1,)), ((), ())), preferred_element_type=F32)


def _dot_tn(a, b):
    return lax.dot_general(a, b, (((0,), (0,)), ((), ())), preferred_element_type=F32)


def _bmm(a, b):
    return lax.dot_general(a, b, (((2,), (1,)), ((0,), (0,))), preferred_element_type=F32)


def _bmm_nt(a, b):
    return lax.dot_general(a, b, (((2,), (2,)), ((0,), (0,))), preferred_element_type=F32)


def _bmm_tn(a, b):
    return lax.dot_general(a, b, (((1,), (1,)), ((0,), (0,))), preferred_element_type=F32)


def _rms(x, g):
    return x * lax.rsqrt(jnp.mean(x * x, axis=-1, keepdims=True) + EPS) * g


def _modulate(x3, mod, gain, which):
    d = x3.shape[-1]
    sh = mod[:, :, (3 * which) * d:(3 * which + 1) * d]
    sc = mod[:, :, (3 * which + 1) * d:(3 * which + 2) * d]
    return _rms(x3, gain) * (1.0 + sc) + sh


def _gate(mod, which, d):
    return mod[:, :, (3 * which + 2) * d:(3 * which + 3) * d]


def _adaln_body(c_ref, w_ref, b_ref, o_ref):
    c = c_ref[...]
    a = (c * jax.nn.sigmoid(c)).astype(BF16)
    o_ref[0] = _dot(a, w_ref[0].astype(BF16)) + b_ref[0]


def _adaln(c_all, ada_w, ada_b):
    depth, d, n = ada_w.shape
    r = c_all.shape[0]
    tn = 1536
    return pl.pallas_call(
        _adaln_body,
        grid=(depth, n // tn),
        in_specs=[pl.BlockSpec((r, d), lambda l, j: (0, 0)),
                  pl.BlockSpec((1, d, tn), lambda l, j: (l, 0, j)),
                  pl.BlockSpec((1, 1, tn), lambda l, j: (l, 0, j))],
        out_specs=pl.BlockSpec((1, r, tn), lambda l, j: (l, 0, j)),
        out_shape=jax.ShapeDtypeStruct((depth, r, n), F32),
        compiler_params=pltpu.CompilerParams(dimension_semantics=("arbitrary", "arbitrary"),
                                             vmem_limit_bytes=VMEM_LIMIT),
        name="adaln",
    )(c_all, ada_w, ada_b.reshape(depth, 1, n))


def _ffn_body(x_ref, mod_ref, gn_ref, wg_ref, wu_ref, wd_ref, o_ref):
    x = x_ref[...]
    sb, _, d = x.shape
    mod = mod_ref[...]
    h = _modulate(x, mod, gn_ref[...], 1).reshape(sb * ROWS, d).astype(BF16)
    a = _dot(h, wg_ref[...])
    b = _dot(h, wu_ref[...])
    act = (a * jax.nn.sigmoid(a) * b).astype(BF16)
    y = _dot(act, wd_ref[...])
    o_ref[...] = x + _gate(mod, 1, d) * y.reshape(sb, ROWS, d)


def _ffn(x3, mod3, gain, wg, wu, wd, sb):
    n8, _, d = x3.shape
    f = wg.shape[1]
    per_seq = mod3.shape[0] != 1
    mod_spec = (pl.BlockSpec((sb, 1, 6 * d), lambda i: (i, 0, 0)) if per_seq
                else pl.BlockSpec((1, 1, 6 * d), lambda i: (0, 0, 0)))
    return pl.pallas_call(
        _ffn_body,
        grid=(n8 // sb,),
        in_specs=[pl.BlockSpec((sb, ROWS, d), lambda i: (i, 0, 0)), mod_spec,
                  _const_spec((1, d)), _const_spec((d, f)), _const_spec((d, f)), _const_spec((f, d))],
        out_specs=pl.BlockSpec((sb, ROWS, d), lambda i: (i, 0, 0)),
        out_shape=jax.ShapeDtypeStruct(x3.shape, F32),
        compiler_params=_params(),
        name="ffn",
    )(x3, mod3, gain.reshape(1, d), wg, wu, wd)


def _head_rms(t, g):
    return t * lax.rsqrt(jnp.mean(t * t, axis=-1, keepdims=True) + EPS) * g


def _group_norm_gate(o, gain, gate):
    mu = jnp.mean(o, axis=-1, keepdims=True)
    var = jnp.mean(jnp.square(o - mu), axis=-1, keepdims=True)
    return (o - mu) * lax.rsqrt(var + EPS) * gain * (gate * jax.nn.sigmoid(gate))


def _ret_decay(hb, c):
    lg = _ret_log_gamma(hb)
    ii = lax.broadcasted_iota(jnp.int32, (c, c), 0)
    jj = lax.broadcasted_iota(jnp.int32, (c, c), 1)
    diff = (ii - jj).astype(F32)
    d_in = jnp.where(diff >= 0, jnp.exp(lg * jnp.maximum(diff, 0.0)), 0.0)
    row = lax.broadcasted_iota(jnp.int32, (c, B_KD), 0).astype(F32)
    d_q = jnp.exp(lg * (row + 1.0))
    d_k = jnp.exp(lg * (c - 1.0 - row))
    d_c = math.exp(lg * c)
    return d_in, d_q, d_k, d_c


def _mix_even_prompt_body(x_ref, mod_ref, gn_ref, win_ref, qg_ref, kg_ref, sink_ref, rg_ref, wout_ref,
                          o_ref, pk_ref, pv_ref, ps_ref, mix_ref, kbuf_ref, vbuf_ref):
    step = pl.program_id(0)
    blk = WINDOW

    @pl.when(step == 0)
    def _():
        kbuf_ref[...] = jnp.zeros_like(kbuf_ref)
        vbuf_ref[...] = jnp.zeros_like(vbuf_ref)
        ps_ref[...] = jnp.zeros_like(ps_ref)

    x = x_ref[...]
    sb, _, d = x.shape
    tb = sb * ROWS
    mod = mod_ref[...]
    h = _modulate(x, mod, gn_ref[...], 0).reshape(tb, d).astype(BF16)
    proj = _dot(h, win_ref[...])
    qg = qg_ref[...]
    kg = kg_ref[...]
    rg = rg_ref[...]

    qi = lax.broadcasted_iota(jnp.int32, (blk, 2 * blk), 0)
    ki = lax.broadcasted_iota(jnp.int32, (blk, 2 * blk), 1)
    dist = qi + blk - ki
    in_window = (dist >= 0) & (dist < WINDOW)
    distf = dist.astype(F32)
    decays = [_ret_decay(hb, blk) for hb in range(B_HEADS)]

    for j in range(tb // blk):
        r0 = j * blk
        valid = in_window
        if j == 0:
            valid = valid & ((ki >= blk) | (step > 0))
        for kv in range(A_KV):
            kh = proj[r0:r0 + blk, OFF_KA + kv * A_HD:OFF_KA + (kv + 1) * A_HD]
            kbuf_ref[kv, blk:2 * blk, :] = _head_rms(kh, kg)
            vbuf_ref[kv, blk:2 * blk, :] = proj[r0:r0 + blk, OFF_VA + kv * A_HD:OFF_VA + (kv + 1) * A_HD]
            k2 = kbuf_ref[kv].astype(BF16)
            v2 = vbuf_ref[kv].astype(BF16)
            for g in range(A_GROUP):
                hd = kv * A_GROUP + g
                qh = _head_rms(proj[r0:r0 + blk, OFF_QA + hd * A_HD:OFF_QA + (hd + 1) * A_HD], qg)
                s = _dot_nt(qh.astype(BF16), k2) * (A_HD ** -0.5) - _alibi_slope(hd) * distf
                s = jnp.where(valid, s, NEG_INF)
                sink = sink_ref[hd]
                mx = jnp.maximum(jnp.max(s, axis=-1, keepdims=True), sink)
                p = jnp.exp(s - mx)
                den = jnp.sum(p, axis=-1, keepdims=True) + jnp.exp(sink - mx)
                mix_ref[r0:r0 + blk, hd * A_HD:(hd + 1) * A_HD] = _dot(p.astype(BF16), v2) / den
            kbuf_ref[kv, 0:blk, :] = kbuf_ref[kv, blk:2 * blk, :]
            vbuf_ref[kv, 0:blk, :] = vbuf_ref[kv, blk:2 * blk, :]
        for hb in range(B_HEADS):
            d_in, d_q, d_k, d_c = decays[hb]
            qb = proj[r0:r0 + blk, OFF_QB + hb * B_KD:OFF_QB + (hb + 1) * B_KD]
            kb = proj[r0:r0 + blk, OFF_KB + hb * B_KD:OFF_KB + (hb + 1) * B_KD] * (B_KD ** -0.5)
            vb = proj[r0:r0 + blk, OFF_VB + hb * B_VD:OFF_VB + (hb + 1) * B_VD].astype(BF16)
            gb = proj[r0:r0 + blk, OFF_GB + hb * B_VD:OFF_GB + (hb + 1) * B_VD]
            state = ps_ref[hb]
            qb16 = qb.astype(BF16)
            inner = _dot_nt(qb16, kb.astype(BF16)) * d_in
            o = _dot(inner.astype(BF16), vb) + _dot(qb16, state.astype(BF16)) * d_q
            ps_ref[hb] = state * d_c + _dot_tn((kb * d_k).astype(BF16), vb)
            mix_ref[r0:r0 + blk, QA_W + hb * B_VD:QA_W + (hb + 1) * B_VD] = _group_norm_gate(
                o, rg[:, hb * B_VD:(hb + 1) * B_VD], gb)

    out = _dot(mix_ref[...].astype(BF16), wout_ref[...])
    o_ref[...] = x + _gate(mod, 0, d) * out.reshape(sb, ROWS, d)

    @pl.when(step == pl.num_programs(0) - 1)
    def _():
        for kv in range(A_KV):
            pk_ref[:, kv * A_HD:(kv + 1) * A_HD] = kbuf_ref[kv, 0:blk, :]
            pv_ref[:, kv * A_HD:(kv + 1) * A_HD] = vbuf_ref[kv, 0:blk, :]


def _mix_even_prompt(x3, mod3, gain, w_in, q_gain, k_gain, sinks, ret_gain, w_out, sb):
    n8, _, d = x3.shape
    in_w = w_in.shape[1]
    tb = sb * ROWS
    return pl.pallas_call(
        _mix_even_prompt_body,
        grid=(n8 // sb,),
        in_specs=[pl.BlockSpec((sb, ROWS, d), lambda i: (i, 0, 0)),
                  pl.BlockSpec((1, 1, 6 * d), lambda i: (0, 0, 0)),
                  _const_spec((1, d)), _const_spec((d, in_w)),
                  _const_spec((1, A_HD)), _const_spec((1, A_HD)),
                  pl.BlockSpec(memory_space=pltpu.SMEM),
                  _const_spec((1, VB_W)), _const_spec((QA_W + VB_W, d))],
        out_specs=[pl.BlockSpec((sb, ROWS, d), lambda i: (i, 0, 0)),
                   pl.BlockSpec((WINDOW, KA_W), lambda i: (0, 0)),
                   pl.BlockSpec((WINDOW, VA_W), lambda i: (0, 0)),
                   pl.BlockSpec((B_HEADS, B_KD, B_VD), lambda i: (0, 0, 0))],
        out_shape=[jax.ShapeDtypeStruct(x3.shape, F32),
                   jax.ShapeDtypeStruct((WINDOW, KA_W), F32),
                   jax.ShapeDtypeStruct((WINDOW, VA_W), F32),
                   jax.ShapeDtypeStruct((B_HEADS, B_KD, B_VD), F32)],
        scratch_shapes=[pltpu.VMEM((tb, QA_W + VB_W), F32),
                        pltpu.VMEM((A_KV, 2 * WINDOW, A_HD), F32),
                        pltpu.VMEM((A_KV, 2 * WINDOW, A_HD), F32)],
        compiler_params=_params(),
        name="mix_even_prompt",
    )(x3, mod3, gain.reshape(1, d), w_in, q_gain.reshape(1, A_HD), k_gain.reshape(1, A_HD),
      sinks, ret_gain.reshape(1, VB_W), w_out)


def _mix_even_sample_body(x_ref, mod_ref, gn_ref, win_ref, qg_ref, kg_ref, sink_ref, rg_ref, wout_ref,
                          ck_ref, cv_ref, s0_ref,
                          o_ref, nk_ref, nv_ref, ns_ref, mix_ref):
    x = x_ref[...]
    sb, length, d = x.shape
    tb = sb * length
    w = ck_ref.shape[1]
    mod = mod_ref[...]
    h = _modulate(x, mod, gn_ref[...], 0).reshape(tb, d).astype(BF16)
    proj = _dot(h, win_ref[...])
    qg = qg_ref[...]
    kg = kg_ref[...]
    rg = rg_ref[...]

    rows = A_GROUP * length
    qpos_c = lax.broadcasted_iota(jnp.int32, (rows, w), 0) % length
    kpos_c = lax.broadcasted_iota(jnp.int32, (rows, w), 1)
    dist_c = w + qpos_c - kpos_c
    valid_c = (dist_c >= 0) & (dist_c < WINDOW)
    qpos_n = lax.broadcasted_iota(jnp.int32, (rows, length), 0) % length
    kpos_n = lax.broadcasted_iota(jnp.int32, (rows, length), 1)
    dist_n = qpos_n - kpos_n
    valid_n = (dist_n >= 0) & (dist_n < WINDOW)
    row_g = lax.broadcasted_iota(jnp.int32, (rows, 1), 0) // length

    for kv in range(A_KV):
        lanes = slice(kv * A_HD, (kv + 1) * A_HD)
        kn = _head_rms(proj[:, OFF_KA + kv * A_HD:OFF_KA + (kv + 1) * A_HD], kg).reshape(sb, length, A_HD)
        vn = proj[:, OFF_VA + kv * A_HD:OFF_VA + (kv + 1) * A_HD].reshape(sb, length, A_HD)
        nk_ref[:, 0:w - length, lanes] = ck_ref[:, length:w, lanes]
        nv_ref[:, 0:w - length, lanes] = cv_ref[:, length:w, lanes]
        nk_ref[:, w - length:w, lanes] = kn
        nv_ref[:, w - length:w, lanes] = vn
        kc = ck_ref[:, :, lanes].astype(BF16)
        vc = cv_ref[:, :, lanes].astype(BF16)
        q4 = jnp.concatenate(
            [_head_rms(proj[:, OFF_QA + (kv * A_GROUP + g) * A_HD:OFF_QA + (kv * A_GROUP + g + 1) * A_HD], qg)
             .reshape(sb, length, A_HD) for g in range(A_GROUP)], axis=1).astype(BF16)
        slope = jnp.zeros((rows, 1), F32)
        sink = jnp.zeros((rows, 1), F32)
        for g in range(A_GROUP):
            hd = kv * A_GROUP + g
            slope = jnp.where(row_g == g, _alibi_slope(hd), slope)
            sink = jnp.where(row_g == g, sink_ref[hd], sink)
        scale = A_HD ** -0.5
        s_c = _bmm_nt(q4, kc) * scale - slope * dist_c.astype(F32)
        s_c = jnp.where(valid_c, s_c, NEG_INF)
        s_n = _bmm_nt(q4, kn.astype(BF16)) * scale - slope * dist_n.astype(F32)
        s_n = jnp.where(valid_n, s_n, NEG_INF)
        mx = jnp.maximum(jnp.maximum(jnp.max(s_c, axis=-1, keepdims=True),
                                     jnp.max(s_n, axis=-1, keepdims=True)), sink)
        p_c = jnp.exp(s_c - mx)
        p_n = jnp.exp(s_n - mx)
        den = (jnp.sum(p_c, axis=-1, keepdims=True) + jnp.sum(p_n, axis=-1, keepdims=True)
               + jnp.exp(sink - mx))
        o4 = (_bmm(p_c.astype(BF16), vc) + _bmm(p_n.astype(BF16), vn.astype(BF16))) / den
        for g in range(A_GROUP):
            hd = kv * A_GROUP + g
            mix_ref[:, hd * A_HD:(hd + 1) * A_HD] = o4[:, g * length:(g + 1) * length, :].reshape(tb, A_HD)

    for hb in range(B_HEADS):
        d_in, d_q, d_k, d_c = _ret_decay(hb, length)
        qb = proj[:, OFF_QB + hb * B_KD:OFF_QB + (hb + 1) * B_KD].reshape(sb, length, B_KD).astype(BF16)
        kb = proj[:, OFF_KB + hb * B_KD:OFF_KB + (hb + 1) * B_KD].reshape(sb, length, B_KD) * (B_KD ** -0.5)
        vb = proj[:, OFF_VB + hb * B_VD:OFF_VB + (hb + 1) * B_VD].reshape(sb, length, B_VD).astype(BF16)
        gb = proj[:, OFF_GB + hb * B_VD:OFF_GB + (hb + 1) * B_VD]
        state = s0_ref[:, hb]
        inner = _bmm_nt(qb, kb.astype(BF16)) * d_in
        o = _bmm(inner.astype(BF16), vb) + _bmm(qb, state.astype(BF16)) * d_q
        ns_ref[:, hb] = state * d_c + _bmm_tn((kb * d_k).astype(BF16), vb)
        mix_ref[:, QA_W + hb * B_VD:QA_W + (hb + 1) * B_VD] = _group_norm_gate(
            o.reshape(tb, B_VD), rg[:, hb * B_VD:(hb + 1) * B_VD], gb)

    out = _dot(mix_ref[...].astype(BF16), wout_ref[...])
    o_ref[...] = x + _gate(mod, 0, d) * out.reshape(sb, length, d)


def _mix_even_sample(x3, mod3, gain, w_in, q_gain, k_gain, sinks, ret_gain, w_out, cache_k, cache_v, state, sb):
    n, length, d = x3.shape
    in_w = w_in.shape[1]
    w = cache_k.shape[1]
    seq_spec = lambda shape: pl.BlockSpec((sb,) + shape, lambda i, _n=len(shape): (i,) + (0,) * _n)
    return pl.pallas_call(
        _mix_even_sample_body,
        grid=(n // sb,),
        in_specs=[seq_spec((length, d)), seq_spec((1, 6 * d)),
                  _const_spec((1, d)), _const_spec((d, in_w)),
                  _const_spec((1, A_HD)), _const_spec((1, A_HD)),
                  pl.BlockSpec(memory_space=pltpu.SMEM),
                  _const_spec((1, VB_W)), _const_spec((QA_W + VB_W, d)),
                  seq_spec((w, KA_W)), seq_spec((w, VA_W)), seq_spec((B_HEADS, B_KD, B_VD))],
        out_specs=[seq_spec((length, d)), seq_spec((w, KA_W)), seq_spec((w, VA_W)),
                   seq_spec((B_HEADS, B_KD, B_VD))],
        out_shape=[jax.ShapeDtypeStruct(x3.shape, F32),
                   jax.ShapeDtypeStruct(cache_k.shape, F32),
                   jax.ShapeDtypeStruct(cache_v.shape, F32),
                   jax.ShapeDtypeStruct(state.shape, F32)],
        scratch_shapes=[pltpu.VMEM((sb * length, QA_W + VB_W), F32)],
        compiler_params=_params(),
        name="mix_even_sample",
    )(x3, mod3, gain.reshape(1, d), w_in, q_gain.reshape(1, A_HD), k_gain.reshape(1, A_HD),
      sinks, ret_gain.reshape(1, VB_W), w_out, cache_k, cache_v, state)


def _s5_prep_body(are_ref, aim_ref, ldt_ref, btr_ref, bti_ref, lre_ref, lim_ref, bbr_ref, bbi_ref):
    a_re = are_ref[...]
    a_im = aim_ref[...]
    dt = jnp.exp(ldt_ref[...])
    mag = jnp.exp(a_re * dt)
    lam_re = mag * jnp.cos(a_im * dt)
    lam_im = mag * jnp.sin(a_im * dt)
    den = a_re * a_re + a_im * a_im
    n_re = lam_re - 1.0
    n_im = lam_im
    f_re = (n_re * a_re + n_im * a_im) / den
    f_im = (n_im * a_re - n_re * a_im) / den
    br = btr_ref[...]
    bi = bti_ref[...]
    lre_ref[...] = lam_re
    lim_ref[...] = lam_im
    bbr_ref[...] = f_re * br - f_im * bi
    bbi_ref[...] = f_re * bi + f_im * br


def _s5_prep(a_re, a_im, log_dt, b_re, b_im):
    g, p = a_re.shape
    k = b_re.shape[-1]
    bt_re = jnp.swapaxes(b_re, 1, 2)
    bt_im = jnp.swapaxes(b_im, 1, 2)
    out = pl.pallas_call(
        _s5_prep_body,
        out_shape=[jax.ShapeDtypeStruct((g, 1, p), F32), jax.ShapeDtypeStruct((g, 1, p), F32),
                   jax.ShapeDtypeStruct((g, k, p), F32), jax.ShapeDtypeStruct((g, k, p), F32)],
        name="s5_prep",
    )(a_re.reshape(g, 1, p), a_im.reshape(g, 1, p), log_dt.reshape(g, 1, 1), bt_re, bt_im)
    return out


def _block_diag(t):
    g, a, b = t.shape
    t = t.reshape(g // S5_OCT, S5_OCT, a, b)
    eye = jnp.eye(S5_OCT, dtype=t.dtype)
    out = t[:, :, :, None, :] * eye[None, :, None, :, None]
    return out.reshape(g // S5_OCT, S5_OCT * a, S5_OCT * b)


def _gelu_glu_out(x, mod, y, u, dskip, glua_ref, glub_ref):
    sb, rows, d = x.shape
    y = y + dskip * u
    yg = jax.nn.gelu(y, approximate=True).astype(BF16)
    out = _dot(yg, glua_ref[...]) * jax.nn.sigmoid(_dot(yg, glub_ref[...]))
    return x + _gate(mod, 0, d) * out.reshape(sb, rows, d)


def _mix_odd_prompt_body(x_ref, mod_ref, gn_ref, bblk_ref, lam_ref, cre_ref, cim_ref, dskip_ref,
                         glua_ref, glub_ref, o_ref, hre_ref, him_ref, d_ref, y_ref):
    step = pl.program_id(0)
    x = x_ref[...]
    sb, _, d = x.shape
    tm = sb * ROWS
    n_oct = bblk_ref.shape[0]
    half = bblk_ref.shape[2] // 2
    n_ch = bblk_ref.shape[2] // 128
    hc = n_ch // 2

    @pl.when(step == 0)
    def _():
        hre_ref[...] = jnp.zeros_like(hre_ref)
        him_ref[...] = jnp.zeros_like(him_ref)

    mod = mod_ref[...]
    u = _modulate(x, mod, gn_ref[...], 0).reshape(tm, d)
    u16 = u.astype(BF16)
    for s in range(n_oct):
        bu = _dot(u16[:, s * 128:(s + 1) * 128], bblk_ref[s])
        for c in range(n_ch):
            d_ref[c, pl.ds(s, tm, stride=n_oct), :] = bu[:, c * 128:(c + 1) * 128]

    lam_re = lam_ref[0]
    lam_im = lam_ref[1]

    def scan_step(t, carry):
        h_re, h_im = carry
        r = pl.multiple_of(t * n_oct, n_oct)
        b_re = jnp.concatenate([d_ref[c, pl.ds(r, n_oct), :] for c in range(hc)], axis=1)
        b_im = jnp.concatenate([d_ref[hc + c, pl.ds(r, n_oct), :] for c in range(hc)], axis=1)
        n_re = lam_re * h_re - lam_im * h_im + b_re
        n_im = lam_re * h_im + lam_im * h_re + b_im
        for c in range(hc):
            d_ref[c, pl.ds(r, n_oct), :] = n_re[:, c * 128:(c + 1) * 128]
            d_ref[hc + c, pl.ds(r, n_oct), :] = n_im[:, c * 128:(c + 1) * 128]
        return n_re, n_im

    h_re, h_im = lax.fori_loop(0, tm, scan_step, (hre_ref[...], him_ref[...]), unroll=8)
    hre_ref[...] = h_re
    him_ref[...] = h_im

    for s in range(n_oct):
        hs_re = jnp.concatenate([d_ref[c, pl.ds(s, tm, stride=n_oct), :] for c in range(hc)], axis=1)
        hs_im = jnp.concatenate([d_ref[hc + c, pl.ds(s, tm, stride=n_oct), :] for c in range(hc)], axis=1)
        y_ref[:, s * 128:(s + 1) * 128] = (_dot(hs_re.astype(BF16), cre_ref[s])
                                           - _dot(hs_im.astype(BF16), cim_ref[s]))
    o_ref[...] = _gelu_glu_out(x, mod, y_ref[...], u, dskip_ref[...], glua_ref, glub_ref)


def _mix_odd_prompt(x3, mod3, gain, bblk, lam_d, cre, cim, dskip, glu_a, glu_b, sb):
    n8, _, d = x3.shape
    n_oct, kin, wid = bblk.shape
    half = wid // 2
    tm = sb * ROWS
    return pl.pallas_call(
        _mix_odd_prompt_body,
        grid=(n8 // sb,),
        in_specs=[pl.BlockSpec((sb, ROWS, d), lambda i: (i, 0, 0)),
                  pl.BlockSpec((1, 1, 6 * d), lambda i: (0, 0, 0)),
                  _const_spec((1, d)), _const_spec(bblk.shape), _const_spec(lam_d.shape),
                  _const_spec(cre.shape), _const_spec(cim.shape), _const_spec((1, d)),
                  _const_spec((d, d)), _const_spec((d, d))],
        out_specs=[pl.BlockSpec((sb, ROWS, d), lambda i: (i, 0, 0)),
                   pl.BlockSpec((n_oct, half), lambda i: (0, 0)),
                   pl.BlockSpec((n_oct, half), lambda i: (0, 0))],
        out_shape=[jax.ShapeDtypeStruct(x3.shape, F32),
                   jax.ShapeDtypeStruct((n_oct, half), F32),
                   jax.ShapeDtypeStruct((n_oct, half), F32)],
        scratch_shapes=[pltpu.VMEM((wid // 128, tm * n_oct, 128), F32), pltpu.VMEM((tm, d), F32)],
        compiler_params=_params(),
        name="mix_odd_prompt",
    )(x3, mod3, gain.reshape(1, d), bblk, lam_d, cre, cim, dskip.reshape(1, d), glu_a, glu_b)


def _mix_odd_sample_body(x_ref, mod_ref, gn_ref, bblk_ref, lam_ref, cre_ref, cim_ref, dskip_ref,
                         glua_ref, glub_ref, sre_ref, sim_ref, o_ref, nre_ref, nim_ref, d_ref, y_ref):
    x = x_ref[...]
    sb, length, d = x.shape
    tm = sb * length
    n_oct = bblk_ref.shape[0]
    half = bblk_ref.shape[2] // 2
    mod = mod_ref[...]
    u = _modulate(x, mod, gn_ref[...], 0).reshape(tm, d)
    u16 = u.astype(BF16)
    n_ch = bblk_ref.shape[2] // 128
    hc = n_ch // 2
    for s in range(n_oct):
        bu = _dot(u16[:, s * 128:(s + 1) * 128], bblk_ref[s])
        for c in range(n_ch):
            d_ref[c] = bu[:, c * 128:(c + 1) * 128]
        lam_re = lam_ref[0, s:s + 1, :]
        lam_im = lam_ref[1, s:s + 1, :]
        h_re = sre_ref[:, s * half:(s + 1) * half]
        h_im = sim_ref[:, s * half:(s + 1) * half]
        for t in range(length):
            b_re = jnp.concatenate([d_ref[c, pl.ds(t, sb, stride=length), :] for c in range(hc)], axis=1)
            b_im = jnp.concatenate([d_ref[hc + c, pl.ds(t, sb, stride=length), :] for c in range(hc)], axis=1)
            n_re = lam_re * h_re - lam_im * h_im + b_re
            n_im = lam_re * h_im + lam_im * h_re + b_im
            for c in range(hc):
                d_ref[c, pl.ds(t, sb, stride=length), :] = n_re[:, c * 128:(c + 1) * 128]
                d_ref[hc + c, pl.ds(t, sb, stride=length), :] = n_im[:, c * 128:(c + 1) * 128]
            h_re, h_im = n_re, n_im
        nre_ref[:, s * half:(s + 1) * half] = h_re
        nim_ref[:, s * half:(s + 1) * half] = h_im
        hs_re = jnp.concatenate([d_ref[c] for c in range(hc)], axis=1)
        hs_im = jnp.concatenate([d_ref[hc + c] for c in range(hc)], axis=1)
        y_ref[:, s * 128:(s + 1) * 128] = (_dot(hs_re.astype(BF16), cre_ref[s])
                                           - _dot(hs_im.astype(BF16), cim_ref[s]))
    o_ref[...] = _gelu_glu_out(x, mod, y_ref[...], u, dskip_ref[...], glua_ref, glub_ref)


def _mix_odd_sample(x3, mod3, gain, bblk, lam_d, cre, cim, dskip, glu_a, glu_b, s_re, s_im, sb):
    n, length, d = x3.shape
    n_oct, kin, wid = bblk.shape
    tm = sb * length
    nstate = s_re.shape[1]
    return pl.pallas_call(
        _mix_odd_sample_body,
        grid=(n // sb,),
        in_specs=[pl.BlockSpec((sb, length, d), lambda i: (i, 0, 0)),
                  pl.BlockSpec((sb, 1, 6 * d), lambda i: (i, 0, 0)),
                  _const_spec((1, d)), _const_spec(bblk.shape), _const_spec(lam_d.shape),
                  _const_spec(cre.shape), _const_spec(cim.shape), _const_spec((1, d)),
                  _const_spec((d, d)), _const_spec((d, d)),
                  pl.BlockSpec((sb, nstate), lambda i: (i, 0)),
                  pl.BlockSpec((sb, nstate), lambda i: (i, 0))],
        out_specs=[pl.BlockSpec((sb, length, d), lambda i: (i, 0, 0)),
                   pl.BlockSpec((sb, nstate), lambda i: (i, 0)),
                   pl.BlockSpec((sb, nstate), lambda i: (i, 0))],
        out_shape=[jax.ShapeDtypeStruct(x3.shape, F32),
                   jax.ShapeDtypeStruct(s_re.shape, F32),
                   jax.ShapeDtypeStruct(s_im.shape, F32)],
        scratch_shapes=[pltpu.VMEM((wid // 128, tm, 128), F32), pltpu.VMEM((tm, d), F32)],
        compiler_params=_params(),
        name="mix_odd_sample",
    )(x3, mod3, gain.reshape(1, d), bblk, lam_d, cre, cim, dskip.reshape(1, d), glu_a, glu_b, s_re, s_im)


def _pick(n, want):
    while n % want:
        want //= 2
    return max(want, 1)


def kernel(x_prompt, x_sample, cache_win_k, cache_win_v, state_ret, state_s5_re, state_s5_im, c_prompt, c_sample, ada_w, ada_b, norm_mix, norm_ffn, ffn_wg, ffn_wu, ffn_wd, even_w_in, even_q_gain, even_k_gain, even_sinks, even_ret_gain, even_w_out, odd_A_re, odd_A_im, odd_log_dt, odd_B_re, odd_B_im, odd_C_re, odd_C_im, odd_D, odd_glu_a, odd_glu_b):
    bp, lp, d = x_prompt.shape
    ns, ls, _ = x_sample.shape
    assert bp == 1 and ls == ROWS and lp % WINDOW == 0
    w = cache_win_k.shape[2]
    groups, p_state = odd_A_re.shape[1:]

    n_c = bp + ns
    n_pad = -n_c % ROWS
    c_all = jnp.concatenate([c_prompt, c_sample, jnp.zeros((n_pad, d), F32)], axis=0)
    mod = _adaln(c_all, ada_w, ada_b)
    mod_p = [mod[l, 0:1].reshape(1, 1, 6 * d) for l in range(2)]
    mod_s = [mod[l, bp:bp + ns].reshape(ns, 1, 6 * d) for l in range(2)]

    bf = lambda t: t.astype(BF16)
    w_in, w_out = bf(even_w_in[0]), bf(even_w_out[0])
    wg, wu, wd = bf(ffn_wg), bf(ffn_wu), bf(ffn_wd)
    glu_a, glu_b = bf(odd_glu_a[0]), bf(odd_glu_b[0])

    xp = x_prompt.reshape(lp // ROWS, ROWS, d)
    xs = x_sample

    sb_p = _pick(lp // ROWS, 32)
    sb_s = _pick(ns, 32)
    sb_s_even = _pick(ns, 16)

    xp, p_k, p_v, p_ret = _mix_even_prompt(xp, mod_p[0], norm_mix[0], w_in, even_q_gain[0], even_k_gain[0],
                                           even_sinks[0], even_ret_gain[0], w_out, sb_p)
    xs, s_k, s_v, s_ret = _mix_even_sample(xs, mod_s[0], norm_mix[0], w_in, even_q_gain[0], even_k_gain[0],
                                           even_sinks[0], even_ret_gain[0], w_out,
                                           cache_win_k[0].reshape(ns, w, KA_W), cache_win_v[0].reshape(ns, w, VA_W),
                                           state_ret[0], sb_s_even)
    xp = _ffn(xp, mod_p[0], norm_ffn[0], wg[0], wu[0], wd[0], sb_p)
    xs = _ffn(xs, mod_s[0], norm_ffn[0], wg[0], wu[0], wd[0], sb_s)

    lam_re, lam_im, bbt_re, bbt_im = _s5_prep(odd_A_re[0], odd_A_im[0], odd_log_dt[0], odd_B_re[0], odd_B_im[0])
    n_oct = groups // S5_OCT
    half = S5_OCT * p_state
    bblk = bf(jnp.concatenate([_block_diag(bbt_re), _block_diag(bbt_im)], axis=-1))
    cre = bf(_block_diag(jnp.swapaxes(odd_C_re[0], 1, 2)))
    cim = bf(_block_diag(jnp.swapaxes(odd_C_im[0], 1, 2)))
    lam_d = jnp.stack([lam_re.reshape(n_oct, half), lam_im.reshape(n_oct, half)])

    xp, p_re, p_im = _mix_odd_prompt(xp, mod_p[1], norm_mix[1], bblk, lam_d, cre, cim, odd_D[0], glu_a, glu_b, sb_p)
    xs, s_re, s_im = _mix_odd_sample(xs, mod_s[1], norm_mix[1], bblk, lam_d, cre, cim, odd_D[0], glu_a, glu_b,
                                     state_s5_re[0].reshape(ns, groups * p_state),
                                     state_s5_im[0].reshape(ns, groups * p_state), sb_s)
    xp = _ffn(xp, mod_p[1], norm_ffn[1], wg[1], wu[1], wd[1], sb_p)
    xs = _ffn(xs, mod_s[1], norm_ffn[1], wg[1], wu[1], wd[1], sb_s)

    y_prompt = xp.reshape(bp, lp, d)
    y_sample = xs
    return (y_prompt, y_sample,
            p_k.reshape(1, bp, WINDOW, A_KV, A_HD), p_v.reshape(1, bp, WINDOW, A_KV, A_HD),
            p_ret.reshape(1, bp, B_HEADS, B_KD, B_VD),
            p_re.reshape(1, bp, groups, p_state), p_im.reshape(1, bp, groups, p_state),
            s_k.reshape(1, ns, w, A_KV, A_HD), s_v.reshape(1, ns, w, A_KV, A_HD),
            s_ret.reshape(1, ns, B_HEADS, B_KD, B_VD),
            s_re.reshape(1, ns, groups, p_state), s_im.reshape(1, ns, groups, p_state))
```

```python
import functools
import math

import jax
import jax.numpy as jnp
from jax import lax
from jax.experimental import pallas as pl
from jax.experimental.pallas import tpu as pltpu

F32 = jnp.float32
BF16 = jnp.bfloat16

EPS = 1e-6
NEG_INF = -1e30
ROWS = 8

A_HEADS, A_KV, A_GROUP, A_HD = 8, 2, 4, 64
WINDOW = 128
B_HEADS, B_KD, B_VD = 4, 128, 128
S5_GROUP, S5_STATE = 16, 64
S5_OCT = 8

QA_W, KA_W, VA_W = A_HEADS * A_HD, A_KV * A_HD, A_KV * A_HD
QB_W, KB_W, VB_W, GB_W = B_HEADS * B_KD, B_HEADS * B_KD, B_HEADS * B_VD, B_HEADS * B_VD
OFF_QA = 0
OFF_KA = OFF_QA + QA_W
OFF_VA = OFF_KA + KA_W
OFF_QB = OFF_VA + VA_W
OFF_KB = OFF_QB + QB_W
OFF_VB = OFF_KB + KB_W
OFF_GB = OFF_VB + VB_W

VMEM_LIMIT = 56 * 1024 * 1024


def _ret_log_gamma(h):
    return math.log1p(-(2.0 ** (-5.0 - h)))


def _alibi_slope(h):
    return 2.0 ** (-8.0 * (h + 1) / A_HEADS)


def _const_spec(shape):
    nd = len(shape)
    return pl.BlockSpec(shape, lambda i, _n=nd: (0,) * _n, pipeline_mode=pl.Buffered(1))


def _params():
    return pltpu.CompilerParams(dimension_semantics=("arbitrary",), vmem_limit_bytes=VMEM_LIMIT)


def _dot(a, b):
    return jnp.dot(a, b, preferred_element_type=F32)


def _dot_nt(a, b):
    return lax.dot_general(a, b, (((1,), (1,)), ((), ())), preferred_element_type=F32)


def _dot_tn(a, b):
    return lax.dot_general(a, b, (((0,), (0,)), ((), ())), preferred_element_type=F32)


def _bmm(a, b):
    return lax.dot_general(a, b, (((2,), (1,)), ((0,), (0,))), preferred_element_type=F32)


def _bmm_nt(a, b):
    return lax.dot_general(a, b, (((2,), (2,)), ((0,), (0,))), preferred_element_type=F32)


def _bmm_tn(a, b):
    return lax.dot_general(a, b, (((1,), (1,)), ((0,), (0,))), preferred_element_type=F32)


def _rms(x, g):
    return x * lax.rsqrt(jnp.mean(x * x, axis=-1, keepdims=True) + EPS) * g


def _modulate(x3, mod, gain, which):
    d = x3.shape[-1]
    sh = mod[:, :, (3 * which) * d:(3 * which + 1) * d]
    sc = mod[:, :, (3 * which + 1) * d:(3 * which + 2) * d]
    return _rms(x3, gain) * (1.0 + sc) + sh


def _gate(mod, which, d):
    return mod[:, :, (3 * which + 2) * d:(3 * which + 3) * d]


def _adaln_body(c_ref, w_ref, b_ref, o_ref):
    c = c_ref[...]
    a = (c * jax.nn.sigmoid(c)).astype(BF16)
    o_ref[0] = _dot(a, w_ref[0].astype(BF16)) + b_ref[0]


def _adaln(c_all, ada_w, ada_b):
    depth, d, n = ada_w.shape
    r = c_all.shape[0]
    tn = 1536
    return pl.pallas_call(
        _adaln_body,
        grid=(depth, n // tn),
        in_specs=[pl.BlockSpec((r, d), lambda l, j: (0, 0)),
                  pl.BlockSpec((1, d, tn), lambda l, j: (l, 0, j)),
                  pl.BlockSpec((1, 1, tn), lambda l, j: (l, 0, j))],
        out_specs=pl.BlockSpec((1, r, tn), lambda l, j: (l, 0, j)),
        out_shape=jax.ShapeDtypeStruct((depth, r, n), F32),
        compiler_params=pltpu.CompilerParams(dimension_semantics=("arbitrary", "arbitrary"),
                                             vmem_limit_bytes=VMEM_LIMIT),
        name="adaln",
    )(c_all, ada_w, ada_b.reshape(depth, 1, n))


def _ffn_body(x_ref, mod_ref, gn_ref, wg_ref, wu_ref, wd_ref, o_ref):
    x = x_ref[...]
    sb, _, d = x.shape
    mod = mod_ref[...]
    h = _modulate(x, mod, gn_ref[...], 1).reshape(sb * ROWS, d).astype(BF16)
    a = _dot(h, wg_ref[...])
    b = _dot(h, wu_ref[...])
    act = (a * jax.nn.sigmoid(a) * b).astype(BF16)
    y = _dot(act, wd_ref[...])
    o_ref[...] = x + _gate(mod, 1, d) * y.reshape(sb, ROWS, d)


def _ffn(x3, mod3, gain, wg, wu, wd, sb):
    n8, _, d = x3.shape
    f = wg.shape[1]
    per_seq = mod3.shape[0] != 1
    mod_spec = (pl.BlockSpec((sb, 1, 6 * d), lambda i: (i, 0, 0)) if per_seq
                else pl.BlockSpec((1, 1, 6 * d), lambda i: (0, 0, 0)))
    return pl.pallas_call(
        _ffn_body,
        grid=(n8 // sb,),
        in_specs=[pl.BlockSpec((sb, ROWS, d), lambda i: (i, 0, 0)), mod_spec,
                  _const_spec((1, d)), _const_spec((d, f)), _const_spec((d, f)), _const_spec((f, d))],
        out_specs=pl.BlockSpec((sb, ROWS, d), lambda i: (i, 0, 0)),
        out_shape=jax.ShapeDtypeStruct(x3.shape, F32),
        compiler_params=_params(),
        name="ffn",
    )(x3, mod3, gain.reshape(1, d), wg, wu, wd)


def _head_rms(t, g):
    return t * lax.rsqrt(jnp.mean(t * t, axis=-1, keepdims=True) + EPS) * g


def _group_norm_gate(o, gain, gate):
    mu = jnp.mean(o, axis=-1, keepdims=True)
    var = jnp.mean(jnp.square(o - mu), axis=-1, keepdims=True)
    return (o - mu) * lax.rsqrt(var + EPS) * gain * (gate * jax.nn.sigmoid(gate))


def _ret_decay(hb, c):
    lg = _ret_log_gamma(hb)
    ii = lax.broadcasted_iota(jnp.int32, (c, c), 0)
    jj = lax.broadcasted_iota(jnp.int32, (c, c), 1)
    diff = (ii - jj).astype(F32)
    d_in = jnp.where(diff >= 0, jnp.exp(lg * jnp.maximum(diff, 0.0)), 0.0)
    row = lax.broadcasted_iota(jnp.int32, (c, B_KD), 0).astype(F32)
    d_q = jnp.exp(lg * (row + 1.0))
    d_k = jnp.exp(lg * (c - 1.0 - row))
    d_c = math.exp(lg * c)
    return d_in, d_q, d_k, d_c


P_OFF_KA = QA_W
P_OFF_VA = P_OFF_KA + 2 * KA_W
P_OFF_QB = P_OFF_VA + 2 * VA_W
P_OFF_VB = P_OFF_QB + QB_W
P_OFF_GB = P_OFF_VB + VB_W
P_WIDTH = P_OFF_GB + GB_W
PAIR = 2 * A_HD


def _dedup(t):
    low = lax.broadcasted_iota(jnp.int32, (t.shape[0], PAIR), 1) < A_HD
    return jnp.where(low, t[:, 0:PAIR], t[:, PAIR:2 * PAIR])


def _mix_even_prompt_body(x_ref, mod_ref, gn_ref, win_ref, wkt_ref, qg_ref, kg_ref, sink_ref, rg_ref, wout_ref,
                          o_ref, pk_ref, pv_ref, ps_ref, mix_ref, carry_ref, bias_ref, dec_ref, ones_ref):
    step = pl.program_id(0)
    blk = WINDOW
    rows4 = A_GROUP * blk

    @pl.when(step == 0)
    def _():
        carry_ref[...] = jnp.zeros_like(carry_ref)
        ps_ref[...] = jnp.zeros_like(ps_ref)
        er = lax.broadcasted_iota(jnp.int32, ones_ref.shape, 0) // A_HD
        ec = lax.broadcasted_iota(jnp.int32, ones_ref.shape, 1) // A_HD
        ones_ref[...] = jnp.where(er == ec, 1.0 / A_HD, 0.0).astype(BF16)
        row = lax.broadcasted_iota(jnp.int32, (rows4, 2 * blk), 0)
        dist = row % blk + blk - lax.broadcasted_iota(jnp.int32, (rows4, 2 * blk), 1)
        in_window = (dist >= 0) & (dist < WINDOW)
        for kv in range(A_KV):
            slope = jnp.zeros((rows4, 2 * blk), F32)
            for g in range(A_GROUP):
                slope = jnp.where(row // blk == g, _alibi_slope(kv * A_GROUP + g), slope)
            bias_ref[kv] = jnp.where(in_window, slope * dist.astype(F32), -NEG_INF)
        for hb in range(B_HEADS):
            lg = _ret_log_gamma(hb)
            ii = lax.broadcasted_iota(jnp.int32, (blk, blk), 0).astype(F32)
            jj = lax.broadcasted_iota(jnp.int32, (blk, blk), 1).astype(F32)
            diff = ii - jj
            dec_ref[hb, 0] = jnp.where(diff >= 0, jnp.exp(lg * jnp.maximum(diff, 0.0)), 0.0)
            dec_ref[hb, 1] = jnp.exp(lg * (ii + 1.0))
            dec_ref[hb, 2] = jnp.exp(lg * (blk - 1.0 - jj))

    x = x_ref[...]
    sb, _, d = x.shape
    tb = sb * ROWS
    mod = mod_ref[...]
    h = _modulate(x, mod, gn_ref[...], 0).reshape(tb, d).astype(BF16)
    proj = _dot(h, win_ref[...])
    kt_all = _dot_nt(wkt_ref[...], h) * (B_KD ** -0.5)
    rg = rg_ref[...]

    qa = proj[:, 0:QA_W]
    ka = proj[:, P_OFF_KA:P_OFF_KA + 2 * KA_W]
    va = proj[:, P_OFF_VA:P_OFF_VA + 2 * VA_W]
    q_hat = (qa * lax.rsqrt(_dot((qa * qa).astype(BF16), ones_ref[...]) + EPS) * qg_ref[...]).astype(BF16)
    k_hat = ka * lax.rsqrt(_dot((ka * ka).astype(BF16), ones_ref[0:2 * KA_W, 0:2 * KA_W]) + EPS) * kg_ref[...]
    k_hat16 = k_hat.astype(BF16)
    va16 = va.astype(BF16)
    prev = carry_ref[step % 2]
    carry_ref[(step + 1) % 2] = jnp.concatenate([k_hat16[tb - blk:tb], va16[tb - blk:tb]], axis=1)

    mean_mat = jnp.full((B_VD, B_VD), 1.0 / B_VD, BF16)
    row_g = lax.broadcasted_iota(jnp.int32, (rows4, 1), 0) // blk
    key_is_prev = lax.broadcasted_iota(jnp.int32, (rows4, 2 * blk), 1) < blk
    first_penalty = jnp.where(step == 0, -NEG_INF, 0.0)
    lane_low = lax.broadcasted_iota(jnp.int32, (blk, PAIR), 1) < A_HD
    ones_cols = jnp.ones((2 * blk, PAIR), BF16)

    n_blk = tb // blk
    att = [(j, kv) for j in range(n_blk) for kv in range(A_KV)]
    ret = [(j, hb) for j in range(n_blk) for hb in range(B_HEADS)]

    sinks, scores = {}, {}
    for j, kv in att:
        r0 = j * blk
        kcol = slice(kv * PAIR, (kv + 1) * PAIR)
        if j == 0:
            k2 = jnp.concatenate([prev[:, kcol], k_hat16[0:blk, kcol]], axis=0)
        else:
            k2 = k_hat16[r0 - blk:r0 + blk, kcol]
        q4 = jnp.concatenate(
            [jnp.where(lane_low == (g % 2 == 0),
                       q_hat[r0:r0 + blk, (kv * A_GROUP + g - g % 2) * A_HD:(kv * A_GROUP + g - g % 2 + 2) * A_HD],
                       jnp.zeros((), BF16))
             for g in range(A_GROUP)], axis=0)
        scores[j, kv] = _dot_nt(q4, k2)
        sink = jnp.zeros((rows4, 1), F32)
        for g in range(A_GROUP):
            sink = jnp.where(row_g == g, sink_ref[kv * A_GROUP + g], sink)
        sinks[j, kv] = sink
    qb, vb, kt, inner = {}, {}, {}, {}
    for j, hb in ret:
        r0 = j * blk
        qb[j, hb] = proj[r0:r0 + blk, P_OFF_QB + hb * B_KD:P_OFF_QB + (hb + 1) * B_KD].astype(BF16)
        vb[j, hb] = proj[r0:r0 + blk, P_OFF_VB + hb * B_VD:P_OFF_VB + (hb + 1) * B_VD].astype(BF16)
        kt[j, hb] = kt_all[hb * B_KD:(hb + 1) * B_KD, r0:r0 + blk]
        inner[j, hb] = _dot(qb[j, hb], kt[j, hb].astype(BF16))

    probs, maxes = {}, {}
    for j, kv in att:
        s = scores[j, kv] * (A_HD ** -0.5) - bias_ref[kv]
        if j == 0:
            s = s - jnp.where(key_is_prev, first_penalty, 0.0)
        mx = jnp.maximum(jnp.max(s, axis=-1, keepdims=True), sinks[j, kv])
        probs[j, kv] = jnp.exp(s - mx).astype(BF16)
        maxes[j, kv] = mx
    state = {}
    for hb in range(B_HEADS):
        d_c = math.exp(_ret_log_gamma(hb) * blk)
        state[0, hb] = ps_ref[hb]
        for j in range(n_blk):
            state[j + 1, hb] = state[j, hb] * d_c + _dot((kt[j, hb] * dec_ref[hb, 2]).astype(BF16), vb[j, hb])
        ps_ref[hb] = state[n_blk, hb]

    for j, kv in att:
        r0 = j * blk
        kcol = slice(kv * PAIR, (kv + 1) * PAIR)
        vcol = slice(2 * KA_W + kv * PAIR, 2 * KA_W + (kv + 1) * PAIR)
        if j == 0:
            v2 = jnp.concatenate([prev[:, vcol], va16[0:blk, kcol]], axis=0)
        else:
            v2 = va16[r0 - blk:r0 + blk, kcol]
        pv = _dot(probs[j, kv], jnp.concatenate([v2, ones_cols], axis=1))
        o4 = pv[:, 0:PAIR] / (pv[:, PAIR:2 * PAIR] + jnp.exp(sinks[j, kv] - maxes[j, kv]))
        for g in range(A_GROUP):
            hd = kv * A_GROUP + g
            half = slice((hd % 2) * A_HD, (hd % 2 + 1) * A_HD)
            mix_ref[r0:r0 + blk, hd * A_HD:(hd + 1) * A_HD] = o4[g * blk:(g + 1) * blk, half]
    o_ret = {}
    for j, hb in ret:
        o_ret[j, hb] = (_dot((inner[j, hb] * dec_ref[hb, 0]).astype(BF16), vb[j, hb])
                        + _dot(qb[j, hb], state[j, hb].astype(BF16)) * dec_ref[hb, 1])

    cen = {k: o_ret[k] - _dot(o_ret[k].astype(BF16), mean_mat) for k in ret}
    var = {k: _dot((cen[k] * cen[k]).astype(BF16), mean_mat) for k in ret}
    for j, hb in ret:
        r0 = j * blk
        gb = proj[r0:r0 + blk, P_OFF_GB + hb * B_VD:P_OFF_GB + (hb + 1) * B_VD]
        mix_ref[r0:r0 + blk, QA_W + hb * B_VD:QA_W + (hb + 1) * B_VD] = (
            cen[j, hb] * lax.rsqrt(var[j, hb] + EPS) * rg[:, hb * B_VD:(hb + 1) * B_VD] * (gb * jax.nn.sigmoid(gb)))

    out = _dot(mix_ref[...].astype(BF16), wout_ref[...])
    o_ref[...] = x + _gate(mod, 0, d) * out.reshape(sb, ROWS, d)

    @pl.when(step == pl.num_programs(0) - 1)
    def _():
        pk_ref[...] = _dedup(k_hat[tb - blk:tb, :])
        pv_ref[...] = _dedup(va[tb - blk:tb, :])


def _dup_heads(t, width):
    lead = t.shape[:-1]
    t = t.reshape(lead + (-1, 1, width))
    return jnp.broadcast_to(t, lead + (t.shape[-3], 2, width)).reshape(lead + (-1,))


def _mix_even_prompt(x3, mod3, gain, w_in, q_gain, k_gain, sinks, ret_gain, w_out, sb):
    n8, _, d = x3.shape
    tb = sb * ROWS
    w_main = jnp.concatenate([w_in[:, :OFF_KA], _dup_heads(w_in[:, OFF_KA:OFF_VA], A_HD),
                              _dup_heads(w_in[:, OFF_VA:OFF_QB], A_HD), w_in[:, OFF_QB:OFF_KB],
                              w_in[:, OFF_VB:]], axis=1)
    wk_t = w_in[:, OFF_KB:OFF_VB].T
    return pl.pallas_call(
        _mix_even_prompt_body,
        grid=(n8 // sb,),
        in_specs=[pl.BlockSpec((sb, ROWS, d), lambda i: (i, 0, 0)),
                  pl.BlockSpec((1, 1, 6 * d), lambda i: (0, 0, 0)),
                  _const_spec((1, d)), _const_spec((d, P_WIDTH)), _const_spec((KB_W, d)),
                  _const_spec((1, QA_W)), _const_spec((1, 2 * KA_W)),
                  pl.BlockSpec(memory_space=pltpu.SMEM),
                  _const_spec((1, VB_W)), _const_spec((QA_W + VB_W, d))],
        out_specs=[pl.BlockSpec((sb, ROWS, d), lambda i: (i, 0, 0)),
                   pl.BlockSpec((WINDOW, KA_W), lambda i: (0, 0)),
                   pl.BlockSpec((WINDOW, VA_W), lambda i: (0, 0)),
                   pl.BlockSpec((B_HEADS, B_KD, B_VD), lambda i: (0, 0, 0))],
        out_shape=[jax.ShapeDtypeStruct(x3.shape, F32),
                   jax.ShapeDtypeStruct((WINDOW, KA_W), F32),
                   jax.ShapeDtypeStruct((WINDOW, VA_W), F32),
                   jax.ShapeDtypeStruct((B_HEADS, B_KD, B_VD), F32)],
        scratch_shapes=[pltpu.VMEM((tb, QA_W + VB_W), F32),
                        pltpu.VMEM((2, WINDOW, 2 * KA_W + 2 * VA_W), BF16),
                        pltpu.VMEM((A_KV, A_GROUP * WINDOW, 2 * WINDOW), F32),
                        pltpu.VMEM((B_HEADS, 3, WINDOW, WINDOW), F32),
                        pltpu.VMEM((QA_W, QA_W), BF16)],
        compiler_params=_params(),
        name="mix_even_prompt",
    )(x3, mod3, gain.reshape(1, d), w_main, wk_t, jnp.tile(q_gain, A_HEADS).reshape(1, QA_W),
      jnp.tile(k_gain, 2 * A_KV).reshape(1, 2 * KA_W),
      sinks, ret_gain.reshape(1, VB_W), w_out)


def _mix_even_sample_body(x_ref, mod_ref, gn_ref, win_ref, qg_ref, kg_ref, sink_ref, rg_ref, wout_ref,
                          ck_ref, cv_ref, s0_ref,
                          o_ref, nk_ref, nv_ref, ns_ref, mix_ref):
    x = x_ref[...]
    sb, length, d = x.shape
    tb = sb * length
    w = ck_ref.shape[1]
    mod = mod_ref[...]
    h = _modulate(x, mod, gn_ref[...], 0).reshape(tb, d).astype(BF16)
    proj = _dot(h, win_ref[...])
    qg = qg_ref[...]
    kg = kg_ref[...]
    rg = rg_ref[...]

    rows = A_GROUP * length
    qpos_c = lax.broadcasted_iota(jnp.int32, (rows, w), 0) % length
    kpos_c = lax.broadcasted_iota(jnp.int32, (rows, w), 1)
    dist_c = w + qpos_c - kpos_c
    valid_c = (dist_c >= 0) & (dist_c < WINDOW)
    qpos_n = lax.broadcasted_iota(jnp.int32, (rows, length), 0) % length
    kpos_n = lax.broadcasted_iota(jnp.int32, (rows, length), 1)
    dist_n = qpos_n - kpos_n
    valid_n = (dist_n >= 0) & (dist_n < WINDOW)
    row_g = lax.broadcasted_iota(jnp.int32, (rows, 1), 0) // length

    for kv in range(A_KV):
        lanes = slice(kv * A_HD, (kv + 1) * A_HD)
        kn = _head_rms(proj[:, OFF_KA + kv * A_HD:OFF_KA + (kv + 1) * A_HD], kg).reshape(sb, length, A_HD)
        vn = proj[:, OFF_VA + kv * A_HD:OFF_VA + (kv + 1) * A_HD].reshape(sb, length, A_HD)
        nk_ref[:, 0:w - length, lanes] = ck_ref[:, length:w, lanes]
        nv_ref[:, 0:w - length, lanes] = cv_ref[:, length:w, lanes]
        nk_ref[:, w - length:w, lanes] = kn
        nv_ref[:, w - length:w, lanes] = vn
        kc = ck_ref[:, :, lanes].astype(BF16)
        vc = cv_ref[:, :, lanes].astype(BF16)
        q4 = jnp.concatenate(
            [_head_rms(proj[:, OFF_QA + (kv * A_GROUP + g) * A_HD:OFF_QA + (kv * A_GROUP + g + 1) * A_HD], qg)
             .reshape(sb, length, A_HD) for g in range(A_GROUP)], axis=1).astype(BF16)
        slope = jnp.zeros((rows, 1), F32)
        sink = jnp.zeros((rows, 1), F32)
        for g in range(A_GROUP):
            hd = kv * A_GROUP + g
            slope = jnp.where(row_g == g, _alibi_slope(hd), slope)
            sink = jnp.where(row_g == g, sink_ref[hd], sink)
        scale = A_HD ** -0.5
        s_c = _bmm_nt(q4, kc) * scale - slope * dist_c.astype(F32)
        s_c = jnp.where(valid_c, s_c, NEG_INF)
        s_n = _bmm_nt(q4, kn.astype(BF16)) * scale - slope * dist_n.astype(F32)
        s_n = jnp.where(valid_n, s_n, NEG_INF)
        mx = jnp.maximum(jnp.maximum(jnp.max(s_c, axis=-1, keepdims=True),
                                     jnp.max(s_n, axis=-1, keepdims=True)), sink)
        p_c = jnp.exp(s_c - mx)
        p_n = jnp.exp(s_n - mx)
        den = (jnp.sum(p_c, axis=-1, keepdims=True) + jnp.sum(p_n, axis=-1, keepdims=True)
               + jnp.exp(sink - mx))
        o4 = (_bmm(p_c.astype(BF16), vc) + _bmm(p_n.astype(BF16), vn.astype(BF16))) / den
        for g in range(A_GROUP):
            hd = kv * A_GROUP + g
            mix_ref[:, hd * A_HD:(hd + 1) * A_HD] = o4[:, g * length:(g + 1) * length, :].reshape(tb, A_HD)

    for hb in range(B_HEADS):
        d_in, d_q, d_k, d_c = _ret_decay(hb, length)
        qb = proj[:, OFF_QB + hb * B_KD:OFF_QB + (hb + 1) * B_KD].reshape(sb, length, B_KD).astype(BF16)
        kb = proj[:, OFF_KB + hb * B_KD:OFF_KB + (hb + 1) * B_KD].reshape(sb, length, B_KD) * (B_KD ** -0.5)
        vb = proj[:, OFF_VB + hb * B_VD:OFF_VB + (hb + 1) * B_VD].reshape(sb, length, B_VD).astype(BF16)
        gb = proj[:, OFF_GB + hb * B_VD:OFF_GB + (hb + 1) * B_VD]
        state = s0_ref[:, hb]
        inner = _bmm_nt(qb, kb.astype(BF16)) * d_in
        o = _bmm(inner.astype(BF16), vb) + _bmm(qb, state.astype(BF16)) * d_q
        ns_ref[:, hb] = state * d_c + _bmm_tn((kb * d_k).astype(BF16), vb)
        mix_ref[:, QA_W + hb * B_VD:QA_W + (hb + 1) * B_VD] = _group_norm_gate(
            o.reshape(tb, B_VD), rg[:, hb * B_VD:(hb + 1) * B_VD], gb)

    out = _dot(mix_ref[...].astype(BF16), wout_ref[...])
    o_ref[...] = x + _gate(mod, 0, d) * out.reshape(sb, length, d)


def _mix_even_sample(x3, mod3, gain, w_in, q_gain, k_gain, sinks, ret_gain, w_out, cache_k, cache_v, state, sb):
    n, length, d = x3.shape
    in_w = w_in.shape[1]
    w = cache_k.shape[1]
    seq_spec = lambda shape: pl.BlockSpec((sb,) + shape, lambda i, _n=len(shape): (i,) + (0,) * _n)
    return pl.pallas_call(
        _mix_even_sample_body,
        grid=(n // sb,),
        in_specs=[seq_spec((length, d)), seq_spec((1, 6 * d)),
                  _const_spec((1, d)), _const_spec((d, in_w)),
                  _const_spec((1, A_HD)), _const_spec((1, A_HD)),
                  pl.BlockSpec(memory_space=pltpu.SMEM),
                  _const_spec((1, VB_W)), _const_spec((QA_W + VB_W, d)),
                  seq_spec((w, KA_W)), seq_spec((w, VA_W)), seq_spec((B_HEADS, B_KD, B_VD))],
        out_specs=[seq_spec((length, d)), seq_spec((w, KA_W)), seq_spec((w, VA_W)),
                   seq_spec((B_HEADS, B_KD, B_VD))],
        out_shape=[jax.ShapeDtypeStruct(x3.shape, F32),
                   jax.ShapeDtypeStruct(cache_k.shape, F32),
                   jax.ShapeDtypeStruct(cache_v.shape, F32),
                   jax.ShapeDtypeStruct(state.shape, F32)],
        scratch_shapes=[pltpu.VMEM((sb * length, QA_W + VB_W), F32)],
        compiler_params=_params(),
        name="mix_even_sample",
    )(x3, mod3, gain.reshape(1, d), w_in, q_gain.reshape(1, A_HD), k_gain.reshape(1, A_HD),
      sinks, ret_gain.reshape(1, VB_W), w_out, cache_k, cache_v, state)


def _s5_prep_body(are_ref, aim_ref, ldt_ref, btr_ref, bti_ref, lre_ref, lim_ref, bbr_ref, bbi_ref):
    a_re = are_ref[...]
    a_im = aim_ref[...]
    dt = jnp.exp(ldt_ref[...])
    mag = jnp.exp(a_re * dt)
    lam_re = mag * jnp.cos(a_im * dt)
    lam_im = mag * jnp.sin(a_im * dt)
    den = a_re * a_re + a_im * a_im
    n_re = lam_re - 1.0
    n_im = lam_im
    f_re = (n_re * a_re + n_im * a_im) / den
    f_im = (n_im * a_re - n_re * a_im) / den
    br = btr_ref[...]
    bi = bti_ref[...]
    lre_ref[...] = lam_re
    lim_ref[...] = lam_im
    bbr_ref[...] = f_re * br - f_im * bi
    bbi_ref[...] = f_re * bi + f_im * br


def _s5_prep(a_re, a_im, log_dt, b_re, b_im):
    g, p = a_re.shape
    k = b_re.shape[-1]
    bt_re = jnp.swapaxes(b_re, 1, 2)
    bt_im = jnp.swapaxes(b_im, 1, 2)
    out = pl.pallas_call(
        _s5_prep_body,
        out_shape=[jax.ShapeDtypeStruct((g, 1, p), F32), jax.ShapeDtypeStruct((g, 1, p), F32),
                   jax.ShapeDtypeStruct((g, k, p), F32), jax.ShapeDtypeStruct((g, k, p), F32)],
        name="s5_prep",
    )(a_re.reshape(g, 1, p), a_im.reshape(g, 1, p), log_dt.reshape(g, 1, 1), bt_re, bt_im)
    return out


def _block_diag(t):
    g, a, b = t.shape
    t = t.reshape(g // S5_OCT, S5_OCT, a, b)
    eye = jnp.eye(S5_OCT, dtype=t.dtype)
    out = t[:, :, :, None, :] * eye[None, :, None, :, None]
    return out.reshape(g // S5_OCT, S5_OCT * a, S5_OCT * b)


def _gelu_glu_out(x, mod, y, u, dskip, glua_ref, glub_ref):
    sb, rows, d = x.shape
    y = y + dskip * u
    yg = jax.nn.gelu(y, approximate=True).astype(BF16)
    out = _dot(yg, glua_ref[...]) * jax.nn.sigmoid(_dot(yg, glub_ref[...]))
    return x + _gate(mod, 0, d) * out.reshape(sb, rows, d)


def _mix_odd_prompt_body(x_ref, mod_ref, gn_ref, bblk_ref, lam_ref, cre_ref, cim_ref, dskip_ref,
                         glua_ref, glub_ref, o_ref, hre_ref, him_ref, d_ref, y_ref):
    step = pl.program_id(0)
    x = x_ref[...]
    sb, _, d = x.shape
    tm = sb * ROWS
    n_oct = bblk_ref.shape[0]
    half = bblk_ref.shape[2] // 2
    n_ch = bblk_ref.shape[2] // 128
    hc = n_ch // 2

    @pl.when(step == 0)
    def _():
        hre_ref[...] = jnp.zeros_like(hre_ref)
        him_ref[...] = jnp.zeros_like(him_ref)

    mod = mod_ref[...]
    u = _modulate(x, mod, gn_ref[...], 0).reshape(tm, d)
    u16 = u.astype(BF16)
    for s in range(n_oct):
        bu = _dot(u16[:, s * 128:(s + 1) * 128], bblk_ref[s])
        for c in range(n_ch):
            d_ref[c, pl.ds(s, tm, stride=n_oct), :] = bu[:, c * 128:(c + 1) * 128]

    lam_re = lam_ref[0]
    lam_im = lam_ref[1]

    def scan_step(t, carry):
        h_re, h_im = carry
        r = pl.multiple_of(t * n_oct, n_oct)
        b_re = jnp.concatenate([d_ref[c, pl.ds(r, n_oct), :] for c in range(hc)], axis=1)
        b_im = jnp.concatenate([d_ref[hc + c, pl.ds(r, n_oct), :] for c in range(hc)], axis=1)
        n_re = lam_re * h_re - lam_im * h_im + b_re
        n_im = lam_re * h_im + lam_im * h_re + b_im
        for c in range(hc):
            d_ref[c, pl.ds(r, n_oct), :] = n_re[:, c * 128:(c + 1) * 128]
            d_ref[hc + c, pl.ds(r, n_oct), :] = n_im[:, c * 128:(c + 1) * 128]
        return n_re, n_im

    h_re, h_im = lax.fori_loop(0, tm, scan_step, (hre_ref[...], him_ref[...]), unroll=8)
    hre_ref[...] = h_re
    him_ref[...] = h_im

    for s in range(n_oct):
        hs_re = jnp.concatenate([d_ref[c, pl.ds(s, tm, stride=n_oct), :] for c in range(hc)], axis=1)
        hs_im = jnp.concatenate([d_ref[hc + c, pl.ds(s, tm, stride=n_oct), :] for c in range(hc)], axis=1)
        y_ref[:, s * 128:(s + 1) * 128] = (_dot(hs_re.astype(BF16), cre_ref[s])
                                           - _dot(hs_im.astype(BF16), cim_ref[s]))
    o_ref[...] = _gelu_glu_out(x, mod, y_ref[...], u, dskip_ref[...], glua_ref, glub_ref)


def _mix_odd_prompt(x3, mod3, gain, bblk, lam_d, cre, cim, dskip, glu_a, glu_b, sb):
    n8, _, d = x3.shape
    n_oct, kin, wid = bblk.shape
    half = wid // 2
    tm = sb * ROWS
    return pl.pallas_call(
        _mix_odd_prompt_body,
        grid=(n8 // sb,),
        in_specs=[pl.BlockSpec((sb, ROWS, d), lambda i: (i, 0, 0)),
                  pl.BlockSpec((1, 1, 6 * d), lambda i: (0, 0, 0)),
                  _const_spec((1, d)), _const_spec(bblk.shape), _const_spec(lam_d.shape),
                  _const_spec(cre.shape), _const_spec(cim.shape), _const_spec((1, d)),
                  _const_spec((d, d)), _const_spec((d, d))],
        out_specs=[pl.BlockSpec((sb, ROWS, d), lambda i: (i, 0, 0)),
                   pl.BlockSpec((n_oct, half), lambda i: (0, 0)),
                   pl.BlockSpec((n_oct, half), lambda i: (0, 0))],
        out_shape=[jax.ShapeDtypeStruct(x3.shape, F32),
                   jax.ShapeDtypeStruct((n_oct, half), F32),
                   jax.ShapeDtypeStruct((n_oct, half), F32)],
        scratch_shapes=[pltpu.VMEM((wid // 128, tm * n_oct, 128), F32), pltpu.VMEM((tm, d), F32)],
        compiler_params=_params(),
        name="mix_odd_prompt",
    )(x3, mod3, gain.reshape(1, d), bblk, lam_d, cre, cim, dskip.reshape(1, d), glu_a, glu_b)


def _mix_odd_sample_body(x_ref, mod_ref, gn_ref, bblk_ref, lam_ref, cre_ref, cim_ref, dskip_ref,
                         glua_ref, glub_ref, sre_ref, sim_ref, o_ref, nre_ref, nim_ref, d_ref, y_ref):
    x = x_ref[...]
    sb, length, d = x.shape
    tm = sb * length
    n_oct = bblk_ref.shape[0]
    half = bblk_ref.shape[2] // 2
    mod = mod_ref[...]
    u = _modulate(x, mod, gn_ref[...], 0).reshape(tm, d)
    u16 = u.astype(BF16)
    n_ch = bblk_ref.shape[2] // 128
    hc = n_ch // 2
    for s in range(n_oct):
        bu = _dot(u16[:, s * 128:(s + 1) * 128], bblk_ref[s])
        for c in range(n_ch):
            d_ref[c] = bu[:, c * 128:(c + 1) * 128]
        lam_re = lam_ref[0, s:s + 1, :]
        lam_im = lam_ref[1, s:s + 1, :]
        h_re = sre_ref[:, s * half:(s + 1) * half]
        h_im = sim_ref[:, s * half:(s + 1) * half]
        for t in range(length):
            b_re = jnp.concatenate([d_ref[c, pl.ds(t, sb, stride=length), :] for c in range(hc)], axis=1)
            b_im = jnp.concatenate([d_ref[hc + c, pl.ds(t, sb, stride=length), :] for c in range(hc)], axis=1)
            n_re = lam_re * h_re - lam_im * h_im + b_re
            n_im = lam_re * h_im + lam_im * h_re + b_im
            for c in range(hc):
                d_ref[c, pl.ds(t, sb, stride=length), :] = n_re[:, c * 128:(c + 1) * 128]
                d_ref[hc + c, pl.ds(t, sb, stride=length), :] = n_im[:, c * 128:(c + 1) * 128]
            h_re, h_im = n_re, n_im
        nre_ref[:, s * half:(s + 1) * half] = h_re
        nim_ref[:, s * half:(s + 1) * half] = h_im
        hs_re = jnp.concatenate([d_ref[c] for c in range(hc)], axis=1)
        hs_im = jnp.concatenate([d_ref[hc + c] for c in range(hc)], axis=1)
        y_ref[:, s * 128:(s + 1) * 128] = (_dot(hs_re.astype(BF16), cre_ref[s])
                                           - _dot(hs_im.astype(BF16), cim_ref[s]))
    o_ref[...] = _gelu_glu_out(x, mod, y_ref[...], u, dskip_ref[...], glua_ref, glub_ref)


def _mix_odd_sample(x3, mod3, gain, bblk, lam_d, cre, cim, dskip, glu_a, glu_b, s_re, s_im, sb):
    n, length, d = x3.shape
    n_oct, kin, wid = bblk.shape
    tm = sb * length
    nstate = s_re.shape[1]
    return pl.pallas_call(
        _mix_odd_sample_body,
        grid=(n // sb,),
        in_specs=[pl.BlockSpec((sb, length, d), lambda i: (i, 0, 0)),
                  pl.BlockSpec((sb, 1, 6 * d), lambda i: (i, 0, 0)),
                  _const_spec((1, d)), _const_spec(bblk.shape), _const_spec(lam_d.shape),
                  _const_spec(cre.shape), _const_spec(cim.shape), _const_spec((1, d)),
                  _const_spec((d, d)), _const_spec((d, d)),
                  pl.BlockSpec((sb, nstate), lambda i: (i, 0)),
                  pl.BlockSpec((sb, nstate), lambda i: (i, 0))],
        out_specs=[pl.BlockSpec((sb, length, d), lambda i: (i, 0, 0)),
                   pl.BlockSpec((sb, nstate), lambda i: (i, 0)),
                   pl.BlockSpec((sb, nstate), lambda i: (i, 0))],
        out_shape=[jax.ShapeDtypeStruct(x3.shape, F32),
                   jax.ShapeDtypeStruct(s_re.shape, F32),
                   jax.ShapeDtypeStruct(s_im.shape, F32)],
        scratch_shapes=[pltpu.VMEM((wid // 128, tm, 128), F32), pltpu.VMEM((tm, d), F32)],
        compiler_params=_params(),
        name="mix_odd_sample",
    )(x3, mod3, gain.reshape(1, d), bblk, lam_d, cre, cim, dskip.reshape(1, d), glu_a, glu_b, s_re, s_im)


def _pick(n, want):
    while n % want:
        want //= 2
    return max(want, 1)


def kernel(x_prompt, x_sample, cache_win_k, cache_win_v, state_ret, state_s5_re, state_s5_im, c_prompt, c_sample, ada_w, ada_b, norm_mix, norm_ffn, ffn_wg, ffn_wu, ffn_wd, even_w_in, even_q_gain, even_k_gain, even_sinks, even_ret_gain, even_w_out, odd_A_re, odd_A_im, odd_log_dt, odd_B_re, odd_B_im, odd_C_re, odd_C_im, odd_D, odd_glu_a, odd_glu_b):
    bp, lp, d = x_prompt.shape
    ns, ls, _ = x_sample.shape
    assert bp == 1 and ls == ROWS and lp % WINDOW == 0
    w = cache_win_k.shape[2]
    groups, p_state = odd_A_re.shape[1:]

    n_c = bp + ns
    n_pad = -n_c % ROWS
    c_all = jnp.concatenate([c_prompt, c_sample, jnp.zeros((n_pad, d), F32)], axis=0)
    mod = _adaln(c_all, ada_w, ada_b)
    mod_p = [mod[l, 0:1].reshape(1, 1, 6 * d) for l in range(2)]
    mod_s = [mod[l, bp:bp + ns].reshape(ns, 1, 6 * d) for l in range(2)]

    bf = lambda t: t.astype(BF16)
    w_in, w_out = bf(even_w_in[0]), bf(even_w_out[0])
    wg, wu, wd = bf(ffn_wg), bf(ffn_wu), bf(ffn_wd)
    glu_a, glu_b = bf(odd_glu_a[0]), bf(odd_glu_b[0])

    xp = x_prompt.reshape(lp // ROWS, ROWS, d)
    xs = x_sample

    sb_p = _pick(lp // ROWS, 32)
    sb_s = _pick(ns, 32)
    sb_s_even = _pick(ns, 16)

    xp, p_k, p_v, p_ret = _mix_even_prompt(xp, mod_p[0], norm_mix[0], w_in, even_q_gain[0], even_k_gain[0],
                                           even_sinks[0], even_ret_gain[0], w_out, sb_p)
    xs, s_k, s_v, s_ret = _mix_even_sample(xs, mod_s[0], norm_mix[0], w_in, even_q_gain[0], even_k_gain[0],
                                           even_sinks[0], even_ret_gain[0], w_out,
                                           cache_win_k[0].reshape(ns, w, KA_W), cache_win_v[0].reshape(ns, w, VA_W),
                                           state_ret[0], sb_s_even)
    xp = _ffn(xp, mod_p[0], norm_ffn[0], wg[0], wu[0], wd[0], sb_p)
    xs = _ffn(xs, mod_s[0], norm_ffn[0], wg[0], wu[0], wd[0], sb_s)

    lam_re, lam_im, bbt_re, bbt_im = _s5_prep(odd_A_re[0], odd_A_im[0], odd_log_dt[0], odd_B_re[0], odd_B_im[0])
    n_oct = groups // S5_OCT
    half = S5_OCT * p_state
    bblk = bf(jnp.concatenate([_block_diag(bbt_re), _block_diag(bbt_im)], axis=-1))
    cre = bf(_block_diag(jnp.swapaxes(odd_C_re[0], 1, 2)))
    cim = bf(_block_diag(jnp.swapaxes(odd_C_im[0], 1, 2)))
    lam_d = jnp.stack([lam_re.reshape(n_oct, half), lam_im.reshape(n_oct, half)])

    xp, p_re, p_im = _mix_odd_prompt(xp, mod_p[1], norm_mix[1], bblk, lam_d, cre, cim, odd_D[0], glu_a, glu_b, sb_p)
    xs, s_re, s_im = _mix_odd_sample(xs, mod_s[1], norm_mix[1], bblk, lam_d, cre, cim, odd_D[0], glu_a, glu_b,
                                     state_s5_re[0].reshape(ns, groups * p_state),
                                     state_s5_im[0].reshape(ns, groups * p_state), sb_s)
    xp = _ffn(xp, mod_p[1], norm_ffn[1], wg[1], wu[1], wd[1], sb_p)
    xs = _ffn(xs, mod_s[1], norm_ffn[1], wg[1], wu[1], wd[1], sb_s)

    y_prompt = xp.reshape(bp, lp, d)
    y_sample = xs
    return (y_prompt, y_sample,
            p_k.reshape(1, bp, WINDOW, A_KV, A_HD), p_v.reshape(1, bp, WINDOW, A_KV, A_HD),
            p_ret.reshape(1, bp, B_HEADS, B_KD, B_VD),
            p_re.reshape(1, bp, groups, p_state), p_im.reshape(1, bp, groups, p_state),
            s_k.reshape(1, ns, w, A_KV, A_HD), s_v.reshape(1, ns, w, A_KV, A_HD),
            s_ret.reshape(1, ns, B_HEADS, B_KD, B_VD),
            s_re.reshape(1, ns, groups, p_state), s_im.reshape(1, ns, groups, p_state))
```

```python
import functools
import math

import jax
import jax.numpy as jnp
from jax import lax
from jax.experimental import pallas as pl
from jax.experimental.pallas import tpu as pltpu

F32 = jnp.float32
BF16 = jnp.bfloat16

EPS = 1e-6
NEG_INF = -1e30
ROWS = 8

A_HEADS, A_KV, A_GROUP, A_HD = 8, 2, 4, 64
WINDOW = 128
B_HEADS, B_KD, B_VD = 4, 128, 128
S5_GROUP, S5_STATE = 16, 64
S5_OCT = 8

QA_W, KA_W, VA_W = A_HEADS * A_HD, A_KV * A_HD, A_KV * A_HD
QB_W, KB_W, VB_W, GB_W = B_HEADS * B_KD, B_HEADS * B_KD, B_HEADS * B_VD, B_HEADS * B_VD
OFF_QA = 0
OFF_KA = OFF_QA + QA_W
OFF_VA = OFF_KA + KA_W
OFF_QB = OFF_VA + VA_W
OFF_KB = OFF_QB + QB_W
OFF_VB = OFF_KB + KB_W
OFF_GB = OFF_VB + VB_W

VMEM_LIMIT = 56 * 1024 * 1024


def _ret_log_gamma(h):
    return math.log1p(-(2.0 ** (-5.0 - h)))


def _alibi_slope(h):
    return 2.0 ** (-8.0 * (h + 1) / A_HEADS)


def _const_spec(shape):
    nd = len(shape)
    return pl.BlockSpec(shape, lambda i, _n=nd: (0,) * _n, pipeline_mode=pl.Buffered(1))


def _params():
    return pltpu.CompilerParams(dimension_semantics=("arbitrary",), vmem_limit_bytes=VMEM_LIMIT)


def _dot(a, b):
    return jnp.dot(a, b, preferred_element_type=F32)


def _dot_nt(a, b):
    return lax.dot_general(a, b, (((1,), (1,)), ((), ())), preferred_element_type=F32)


def _dot_tn(a, b):
    return lax.dot_general(a, b, (((0,), (0,)), ((), ())), preferred_element_type=F32)


def _bmm(a, b):
    return lax.dot_general(a, b, (((2,), (1,)), ((0,), (0,))), preferred_element_type=F32)


def _bmm_nt(a, b):
    return lax.dot_general(a, b, (((2,), (2,)), ((0,), (0,))), preferred_element_type=F32)


def _bmm_tn(a, b):
    return lax.dot_general(a, b, (((1,), (1,)), ((0,), (0,))), preferred_element_type=F32)


def _rms(x, g):
    return x * lax.rsqrt(jnp.mean(x * x, axis=-1, keepdims=True) + EPS) * g


def _modulate(x3, mod, gain, which):
    d = x3.shape[-1]
    sh = mod[:, :, (3 * which) * d:(3 * which + 1) * d]
    sc = mod[:, :, (3 * which + 1) * d:(3 * which + 2) * d]
    return _rms(x3, gain) * (1.0 + sc) + sh


def _gate(mod, which, d):
    return mod[:, :, (3 * which + 2) * d:(3 * which + 3) * d]


def _adaln_body(c_ref, w_ref, b_ref, o_ref):
    c = c_ref[...]
    a = (c * jax.nn.sigmoid(c)).astype(BF16)
    o_ref[0] = _dot(a, w_ref[0].astype(BF16)) + b_ref[0]


def _adaln(c_all, ada_w, ada_b):
    depth, d, n = ada_w.shape
    r = c_all.shape[0]
    tn = 1536
    return pl.pallas_call(
        _adaln_body,
        grid=(depth, n // tn),
        in_specs=[pl.BlockSpec((r, d), lambda l, j: (0, 0)),
                  pl.BlockSpec((1, d, tn), lambda l, j: (l, 0, j)),
                  pl.BlockSpec((1, 1, tn), lambda l, j: (l, 0, j))],
        out_specs=pl.BlockSpec((1, r, tn), lambda l, j: (l, 0, j)),
        out_shape=jax.ShapeDtypeStruct((depth, r, n), F32),
        compiler_params=pltpu.CompilerParams(dimension_semantics=("arbitrary", "arbitrary"),
                                             vmem_limit_bytes=VMEM_LIMIT),
        name="adaln",
    )(c_all, ada_w, ada_b.reshape(depth, 1, n))


def _ffn_body(x_ref, mod_ref, gn_ref, wg_ref, wu_ref, wd_ref, o_ref):
    x = x_ref[...]
    sb, _, d = x.shape
    mod = mod_ref[...]
    h = _modulate(x, mod, gn_ref[...], 1).reshape(sb * ROWS, d).astype(BF16)
    a = _dot(h, wg_ref[...])
    b = _dot(h, wu_ref[...])
    act = (a * jax.nn.sigmoid(a) * b).astype(BF16)
    y = _dot(act, wd_ref[...])
    o_ref[...] = x + _gate(mod, 1, d) * y.reshape(sb, ROWS, d)


def _ffn(x3, mod3, gain, wg, wu, wd, sb):
    n8, _, d = x3.shape
    f = wg.shape[1]
    per_seq = mod3.shape[0] != 1
    mod_spec = (pl.BlockSpec((sb, 1, 6 * d), lambda i: (i, 0, 0)) if per_seq
                else pl.BlockSpec((1, 1, 6 * d), lambda i: (0, 0, 0)))
    return pl.pallas_call(
        _ffn_body,
        grid=(n8 // sb,),
        in_specs=[pl.BlockSpec((sb, ROWS, d), lambda i: (i, 0, 0)), mod_spec,
                  _const_spec((1, d)), _const_spec((d, f)), _const_spec((d, f)), _const_spec((f, d))],
        out_specs=pl.BlockSpec((sb, ROWS, d), lambda i: (i, 0, 0)),
        out_shape=jax.ShapeDtypeStruct(x3.shape, F32),
        compiler_params=_params(),
        name="ffn",
    )(x3, mod3, gain.reshape(1, d), wg, wu, wd)


def _head_rms(t, g):
    return t * lax.rsqrt(jnp.mean(t * t, axis=-1, keepdims=True) + EPS) * g


def _group_norm_gate(o, gain, gate):
    mu = jnp.mean(o, axis=-1, keepdims=True)
    var = jnp.mean(jnp.square(o - mu), axis=-1, keepdims=True)
    return (o - mu) * lax.rsqrt(var + EPS) * gain * (gate * jax.nn.sigmoid(gate))


def _ret_decay(hb, c):
    lg = _ret_log_gamma(hb)
    ii = lax.broadcasted_iota(jnp.int32, (c, c), 0)
    jj = lax.broadcasted_iota(jnp.int32, (c, c), 1)
    diff = (ii - jj).astype(F32)
    d_in = jnp.where(diff >= 0, jnp.exp(lg * jnp.maximum(diff, 0.0)), 0.0)
    row = lax.broadcasted_iota(jnp.int32, (c, B_KD), 0).astype(F32)
    d_q = jnp.exp(lg * (row + 1.0))
    d_k = jnp.exp(lg * (c - 1.0 - row))
    d_c = math.exp(lg * c)
    return d_in, d_q, d_k, d_c


P_OFF_KA = QA_W
P_OFF_VA = P_OFF_KA + 2 * KA_W
P_OFF_QB = P_OFF_VA + 2 * VA_W
P_OFF_VB = P_OFF_QB + QB_W
P_OFF_GB = P_OFF_VB + VB_W
P_WIDTH = P_OFF_GB + GB_W
PAIR = 2 * A_HD


def _dedup(t):
    low = lax.broadcasted_iota(jnp.int32, (t.shape[0], PAIR), 1) < A_HD
    return jnp.where(low, t[:, 0:PAIR], t[:, PAIR:2 * PAIR])


def _mix_even_prompt_body(x_ref, mod_ref, gn_ref, win_ref, wkt_ref, qg_ref, kg_ref, sink_ref, rg_ref, wout_ref,
                          o_ref, pk_ref, pv_ref, ps_ref, mix_ref, carry_ref, bias_ref, dec_ref, ones_ref):
    step = pl.program_id(0)
    blk = WINDOW
    rows4 = A_GROUP * blk

    @pl.when(step == 0)
    def _():
        carry_ref[...] = jnp.zeros_like(carry_ref)
        ps_ref[...] = jnp.zeros_like(ps_ref)
        er = lax.broadcasted_iota(jnp.int32, ones_ref.shape, 0) // A_HD
        ec = lax.broadcasted_iota(jnp.int32, ones_ref.shape, 1) // A_HD
        ones_ref[...] = jnp.where(er == ec, 1.0 / A_HD, 0.0).astype(BF16)
        row = lax.broadcasted_iota(jnp.int32, (rows4, 2 * blk), 0)
        dist = row % blk + blk - lax.broadcasted_iota(jnp.int32, (rows4, 2 * blk), 1)
        in_window = (dist >= 0) & (dist < WINDOW)
        for kv in range(A_KV):
            slope = jnp.zeros((rows4, 2 * blk), F32)
            for g in range(A_GROUP):
                slope = jnp.where(row // blk == g, _alibi_slope(kv * A_GROUP + g), slope)
            bias_ref[kv] = jnp.where(in_window, slope * dist.astype(F32), -NEG_INF)
        for hb in range(B_HEADS):
            lg = _ret_log_gamma(hb)
            ii = lax.broadcasted_iota(jnp.int32, (blk, blk), 0).astype(F32)
            jj = lax.broadcasted_iota(jnp.int32, (blk, blk), 1).astype(F32)
            diff = ii - jj
            dec_ref[hb, 0] = jnp.where(diff >= 0, jnp.exp(lg * jnp.maximum(diff, 0.0)), 0.0)
            dec_ref[hb, 1] = jnp.exp(lg * (ii + 1.0))
            dec_ref[hb, 2] = jnp.exp(lg * (blk - 1.0 - jj))

    x = x_ref[...]
    sb, _, d = x.shape
    tb = sb * ROWS
    mod = mod_ref[...]
    h = _modulate(x, mod, gn_ref[...], 0).reshape(tb, d).astype(BF16)
    proj = _dot(h, win_ref[...])
    kt_all = _dot_nt(wkt_ref[...], h) * (B_KD ** -0.5)
    rg = rg_ref[...]

    qa = proj[:, 0:QA_W]
    ka = proj[:, P_OFF_KA:P_OFF_KA + 2 * KA_W]
    va = proj[:, P_OFF_VA:P_OFF_VA + 2 * VA_W]
    q_hat = (qa * lax.rsqrt(_dot((qa * qa).astype(BF16), ones_ref[...]) + EPS) * qg_ref[...]).astype(BF16)
    k_hat = ka * lax.rsqrt(_dot((ka * ka).astype(BF16), ones_ref[0:2 * KA_W, 0:2 * KA_W]) + EPS) * kg_ref[...]
    k_hat16 = k_hat.astype(BF16)
    va16 = va.astype(BF16)
    prev = carry_ref[step % 2]
    carry_ref[(step + 1) % 2] = jnp.concatenate([k_hat16[tb - blk:tb], va16[tb - blk:tb]], axis=1)

    mean_mat = jnp.full((B_VD, B_VD), 1.0 / B_VD, BF16)
    row_g = lax.broadcasted_iota(jnp.int32, (rows4, 1), 0) // blk
    key_is_prev = lax.broadcasted_iota(jnp.int32, (rows4, 2 * blk), 1) < blk
    first_penalty = jnp.where(step == 0, -NEG_INF, 0.0)
    lane_low = lax.broadcasted_iota(jnp.int32, (blk, PAIR), 1) < A_HD
    ones_cols = jnp.ones((2 * blk, PAIR), BF16)

    n_blk = tb // blk
    att = [(j, kv) for j in range(n_blk) for kv in range(A_KV)]
    ret = [(j, hb) for j in range(n_blk) for hb in range(B_HEADS)]

    sinks, scores = {}, {}
    for j, kv in att:
        r0 = j * blk
        kcol = slice(kv * PAIR, (kv + 1) * PAIR)
        if j == 0:
            k2 = jnp.concatenate([prev[:, kcol], k_hat16[0:blk, kcol]], axis=0)
        else:
            k2 = k_hat16[r0 - blk:r0 + blk, kcol]
        q4 = jnp.concatenate(
            [jnp.where(lane_low == (g % 2 == 0),
                       q_hat[r0:r0 + blk, (kv * A_GROUP + g - g % 2) * A_HD:(kv * A_GROUP + g - g % 2 + 2) * A_HD],
                       jnp.zeros((), BF16))
             for g in range(A_GROUP)], axis=0)
        scores[j, kv] = _dot_nt(q4, k2)
        sink = jnp.zeros((rows4, 1), F32)
        for g in range(A_GROUP):
            sink = jnp.where(row_g == g, sink_ref[kv * A_GROUP + g], sink)
        sinks[j, kv] = sink
    qb, vb, kt, inner = {}, {}, {}, {}
    for j, hb in ret:
        r0 = j * blk
        qb[j, hb] = proj[r0:r0 + blk, P_OFF_QB + hb * B_KD:P_OFF_QB + (hb + 1) * B_KD].astype(BF16)
        vb[j, hb] = proj[r0:r0 + blk, P_OFF_VB + hb * B_VD:P_OFF_VB + (hb + 1) * B_VD].astype(BF16)
        kt[j, hb] = kt_all[hb * B_KD:(hb + 1) * B_KD, r0:r0 + blk]
        inner[j, hb] = _dot(qb[j, hb], kt[j, hb].astype(BF16))

    probs, maxes = {}, {}
    for j, kv in att:
        s = scores[j, kv] * (A_HD ** -0.5) - bias_ref[kv]
        if j == 0:
            s = s - jnp.where(key_is_prev, first_penalty, 0.0)
        mx = jnp.maximum(jnp.max(s, axis=-1, keepdims=True), sinks[j, kv])
        probs[j, kv] = jnp.exp(s - mx).astype(BF16)
        maxes[j, kv] = mx
    state = {}
    for hb in range(B_HEADS):
        d_c = math.exp(_ret_log_gamma(hb) * blk)
        state[0, hb] = ps_ref[hb]
        for j in range(n_blk):
            state[j + 1, hb] = state[j, hb] * d_c + _dot((kt[j, hb] * dec_ref[hb, 2]).astype(BF16), vb[j, hb])
        ps_ref[hb] = state[n_blk, hb]

    for j, kv in att:
        r0 = j * blk
        kcol = slice(kv * PAIR, (kv + 1) * PAIR)
        vcol = slice(2 * KA_W + kv * PAIR, 2 * KA_W + (kv + 1) * PAIR)
        if j == 0:
            v2 = jnp.concatenate([prev[:, vcol], va16[0:blk, kcol]], axis=0)
        else:
            v2 = va16[r0 - blk:r0 + blk, kcol]
        pv = _dot(probs[j, kv], jnp.concatenate([v2, ones_cols], axis=1))
        o4 = pv[:, 0:PAIR] / (pv[:, PAIR:2 * PAIR] + jnp.exp(sinks[j, kv] - maxes[j, kv]))
        for g in range(A_GROUP):
            hd = kv * A_GROUP + g
            half = slice((hd % 2) * A_HD, (hd % 2 + 1) * A_HD)
            mix_ref[r0:r0 + blk, hd * A_HD:(hd + 1) * A_HD] = o4[g * blk:(g + 1) * blk, half]
    o_ret = {}
    for j, hb in ret:
        o_ret[j, hb] = (_dot((inner[j, hb] * dec_ref[hb, 0]).astype(BF16), vb[j, hb])
                        + _dot(qb[j, hb], state[j, hb].astype(BF16)) * dec_ref[hb, 1])

    cen = {k: o_ret[k] - _dot(o_ret[k].astype(BF16), mean_mat) for k in ret}
    var = {k: _dot((cen[k] * cen[k]).astype(BF16), mean_mat) for k in ret}
    for j, hb in ret:
        r0 = j * blk
        gb = proj[r0:r0 + blk, P_OFF_GB + hb * B_VD:P_OFF_GB + (hb + 1) * B_VD]
        mix_ref[r0:r0 + blk, QA_W + hb * B_VD:QA_W + (hb + 1) * B_VD] = (
            cen[j, hb] * lax.rsqrt(var[j, hb] + EPS) * rg[:, hb * B_VD:(hb + 1) * B_VD] * (gb * jax.nn.sigmoid(gb)))

    out = _dot(mix_ref[...].astype(BF16), wout_ref[...])
    o_ref[...] = x + _gate(mod, 0, d) * out.reshape(sb, ROWS, d)

    @pl.when(step == pl.num_programs(0) - 1)
    def _():
        pk_ref[...] = _dedup(k_hat[tb - blk:tb, :])
        pv_ref[...] = _dedup(va[tb - blk:tb, :])


def _dup_heads(t, width):
    lead = t.shape[:-1]
    t = t.reshape(lead + (-1, 1, width))
    return jnp.broadcast_to(t, lead + (t.shape[-3], 2, width)).reshape(lead + (-1,))


def _mix_even_prompt(x3, mod3, gain, w_in, q_gain, k_gain, sinks, ret_gain, w_out, sb):
    n8, _, d = x3.shape
    tb = sb * ROWS
    w_main = jnp.concatenate([w_in[:, :OFF_KA], _dup_heads(w_in[:, OFF_KA:OFF_VA], A_HD),
                              _dup_heads(w_in[:, OFF_VA:OFF_QB], A_HD), w_in[:, OFF_QB:OFF_KB],
                              w_in[:, OFF_VB:]], axis=1)
    wk_t = w_in[:, OFF_KB:OFF_VB].T
    return pl.pallas_call(
        _mix_even_prompt_body,
        grid=(n8 // sb,),
        in_specs=[pl.BlockSpec((sb, ROWS, d), lambda i: (i, 0, 0)),
                  pl.BlockSpec((1, 1, 6 * d), lambda i: (0, 0, 0)),
                  _const_spec((1, d)), _const_spec((d, P_WIDTH)), _const_spec((KB_W, d)),
                  _const_spec((1, QA_W)), _const_spec((1, 2 * KA_W)),
                  pl.BlockSpec(memory_space=pltpu.SMEM),
                  _const_spec((1, VB_W)), _const_spec((QA_W + VB_W, d))],
        out_specs=[pl.BlockSpec((sb, ROWS, d), lambda i: (i, 0, 0)),
                   pl.BlockSpec((WINDOW, KA_W), lambda i: (0, 0)),
                   pl.BlockSpec((WINDOW, VA_W), lambda i: (0, 0)),
                   pl.BlockSpec((B_HEADS, B_KD, B_VD), lambda i: (0, 0, 0))],
        out_shape=[jax.ShapeDtypeStruct(x3.shape, F32),
                   jax.ShapeDtypeStruct((WINDOW, KA_W), F32),
                   jax.ShapeDtypeStruct((WINDOW, VA_W), F32),
                   jax.ShapeDtypeStruct((B_HEADS, B_KD, B_VD), F32)],
        scratch_shapes=[pltpu.VMEM((tb, QA_W + VB_W), F32),
                        pltpu.VMEM((2, WINDOW, 2 * KA_W + 2 * VA_W), BF16),
                        pltpu.VMEM((A_KV, A_GROUP * WINDOW, 2 * WINDOW), F32),
                        pltpu.VMEM((B_HEADS, 3, WINDOW, WINDOW), F32),
                        pltpu.VMEM((QA_W, QA_W), BF16)],
        compiler_params=_params(),
        name="mix_even_prompt",
    )(x3, mod3, gain.reshape(1, d), w_main, wk_t, jnp.tile(q_gain, A_HEADS).reshape(1, QA_W),
      jnp.tile(k_gain, 2 * A_KV).reshape(1, 2 * KA_W),
      sinks, ret_gain.reshape(1, VB_W), w_out)


def _mix_even_sample_body(x_ref, mod_ref, gn_ref, win_ref, qg_ref, kg_ref, sink_ref, rg_ref, wout_ref,
                          ck_ref, cv_ref, s0_ref,
                          o_ref, nk_ref, nv_ref, ns_ref, mix_ref):
    x = x_ref[...]
    sb, length, d = x.shape
    tb = sb * length
    w = ck_ref.shape[1]
    mod = mod_ref[...]
    h = _modulate(x, mod, gn_ref[...], 0).reshape(tb, d).astype(BF16)
    proj = _dot(h, win_ref[...])
    qg = qg_ref[...]
    kg = kg_ref[...]
    rg = rg_ref[...]

    rows = A_GROUP * length
    qpos_c = lax.broadcasted_iota(jnp.int32, (rows, w), 0) % length
    kpos_c = lax.broadcasted_iota(jnp.int32, (rows, w), 1)
    dist_c = w + qpos_c - kpos_c
    valid_c = (dist_c >= 0) & (dist_c < WINDOW)
    qpos_n = lax.broadcasted_iota(jnp.int32, (rows, length), 0) % length
    kpos_n = lax.broadcasted_iota(jnp.int32, (rows, length), 1)
    dist_n = qpos_n - kpos_n
    valid_n = (dist_n >= 0) & (dist_n < WINDOW)
    row_g = lax.broadcasted_iota(jnp.int32, (rows, 1), 0) // length

    for kv in range(A_KV):
        lanes = slice(kv * A_HD, (kv + 1) * A_HD)
        kn = _head_rms(proj[:, OFF_KA + kv * A_HD:OFF_KA + (kv + 1) * A_HD], kg).reshape(sb, length, A_HD)
        vn = proj[:, OFF_VA + kv * A_HD:OFF_VA + (kv + 1) * A_HD].reshape(sb, length, A_HD)
        nk_ref[:, 0:w - length, lanes] = ck_ref[:, length:w, lanes]
        nv_ref[:, 0:w - length, lanes] = cv_ref[:, length:w, lanes]
        nk_ref[:, w - length:w, lanes] = kn
        nv_ref[:, w - length:w, lanes] = vn
        kc = ck_ref[:, :, lanes].astype(BF16)
        vc = cv_ref[:, :, lanes].astype(BF16)
        q4 = jnp.concatenate(
            [_head_rms(proj[:, OFF_QA + (kv * A_GROUP + g) * A_HD:OFF_QA + (kv * A_GROUP + g + 1) * A_HD], qg)
             .reshape(sb, length, A_HD) for g in range(A_GROUP)], axis=1).astype(BF16)
        slope = jnp.zeros((rows, 1), F32)
        sink = jnp.zeros((rows, 1), F32)
        for g in range(A_GROUP):
            hd = kv * A_GROUP + g
            slope = jnp.where(row_g == g, _alibi_slope(hd), slope)
            sink = jnp.where(row_g == g, sink_ref[hd], sink)
        scale = A_HD ** -0.5
        s_c = _bmm_nt(q4, kc) * scale - slope * dist_c.astype(F32)
        s_c = jnp.where(valid_c, s_c, NEG_INF)
        s_n = _bmm_nt(q4, kn.astype(BF16)) * scale - slope * dist_n.astype(F32)
        s_n = jnp.where(valid_n, s_n, NEG_INF)
        mx = jnp.maximum(jnp.maximum(jnp.max(s_c, axis=-1, keepdims=True),
                                     jnp.max(s_n, axis=-1, keepdims=True)), sink)
        p_c = jnp.exp(s_c - mx)
        p_n = jnp.exp(s_n - mx)
        den = (jnp.sum(p_c, axis=-1, keepdims=True) + jnp.sum(p_n, axis=-1, keepdims=True)
               + jnp.exp(sink - mx))
        o4 = (_bmm(p_c.astype(BF16), vc) + _bmm(p_n.astype(BF16), vn.astype(BF16))) / den
        for g in range(A_GROUP):
            hd = kv * A_GROUP + g
            mix_ref[:, hd * A_HD:(hd + 1) * A_HD] = o4[:, g * length:(g + 1) * length, :].reshape(tb, A_HD)

    for hb in range(B_HEADS):
        d_in, d_q, d_k, d_c = _ret_decay(hb, length)
        qb = proj[:, OFF_QB + hb * B_KD:OFF_QB + (hb + 1) * B_KD].reshape(sb, length, B_KD).astype(BF16)
        kb = proj[:, OFF_KB + hb * B_KD:OFF_KB + (hb + 1) * B_KD].reshape(sb, length, B_KD) * (B_KD ** -0.5)
        vb = proj[:, OFF_VB + hb * B_VD:OFF_VB + (hb + 1) * B_VD].reshape(sb, length, B_VD).astype(BF16)
        gb = proj[:, OFF_GB + hb * B_VD:OFF_GB + (hb + 1) * B_VD]
        state = s0_ref[:, hb]
        inner = _bmm_nt(qb, kb.astype(BF16)) * d_in
        o = _bmm(inner.astype(BF16), vb) + _bmm(qb, state.astype(BF16)) * d_q
        ns_ref[:, hb] = state * d_c + _bmm_tn((kb * d_k).astype(BF16), vb)
        mix_ref[:, QA_W + hb * B_VD:QA_W + (hb + 1) * B_VD] = _group_norm_gate(
            o.reshape(tb, B_VD), rg[:, hb * B_VD:(hb + 1) * B_VD], gb)

    out = _dot(mix_ref[...].astype(BF16), wout_ref[...])
    o_ref[...] = x + _gate(mod, 0, d) * out.reshape(sb, length, d)


def _mix_even_sample(x3, mod3, gain, w_in, q_gain, k_gain, sinks, ret_gain, w_out, cache_k, cache_v, state, sb):
    n, length, d = x3.shape
    in_w = w_in.shape[1]
    w = cache_k.shape[1]
    seq_spec = lambda shape: pl.BlockSpec((sb,) + shape, lambda i, _n=len(shape): (i,) + (0,) * _n)
    return pl.pallas_call(
        _mix_even_sample_body,
        grid=(n // sb,),
        in_specs=[seq_spec((length, d)), seq_spec((1, 6 * d)),
                  _const_spec((1, d)), _const_spec((d, in_w)),
                  _const_spec((1, A_HD)), _const_spec((1, A_HD)),
                  pl.BlockSpec(memory_space=pltpu.SMEM),
                  _const_spec((1, VB_W)), _const_spec((QA_W + VB_W, d)),
                  seq_spec((w, KA_W)), seq_spec((w, VA_W)), seq_spec((B_HEADS, B_KD, B_VD))],
        out_specs=[seq_spec((length, d)), seq_spec((w, KA_W)), seq_spec((w, VA_W)),
                   seq_spec((B_HEADS, B_KD, B_VD))],
        out_shape=[jax.ShapeDtypeStruct(x3.shape, F32),
                   jax.ShapeDtypeStruct(cache_k.shape, F32),
                   jax.ShapeDtypeStruct(cache_v.shape, F32),
                   jax.ShapeDtypeStruct(state.shape, F32)],
        scratch_shapes=[pltpu.VMEM((sb * length, QA_W + VB_W), F32)],
        compiler_params=_params(),
        name="mix_even_sample",
    )(x3, mod3, gain.reshape(1, d), w_in, q_gain.reshape(1, A_HD), k_gain.reshape(1, A_HD),
      sinks, ret_gain.reshape(1, VB_W), w_out, cache_k, cache_v, state)


def _s5_prep_body(are_ref, aim_ref, ldt_ref, btr_ref, bti_ref, lre_ref, lim_ref, bbr_ref, bbi_ref):
    a_re = are_ref[...]
    a_im = aim_ref[...]
    dt = jnp.exp(ldt_ref[...])
    mag = jnp.exp(a_re * dt)
    lam_re = mag * jnp.cos(a_im * dt)
    lam_im = mag * jnp.sin(a_im * dt)
    den = a_re * a_re + a_im * a_im
    n_re = lam_re - 1.0
    n_im = lam_im
    f_re = (n_re * a_re + n_im * a_im) / den
    f_im = (n_im * a_re - n_re * a_im) / den
    br = btr_ref[...]
    bi = bti_ref[...]
    lre_ref[...] = lam_re
    lim_ref[...] = lam_im
    bbr_ref[...] = f_re * br - f_im * bi
    bbi_ref[...] = f_re * bi + f_im * br


def _s5_prep(a_re, a_im, log_dt, b_re, b_im):
    g, p = a_re.shape
    k = b_re.shape[-1]
    bt_re = jnp.swapaxes(b_re, 1, 2)
    bt_im = jnp.swapaxes(b_im, 1, 2)
    out = pl.pallas_call(
        _s5_prep_body,
        out_shape=[jax.ShapeDtypeStruct((g, 1, p), F32), jax.ShapeDtypeStruct((g, 1, p), F32),
                   jax.ShapeDtypeStruct((g, k, p), F32), jax.ShapeDtypeStruct((g, k, p), F32)],
        name="s5_prep",
    )(a_re.reshape(g, 1, p), a_im.reshape(g, 1, p), log_dt.reshape(g, 1, 1), bt_re, bt_im)
    return out


def _block_diag(t):
    g, a, b = t.shape
    t = t.reshape(g // S5_OCT, S5_OCT, a, b)
    eye = jnp.eye(S5_OCT, dtype=t.dtype)
    out = t[:, :, :, None, :] * eye[None, :, None, :, None]
    return out.reshape(g // S5_OCT, S5_OCT * a, S5_OCT * b)


def _gelu_glu_out(x, mod, y, u, dskip, glua_ref, glub_ref):
    sb, rows, d = x.shape
    y = y + dskip * u
    yg = jax.nn.gelu(y, approximate=True).astype(BF16)
    out = _dot(yg, glua_ref[...]) * jax.nn.sigmoid(_dot(yg, glub_ref[...]))
    return x + _gate(mod, 0, d) * out.reshape(sb, rows, d)


def _mix_odd_prompt_body(x_ref, mod_ref, gn_ref, bblk_ref, lam_ref, cre_ref, cim_ref, dskip_ref,
                         glua_ref, glub_ref, o_ref, hre_ref, him_ref, d_ref, e_ref, y_ref):
    step = pl.program_id(0)
    x = x_ref[...]
    sb, _, d = x.shape
    tm = sb * ROWS
    n_oct = bblk_ref.shape[0]
    half = bblk_ref.shape[2] // 2
    n_ch = bblk_ref.shape[2] // 128
    hc = n_ch // 2

    @pl.when(step == 0)
    def _():
        hre_ref[...] = jnp.zeros_like(hre_ref)
        him_ref[...] = jnp.zeros_like(him_ref)

    mod = mod_ref[...]
    u = _modulate(x, mod, gn_ref[...], 0).reshape(tm, d)
    u16 = u.astype(BF16)
    for s in range(n_oct):
        bu = _dot(u16[:, s * 128:(s + 1) * 128], bblk_ref[s])
        for c in range(n_ch):
            d_ref[c, pl.ds(s, tm, stride=n_oct), :] = bu[:, c * 128:(c + 1) * 128]

    lam_re = lam_ref[0]
    lam_im = lam_ref[1]

    def scan_group(j, carry):
        h_re, h_im = carry
        for t8 in range(ROWS):
            r = pl.multiple_of((j * ROWS + t8) * n_oct, n_oct)
            b_re = jnp.concatenate([d_ref[c, pl.ds(r, n_oct), :] for c in range(hc)], axis=1)
            b_im = jnp.concatenate([d_ref[hc + c, pl.ds(r, n_oct), :] for c in range(hc)], axis=1)
            h_re, h_im = (lam_re * h_re - lam_im * h_im + b_re, lam_re * h_im + lam_im * h_re + b_im)
            for c in range(hc):
                e_ref[c, j, pl.ds(t8, n_oct, stride=ROWS), :] = h_re[:, c * 128:(c + 1) * 128]
                e_ref[hc + c, j, pl.ds(t8, n_oct, stride=ROWS), :] = h_im[:, c * 128:(c + 1) * 128]
        return h_re, h_im

    h_re, h_im = lax.fori_loop(0, tm // ROWS, scan_group, (hre_ref[...], him_ref[...]))
    hre_ref[...] = h_re
    him_ref[...] = h_im

    for s in range(n_oct):
        rows = slice(s * ROWS, (s + 1) * ROWS)
        hs_re = jnp.concatenate([e_ref[c, :, rows, :].reshape(tm, 128) for c in range(hc)], axis=1)
        hs_im = jnp.concatenate([e_ref[hc + c, :, rows, :].reshape(tm, 128) for c in range(hc)], axis=1)
        y_ref[:, s * 128:(s + 1) * 128] = (_dot(hs_re.astype(BF16), cre_ref[s])
                                           - _dot(hs_im.astype(BF16), cim_ref[s]))
    o_ref[...] = _gelu_glu_out(x, mod, y_ref[...], u, dskip_ref[...], glua_ref, glub_ref)


def _mix_odd_prompt(x3, mod3, gain, bblk, lam_d, cre, cim, dskip, glu_a, glu_b, sb):
    n8, _, d = x3.shape
    n_oct, kin, wid = bblk.shape
    half = wid // 2
    tm = sb * ROWS
    return pl.pallas_call(
        _mix_odd_prompt_body,
        grid=(n8 // sb,),
        in_specs=[pl.BlockSpec((sb, ROWS, d), lambda i: (i, 0, 0)),
                  pl.BlockSpec((1, 1, 6 * d), lambda i: (0, 0, 0)),
                  _const_spec((1, d)), _const_spec(bblk.shape), _const_spec(lam_d.shape),
                  _const_spec(cre.shape), _const_spec(cim.shape), _const_spec((1, d)),
                  _const_spec((d, d)), _const_spec((d, d))],
        out_specs=[pl.BlockSpec((sb, ROWS, d), lambda i: (i, 0, 0)),
                   pl.BlockSpec((n_oct, half), lambda i: (0, 0)),
                   pl.BlockSpec((n_oct, half), lambda i: (0, 0))],
        out_shape=[jax.ShapeDtypeStruct(x3.shape, F32),
                   jax.ShapeDtypeStruct((n_oct, half), F32),
                   jax.ShapeDtypeStruct((n_oct, half), F32)],
        scratch_shapes=[pltpu.VMEM((wid // 128, tm * n_oct, 128), F32),
                        pltpu.VMEM((wid // 128, tm // ROWS, ROWS * n_oct, 128), F32),
                        pltpu.VMEM((tm, d), F32)],
        compiler_params=_params(),
        name="mix_odd_prompt",
    )(x3, mod3, gain.reshape(1, d), bblk, lam_d, cre, cim, dskip.reshape(1, d), glu_a, glu_b)


def _mix_odd_sample_body(x_ref, mod_ref, gn_ref, bblk_ref, lam_ref, cre_ref, cim_ref, dskip_ref,
                         glua_ref, glub_ref, sre_ref, sim_ref, o_ref, nre_ref, nim_ref, d_ref, y_ref):
    x = x_ref[...]
    sb, length, d = x.shape
    tm = sb * length
    n_oct = bblk_ref.shape[0]
    half = bblk_ref.shape[2] // 2
    mod = mod_ref[...]
    u = _modulate(x, mod, gn_ref[...], 0).reshape(tm, d)
    u16 = u.astype(BF16)
    n_ch = bblk_ref.shape[2] // 128
    hc = n_ch // 2
    for s in range(n_oct):
        bu = _dot(u16[:, s * 128:(s + 1) * 128], bblk_ref[s])
        for c in range(n_ch):
            d_ref[c] = bu[:, c * 128:(c + 1) * 128]
        lam_re = lam_ref[0, s:s + 1, :]
        lam_im = lam_ref[1, s:s + 1, :]
        h_re = sre_ref[:, s * half:(s + 1) * half]
        h_im = sim_ref[:, s * half:(s + 1) * half]
        for t in range(length):
            b_re = jnp.concatenate([d_ref[c, pl.ds(t, sb, stride=length), :] for c in range(hc)], axis=1)
            b_im = jnp.concatenate([d_ref[hc + c, pl.ds(t, sb, stride=length), :] for c in range(hc)], axis=1)
            n_re = lam_re * h_re - lam_im * h_im + b_re
            n_im = lam_re * h_im + lam_im * h_re + b_im
            for c in range(hc):
                d_ref[c, pl.ds(t, sb, stride=length), :] = n_re[:, c * 128:(c + 1) * 128]
                d_ref[hc + c, pl.ds(t, sb, stride=length), :] = n_im[:, c * 128:(c + 1) * 128]
            h_re, h_im = n_re, n_im
        nre_ref[:, s * half:(s + 1) * half] = h_re
        nim_ref[:, s * half:(s + 1) * half] = h_im
        hs_re = jnp.concatenate([d_ref[c] for c in range(hc)], axis=1)
        hs_im = jnp.concatenate([d_ref[hc + c] for c in range(hc)], axis=1)
        y_ref[:, s * 128:(s + 1) * 128] = (_dot(hs_re.astype(BF16), cre_ref[s])
                                           - _dot(hs_im.astype(BF16), cim_ref[s]))
    o_ref[...] = _gelu_glu_out(x, mod, y_ref[...], u, dskip_ref[...], glua_ref, glub_ref)


def _mix_odd_sample(x3, mod3, gain, bblk, lam_d, cre, cim, dskip, glu_a, glu_b, s_re, s_im, sb):
    n, length, d = x3.shape
    n_oct, kin, wid = bblk.shape
    tm = sb * length
    nstate = s_re.shape[1]
    return pl.pallas_call(
        _mix_odd_sample_body,
        grid=(n // sb,),
        in_specs=[pl.BlockSpec((sb, length, d), lambda i: (i, 0, 0)),
                  pl.BlockSpec((sb, 1, 6 * d), lambda i: (i, 0, 0)),
                  _const_spec((1, d)), _const_spec(bblk.shape), _const_spec(lam_d.shape),
                  _const_spec(cre.shape), _const_spec(cim.shape), _const_spec((1, d)),
                  _const_spec((d, d)), _const_spec((d, d)),
                  pl.BlockSpec((sb, nstate), lambda i: (i, 0)),
                  pl.BlockSpec((sb, nstate), lambda i: (i, 0))],
        out_specs=[pl.BlockSpec((sb, length, d), lambda i: (i, 0, 0)),
                   pl.BlockSpec((sb, nstate), lambda i: (i, 0)),
                   pl.BlockSpec((sb, nstate), lambda i: (i, 0))],
        out_shape=[jax.ShapeDtypeStruct(x3.shape, F32),
                   jax.ShapeDtypeStruct(s_re.shape, F32),
                   jax.ShapeDtypeStruct(s_im.shape, F32)],
        scratch_shapes=[pltpu.VMEM((wid // 128, tm, 128), F32), pltpu.VMEM((tm, d), F32)],
        compiler_params=_params(),
        name="mix_odd_sample",
    )(x3, mod3, gain.reshape(1, d), bblk, lam_d, cre, cim, dskip.reshape(1, d), glu_a, glu_b, s_re, s_im)


def _pick(n, want):
    while n % want:
        want //= 2
    return max(want, 1)


def kernel(x_prompt, x_sample, cache_win_k, cache_win_v, state_ret, state_s5_re, state_s5_im, c_prompt, c_sample, ada_w, ada_b, norm_mix, norm_ffn, ffn_wg, ffn_wu, ffn_wd, even_w_in, even_q_gain, even_k_gain, even_sinks, even_ret_gain, even_w_out, odd_A_re, odd_A_im, odd_log_dt, odd_B_re, odd_B_im, odd_C_re, odd_C_im, odd_D, odd_glu_a, odd_glu_b):
    bp, lp, d = x_prompt.shape
    ns, ls, _ = x_sample.shape
    assert bp == 1 and ls == ROWS and lp % WINDOW == 0
    w = cache_win_k.shape[2]
    groups, p_state = odd_A_re.shape[1:]

    n_c = bp + ns
    n_pad = -n_c % ROWS
    c_all = jnp.concatenate([c_prompt, c_sample, jnp.zeros((n_pad, d), F32)], axis=0)
    mod = _adaln(c_all, ada_w, ada_b)
    mod_p = [mod[l, 0:1].reshape(1, 1, 6 * d) for l in range(2)]
    mod_s = [mod[l, bp:bp + ns].reshape(ns, 1, 6 * d) for l in range(2)]

    bf = lambda t: t.astype(BF16)
    w_in, w_out = bf(even_w_in[0]), bf(even_w_out[0])
    wg, wu, wd = bf(ffn_wg), bf(ffn_wu), bf(ffn_wd)
    glu_a, glu_b = bf(odd_glu_a[0]), bf(odd_glu_b[0])

    xp = x_prompt.reshape(lp // ROWS, ROWS, d)
    xs = x_sample

    sb_p = _pick(lp // ROWS, 32)
    sb_ffn = _pick(lp // ROWS, 64)
    sb_s = _pick(ns, 32)
    sb_s_even = _pick(ns, 16)

    xp, p_k, p_v, p_ret = _mix_even_prompt(xp, mod_p[0], norm_mix[0], w_in, even_q_gain[0], even_k_gain[0],
                                           even_sinks[0], even_ret_gain[0], w_out, sb_p)
    xs, s_k, s_v, s_ret = _mix_even_sample(xs, mod_s[0], norm_mix[0], w_in, even_q_gain[0], even_k_gain[0],
                                           even_sinks[0], even_ret_gain[0], w_out,
                                           cache_win_k[0].reshape(ns, w, KA_W), cache_win_v[0].reshape(ns, w, VA_W),
                                           state_ret[0], sb_s_even)
    xp = _ffn(xp, mod_p[0], norm_ffn[0], wg[0], wu[0], wd[0], sb_ffn)
    xs = _ffn(xs, mod_s[0], norm_ffn[0], wg[0], wu[0], wd[0], sb_s)

    lam_re, lam_im, bbt_re, bbt_im = _s5_prep(odd_A_re[0], odd_A_im[0], odd_log_dt[0], odd_B_re[0], odd_B_im[0])
    n_oct = groups // S5_OCT
    half = S5_OCT * p_state
    bblk = bf(jnp.concatenate([_block_diag(bbt_re), _block_diag(bbt_im)], axis=-1))
    cre = bf(_block_diag(jnp.swapaxes(odd_C_re[0], 1, 2)))
    cim = bf(_block_diag(jnp.swapaxes(odd_C_im[0], 1, 2)))
    lam_d = jnp.stack([lam_re.reshape(n_oct, half), lam_im.reshape(n_oct, half)])

    xp, p_re, p_im = _mix_odd_prompt(xp, mod_p[1], norm_mix[1], bblk, lam_d, cre, cim, odd_D[0], glu_a, glu_b, sb_p)
    xs, s_re, s_im = _mix_odd_sample(xs, mod_s[1], norm_mix[1], bblk, lam_d, cre, cim, odd_D[0], glu_a, glu_b,
                                     state_s5_re[0].reshape(ns, groups * p_state),
                                     state_s5_im[0].reshape(ns, groups * p_state), sb_s)
    xp = _ffn(xp, mod_p[1], norm_ffn[1], wg[1], wu[1], wd[1], sb_ffn)
    xs = _ffn(xs, mod_s[1], norm_ffn[1], wg[1], wu[1], wd[1], sb_s)

    y_prompt = xp.reshape(bp, lp, d)
    y_sample = xs
    return (y_prompt, y_sample,
            p_k.reshape(1, bp, WINDOW, A_KV, A_HD), p_v.reshape(1, bp, WINDOW, A_KV, A_HD),
            p_ret.reshape(1, bp, B_HEADS, B_KD, B_VD),
            p_re.reshape(1, bp, groups, p_state), p_im.reshape(1, bp, groups, p_state),
            s_k.reshape(1, ns, w, A_KV, A_HD), s_v.reshape(1, ns, w, A_KV, A_HD),
            s_ret.reshape(1, ns, B_HEADS, B_KD, B_VD),
            s_re.reshape(1, ns, groups, p_state), s_im.reshape(1, ns, groups, p_state))
```

```python
import functools
import math

import jax
import jax.numpy as jnp
from jax import lax
from jax.experimental import pallas as pl
from jax.experimental.pallas import tpu as pltpu

F32 = jnp.float32
BF16 = jnp.bfloat16

EPS = 1e-6
NEG_INF = -1e30
ROWS = 8

A_HEADS, A_KV, A_GROUP, A_HD = 8, 2, 4, 64
WINDOW = 128
B_HEADS, B_KD, B_VD = 4, 128, 128
S5_GROUP, S5_STATE = 16, 64
S5_OCT = 8

QA_W, KA_W, VA_W = A_HEADS * A_HD, A_KV * A_HD, A_KV * A_HD
QB_W, KB_W, VB_W, GB_W = B_HEADS * B_KD, B_HEADS * B_KD, B_HEADS * B_VD, B_HEADS * B_VD
OFF_QA = 0
OFF_KA = OFF_QA + QA_W
OFF_VA = OFF_KA + KA_W
OFF_QB = OFF_VA + VA_W
OFF_KB = OFF_QB + QB_W
OFF_VB = OFF_KB + KB_W
OFF_GB = OFF_VB + VB_W

VMEM_LIMIT = 56 * 1024 * 1024


def _ret_log_gamma(h):
    return math.log1p(-(2.0 ** (-5.0 - h)))


def _alibi_slope(h):
    return 2.0 ** (-8.0 * (h + 1) / A_HEADS)


def _const_spec(shape):
    nd = len(shape)
    return pl.BlockSpec(shape, lambda i, _n=nd: (0,) * _n, pipeline_mode=pl.Buffered(1))


def _params():
    return pltpu.CompilerParams(dimension_semantics=("arbitrary",), vmem_limit_bytes=VMEM_LIMIT)


def _dot(a, b):
    return jnp.dot(a, b, preferred_element_type=F32)


def _dot_nt(a, b):
    return lax.dot_general(a, b, (((1,), (1,)), ((), ())), preferred_element_type=F32)


def _dot_tn(a, b):
    return lax.dot_general(a, b, (((0,), (0,)), ((), ())), preferred_element_type=F32)


def _bmm(a, b):
    return lax.dot_general(a, b, (((2,), (1,)), ((0,), (0,))), preferred_element_type=F32)


def _bmm_nt(a, b):
    return lax.dot_general(a, b, (((2,), (2,)), ((0,), (0,))), preferred_element_type=F32)


def _bmm_tn(a, b):
    return lax.dot_general(a, b, (((1,), (1,)), ((0,), (0,))), preferred_element_type=F32)


def _rms(x, g):
    return x * lax.rsqrt(jnp.mean(x * x, axis=-1, keepdims=True) + EPS) * g


def _modulate(x3, mod, gain, which):
    d = x3.shape[-1]
    sh = mod[:, :, (3 * which) * d:(3 * which + 1) * d]
    sc = mod[:, :, (3 * which + 1) * d:(3 * which + 2) * d]
    return _rms(x3, gain) * (1.0 + sc) + sh


def _gate(mod, which, d):
    return mod[:, :, (3 * which + 2) * d:(3 * which + 3) * d]


def _adaln_body(c_ref, w_ref, b_ref, o_ref):
    c = c_ref[...]
    a = (c * jax.nn.sigmoid(c)).astype(BF16)
    o_ref[0] = _dot(a, w_ref[0].astype(BF16)) + b_ref[0]


def _adaln(c_all, ada_w, ada_b):
    depth, d, n = ada_w.shape
    r = c_all.shape[0]
    tn = 1536
    return pl.pallas_call(
        _adaln_body,
        grid=(depth, n // tn),
        in_specs=[pl.BlockSpec((r, d), lambda l, j: (0, 0)),
                  pl.BlockSpec((1, d, tn), lambda l, j: (l, 0, j)),
                  pl.BlockSpec((1, 1, tn), lambda l, j: (l, 0, j))],
        out_specs=pl.BlockSpec((1, r, tn), lambda l, j: (l, 0, j)),
        out_shape=jax.ShapeDtypeStruct((depth, r, n), F32),
        compiler_params=pltpu.CompilerParams(dimension_semantics=("arbitrary", "arbitrary"),
                                             vmem_limit_bytes=VMEM_LIMIT),
        name="adaln",
    )(c_all, ada_w, ada_b.reshape(depth, 1, n))


def _ffn_body(x_ref, mod_ref, gn_ref, wg_ref, wu_ref, wd_ref, o_ref):
    x = x_ref[...]
    sb, _, d = x.shape
    mod = mod_ref[...]
    h = _modulate(x, mod, gn_ref[...], 1).reshape(sb * ROWS, d).astype(BF16)
    a = _dot(h, wg_ref[...])
    b = _dot(h, wu_ref[...])
    act = (a * jax.nn.sigmoid(a) * b).astype(BF16)
    y = _dot(act, wd_ref[...])
    o_ref[...] = x + _gate(mod, 1, d) * y.reshape(sb, ROWS, d)


def _ffn(x3, mod3, gain, wg, wu, wd, sb):
    n8, _, d = x3.shape
    f = wg.shape[1]
    per_seq = mod3.shape[0] != 1
    mod_spec = (pl.BlockSpec((sb, 1, 6 * d), lambda i: (i, 0, 0)) if per_seq
                else pl.BlockSpec((1, 1, 6 * d), lambda i: (0, 0, 0)))
    return pl.pallas_call(
        _ffn_body,
        grid=(n8 // sb,),
        in_specs=[pl.BlockSpec((sb, ROWS, d), lambda i: (i, 0, 0)), mod_spec,
                  _const_spec((1, d)), _const_spec((d, f)), _const_spec((d, f)), _const_spec((f, d))],
        out_specs=pl.BlockSpec((sb, ROWS, d), lambda i: (i, 0, 0)),
        out_shape=jax.ShapeDtypeStruct(x3.shape, F32),
        compiler_params=_params(),
        name="ffn",
    )(x3, mod3, gain.reshape(1, d), wg, wu, wd)


def _head_rms(t, g):
    return t * lax.rsqrt(jnp.mean(t * t, axis=-1, keepdims=True) + EPS) * g


def _group_norm_gate(o, gain, gate):
    mu = jnp.mean(o, axis=-1, keepdims=True)
    var = jnp.mean(jnp.square(o - mu), axis=-1, keepdims=True)
    return (o - mu) * lax.rsqrt(var + EPS) * gain * (gate * jax.nn.sigmoid(gate))


def _ret_decay(hb, c):
    lg = _ret_log_gamma(hb)
    ii = lax.broadcasted_iota(jnp.int32, (c, c), 0)
    jj = lax.broadcasted_iota(jnp.int32, (c, c), 1)
    diff = (ii - jj).astype(F32)
    d_in = jnp.where(diff >= 0, jnp.exp(lg * jnp.maximum(diff, 0.0)), 0.0)
    row = lax.broadcasted_iota(jnp.int32, (c, B_KD), 0).astype(F32)
    d_q = jnp.exp(lg * (row + 1.0))
    d_k = jnp.exp(lg * (c - 1.0 - row))
    d_c = math.exp(lg * c)
    return d_in, d_q, d_k, d_c


P_OFF_KA = QA_W
P_OFF_VA = P_OFF_KA + 2 * KA_W
P_OFF_QB = P_OFF_VA + 2 * VA_W
P_OFF_VB = P_OFF_QB + QB_W
P_OFF_GB = P_OFF_VB + VB_W
P_WIDTH = P_OFF_GB + GB_W
PAIR = 2 * A_HD


def _dedup(t):
    low = lax.broadcasted_iota(jnp.int32, (t.shape[0], PAIR), 1) < A_HD
    return jnp.where(low, t[:, 0:PAIR], t[:, PAIR:2 * PAIR])


def _mix_even_prompt_body(x_ref, mod_ref, gn_ref, win_ref, wkt_ref, qg_ref, kg_ref, sink_ref, rg_ref, wout_ref,
                          o_ref, pk_ref, pv_ref, ps_ref, mix_ref, carry_ref, bias_ref, dec_ref, ones_ref):
    step = pl.program_id(0)
    blk = WINDOW
    rows4 = A_GROUP * blk

    @pl.when(step == 0)
    def _():
        carry_ref[...] = jnp.zeros_like(carry_ref)
        ps_ref[...] = jnp.zeros_like(ps_ref)
        er = lax.broadcasted_iota(jnp.int32, ones_ref.shape, 0) // A_HD
        ec = lax.broadcasted_iota(jnp.int32, ones_ref.shape, 1) // A_HD
        ones_ref[...] = jnp.where(er == ec, 1.0 / A_HD, 0.0).astype(BF16)
        row = lax.broadcasted_iota(jnp.int32, (rows4, 2 * blk), 0)
        dist = row % blk + blk - lax.broadcasted_iota(jnp.int32, (rows4, 2 * blk), 1)
        in_window = (dist >= 0) & (dist < WINDOW)
        for kv in range(A_KV):
            slope = jnp.zeros((rows4, 2 * blk), F32)
            for g in range(A_GROUP):
                slope = jnp.where(row // blk == g, _alibi_slope(kv * A_GROUP + g), slope)
            bias_ref[kv] = jnp.where(in_window, slope * dist.astype(F32), -NEG_INF)
        for hb in range(B_HEADS):
            lg = _ret_log_gamma(hb)
            ii = lax.broadcasted_iota(jnp.int32, (blk, blk), 0).astype(F32)
            jj = lax.broadcasted_iota(jnp.int32, (blk, blk), 1).astype(F32)
            diff = ii - jj
            dec_ref[hb, 0] = jnp.where(diff >= 0, jnp.exp(lg * jnp.maximum(diff, 0.0)), 0.0)
            dec_ref[hb, 1] = jnp.exp(lg * (ii + 1.0))
            dec_ref[hb, 2] = jnp.exp(lg * (blk - 1.0 - jj))

    x = x_ref[...]
    sb, _, d = x.shape
    tb = sb * ROWS
    mod = mod_ref[...]
    h = _modulate(x, mod, gn_ref[...], 0).reshape(tb, d).astype(BF16)
    proj = _dot(h, win_ref[...])
    kt_all = _dot_nt(wkt_ref[...], h) * (B_KD ** -0.5)
    rg = rg_ref[...]

    qa = proj[:, 0:QA_W]
    ka = proj[:, P_OFF_KA:P_OFF_KA + 2 * KA_W]
    va = proj[:, P_OFF_VA:P_OFF_VA + 2 * VA_W]
    q_hat = (qa * lax.rsqrt(_dot((qa * qa).astype(BF16), ones_ref[...]) + EPS) * qg_ref[...]).astype(BF16)
    k_hat = ka * lax.rsqrt(_dot((ka * ka).astype(BF16), ones_ref[0:2 * KA_W, 0:2 * KA_W]) + EPS) * kg_ref[...]
    k_hat16 = k_hat.astype(BF16)
    va16 = va.astype(BF16)
    prev = carry_ref[step % 2]
    carry_ref[(step + 1) % 2] = jnp.concatenate([k_hat16[tb - blk:tb], va16[tb - blk:tb]], axis=1)

    mean_mat = jnp.full((B_VD, B_VD), 1.0 / B_VD, BF16)
    row_g = lax.broadcasted_iota(jnp.int32, (rows4, 1), 0) // blk
    key_is_prev = lax.broadcasted_iota(jnp.int32, (rows4, 2 * blk), 1) < blk
    first_penalty = jnp.where(step == 0, -NEG_INF, 0.0)
    lane_low = lax.broadcasted_iota(jnp.int32, (blk, PAIR), 1) < A_HD
    ones_cols = jnp.ones((2 * blk, PAIR), BF16)

    n_blk = tb // blk
    att = [(j, kv) for j in range(n_blk) for kv in range(A_KV)]
    ret = [(j, hb) for j in range(n_blk) for hb in range(B_HEADS)]

    sinks, scores = {}, {}
    for j, kv in att:
        r0 = j * blk
        kcol = slice(kv * PAIR, (kv + 1) * PAIR)
        if j == 0:
            k2 = jnp.concatenate([prev[:, kcol], k_hat16[0:blk, kcol]], axis=0)
        else:
            k2 = k_hat16[r0 - blk:r0 + blk, kcol]
        q4 = jnp.concatenate(
            [jnp.where(lane_low == (g % 2 == 0),
                       q_hat[r0:r0 + blk, (kv * A_GROUP + g - g % 2) * A_HD:(kv * A_GROUP + g - g % 2 + 2) * A_HD],
                       jnp.zeros((), BF16))
             for g in range(A_GROUP)], axis=0)
        scores[j, kv] = _dot_nt(q4, k2)
        sink = jnp.zeros((rows4, 1), F32)
        for g in range(A_GROUP):
            sink = jnp.where(row_g == g, sink_ref[kv * A_GROUP + g], sink)
        sinks[j, kv] = sink
    qb, vb, kt, inner = {}, {}, {}, {}
    for j, hb in ret:
        r0 = j * blk
        qb[j, hb] = proj[r0:r0 + blk, P_OFF_QB + hb * B_KD:P_OFF_QB + (hb + 1) * B_KD].astype(BF16)
        vb[j, hb] = proj[r0:r0 + blk, P_OFF_VB + hb * B_VD:P_OFF_VB + (hb + 1) * B_VD].astype(BF16)
        kt[j, hb] = kt_all[hb * B_KD:(hb + 1) * B_KD, r0:r0 + blk]
        inner[j, hb] = _dot(qb[j, hb], kt[j, hb].astype(BF16))

    probs, maxes = {}, {}
    for j, kv in att:
        s = scores[j, kv] * (A_HD ** -0.5) - bias_ref[kv]
        if j == 0:
            s = s - jnp.where(key_is_prev, first_penalty, 0.0)
        mx = jnp.maximum(jnp.max(s, axis=-1, keepdims=True), sinks[j, kv])
        probs[j, kv] = jnp.exp(s - mx).astype(BF16)
        maxes[j, kv] = mx
    state = {}
    for hb in range(B_HEADS):
        d_c = math.exp(_ret_log_gamma(hb) * blk)
        state[0, hb] = ps_ref[hb]
        for j in range(n_blk):
            state[j + 1, hb] = state[j, hb] * d_c + _dot((kt[j, hb] * dec_ref[hb, 2]).astype(BF16), vb[j, hb])
        ps_ref[hb] = state[n_blk, hb]

    for j, kv in att:
        r0 = j * blk
        kcol = slice(kv * PAIR, (kv + 1) * PAIR)
        vcol = slice(2 * KA_W + kv * PAIR, 2 * KA_W + (kv + 1) * PAIR)
        if j == 0:
            v2 = jnp.concatenate([prev[:, vcol], va16[0:blk, kcol]], axis=0)
        else:
            v2 = va16[r0 - blk:r0 + blk, kcol]
        pv = _dot(probs[j, kv], jnp.concatenate([v2, ones_cols], axis=1))
        o4 = pv[:, 0:PAIR] / (pv[:, PAIR:2 * PAIR] + jnp.exp(sinks[j, kv] - maxes[j, kv]))
        for g in range(A_GROUP):
            hd = kv * A_GROUP + g
            half = slice((hd % 2) * A_HD, (hd % 2 + 1) * A_HD)
            mix_ref[r0:r0 + blk, hd * A_HD:(hd + 1) * A_HD] = o4[g * blk:(g + 1) * blk, half]
    o_ret = {}
    for j, hb in ret:
        o_ret[j, hb] = (_dot((inner[j, hb] * dec_ref[hb, 0]).astype(BF16), vb[j, hb])
                        + _dot(qb[j, hb], state[j, hb].astype(BF16)) * dec_ref[hb, 1])

    cen = {k: o_ret[k] - _dot(o_ret[k].astype(BF16), mean_mat) for k in ret}
    var = {k: _dot((cen[k] * cen[k]).astype(BF16), mean_mat) for k in ret}
    for j, hb in ret:
        r0 = j * blk
        gb = proj[r0:r0 + blk, P_OFF_GB + hb * B_VD:P_OFF_GB + (hb + 1) * B_VD]
        mix_ref[r0:r0 + blk, QA_W + hb * B_VD:QA_W + (hb + 1) * B_VD] = (
            cen[j, hb] * lax.rsqrt(var[j, hb] + EPS) * rg[:, hb * B_VD:(hb + 1) * B_VD] * (gb * jax.nn.sigmoid(gb)))

    out = _dot(mix_ref[...].astype(BF16), wout_ref[...])
    o_ref[...] = x + _gate(mod, 0, d) * out.reshape(sb, ROWS, d)

    @pl.when(step == pl.num_programs(0) - 1)
    def _():
        pk_ref[...] = _dedup(k_hat[tb - blk:tb, :])
        pv_ref[...] = _dedup(va[tb - blk:tb, :])


def _dup_heads(t, width):
    lead = t.shape[:-1]
    t = t.reshape(lead + (-1, 1, width))
    return jnp.broadcast_to(t, lead + (t.shape[-3], 2, width)).reshape(lead + (-1,))


def _mix_even_prompt(x3, mod3, gain, w_in, q_gain, k_gain, sinks, ret_gain, w_out, sb):
    n8, _, d = x3.shape
    tb = sb * ROWS
    w_main = jnp.concatenate([w_in[:, :OFF_KA], _dup_heads(w_in[:, OFF_KA:OFF_VA], A_HD),
                              _dup_heads(w_in[:, OFF_VA:OFF_QB], A_HD), w_in[:, OFF_QB:OFF_KB],
                              w_in[:, OFF_VB:]], axis=1)
    wk_t = w_in[:, OFF_KB:OFF_VB].T
    return pl.pallas_call(
        _mix_even_prompt_body,
        grid=(n8 // sb,),
        in_specs=[pl.BlockSpec((sb, ROWS, d), lambda i: (i, 0, 0)),
                  pl.BlockSpec((1, 1, 6 * d), lambda i: (0, 0, 0)),
                  _const_spec((1, d)), _const_spec((d, P_WIDTH)), _const_spec((KB_W, d)),
                  _const_spec((1, QA_W)), _const_spec((1, 2 * KA_W)),
                  pl.BlockSpec(memory_space=pltpu.SMEM),
                  _const_spec((1, VB_W)), _const_spec((QA_W + VB_W, d))],
        out_specs=[pl.BlockSpec((sb, ROWS, d), lambda i: (i, 0, 0)),
                   pl.BlockSpec((WINDOW, KA_W), lambda i: (0, 0)),
                   pl.BlockSpec((WINDOW, VA_W), lambda i: (0, 0)),
                   pl.BlockSpec((B_HEADS, B_KD, B_VD), lambda i: (0, 0, 0))],
        out_shape=[jax.ShapeDtypeStruct(x3.shape, F32),
                   jax.ShapeDtypeStruct((WINDOW, KA_W), F32),
                   jax.ShapeDtypeStruct((WINDOW, VA_W), F32),
                   jax.ShapeDtypeStruct((B_HEADS, B_KD, B_VD), F32)],
        scratch_shapes=[pltpu.VMEM((tb, QA_W + VB_W), F32),
                        pltpu.VMEM((2, WINDOW, 2 * KA_W + 2 * VA_W), BF16),
                        pltpu.VMEM((A_KV, A_GROUP * WINDOW, 2 * WINDOW), F32),
                        pltpu.VMEM((B_HEADS, 3, WINDOW, WINDOW), F32),
                        pltpu.VMEM((QA_W, QA_W), BF16)],
        compiler_params=_params(),
        name="mix_even_prompt",
    )(x3, mod3, gain.reshape(1, d), w_main, wk_t, jnp.tile(q_gain, A_HEADS).reshape(1, QA_W),
      jnp.tile(k_gain, 2 * A_KV).reshape(1, 2 * KA_W),
      sinks, ret_gain.reshape(1, VB_W), w_out)


def _mix_even_sample_body(x_ref, mod_ref, gn_ref, win_ref, qg_ref, kg_ref, sink_ref, rg_ref, wout_ref,
                          ck_ref, cv_ref, s0_ref,
                          o_ref, nk_ref, nv_ref, ns_ref, mix_ref):
    x = x_ref[...]
    sb, length, d = x.shape
    tb = sb * length
    w = ck_ref.shape[1]
    mod = mod_ref[...]
    h = _modulate(x, mod, gn_ref[...], 0).reshape(tb, d).astype(BF16)
    proj = _dot(h, win_ref[...])
    qg = qg_ref[...]
    kg = kg_ref[...]
    rg = rg_ref[...]

    rows = A_GROUP * length
    qpos_c = lax.broadcasted_iota(jnp.int32, (rows, w), 0) % length
    kpos_c = lax.broadcasted_iota(jnp.int32, (rows, w), 1)
    dist_c = w + qpos_c - kpos_c
    valid_c = (dist_c >= 0) & (dist_c < WINDOW)
    qpos_n = lax.broadcasted_iota(jnp.int32, (rows, length), 0) % length
    kpos_n = lax.broadcasted_iota(jnp.int32, (rows, length), 1)
    dist_n = qpos_n - kpos_n
    valid_n = (dist_n >= 0) & (dist_n < WINDOW)
    row_g = lax.broadcasted_iota(jnp.int32, (rows, 1), 0) // length

    for kv in range(A_KV):
        lanes = slice(kv * A_HD, (kv + 1) * A_HD)
        kn = _head_rms(proj[:, OFF_KA + kv * A_HD:OFF_KA + (kv + 1) * A_HD], kg).reshape(sb, length, A_HD)
        vn = proj[:, OFF_VA + kv * A_HD:OFF_VA + (kv + 1) * A_HD].reshape(sb, length, A_HD)
        nk_ref[:, 0:w - length, lanes] = ck_ref[:, length:w, lanes]
        nv_ref[:, 0:w - length, lanes] = cv_ref[:, length:w, lanes]
        nk_ref[:, w - length:w, lanes] = kn
        nv_ref[:, w - length:w, lanes] = vn
        kc = ck_ref[:, :, lanes].astype(BF16)
        vc = cv_ref[:, :, lanes].astype(BF16)
        q4 = jnp.concatenate(
            [_head_rms(proj[:, OFF_QA + (kv * A_GROUP + g) * A_HD:OFF_QA + (kv * A_GROUP + g + 1) * A_HD], qg)
             .reshape(sb, length, A_HD) for g in range(A_GROUP)], axis=1).astype(BF16)
        slope = jnp.zeros((rows, 1), F32)
        sink = jnp.zeros((rows, 1), F32)
        for g in range(A_GROUP):
            hd = kv * A_GROUP + g
            slope = jnp.where(row_g == g, _alibi_slope(hd), slope)
            sink = jnp.where(row_g == g, sink_ref[hd], sink)
        scale = A_HD ** -0.5
        s_c = _bmm_nt(q4, kc) * scale - slope * dist_c.astype(F32)
        s_c = jnp.where(valid_c, s_c, NEG_INF)
        s_n = _bmm_nt(q4, kn.astype(BF16)) * scale - slope * dist_n.astype(F32)
        s_n = jnp.where(valid_n, s_n, NEG_INF)
        mx = jnp.maximum(jnp.maximum(jnp.max(s_c, axis=-1, keepdims=True),
                                     jnp.max(s_n, axis=-1, keepdims=True)), sink)
        p_c = jnp.exp(s_c - mx)
        p_n = jnp.exp(s_n - mx)
        den = (jnp.sum(p_c, axis=-1, keepdims=True) + jnp.sum(p_n, axis=-1, keepdims=True)
               + jnp.exp(sink - mx))
        o4 = (_bmm(p_c.astype(BF16), vc) + _bmm(p_n.astype(BF16), vn.astype(BF16))) / den
        for g in range(A_GROUP):
            hd = kv * A_GROUP + g
            mix_ref[:, hd * A_HD:(hd + 1) * A_HD] = o4[:, g * length:(g + 1) * length, :].reshape(tb, A_HD)

    for hb in range(B_HEADS):
        d_in, d_q, d_k, d_c = _ret_decay(hb, length)
        qb = proj[:, OFF_QB + hb * B_KD:OFF_QB + (hb + 1) * B_KD].reshape(sb, length, B_KD).astype(BF16)
        kb = proj[:, OFF_KB + hb * B_KD:OFF_KB + (hb + 1) * B_KD].reshape(sb, length, B_KD) * (B_KD ** -0.5)
        vb = proj[:, OFF_VB + hb * B_VD:OFF_VB + (hb + 1) * B_VD].reshape(sb, length, B_VD).astype(BF16)
        gb = proj[:, OFF_GB + hb * B_VD:OFF_GB + (hb + 1) * B_VD]
        state = s0_ref[:, hb]
        inner = _bmm_nt(qb, kb.astype(BF16)) * d_in
        o = _bmm(inner.astype(BF16), vb) + _bmm(qb, state.astype(BF16)) * d_q
        ns_ref[:, hb] = state * d_c + _bmm_tn((kb * d_k).astype(BF16), vb)
        mix_ref[:, QA_W + hb * B_VD:QA_W + (hb + 1) * B_VD] = _group_norm_gate(
            o.reshape(tb, B_VD), rg[:, hb * B_VD:(hb + 1) * B_VD], gb)

    out = _dot(mix_ref[...].astype(BF16), wout_ref[...])
    o_ref[...] = x + _gate(mod, 0, d) * out.reshape(sb, length, d)


def _mix_even_sample(x3, mod3, gain, w_in, q_gain, k_gain, sinks, ret_gain, w_out, cache_k, cache_v, state, sb):
    n, length, d = x3.shape
    in_w = w_in.shape[1]
    w = cache_k.shape[1]
    seq_spec = lambda shape: pl.BlockSpec((sb,) + shape, lambda i, _n=len(shape): (i,) + (0,) * _n)
    return pl.pallas_call(
        _mix_even_sample_body,
        grid=(n // sb,),
        in_specs=[seq_spec((length, d)), seq_spec((1, 6 * d)),
                  _const_spec((1, d)), _const_spec((d, in_w)),
                  _const_spec((1, A_HD)), _const_spec((1, A_HD)),
                  pl.BlockSpec(memory_space=pltpu.SMEM),
                  _const_spec((1, VB_W)), _const_spec((QA_W + VB_W, d)),
                  seq_spec((w, KA_W)), seq_spec((w, VA_W)), seq_spec((B_HEADS, B_KD, B_VD))],
        out_specs=[seq_spec((length, d)), seq_spec((w, KA_W)), seq_spec((w, VA_W)),
                   seq_spec((B_HEADS, B_KD, B_VD))],
        out_shape=[jax.ShapeDtypeStruct(x3.shape, F32),
                   jax.ShapeDtypeStruct(cache_k.shape, F32),
                   jax.ShapeDtypeStruct(cache_v.shape, F32),
                   jax.ShapeDtypeStruct(state.shape, F32)],
        scratch_shapes=[pltpu.VMEM((sb * length, QA_W + VB_W), F32)],
        compiler_params=_params(),
        name="mix_even_sample",
    )(x3, mod3, gain.reshape(1, d), w_in, q_gain.reshape(1, A_HD), k_gain.reshape(1, A_HD),
      sinks, ret_gain.reshape(1, VB_W), w_out, cache_k, cache_v, state)


def _cmul(ar, ai, br, bi):
    return ar * br - ai * bi, ar * bi + ai * br


def _s5_prep_body(t_len, n_chunks, are_ref, aim_ref, ldt_ref, btr_ref, bti_ref,
                  lre_ref, lim_ref, bbr_ref, bbi_ref, pwr_ref, pwi_ref):
    a_re = are_ref[...]
    a_im = aim_ref[...]
    dt = jnp.exp(ldt_ref[...])
    mag = jnp.exp(a_re * dt)
    lam_re = mag * jnp.cos(a_im * dt)
    lam_im = mag * jnp.sin(a_im * dt)
    den = a_re * a_re + a_im * a_im
    n_re = lam_re - 1.0
    n_im = lam_im
    f_re = (n_re * a_re + n_im * a_im) / den
    f_im = (n_im * a_re - n_re * a_im) / den
    br = btr_ref[...]
    bi = bti_ref[...]
    lre_ref[...] = lam_re
    lim_ref[...] = lam_im
    bbr_ref[...] = f_re * br - f_im * bi
    bbi_ref[...] = f_re * bi + f_im * br
    cr, ci = lam_re, lam_im
    for t in range(t_len):
        pwr_ref[t] = cr
        pwi_ref[t] = ci
        if t + 1 < t_len:
            cr, ci = _cmul(cr, ci, lam_re, lam_im)
    tr, ti = cr, ci
    cr, ci = jnp.ones_like(lam_re), jnp.zeros_like(lam_im)
    for m in range(n_chunks + 1):
        pwr_ref[t_len + m] = cr
        pwi_ref[t_len + m] = ci
        cr, ci = _cmul(cr, ci, tr, ti)


def _s5_prep(a_re, a_im, log_dt, b_re, b_im, t_len, n_chunks):
    g, p = a_re.shape
    k = b_re.shape[-1]
    bt_re = jnp.swapaxes(b_re, 1, 2)
    bt_im = jnp.swapaxes(b_im, 1, 2)
    n_pow = t_len + n_chunks + 1
    out = pl.pallas_call(
        functools.partial(_s5_prep_body, t_len, n_chunks),
        out_shape=[jax.ShapeDtypeStruct((g, 1, p), F32), jax.ShapeDtypeStruct((g, 1, p), F32),
                   jax.ShapeDtypeStruct((g, k, p), F32), jax.ShapeDtypeStruct((g, k, p), F32),
                   jax.ShapeDtypeStruct((n_pow, g, 1, p), F32), jax.ShapeDtypeStruct((n_pow, g, 1, p), F32)],
        name="s5_prep",
    )(a_re.reshape(g, 1, p), a_im.reshape(g, 1, p), log_dt.reshape(g, 1, 1), bt_re, bt_im)
    return out


def _block_diag(t):
    g, a, b = t.shape
    t = t.reshape(g // S5_OCT, S5_OCT, a, b)
    eye = jnp.eye(S5_OCT, dtype=t.dtype)
    out = t[:, :, :, None, :] * eye[None, :, None, :, None]
    return out.reshape(g // S5_OCT, S5_OCT * a, S5_OCT * b)


def _gelu_glu_out(x, mod, y, u, dskip, glua_ref, glub_ref):
    sb, rows, d = x.shape
    y = y + dskip * u
    yg = jax.nn.gelu(y, approximate=True).astype(BF16)
    out = _dot(yg, glua_ref[...]) * jax.nn.sigmoid(_dot(yg, glub_ref[...]))
    return x + _gate(mod, 0, d) * out.reshape(sb, rows, d)


def _mix_odd_prompt_body(x_ref, xn_ref, mod_ref, gn_ref, bblk_ref, lam_ref, pwb_ref, ltp_ref, cre_ref, cim_ref,
                         dskip_ref, glua_ref, glub_ref, o_ref, hre_ref, him_ref,
                         un_ref, l0_ref, l1_ref, up0_ref, up1_ref, pt_ref):
    step = pl.program_id(0)
    sb, _, d = x_ref.shape
    tm = sb * ROWS
    t_len = tm // ROWS
    n_oct = bblk_ref.shape[0]
    half = bblk_ref.shape[2] // 2
    mod = mod_ref[...]

    def permuted_input(src_ref, up_dst):
        u = _modulate(src_ref[...], mod, gn_ref[...], 0).reshape(tm, d)
        for k in range(d // 128):
            un_ref[k] = u[:, k * 128:(k + 1) * 128]
        up = jnp.concatenate(
            [jnp.concatenate([un_ref[k, pl.ds(t, ROWS, stride=t_len), :] for k in range(d // 128)], axis=1)
             for t in range(t_len)], axis=0)
        up_dst[...] = up
        return up.astype(BF16)

    @pl.when(step == 0)
    def _():
        hre_ref[...] = jnp.zeros_like(hre_ref)
        him_ref[...] = jnp.zeros_like(him_ref)
        nat = lax.broadcasted_iota(jnp.int32, (tm, tm), 0)
        prm = lax.broadcasted_iota(jnp.int32, (tm, tm), 1)
        pt_ref[...] = jnp.where(prm == (nat % t_len) * ROWS + nat // t_len, 1.0, 0.0).astype(BF16)
        up16 = permuted_input(x_ref, up0_ref)
        for s in range(n_oct):
            l0_ref[s] = _dot(up16[:, s * 128:(s + 1) * 128], bblk_ref[s])

    def run(l_ref, l_next, up_ref, up_next):
        for s0 in range(0, n_oct, 2):
            pair = (s0, s0 + 1)
            lam_re = [jnp.broadcast_to(lam_ref[0, s:s + 1, :], (ROWS, half)) for s in pair]
            lam_im = [jnp.broadcast_to(lam_ref[1, s:s + 1, :], (ROWS, half)) for s in pair]

            def local_step(t, carry):
                r = pl.multiple_of(t * ROWS, ROWS)
                out = []
                for i, s in enumerate(pair):
                    h_re, h_im = carry[2 * i], carry[2 * i + 1]
                    n_re = lam_re[i] * h_re - lam_im[i] * h_im + l_ref[s, pl.ds(r, ROWS), 0:half]
                    n_im = lam_re[i] * h_im + lam_im[i] * h_re + l_ref[s, pl.ds(r, ROWS), half:2 * half]
                    l_ref[s, pl.ds(r, ROWS), 0:half] = n_re
                    l_ref[s, pl.ds(r, ROWS), half:2 * half] = n_im
                    out += [n_re, n_im]
                return tuple(out)

            zero = jnp.zeros((ROWS, half), F32)
            lax.fori_loop(0, t_len, local_step, (zero, zero, zero, zero), unroll=2)

        next16 = permuted_input(xn_ref, up_next)
        chunk = lax.broadcasted_iota(jnp.int32, (ROWS, half), 0)
        y_parts = []
        for s in range(n_oct):
            l_next[s] = _dot(next16[:, s * 128:(s + 1) * 128], bblk_ref[s])
            p_re = l_ref[s, tm - ROWS:tm, 0:half]
            p_im = l_ref[s, tm - ROWS:tm, half:2 * half]
            for i, sh in enumerate((1, 2, 4)):
                m_re, m_im = _cmul(ltp_ref[s, i:i + 1, 0:half], ltp_ref[s, i:i + 1, half:2 * half],
                                   jnp.where(chunk >= sh, pltpu.roll(p_re, sh, 0), 0.0),
                                   jnp.where(chunk >= sh, pltpu.roll(p_im, sh, 0), 0.0))
                p_re, p_im = p_re + m_re, p_im + m_im
            hin_re = jnp.broadcast_to(hre_ref[s:s + 1, :], (ROWS, half))
            hin_im = jnp.broadcast_to(him_ref[s:s + 1, :], (ROWS, half))
            m_re, m_im = _cmul(ltp_ref[s, ROWS:2 * ROWS, 0:half], ltp_ref[s, ROWS:2 * ROWS, half:2 * half],
                               hin_re, hin_im)
            st_re = m_re + jnp.where(chunk >= 1, pltpu.roll(p_re, 1, 0), 0.0)
            st_im = m_im + jnp.where(chunk >= 1, pltpu.roll(p_im, 1, 0), 0.0)
            m_re, m_im = _cmul(ltp_ref[s, 3:4, 0:half], ltp_ref[s, 3:4, half:2 * half],
                               hre_ref[s:s + 1, :], him_ref[s:s + 1, :])
            hre_ref[s:s + 1, :] = m_re + p_re[ROWS - 1:ROWS, :]
            him_ref[s:s + 1, :] = m_im + p_im[ROWS - 1:ROWS, :]
            loc = l_ref[s].reshape(t_len, ROWS, 2 * half)
            pw = pwb_ref[s]
            f_re, f_im = _cmul(pw[:, :, 0:half], pw[:, :, half:2 * half], st_re[None], st_im[None])
            hs_re = (loc[:, :, 0:half] + f_re).reshape(tm, half).astype(BF16)
            hs_im = (loc[:, :, half:2 * half] + f_im).reshape(tm, half).astype(BF16)
            y_parts.append(_dot(hs_re, cre_ref[s]) - _dot(hs_im, cim_ref[s]))

        y = jnp.concatenate(y_parts, axis=1) + dskip_ref[...] * up_ref[...]
        yg = jax.nn.gelu(y, approximate=True).astype(BF16)
        yn = _dot(pt_ref[...], yg).astype(BF16)
        out = _dot(yn, glua_ref[...]) * jax.nn.sigmoid(_dot(yn, glub_ref[...]))
        o_ref[...] = x_ref[...] + _gate(mod, 0, d) * out.reshape(sb, ROWS, d)

    @pl.when(step % 2 == 0)
    def _():
        run(l0_ref, l1_ref, up0_ref, up1_ref)

    @pl.when(step % 2 == 1)
    def _():
        run(l1_ref, l0_ref, up1_ref, up0_ref)


def _mix_odd_prompt(x3, mod3, gain, bblk, lam_d, pwb, ltp, cre, cim, dskip, glu_a, glu_b, sb):
    n8, _, d = x3.shape
    n_oct, kin, wid = bblk.shape
    half = wid // 2
    tm = sb * ROWS
    n_tiles = n8 // sb
    proj_buf = pltpu.VMEM((n_oct, tm, wid), F32)
    perm_buf = pltpu.VMEM((tm, d), F32)
    return pl.pallas_call(
        _mix_odd_prompt_body,
        grid=(n_tiles,),
        in_specs=[pl.BlockSpec((sb, ROWS, d), lambda i: (i, 0, 0)),
                  pl.BlockSpec((sb, ROWS, d), lambda i: (jnp.minimum(i + 1, n_tiles - 1), 0, 0)),
                  pl.BlockSpec((1, 1, 6 * d), lambda i: (0, 0, 0)),
                  _const_spec((1, d)), _const_spec(bblk.shape), _const_spec(lam_d.shape),
                  _const_spec(pwb.shape), _const_spec(ltp.shape),
                  _const_spec(cre.shape), _const_spec(cim.shape), _const_spec((1, d)),
                  _const_spec((d, d)), _const_spec((d, d))],
        out_specs=[pl.BlockSpec((sb, ROWS, d), lambda i: (i, 0, 0)),
                   pl.BlockSpec((n_oct, half), lambda i: (0, 0)),
                   pl.BlockSpec((n_oct, half), lambda i: (0, 0))],
        out_shape=[jax.ShapeDtypeStruct(x3.shape, F32),
                   jax.ShapeDtypeStruct((n_oct, half), F32),
                   jax.ShapeDtypeStruct((n_oct, half), F32)],
        scratch_shapes=[pltpu.VMEM((d // 128, tm, 128), F32), proj_buf, proj_buf, perm_buf, perm_buf,
                        pltpu.VMEM((tm, tm), BF16)],
        compiler_params=_params(),
        name="mix_odd_prompt",
    )(x3, x3, mod3, gain.reshape(1, d), bblk, lam_d, pwb, ltp, cre, cim, dskip.reshape(1, d), glu_a, glu_b)


def _mix_odd_sample_body(x_ref, mod_ref, gn_ref, bblk_ref, lam_ref, cre_ref, cim_ref, dskip_ref,
                         glua_ref, glub_ref, sre_ref, sim_ref, o_ref, nre_ref, nim_ref, d_ref, y_ref):
    x = x_ref[...]
    sb, length, d = x.shape
    tm = sb * length
    n_oct = bblk_ref.shape[0]
    half = bblk_ref.shape[2] // 2
    mod = mod_ref[...]
    u = _modulate(x, mod, gn_ref[...], 0).reshape(tm, d)
    u16 = u.astype(BF16)
    n_ch = bblk_ref.shape[2] // 128
    hc = n_ch // 2
    for s in range(n_oct):
        bu = _dot(u16[:, s * 128:(s + 1) * 128], bblk_ref[s])
        for c in range(n_ch):
            d_ref[c] = bu[:, c * 128:(c + 1) * 128]
        lam_re = lam_ref[0, s:s + 1, :]
        lam_im = lam_ref[1, s:s + 1, :]
        h_re = sre_ref[:, s * half:(s + 1) * half]
        h_im = sim_ref[:, s * half:(s + 1) * half]
        for t in range(length):
            b_re = jnp.concatenate([d_ref[c, pl.ds(t, sb, stride=length), :] for c in range(hc)], axis=1)
            b_im = jnp.concatenate([d_ref[hc + c, pl.ds(t, sb, stride=length), :] for c in range(hc)], axis=1)
            n_re = lam_re * h_re - lam_im * h_im + b_re
            n_im = lam_re * h_im + lam_im * h_re + b_im
            for c in range(hc):
                d_ref[c, pl.ds(t, sb, stride=length), :] = n_re[:, c * 128:(c + 1) * 128]
                d_ref[hc + c, pl.ds(t, sb, stride=length), :] = n_im[:, c * 128:(c + 1) * 128]
            h_re, h_im = n_re, n_im
        nre_ref[:, s * half:(s + 1) * half] = h_re
        nim_ref[:, s * half:(s + 1) * half] = h_im
        hs_re = jnp.concatenate([d_ref[c] for c in range(hc)], axis=1)
        hs_im = jnp.concatenate([d_ref[hc + c] for c in range(hc)], axis=1)
        y_ref[:, s * 128:(s + 1) * 128] = (_dot(hs_re.astype(BF16), cre_ref[s])
                                           - _dot(hs_im.astype(BF16), cim_ref[s]))
    o_ref[...] = _gelu_glu_out(x, mod, y_ref[...], u, dskip_ref[...], glua_ref, glub_ref)


def _mix_odd_sample(x3, mod3, gain, bblk, lam_d, cre, cim, dskip, glu_a, glu_b, s_re, s_im, sb):
    n, length, d = x3.shape
    n_oct, kin, wid = bblk.shape
    tm = sb * length
    nstate = s_re.shape[1]
    return pl.pallas_call(
        _mix_odd_sample_body,
        grid=(n // sb,),
        in_specs=[pl.BlockSpec((sb, length, d), lambda i: (i, 0, 0)),
                  pl.BlockSpec((sb, 1, 6 * d), lambda i: (i, 0, 0)),
                  _const_spec((1, d)), _const_spec(bblk.shape), _const_spec(lam_d.shape),
                  _const_spec(cre.shape), _const_spec(cim.shape), _const_spec((1, d)),
                  _const_spec((d, d)), _const_spec((d, d)),
                  pl.BlockSpec((sb, nstate), lambda i: (i, 0)),
                  pl.BlockSpec((sb, nstate), lambda i: (i, 0))],
        out_specs=[pl.BlockSpec((sb, length, d), lambda i: (i, 0, 0)),
                   pl.BlockSpec((sb, nstate), lambda i: (i, 0)),
                   pl.BlockSpec((sb, nstate), lambda i: (i, 0))],
        out_shape=[jax.ShapeDtypeStruct(x3.shape, F32),
                   jax.ShapeDtypeStruct(s_re.shape, F32),
                   jax.ShapeDtypeStruct(s_im.shape, F32)],
        scratch_shapes=[pltpu.VMEM((wid // 128, tm, 128), F32), pltpu.VMEM((tm, d), F32)],
        compiler_params=_params(),
        name="mix_odd_sample",
    )(x3, mod3, gain.reshape(1, d), bblk, lam_d, cre, cim, dskip.reshape(1, d), glu_a, glu_b, s_re, s_im)


def _pick(n, want):
    while n % want:
        want //= 2
    return max(want, 1)


def kernel(x_prompt, x_sample, cache_win_k, cache_win_v, state_ret, state_s5_re, state_s5_im, c_prompt, c_sample, ada_w, ada_b, norm_mix, norm_ffn, ffn_wg, ffn_wu, ffn_wd, even_w_in, even_q_gain, even_k_gain, even_sinks, even_ret_gain, even_w_out, odd_A_re, odd_A_im, odd_log_dt, odd_B_re, odd_B_im, odd_C_re, odd_C_im, odd_D, odd_glu_a, odd_glu_b):
    bp, lp, d = x_prompt.shape
    ns, ls, _ = x_sample.shape
    assert bp == 1 and ls == ROWS and lp % WINDOW == 0
    w = cache_win_k.shape[2]
    groups, p_state = odd_A_re.shape[1:]

    n_c = bp + ns
    n_pad = -n_c % ROWS
    c_all = jnp.concatenate([c_prompt, c_sample, jnp.zeros((n_pad, d), F32)], axis=0)
    mod = _adaln(c_all, ada_w, ada_b)
    mod_p = [mod[l, 0:1].reshape(1, 1, 6 * d) for l in range(2)]
    mod_s = [mod[l, bp:bp + ns].reshape(ns, 1, 6 * d) for l in range(2)]

    bf = lambda t: t.astype(BF16)
    w_in, w_out = bf(even_w_in[0]), bf(even_w_out[0])
    wg, wu, wd = bf(ffn_wg), bf(ffn_wu), bf(ffn_wd)
    glu_a, glu_b = bf(odd_glu_a[0]), bf(odd_glu_b[0])

    xp = x_prompt.reshape(lp // ROWS, ROWS, d)
    xs = x_sample

    sb_p = _pick(lp // ROWS, 32)
    sb_ffn = _pick(lp // ROWS, 64)
    sb_s = _pick(ns, 32)
    sb_s_even = _pick(ns, 16)

    xp, p_k, p_v, p_ret = _mix_even_prompt(xp, mod_p[0], norm_mix[0], w_in, even_q_gain[0], even_k_gain[0],
                                           even_sinks[0], even_ret_gain[0], w_out, sb_p)
    xs, s_k, s_v, s_ret = _mix_even_sample(xs, mod_s[0], norm_mix[0], w_in, even_q_gain[0], even_k_gain[0],
                                           even_sinks[0], even_ret_gain[0], w_out,
                                           cache_win_k[0].reshape(ns, w, KA_W), cache_win_v[0].reshape(ns, w, VA_W),
                                           state_ret[0], sb_s_even)
    xp = _ffn(xp, mod_p[0], norm_ffn[0], wg[0], wu[0], wd[0], sb_ffn)
    xs = _ffn(xs, mod_s[0], norm_ffn[0], wg[0], wu[0], wd[0], sb_s)

    t_len = sb_p
    lam_re, lam_im, bbt_re, bbt_im, pw_re, pw_im = _s5_prep(odd_A_re[0], odd_A_im[0], odd_log_dt[0],
                                                            odd_B_re[0], odd_B_im[0], t_len, ROWS)
    n_oct = groups // S5_OCT
    half = S5_OCT * p_state
    bblk = bf(jnp.concatenate([_block_diag(bbt_re), _block_diag(bbt_im)], axis=-1))
    cre = bf(_block_diag(jnp.swapaxes(odd_C_re[0], 1, 2)))
    cim = bf(_block_diag(jnp.swapaxes(odd_C_im[0], 1, 2)))
    lam_d = jnp.stack([lam_re.reshape(n_oct, half), lam_im.reshape(n_oct, half)])

    pw_d = jnp.concatenate([pw_re.reshape(-1, n_oct, half), pw_im.reshape(-1, n_oct, half)], axis=-1)
    pwb = jnp.broadcast_to(jnp.swapaxes(pw_d[:t_len], 0, 1)[:, :, None, :], (n_oct, t_len, ROWS, 2 * half))
    lt = jnp.swapaxes(pw_d[t_len:], 0, 1)
    ltp = jnp.concatenate([lt[:, 1:2], lt[:, 2:3], lt[:, 4:5], lt[:, 8:9], jnp.zeros_like(lt[:, 0:4]), lt[:, 0:8]],
                          axis=1)
    xp, p_re, p_im = _mix_odd_prompt(xp, mod_p[1], norm_mix[1], bblk, lam_d, pwb, ltp, cre, cim, odd_D[0],
                                     glu_a, glu_b, sb_p)
    xs, s_re, s_im = _mix_odd_sample(xs, mod_s[1], norm_mix[1], bblk, lam_d, cre, cim, odd_D[0], glu_a, glu_b,
                                     state_s5_re[0].reshape(ns, groups * p_state),
                                     state_s5_im[0].reshape(ns, groups * p_state), sb_s)
    xp = _ffn(xp, mod_p[1], norm_ffn[1], wg[1], wu[1], wd[1], sb_ffn)
    xs = _ffn(xs, mod_s[1], norm_ffn[1], wg[1], wu[1], wd[1], sb_s)

    y_prompt = xp.reshape(bp, lp, d)
    y_sample = xs
    return (y_prompt, y_sample,
            p_k.reshape(1, bp, WINDOW, A_KV, A_HD), p_v.reshape(1, bp, WINDOW, A_KV, A_HD),
            p_ret.reshape(1, bp, B_HEADS, B_KD, B_VD),
            p_re.reshape(1, bp, groups, p_state), p_im.reshape(1, bp, groups, p_state),
            s_k.reshape(1, ns, w, A_KV, A_HD), s_v.reshape(1, ns, w, A_KV, A_HD),
            s_ret.reshape(1, ns, B_HEADS, B_KD, B_VD),
            s_re.reshape(1, ns, groups, p_state), s_im.reshape(1, ns, groups, p_state))
```

```python
import functools
import math

import jax
import jax.numpy as jnp
from jax import lax
from jax.experimental import pallas as pl
from jax.experimental.pallas import tpu as pltpu

F32 = jnp.float32
BF16 = jnp.bfloat16

EPS = 1e-6
NEG_INF = -1e30
ROWS = 8

A_HEADS, A_KV, A_GROUP, A_HD = 8, 2, 4, 64
WINDOW = 128
B_HEADS, B_KD, B_VD = 4, 128, 128
S5_GROUP, S5_STATE = 16, 64
S5_OCT = 8

QA_W, KA_W, VA_W = A_HEADS * A_HD, A_KV * A_HD, A_KV * A_HD
QB_W, KB_W, VB_W, GB_W = B_HEADS * B_KD, B_HEADS * B_KD, B_HEADS * B_VD, B_HEADS * B_VD
OFF_QA = 0
OFF_KA = OFF_QA + QA_W
OFF_VA = OFF_KA + KA_W
OFF_QB = OFF_VA + VA_W
OFF_KB = OFF_QB + QB_W
OFF_VB = OFF_KB + KB_W
OFF_GB = OFF_VB + VB_W

VMEM_LIMIT = 56 * 1024 * 1024


def _ret_log_gamma(h):
    return math.log1p(-(2.0 ** (-5.0 - h)))


def _alibi_slope(h):
    return 2.0 ** (-8.0 * (h + 1) / A_HEADS)


def _const_spec(shape):
    nd = len(shape)
    return pl.BlockSpec(shape, lambda i, _n=nd: (0,) * _n, pipeline_mode=pl.Buffered(1))


def _params():
    return pltpu.CompilerParams(dimension_semantics=("arbitrary",), vmem_limit_bytes=VMEM_LIMIT)


def _dot(a, b):
    return jnp.dot(a, b, preferred_element_type=F32)


def _dot_nt(a, b):
    return lax.dot_general(a, b, (((1,), (1,)), ((), ())), preferred_element_type=F32)


def _dot_tn(a, b):
    return lax.dot_general(a, b, (((0,), (0,)), ((), ())), preferred_element_type=F32)


def _bmm(a, b):
    return lax.dot_general(a, b, (((2,), (1,)), ((0,), (0,))), preferred_element_type=F32)


def _bmm_nt(a, b):
    return lax.dot_general(a, b, (((2,), (2,)), ((0,), (0,))), preferred_element_type=F32)


def _bmm_tn(a, b):
    return lax.dot_general(a, b, (((1,), (1,)), ((0,), (0,))), preferred_element_type=F32)


def _rms(x, g):
    return x * lax.rsqrt(jnp.mean(x * x, axis=-1, keepdims=True) + EPS) * g


def _modulate(x3, mod, gain, which):
    d = x3.shape[-1]
    sh = mod[:, :, (3 * which) * d:(3 * which + 1) * d]
    sc = mod[:, :, (3 * which + 1) * d:(3 * which + 2) * d]
    return _rms(x3, gain) * (1.0 + sc) + sh


def _gate(mod, which, d):
    return mod[:, :, (3 * which + 2) * d:(3 * which + 3) * d]


def _adaln_body(c_ref, w_ref, b_ref, o_ref):
    c = c_ref[...]
    a = (c * jax.nn.sigmoid(c)).astype(BF16)
    o_ref[0] = _dot(a, w_ref[0].astype(BF16)) + b_ref[0]


def _adaln(c_all, ada_w, ada_b):
    depth, d, n = ada_w.shape
    r = c_all.shape[0]
    tn = 1536
    return pl.pallas_call(
        _adaln_body,
        grid=(depth, n // tn),
        in_specs=[pl.BlockSpec((r, d), lambda l, j: (0, 0)),
                  pl.BlockSpec((1, d, tn), lambda l, j: (l, 0, j)),
                  pl.BlockSpec((1, 1, tn), lambda l, j: (l, 0, j))],
        out_specs=pl.BlockSpec((1, r, tn), lambda l, j: (l, 0, j)),
        out_shape=jax.ShapeDtypeStruct((depth, r, n), F32),
        compiler_params=pltpu.CompilerParams(dimension_semantics=("arbitrary", "arbitrary"),
                                             vmem_limit_bytes=VMEM_LIMIT),
        name="adaln",
    )(c_all, ada_w, ada_b.reshape(depth, 1, n))


def _ffn_body(x_ref, mod_ref, gn_ref, wg_ref, wu_ref, wd_ref, o_ref):
    x = x_ref[...]
    sb, _, d = x.shape
    mod = mod_ref[...]
    h = _modulate(x, mod, gn_ref[...], 1).reshape(sb * ROWS, d).astype(BF16)
    a = _dot(h, wg_ref[...])
    b = _dot(h, wu_ref[...])
    act = (a * jax.nn.sigmoid(a) * b).astype(BF16)
    y = _dot(act, wd_ref[...])
    o_ref[...] = x + _gate(mod, 1, d) * y.reshape(sb, ROWS, d)


def _ffn(x3, mod3, gain, wg, wu, wd, sb):
    n8, _, d = x3.shape
    f = wg.shape[1]
    per_seq = mod3.shape[0] != 1
    mod_spec = (pl.BlockSpec((sb, 1, 6 * d), lambda i: (i, 0, 0)) if per_seq
                else pl.BlockSpec((1, 1, 6 * d), lambda i: (0, 0, 0)))
    return pl.pallas_call(
        _ffn_body,
        grid=(n8 // sb,),
        in_specs=[pl.BlockSpec((sb, ROWS, d), lambda i: (i, 0, 0)), mod_spec,
                  _const_spec((1, d)), _const_spec((d, f)), _const_spec((d, f)), _const_spec((f, d))],
        out_specs=pl.BlockSpec((sb, ROWS, d), lambda i: (i, 0, 0)),
        out_shape=jax.ShapeDtypeStruct(x3.shape, F32),
        compiler_params=_params(),
        name="ffn",
    )(x3, mod3, gain.reshape(1, d), wg, wu, wd)


def _head_rms(t, g):
    return t * lax.rsqrt(jnp.mean(t * t, axis=-1, keepdims=True) + EPS) * g


def _group_norm_gate(o, gain, gate):
    mu = jnp.mean(o, axis=-1, keepdims=True)
    var = jnp.mean(jnp.square(o - mu), axis=-1, keepdims=True)
    return (o - mu) * lax.rsqrt(var + EPS) * gain * (gate * jax.nn.sigmoid(gate))


def _ret_decay(hb, c):
    lg = _ret_log_gamma(hb)
    ii = lax.broadcasted_iota(jnp.int32, (c, c), 0)
    jj = lax.broadcasted_iota(jnp.int32, (c, c), 1)
    diff = (ii - jj).astype(F32)
    d_in = jnp.where(diff >= 0, jnp.exp(lg * jnp.maximum(diff, 0.0)), 0.0)
    row = lax.broadcasted_iota(jnp.int32, (c, B_KD), 0).astype(F32)
    d_q = jnp.exp(lg * (row + 1.0))
    d_k = jnp.exp(lg * (c - 1.0 - row))
    d_c = math.exp(lg * c)
    return d_in, d_q, d_k, d_c


P_OFF_KA = QA_W
P_OFF_VA = P_OFF_KA + 2 * KA_W
P_OFF_QB = P_OFF_VA + 2 * VA_W
P_OFF_VB = P_OFF_QB + QB_W
P_OFF_GB = P_OFF_VB + VB_W
P_WIDTH = P_OFF_GB + GB_W
PAIR = 2 * A_HD


def _dedup(t):
    low = lax.broadcasted_iota(jnp.int32, (t.shape[0], PAIR), 1) < A_HD
    return jnp.where(low, t[:, 0:PAIR], t[:, PAIR:2 * PAIR])


def _mix_even_prompt_body(x_ref, mod_ref, gn_ref, win_ref, wkt_ref, qg_ref, kg_ref, sink_ref, rg_ref, wout_ref,
                          o_ref, pk_ref, pv_ref, ps_ref, mix_ref, carry_ref, bias_ref, dec_ref, ones_ref):
    step = pl.program_id(0)
    blk = WINDOW
    rows4 = A_GROUP * blk

    @pl.when(step == 0)
    def _():
        carry_ref[...] = jnp.zeros_like(carry_ref)
        ps_ref[...] = jnp.zeros_like(ps_ref)
        er = lax.broadcasted_iota(jnp.int32, ones_ref.shape, 0) // A_HD
        ec = lax.broadcasted_iota(jnp.int32, ones_ref.shape, 1) // A_HD
        ones_ref[...] = jnp.where(er == ec, 1.0 / A_HD, 0.0).astype(BF16)
        row = lax.broadcasted_iota(jnp.int32, (rows4, 2 * blk), 0)
        dist = row % blk + blk - lax.broadcasted_iota(jnp.int32, (rows4, 2 * blk), 1)
        in_window = (dist >= 0) & (dist < WINDOW)
        for kv in range(A_KV):
            slope = jnp.zeros((rows4, 2 * blk), F32)
            for g in range(A_GROUP):
                slope = jnp.where(row // blk == g, _alibi_slope(kv * A_GROUP + g), slope)
            bias_ref[kv] = jnp.where(in_window, slope * dist.astype(F32), -NEG_INF)
        for hb in range(B_HEADS):
            lg = _ret_log_gamma(hb)
            ii = lax.broadcasted_iota(jnp.int32, (blk, blk), 0).astype(F32)
            jj = lax.broadcasted_iota(jnp.int32, (blk, blk), 1).astype(F32)
            diff = ii - jj
            dec_ref[hb, 0] = jnp.where(diff >= 0, jnp.exp(lg * jnp.maximum(diff, 0.0)), 0.0)
            dec_ref[hb, 1] = jnp.exp(lg * (ii + 1.0))
            dec_ref[hb, 2] = jnp.exp(lg * (blk - 1.0 - jj))

    x = x_ref[...]
    sb, _, d = x.shape
    tb = sb * ROWS
    mod = mod_ref[...]
    h = _modulate(x, mod, gn_ref[...], 0).reshape(tb, d).astype(BF16)
    proj = _dot(h, win_ref[...])
    kt_all = _dot_nt(wkt_ref[...], h) * (B_KD ** -0.5)
    rg = rg_ref[...]

    qa = proj[:, 0:QA_W]
    ka = proj[:, P_OFF_KA:P_OFF_KA + 2 * KA_W]
    va = proj[:, P_OFF_VA:P_OFF_VA + 2 * VA_W]
    q_hat = (qa * lax.rsqrt(_dot((qa * qa).astype(BF16), ones_ref[...]) + EPS) * qg_ref[...]).astype(BF16)
    k_hat = ka * lax.rsqrt(_dot((ka * ka).astype(BF16), ones_ref[0:2 * KA_W, 0:2 * KA_W]) + EPS) * kg_ref[...]
    k_hat16 = k_hat.astype(BF16)
    va16 = va.astype(BF16)
    prev = carry_ref[step % 2]
    carry_ref[(step + 1) % 2] = jnp.concatenate([k_hat16[tb - blk:tb], va16[tb - blk:tb]], axis=1)

    mean_mat = jnp.full((B_VD, B_VD), 1.0 / B_VD, BF16)
    row_g = lax.broadcasted_iota(jnp.int32, (rows4, 1), 0) // blk
    key_is_prev = lax.broadcasted_iota(jnp.int32, (rows4, 2 * blk), 1) < blk
    first_penalty = jnp.where(step == 0, -NEG_INF, 0.0)
    lane_low = lax.broadcasted_iota(jnp.int32, (blk, PAIR), 1) < A_HD
    ones_cols = jnp.ones((2 * blk, PAIR), BF16)

    n_blk = tb // blk
    att = [(j, kv) for j in range(n_blk) for kv in range(A_KV)]
    ret = [(j, hb) for j in range(n_blk) for hb in range(B_HEADS)]

    sinks, scores = {}, {}
    for j, kv in att:
        r0 = j * blk
        kcol = slice(kv * PAIR, (kv + 1) * PAIR)
        if j == 0:
            k2 = jnp.concatenate([prev[:, kcol], k_hat16[0:blk, kcol]], axis=0)
        else:
            k2 = k_hat16[r0 - blk:r0 + blk, kcol]
        q4 = jnp.concatenate(
            [jnp.where(lane_low == (g % 2 == 0),
                       q_hat[r0:r0 + blk, (kv * A_GROUP + g - g % 2) * A_HD:(kv * A_GROUP + g - g % 2 + 2) * A_HD],
                       jnp.zeros((), BF16))
             for g in range(A_GROUP)], axis=0)
        scores[j, kv] = _dot_nt(q4, k2)
        sink = jnp.zeros((rows4, 1), F32)
        for g in range(A_GROUP):
            sink = jnp.where(row_g == g, sink_ref[kv * A_GROUP + g], sink)
        sinks[j, kv] = sink
    qb, vb, kt, inner = {}, {}, {}, {}
    for j, hb in ret:
        r0 = j * blk
        qb[j, hb] = proj[r0:r0 + blk, P_OFF_QB + hb * B_KD:P_OFF_QB + (hb + 1) * B_KD].astype(BF16)
        vb[j, hb] = proj[r0:r0 + blk, P_OFF_VB + hb * B_VD:P_OFF_VB + (hb + 1) * B_VD].astype(BF16)
        kt[j, hb] = kt_all[hb * B_KD:(hb + 1) * B_KD, r0:r0 + blk]
        inner[j, hb] = _dot(qb[j, hb], kt[j, hb].astype(BF16))

    probs, maxes = {}, {}
    for j, kv in att:
        s = scores[j, kv] * (A_HD ** -0.5) - bias_ref[kv]
        if j == 0:
            s = s - jnp.where(key_is_prev, first_penalty, 0.0)
        mx = jnp.maximum(jnp.max(s, axis=-1, keepdims=True), sinks[j, kv])
        probs[j, kv] = jnp.exp(s - mx).astype(BF16)
        maxes[j, kv] = mx
    state = {}
    for hb in range(B_HEADS):
        d_c = math.exp(_ret_log_gamma(hb) * blk)
        state[0, hb] = ps_ref[hb]
        for j in range(n_blk):
            state[j + 1, hb] = state[j, hb] * d_c + _dot((kt[j, hb] * dec_ref[hb, 2]).astype(BF16), vb[j, hb])
        ps_ref[hb] = state[n_blk, hb]

    for j, kv in att:
        r0 = j * blk
        kcol = slice(kv * PAIR, (kv + 1) * PAIR)
        vcol = slice(2 * KA_W + kv * PAIR, 2 * KA_W + (kv + 1) * PAIR)
        if j == 0:
            v2 = jnp.concatenate([prev[:, vcol], va16[0:blk, kcol]], axis=0)
        else:
            v2 = va16[r0 - blk:r0 + blk, kcol]
        pv = _dot(probs[j, kv], jnp.concatenate([v2, ones_cols], axis=1))
        o4 = pv[:, 0:PAIR] / (pv[:, PAIR:2 * PAIR] + jnp.exp(sinks[j, kv] - maxes[j, kv]))
        for g in range(A_GROUP):
            hd = kv * A_GROUP + g
            half = slice((hd % 2) * A_HD, (hd % 2 + 1) * A_HD)
            mix_ref[r0:r0 + blk, hd * A_HD:(hd + 1) * A_HD] = o4[g * blk:(g + 1) * blk, half]
    o_ret = {}
    for j, hb in ret:
        o_ret[j, hb] = (_dot((inner[j, hb] * dec_ref[hb, 0]).astype(BF16), vb[j, hb])
                        + _dot(qb[j, hb], state[j, hb].astype(BF16)) * dec_ref[hb, 1])

    cen = {k: o_ret[k] - _dot(o_ret[k].astype(BF16), mean_mat) for k in ret}
    var = {k: _dot((cen[k] * cen[k]).astype(BF16), mean_mat) for k in ret}
    for j, hb in ret:
        r0 = j * blk
        gb = proj[r0:r0 + blk, P_OFF_GB + hb * B_VD:P_OFF_GB + (hb + 1) * B_VD]
        mix_ref[r0:r0 + blk, QA_W + hb * B_VD:QA_W + (hb + 1) * B_VD] = (
            cen[j, hb] * lax.rsqrt(var[j, hb] + EPS) * rg[:, hb * B_VD:(hb + 1) * B_VD] * (gb * jax.nn.sigmoid(gb)))

    out = _dot(mix_ref[...].astype(BF16), wout_ref[...])
    o_ref[...] = x + _gate(mod, 0, d) * out.reshape(sb, ROWS, d)

    @pl.when(step == pl.num_programs(0) - 1)
    def _():
        pk_ref[...] = _dedup(k_hat[tb - blk:tb, :])
        pv_ref[...] = _dedup(va[tb - blk:tb, :])


def _dup_heads(t, width):
    lead = t.shape[:-1]
    t = t.reshape(lead + (-1, 1, width))
    return jnp.broadcast_to(t, lead + (t.shape[-3], 2, width)).reshape(lead + (-1,))


def _mix_even_prompt(x3, mod3, gain, w_in, q_gain, k_gain, sinks, ret_gain, w_out, sb):
    n8, _, d = x3.shape
    tb = sb * ROWS
    w_main = jnp.concatenate([w_in[:, :OFF_KA], _dup_heads(w_in[:, OFF_KA:OFF_VA], A_HD),
                              _dup_heads(w_in[:, OFF_VA:OFF_QB], A_HD), w_in[:, OFF_QB:OFF_KB],
                              w_in[:, OFF_VB:]], axis=1)
    wk_t = w_in[:, OFF_KB:OFF_VB].T
    return pl.pallas_call(
        _mix_even_prompt_body,
        grid=(n8 // sb,),
        in_specs=[pl.BlockSpec((sb, ROWS, d), lambda i: (i, 0, 0)),
                  pl.BlockSpec((1, 1, 6 * d), lambda i: (0, 0, 0)),
                  _const_spec((1, d)), _const_spec((d, P_WIDTH)), _const_spec((KB_W, d)),
                  _const_spec((1, QA_W)), _const_spec((1, 2 * KA_W)),
                  pl.BlockSpec(memory_space=pltpu.SMEM),
                  _const_spec((1, VB_W)), _const_spec((QA_W + VB_W, d))],
        out_specs=[pl.BlockSpec((sb, ROWS, d), lambda i: (i, 0, 0)),
                   pl.BlockSpec((WINDOW, KA_W), lambda i: (0, 0)),
                   pl.BlockSpec((WINDOW, VA_W), lambda i: (0, 0)),
                   pl.BlockSpec((B_HEADS, B_KD, B_VD), lambda i: (0, 0, 0))],
        out_shape=[jax.ShapeDtypeStruct(x3.shape, F32),
                   jax.ShapeDtypeStruct((WINDOW, KA_W), F32),
                   jax.ShapeDtypeStruct((WINDOW, VA_W), F32),
                   jax.ShapeDtypeStruct((B_HEADS, B_KD, B_VD), F32)],
        scratch_shapes=[pltpu.VMEM((tb, QA_W + VB_W), F32),
                        pltpu.VMEM((2, WINDOW, 2 * KA_W + 2 * VA_W), BF16),
                        pltpu.VMEM((A_KV, A_GROUP * WINDOW, 2 * WINDOW), F32),
                        pltpu.VMEM((B_HEADS, 3, WINDOW, WINDOW), F32),
                        pltpu.VMEM((QA_W, QA_W), BF16)],
        compiler_params=_params(),
        name="mix_even_prompt",
    )(x3, mod3, gain.reshape(1, d), w_main, wk_t, jnp.tile(q_gain, A_HEADS).reshape(1, QA_W),
      jnp.tile(k_gain, 2 * A_KV).reshape(1, 2 * KA_W),
      sinks, ret_gain.reshape(1, VB_W), w_out)


def _mix_even_sample_body(x_ref, mod_ref, gn_ref, win_ref, qg_ref, kg_ref, sink_ref, rg_ref, wout_ref,
                          ck_ref, cv_ref, s0_ref,
                          o_ref, nk_ref, nv_ref, ns_ref, mix_ref):
    x = x_ref[...]
    sb, length, d = x.shape
    tb = sb * length
    w = ck_ref.shape[1]
    mod = mod_ref[...]
    h = _modulate(x, mod, gn_ref[...], 0).reshape(tb, d).astype(BF16)
    proj = _dot(h, win_ref[...])
    qg = qg_ref[...]
    kg = kg_ref[...]
    rg = rg_ref[...]

    rows = A_GROUP * length
    qpos_c = lax.broadcasted_iota(jnp.int32, (rows, w), 0) % length
    kpos_c = lax.broadcasted_iota(jnp.int32, (rows, w), 1)
    dist_c = w + qpos_c - kpos_c
    valid_c = (dist_c >= 0) & (dist_c < WINDOW)
    qpos_n = lax.broadcasted_iota(jnp.int32, (rows, length), 0) % length
    kpos_n = lax.broadcasted_iota(jnp.int32, (rows, length), 1)
    dist_n = qpos_n - kpos_n
    valid_n = (dist_n >= 0) & (dist_n < WINDOW)
    row_g = lax.broadcasted_iota(jnp.int32, (rows, 1), 0) // length

    for kv in range(A_KV):
        lanes = slice(kv * A_HD, (kv + 1) * A_HD)
        kn = _head_rms(proj[:, OFF_KA + kv * A_HD:OFF_KA + (kv + 1) * A_HD], kg).reshape(sb, length, A_HD)
        vn = proj[:, OFF_VA + kv * A_HD:OFF_VA + (kv + 1) * A_HD].reshape(sb, length, A_HD)
        nk_ref[:, 0:w - length, lanes] = ck_ref[:, length:w, lanes]
        nv_ref[:, 0:w - length, lanes] = cv_ref[:, length:w, lanes]
        nk_ref[:, w - length:w, lanes] = kn
        nv_ref[:, w - length:w, lanes] = vn
        kc = ck_ref[:, :, lanes].astype(BF16)
        vc = cv_ref[:, :, lanes].astype(BF16)
        q4 = jnp.concatenate(
            [_head_rms(proj[:, OFF_QA + (kv * A_GROUP + g) * A_HD:OFF_QA + (kv * A_GROUP + g + 1) * A_HD], qg)
             .reshape(sb, length, A_HD) for g in range(A_GROUP)], axis=1).astype(BF16)
        slope = jnp.zeros((rows, 1), F32)
        sink = jnp.zeros((rows, 1), F32)
        for g in range(A_GROUP):
            hd = kv * A_GROUP + g
            slope = jnp.where(row_g == g, _alibi_slope(hd), slope)
            sink = jnp.where(row_g == g, sink_ref[hd], sink)
        scale = A_HD ** -0.5
        s_c = _bmm_nt(q4, kc) * scale - slope * dist_c.astype(F32)
        s_c = jnp.where(valid_c, s_c, NEG_INF)
        s_n = _bmm_nt(q4, kn.astype(BF16)) * scale - slope * dist_n.astype(F32)
        s_n = jnp.where(valid_n, s_n, NEG_INF)
        mx = jnp.maximum(jnp.maximum(jnp.max(s_c, axis=-1, keepdims=True),
                                     jnp.max(s_n, axis=-1, keepdims=True)), sink)
        p_c = jnp.exp(s_c - mx)
        p_n = jnp.exp(s_n - mx)
        den = (jnp.sum(p_c, axis=-1, keepdims=True) + jnp.sum(p_n, axis=-1, keepdims=True)
               + jnp.exp(sink - mx))
        o4 = (_bmm(p_c.astype(BF16), vc) + _bmm(p_n.astype(BF16), vn.astype(BF16))) / den
        for g in range(A_GROUP):
            hd = kv * A_GROUP + g
            mix_ref[:, hd * A_HD:(hd + 1) * A_HD] = o4[:, g * length:(g + 1) * length, :].reshape(tb, A_HD)

    for hb in range(B_HEADS):
        d_in, d_q, d_k, d_c = _ret_decay(hb, length)
        qb = proj[:, OFF_QB + hb * B_KD:OFF_QB + (hb + 1) * B_KD].reshape(sb, length, B_KD).astype(BF16)
        kb = proj[:, OFF_KB + hb * B_KD:OFF_KB + (hb + 1) * B_KD].reshape(sb, length, B_KD) * (B_KD ** -0.5)
        vb = proj[:, OFF_VB + hb * B_VD:OFF_VB + (hb + 1) * B_VD].reshape(sb, length, B_VD).astype(BF16)
        gb = proj[:, OFF_GB + hb * B_VD:OFF_GB + (hb + 1) * B_VD]
        state = s0_ref[:, hb]
        inner = _bmm_nt(qb, kb.astype(BF16)) * d_in
        o = _bmm(inner.astype(BF16), vb) + _bmm(qb, state.astype(BF16)) * d_q
        ns_ref[:, hb] = state * d_c + _bmm_tn((kb * d_k).astype(BF16), vb)
        mix_ref[:, QA_W + hb * B_VD:QA_W + (hb + 1) * B_VD] = _group_norm_gate(
            o.reshape(tb, B_VD), rg[:, hb * B_VD:(hb + 1) * B_VD], gb)

    out = _dot(mix_ref[...].astype(BF16), wout_ref[...])
    o_ref[...] = x + _gate(mod, 0, d) * out.reshape(sb, length, d)


def _mix_even_sample(x3, mod3, gain, w_in, q_gain, k_gain, sinks, ret_gain, w_out, cache_k, cache_v, state, sb):
    n, length, d = x3.shape
    in_w = w_in.shape[1]
    w = cache_k.shape[1]
    seq_spec = lambda shape: pl.BlockSpec((sb,) + shape, lambda i, _n=len(shape): (i,) + (0,) * _n)
    return pl.pallas_call(
        _mix_even_sample_body,
        grid=(n // sb,),
        in_specs=[seq_spec((length, d)), seq_spec((1, 6 * d)),
                  _const_spec((1, d)), _const_spec((d, in_w)),
                  _const_spec((1, A_HD)), _const_spec((1, A_HD)),
                  pl.BlockSpec(memory_space=pltpu.SMEM),
                  _const_spec((1, VB_W)), _const_spec((QA_W + VB_W, d)),
                  seq_spec((w, KA_W)), seq_spec((w, VA_W)), seq_spec((B_HEADS, B_KD, B_VD))],
        out_specs=[seq_spec((length, d)), seq_spec((w, KA_W)), seq_spec((w, VA_W)),
                   seq_spec((B_HEADS, B_KD, B_VD))],
        out_shape=[jax.ShapeDtypeStruct(x3.shape, F32),
                   jax.ShapeDtypeStruct(cache_k.shape, F32),
                   jax.ShapeDtypeStruct(cache_v.shape, F32),
                   jax.ShapeDtypeStruct(state.shape, F32)],
        scratch_shapes=[pltpu.VMEM((sb * length, QA_W + VB_W), F32)],
        compiler_params=_params(),
        name="mix_even_sample",
    )(x3, mod3, gain.reshape(1, d), w_in, q_gain.reshape(1, A_HD), k_gain.reshape(1, A_HD),
      sinks, ret_gain.reshape(1, VB_W), w_out, cache_k, cache_v, state)


def _cmul(ar, ai, br, bi):
    return ar * br - ai * bi, ar * bi + ai * br


def _s5_lambda(a_re, a_im, log_dt):
    dt = jnp.exp(log_dt)
    mag = jnp.exp(a_re * dt)
    return mag * jnp.cos(a_im * dt), mag * jnp.sin(a_im * dt)


def _s5_prep_body(t_len, n_chunks, are_ref, aim_ref, ldt_ref, btr_ref, bti_ref, ard_ref, aid_ref, ldd_ref,
                  lre_ref, lim_ref, bbr_ref, bbi_ref, pwr_ref, pwi_ref):
    a_re = are_ref[...]
    a_im = aim_ref[...]
    lam_re, lam_im = _s5_lambda(a_re, a_im, ldt_ref[...])
    den = a_re * a_re + a_im * a_im
    n_re = lam_re - 1.0
    n_im = lam_im
    f_re = (n_re * a_re + n_im * a_im) / den
    f_im = (n_im * a_re - n_re * a_im) / den
    br = btr_ref[...]
    bi = bti_ref[...]
    lre_ref[...] = lam_re
    lim_ref[...] = lam_im
    bbr_ref[...] = f_re * br - f_im * bi
    bbi_ref[...] = f_re * bi + f_im * br
    lam_re, lam_im = _s5_lambda(ard_ref[...], aid_ref[...], ldd_ref[...])
    cr, ci = lam_re, lam_im
    for t in range(t_len):
        pwr_ref[t] = cr
        pwi_ref[t] = ci
        if t + 1 < t_len:
            cr, ci = _cmul(cr, ci, lam_re, lam_im)
    tr, ti = cr, ci
    cr, ci = jnp.ones_like(lam_re), jnp.zeros_like(lam_im)
    for m in range(n_chunks + 1):
        pwr_ref[t_len + m] = cr
        pwi_ref[t_len + m] = ci
        cr, ci = _cmul(cr, ci, tr, ti)


def _s5_prep(a_re, a_im, log_dt, b_re, b_im, t_len, n_chunks):
    g, p = a_re.shape
    k = b_re.shape[-1]
    bt_re = jnp.swapaxes(b_re, 1, 2)
    bt_im = jnp.swapaxes(b_im, 1, 2)
    n_pow = t_len + n_chunks + 1
    dense = (g // S5_OCT, S5_OCT * p)
    out = pl.pallas_call(
        functools.partial(_s5_prep_body, t_len, n_chunks),
        out_shape=[jax.ShapeDtypeStruct((g, 1, p), F32), jax.ShapeDtypeStruct((g, 1, p), F32),
                   jax.ShapeDtypeStruct((g, k, p), F32), jax.ShapeDtypeStruct((g, k, p), F32),
                   jax.ShapeDtypeStruct((n_pow,) + dense, F32), jax.ShapeDtypeStruct((n_pow,) + dense, F32)],
        name="s5_prep",
    )(a_re.reshape(g, 1, p), a_im.reshape(g, 1, p), log_dt.reshape(g, 1, 1), bt_re, bt_im,
      a_re.reshape(dense), a_im.reshape(dense), jnp.broadcast_to(log_dt[:, None], (g, p)).reshape(dense))
    return out


def _block_diag(t):
    g, a, b = t.shape
    t = t.reshape(g // S5_OCT, S5_OCT, a, b)
    eye = jnp.eye(S5_OCT, dtype=t.dtype)
    out = t[:, :, :, None, :] * eye[None, :, None, :, None]
    return out.reshape(g // S5_OCT, S5_OCT * a, S5_OCT * b)


def _gelu_glu_out(x, mod, y, u, dskip, glua_ref, glub_ref):
    sb, rows, d = x.shape
    y = y + dskip * u
    yg = jax.nn.gelu(y, approximate=True).astype(BF16)
    out = _dot(yg, glua_ref[...]) * jax.nn.sigmoid(_dot(yg, glub_ref[...]))
    return x + _gate(mod, 0, d) * out.reshape(sb, rows, d)


def _mix_odd_prompt_body(x_ref, xn_ref, mod_ref, gn_ref, bblk_ref, lam_ref, pwb_ref, ltp_ref, cre_ref, cim_ref,
                         dskip_ref, glua_ref, glub_ref, o_ref, hre_ref, him_ref,
                         un_ref, l0_ref, l1_ref, up0_ref, up1_ref, pt_ref):
    step = pl.program_id(0)
    sb, _, d = x_ref.shape
    tm = sb * ROWS
    t_len = tm // ROWS
    n_oct = bblk_ref.shape[0]
    half = bblk_ref.shape[2] // 2
    mod = mod_ref[...]

    def permuted_input(src_ref, up_dst):
        u = _modulate(src_ref[...], mod, gn_ref[...], 0).reshape(tm, d)
        for k in range(d // 128):
            un_ref[k] = u[:, k * 128:(k + 1) * 128]
        up = jnp.concatenate(
            [jnp.concatenate([un_ref[k, pl.ds(t, ROWS, stride=t_len), :] for k in range(d // 128)], axis=1)
             for t in range(t_len)], axis=0)
        up_dst[...] = up
        return up.astype(BF16)

    @pl.when(step == 0)
    def _():
        hre_ref[...] = jnp.zeros_like(hre_ref)
        him_ref[...] = jnp.zeros_like(him_ref)
        nat = lax.broadcasted_iota(jnp.int32, (tm, tm), 0)
        prm = lax.broadcasted_iota(jnp.int32, (tm, tm), 1)
        pt_ref[...] = jnp.where(prm == (nat % t_len) * ROWS + nat // t_len, 1.0, 0.0).astype(BF16)
        up16 = permuted_input(x_ref, up0_ref)
        for s in range(n_oct):
            l0_ref[s] = _dot(up16[:, s * 128:(s + 1) * 128], bblk_ref[s])

    def run(l_ref, l_next, up_ref, up_next):
        for s0 in range(0, n_oct, 2):
            pair = (s0, s0 + 1)
            lam_re = [jnp.broadcast_to(lam_ref[0, s:s + 1, :], (ROWS, half)) for s in pair]
            lam_im = [jnp.broadcast_to(lam_ref[1, s:s + 1, :], (ROWS, half)) for s in pair]

            def local_step(t, carry):
                r = pl.multiple_of(t * ROWS, ROWS)
                out = []
                for i, s in enumerate(pair):
                    h_re, h_im = carry[2 * i], carry[2 * i + 1]
                    n_re = lam_re[i] * h_re - lam_im[i] * h_im + l_ref[s, pl.ds(r, ROWS), 0:half]
                    n_im = lam_re[i] * h_im + lam_im[i] * h_re + l_ref[s, pl.ds(r, ROWS), half:2 * half]
                    l_ref[s, pl.ds(r, ROWS), 0:half] = n_re
                    l_ref[s, pl.ds(r, ROWS), half:2 * half] = n_im
                    out += [n_re, n_im]
                return tuple(out)

            zero = jnp.zeros((ROWS, half), F32)
            lax.fori_loop(0, t_len, local_step, (zero, zero, zero, zero), unroll=2)

        next16 = permuted_input(xn_ref, up_next)
        chunk = lax.broadcasted_iota(jnp.int32, (ROWS, half), 0)
        y_parts = []
        for s in range(n_oct):
            l_next[s] = _dot(next16[:, s * 128:(s + 1) * 128], bblk_ref[s])
            p_re = l_ref[s, tm - ROWS:tm, 0:half]
            p_im = l_ref[s, tm - ROWS:tm, half:2 * half]
            for i, sh in enumerate((1, 2, 4)):
                m_re, m_im = _cmul(ltp_ref[s, i:i + 1, 0:half], ltp_ref[s, i:i + 1, half:2 * half],
                                   jnp.where(chunk >= sh, pltpu.roll(p_re, sh, 0), 0.0),
                                   jnp.where(chunk >= sh, pltpu.roll(p_im, sh, 0), 0.0))
                p_re, p_im = p_re + m_re, p_im + m_im
            hin_re = jnp.broadcast_to(hre_ref[s:s + 1, :], (ROWS, half))
            hin_im = jnp.broadcast_to(him_ref[s:s + 1, :], (ROWS, half))
            m_re, m_im = _cmul(ltp_ref[s, ROWS:2 * ROWS, 0:half], ltp_ref[s, ROWS:2 * ROWS, half:2 * half],
                               hin_re, hin_im)
            st_re = m_re + jnp.where(chunk >= 1, pltpu.roll(p_re, 1, 0), 0.0)
            st_im = m_im + jnp.where(chunk >= 1, pltpu.roll(p_im, 1, 0), 0.0)
            m_re, m_im = _cmul(ltp_ref[s, 3:4, 0:half], ltp_ref[s, 3:4, half:2 * half],
                               hre_ref[s:s + 1, :], him_ref[s:s + 1, :])
            hre_ref[s:s + 1, :] = m_re + p_re[ROWS - 1:ROWS, :]
            him_ref[s:s + 1, :] = m_im + p_im[ROWS - 1:ROWS, :]
            loc = l_ref[s].reshape(t_len, ROWS, 2 * half)
            pw = pwb_ref[s]
            f_re, f_im = _cmul(pw[:, :, 0:half], pw[:, :, half:2 * half], st_re[None], st_im[None])
            hs_re = (loc[:, :, 0:half] + f_re).reshape(tm, half).astype(BF16)
            hs_im = (loc[:, :, half:2 * half] + f_im).reshape(tm, half).astype(BF16)
            y_parts.append(_dot(hs_re, cre_ref[s]) - _dot(hs_im, cim_ref[s]))

        y = jnp.concatenate(y_parts, axis=1) + dskip_ref[...] * up_ref[...]
        yg = jax.nn.gelu(y, approximate=True).astype(BF16)
        yn = _dot(pt_ref[...], yg).astype(BF16)
        out = _dot(yn, glua_ref[...]) * jax.nn.sigmoid(_dot(yn, glub_ref[...]))
        o_ref[...] = x_ref[...] + _gate(mod, 0, d) * out.reshape(sb, ROWS, d)

    @pl.when(step % 2 == 0)
    def _():
        run(l0_ref, l1_ref, up0_ref, up1_ref)

    @pl.when(step % 2 == 1)
    def _():
        run(l1_ref, l0_ref, up1_ref, up0_ref)


def _mix_odd_prompt(x3, mod3, gain, bblk, lam_d, pwb, ltp, cre, cim, dskip, glu_a, glu_b, sb):
    n8, _, d = x3.shape
    n_oct, kin, wid = bblk.shape
    half = wid // 2
    tm = sb * ROWS
    n_tiles = n8 // sb
    proj_buf = pltpu.VMEM((n_oct, tm, wid), F32)
    perm_buf = pltpu.VMEM((tm, d), F32)
    return pl.pallas_call(
        _mix_odd_prompt_body,
        grid=(n_tiles,),
        in_specs=[pl.BlockSpec((sb, ROWS, d), lambda i: (i, 0, 0)),
                  pl.BlockSpec((sb, ROWS, d), lambda i: (jnp.minimum(i + 1, n_tiles - 1), 0, 0)),
                  pl.BlockSpec((1, 1, 6 * d), lambda i: (0, 0, 0)),
                  _const_spec((1, d)), _const_spec(bblk.shape), _const_spec(lam_d.shape),
                  _const_spec(pwb.shape), _const_spec(ltp.shape),
                  _const_spec(cre.shape), _const_spec(cim.shape), _const_spec((1, d)),
                  _const_spec((d, d)), _const_spec((d, d))],
        out_specs=[pl.BlockSpec((sb, ROWS, d), lambda i: (i, 0, 0)),
                   pl.BlockSpec((n_oct, half), lambda i: (0, 0)),
                   pl.BlockSpec((n_oct, half), lambda i: (0, 0))],
        out_shape=[jax.ShapeDtypeStruct(x3.shape, F32),
                   jax.ShapeDtypeStruct((n_oct, half), F32),
                   jax.ShapeDtypeStruct((n_oct, half), F32)],
        scratch_shapes=[pltpu.VMEM((d // 128, tm, 128), F32), proj_buf, proj_buf, perm_buf, perm_buf,
                        pltpu.VMEM((tm, tm), BF16)],
        compiler_params=_params(),
        name="mix_odd_prompt",
    )(x3, x3, mod3, gain.reshape(1, d), bblk, lam_d, pwb, ltp, cre, cim, dskip.reshape(1, d), glu_a, glu_b)


def _mix_odd_sample_body(x_ref, mod_ref, gn_ref, bblk_ref, lam_ref, cre_ref, cim_ref, dskip_ref,
                         glua_ref, glub_ref, sre_ref, sim_ref, o_ref, nre_ref, nim_ref, d_ref, y_ref):
    x = x_ref[...]
    sb, length, d = x.shape
    tm = sb * length
    n_oct = bblk_ref.shape[0]
    half = bblk_ref.shape[2] // 2
    mod = mod_ref[...]
    u = _modulate(x, mod, gn_ref[...], 0).reshape(tm, d)
    u16 = u.astype(BF16)
    n_ch = bblk_ref.shape[2] // 128
    hc = n_ch // 2
    for s in range(n_oct):
        bu = _dot(u16[:, s * 128:(s + 1) * 128], bblk_ref[s])
        for c in range(n_ch):
            d_ref[c] = bu[:, c * 128:(c + 1) * 128]
        lam_re = lam_ref[0, s:s + 1, :]
        lam_im = lam_ref[1, s:s + 1, :]
        h_re = sre_ref[:, s * half:(s + 1) * half]
        h_im = sim_ref[:, s * half:(s + 1) * half]
        for t in range(length):
            b_re = jnp.concatenate([d_ref[c, pl.ds(t, sb, stride=length), :] for c in range(hc)], axis=1)
            b_im = jnp.concatenate([d_ref[hc + c, pl.ds(t, sb, stride=length), :] for c in range(hc)], axis=1)
            n_re = lam_re * h_re - lam_im * h_im + b_re
            n_im = lam_re * h_im + lam_im * h_re + b_im
            for c in range(hc):
                d_ref[c, pl.ds(t, sb, stride=length), :] = n_re[:, c * 128:(c + 1) * 128]
                d_ref[hc + c, pl.ds(t, sb, stride=length), :] = n_im[:, c * 128:(c + 1) * 128]
            h_re, h_im = n_re, n_im
        nre_ref[:, s * half:(s + 1) * half] = h_re
        nim_ref[:, s * half:(s + 1) * half] = h_im
        hs_re = jnp.concatenate([d_ref[c] for c in range(hc)], axis=1)
        hs_im = jnp.concatenate([d_ref[hc + c] for c in range(hc)], axis=1)
        y_ref[:, s * 128:(s + 1) * 128] = (_dot(hs_re.astype(BF16), cre_ref[s])
                                           - _dot(hs_im.astype(BF16), cim_ref[s]))
    o_ref[...] = _gelu_glu_out(x, mod, y_ref[...], u, dskip_ref[...], glua_ref, glub_ref)


def _mix_odd_sample(x3, mod3, gain, bblk, lam_d, cre, cim, dskip, glu_a, glu_b, s_re, s_im, sb):
    n, length, d = x3.shape
    n_oct, kin, wid = bblk.shape
    tm = sb * length
    nstate = s_re.shape[1]
    return pl.pallas_call(
        _mix_odd_sample_body,
        grid=(n // sb,),
        in_specs=[pl.BlockSpec((sb, length, d), lambda i: (i, 0, 0)),
                  pl.BlockSpec((sb, 1, 6 * d), lambda i: (i, 0, 0)),
                  _const_spec((1, d)), _const_spec(bblk.shape), _const_spec(lam_d.shape),
                  _const_spec(cre.shape), _const_spec(cim.shape), _const_spec((1, d)),
                  _const_spec((d, d)), _const_spec((d, d)),
                  pl.BlockSpec((sb, nstate), lambda i: (i, 0)),
                  pl.BlockSpec((sb, nstate), lambda i: (i, 0))],
        out_specs=[pl.BlockSpec((sb, length, d), lambda i: (i, 0, 0)),
                   pl.BlockSpec((sb, nstate), lambda i: (i, 0)),
                   pl.BlockSpec((sb, nstate), lambda i: (i, 0))],
        out_shape=[jax.ShapeDtypeStruct(x3.shape, F32),
                   jax.ShapeDtypeStruct(s_re.shape, F32),
                   jax.ShapeDtypeStruct(s_im.shape, F32)],
        scratch_shapes=[pltpu.VMEM((wid // 128, tm, 128), F32), pltpu.VMEM((tm, d), F32)],
        compiler_params=_params(),
        name="mix_odd_sample",
    )(x3, mod3, gain.reshape(1, d), bblk, lam_d, cre, cim, dskip.reshape(1, d), glu_a, glu_b, s_re, s_im)


def _pick(n, want):
    while n % want:
        want //= 2
    return max(want, 1)


def kernel(x_prompt, x_sample, cache_win_k, cache_win_v, state_ret, state_s5_re, state_s5_im, c_prompt, c_sample, ada_w, ada_b, norm_mix, norm_ffn, ffn_wg, ffn_wu, ffn_wd, even_w_in, even_q_gain, even_k_gain, even_sinks, even_ret_gain, even_w_out, odd_A_re, odd_A_im, odd_log_dt, odd_B_re, odd_B_im, odd_C_re, odd_C_im, odd_D, odd_glu_a, odd_glu_b):
    bp, lp, d = x_prompt.shape
    ns, ls, _ = x_sample.shape
    assert bp == 1 and ls == ROWS and lp % WINDOW == 0
    w = cache_win_k.shape[2]
    groups, p_state = odd_A_re.shape[1:]

    n_c = bp + ns
    n_pad = -n_c % ROWS
    c_all = jnp.concatenate([c_prompt, c_sample, jnp.zeros((n_pad, d), F32)], axis=0)
    mod = _adaln(c_all, ada_w, ada_b)
    mod_p = [mod[l, 0:1].reshape(1, 1, 6 * d) for l in range(2)]
    mod_s = [mod[l, bp:bp + ns].reshape(ns, 1, 6 * d) for l in range(2)]

    bf = lambda t: t.astype(BF16)
    w_in, w_out = bf(even_w_in[0]), bf(even_w_out[0])
    wg, wu, wd = bf(ffn_wg), bf(ffn_wu), bf(ffn_wd)
    glu_a, glu_b = bf(odd_glu_a[0]), bf(odd_glu_b[0])

    xp = x_prompt.reshape(lp // ROWS, ROWS, d)
    xs = x_sample

    sb_p = _pick(lp // ROWS, 32)
    sb_ffn = _pick(lp // ROWS, 64)
    sb_s = _pick(ns, 32)
    sb_s_even = _pick(ns, 16)

    xp, p_k, p_v, p_ret = _mix_even_prompt(xp, mod_p[0], norm_mix[0], w_in, even_q_gain[0], even_k_gain[0],
                                           even_sinks[0], even_ret_gain[0], w_out, sb_ffn)
    xs, s_k, s_v, s_ret = _mix_even_sample(xs, mod_s[0], norm_mix[0], w_in, even_q_gain[0], even_k_gain[0],
                                           even_sinks[0], even_ret_gain[0], w_out,
                                           cache_win_k[0].reshape(ns, w, KA_W), cache_win_v[0].reshape(ns, w, VA_W),
                                           state_ret[0], sb_s_even)
    xp = _ffn(xp, mod_p[0], norm_ffn[0], wg[0], wu[0], wd[0], sb_ffn)
    xs = _ffn(xs, mod_s[0], norm_ffn[0], wg[0], wu[0], wd[0], sb_s)

    t_len = sb_p
    lam_re, lam_im, bbt_re, bbt_im, pw_re, pw_im = _s5_prep(odd_A_re[0], odd_A_im[0], odd_log_dt[0],
                                                            odd_B_re[0], odd_B_im[0], t_len, ROWS)
    n_oct = groups // S5_OCT
    half = S5_OCT * p_state
    bblk = bf(jnp.concatenate([_block_diag(bbt_re), _block_diag(bbt_im)], axis=-1))
    cre = bf(_block_diag(jnp.swapaxes(odd_C_re[0], 1, 2)))
    cim = bf(_block_diag(jnp.swapaxes(odd_C_im[0], 1, 2)))
    lam_d = jnp.stack([lam_re.reshape(n_oct, half), lam_im.reshape(n_oct, half)])

    pw_d = jnp.concatenate([pw_re, pw_im], axis=-1)
    pwb = jnp.broadcast_to(jnp.swapaxes(pw_d[:t_len], 0, 1)[:, :, None, :], (n_oct, t_len, ROWS, 2 * half))
    lt = jnp.swapaxes(pw_d[t_len:], 0, 1)
    ltp = jnp.concatenate([lt[:, 1:2], lt[:, 2:3], lt[:, 4:5], lt[:, 8:9], jnp.zeros_like(lt[:, 0:4]), lt[:, 0:8]],
                          axis=1)
    xp, p_re, p_im = _mix_odd_prompt(xp, mod_p[1], norm_mix[1], bblk, lam_d, pwb, ltp, cre, cim, odd_D[0],
                                     glu_a, glu_b, sb_p)
    xs, s_re, s_im = _mix_odd_sample(xs, mod_s[1], norm_mix[1], bblk, lam_d, cre, cim, odd_D[0], glu_a, glu_b,
                                     state_s5_re[0].reshape(ns, groups * p_state),
                                     state_s5_im[0].reshape(ns, groups * p_state), sb_s)
    xp = _ffn(xp, mod_p[1], norm_ffn[1], wg[1], wu[1], wd[1], sb_ffn)
    xs = _ffn(xs, mod_s[1], norm_ffn[1], wg[1], wu[1], wd[1], sb_s)

    y_prompt = xp.reshape(bp, lp, d)
    y_sample = xs
    return (y_prompt, y_sample,
            p_k.reshape(1, bp, WINDOW, A_KV, A_HD), p_v.reshape(1, bp, WINDOW, A_KV, A_HD),
            p_ret.reshape(1, bp, B_HEADS, B_KD, B_VD),
            p_re.reshape(1, bp, groups, p_state), p_im.reshape(1, bp, groups, p_state),
            s_k.reshape(1, ns, w, A_KV, A_HD), s_v.reshape(1, ns, w, A_KV, A_HD),
            s_ret.reshape(1, ns, B_HEADS, B_KD, B_VD),
            s_re.reshape(1, ns, groups, p_state), s_im.reshape(1, ns, groups, p_state))
```

```python
import functools
import math

import jax
import jax.numpy as jnp
from jax import lax
from jax.experimental import pallas as pl
from jax.experimental.pallas import tpu as pltpu

F32 = jnp.float32
BF16 = jnp.bfloat16

EPS = 1e-6
NEG_INF = -1e30
ROWS = 8

A_HEADS, A_KV, A_GROUP, A_HD = 8, 2, 4, 64
WINDOW = 128
B_HEADS, B_KD, B_VD = 4, 128, 128
S5_GROUP, S5_STATE = 16, 64
S5_OCT = 8

QA_W, KA_W, VA_W = A_HEADS * A_HD, A_KV * A_HD, A_KV * A_HD
QB_W, KB_W, VB_W, GB_W = B_HEADS * B_KD, B_HEADS * B_KD, B_HEADS * B_VD, B_HEADS * B_VD
OFF_QA = 0
OFF_KA = OFF_QA + QA_W
OFF_VA = OFF_KA + KA_W
OFF_QB = OFF_VA + VA_W
OFF_KB = OFF_QB + QB_W
OFF_VB = OFF_KB + KB_W
OFF_GB = OFF_VB + VB_W

VMEM_LIMIT = 56 * 1024 * 1024


def _ret_log_gamma(h):
    return math.log1p(-(2.0 ** (-5.0 - h)))


def _alibi_slope(h):
    return 2.0 ** (-8.0 * (h + 1) / A_HEADS)


def _const_spec(shape):
    nd = len(shape)
    return pl.BlockSpec(shape, lambda i, _n=nd: (0,) * _n, pipeline_mode=pl.Buffered(1))


def _params():
    return pltpu.CompilerParams(dimension_semantics=("arbitrary",), vmem_limit_bytes=VMEM_LIMIT)


def _dot(a, b):
    return jnp.dot(a, b, preferred_element_type=F32)


def _dot_nt(a, b):
    return lax.dot_general(a, b, (((1,), (1,)), ((), ())), preferred_element_type=F32)


def _dot_tn(a, b):
    return lax.dot_general(a, b, (((0,), (0,)), ((), ())), preferred_element_type=F32)


def _bmm(a, b):
    return lax.dot_general(a, b, (((2,), (1,)), ((0,), (0,))), preferred_element_type=F32)


def _bmm_nt(a, b):
    return lax.dot_general(a, b, (((2,), (2,)), ((0,), (0,))), preferred_element_type=F32)


def _bmm_tn(a, b):
    return lax.dot_general(a, b, (((1,), (1,)), ((0,), (0,))), preferred_element_type=F32)


def _rms(x, g):
    return x * lax.rsqrt(jnp.mean(x * x, axis=-1, keepdims=True) + EPS) * g


def _modulate(x3, mod, gain, which):
    d = x3.shape[-1]
    sh = mod[:, :, (3 * which) * d:(3 * which + 1) * d]
    sc = mod[:, :, (3 * which + 1) * d:(3 * which + 2) * d]
    return _rms(x3, gain) * (1.0 + sc) + sh


def _load_mod(mod_ref):
    m = mod_ref[...]
    return m if m.ndim == 3 else m[:, None, :]


def _gate(mod, which, d):
    return mod[:, :, (3 * which + 2) * d:(3 * which + 3) * d]


def _adaln_body(c_ref, w_ref, b_ref, o_ref):
    c = c_ref[...]
    a = (c * jax.nn.sigmoid(c)).astype(BF16)
    o_ref[0] = _dot(a, w_ref[0].astype(BF16)) + b_ref[0]


def _adaln(c_all, ada_w, ada_b):
    depth, d, n = ada_w.shape
    r = c_all.shape[0]
    tn = 1536
    return pl.pallas_call(
        _adaln_body,
        grid=(depth, n // tn),
        in_specs=[pl.BlockSpec((r, d), lambda l, j: (0, 0)),
                  pl.BlockSpec((1, d, tn), lambda l, j: (l, 0, j)),
                  pl.BlockSpec((1, 1, tn), lambda l, j: (l, 0, j))],
        out_specs=pl.BlockSpec((1, r, tn), lambda l, j: (l, 0, j)),
        out_shape=jax.ShapeDtypeStruct((depth, r, n), F32),
        compiler_params=pltpu.CompilerParams(dimension_semantics=("arbitrary", "arbitrary"),
                                             vmem_limit_bytes=VMEM_LIMIT),
        name="adaln",
    )(c_all, ada_w, ada_b.reshape(depth, 1, n))


BF16_ROWS = 16


def _cast_specs(w, n_steps):
    rows, cols = w.shape
    hold = 1
    while rows % (n_steps // hold) or (rows // (n_steps // hold)) % BF16_ROWS:
        hold *= 2
        assert hold <= n_steps and n_steps % hold == 0
    spec = pl.BlockSpec((rows // (n_steps // hold), cols), lambda i, _h=hold: (i // _h, 0))
    return spec, jax.ShapeDtypeStruct(w.shape, BF16)


def _cast_blocks(in_refs, out_refs):
    for src, dst in zip(in_refs, out_refs):
        dst[...] = src[...].astype(BF16)


def _ffn_body(n_cast, x_ref, mod_ref, gn_ref, wg_ref, wu_ref, wd_ref, *refs):
    cast_in, o_ref, cast_out = refs[:n_cast], refs[n_cast], refs[n_cast + 1:]
    x = x_ref[...]
    sb, _, d = x.shape
    mod = _load_mod(mod_ref)
    h = _modulate(x, mod, gn_ref[...], 1).reshape(sb * ROWS, d).astype(BF16)
    a = _dot(h, wg_ref[...])
    b = _dot(h, wu_ref[...])
    act = (a * jax.nn.sigmoid(a) * b).astype(BF16)
    y = _dot(act, wd_ref[...])
    o_ref[...] = x + _gate(mod, 1, d) * y.reshape(sb, ROWS, d)
    _cast_blocks(cast_in, cast_out)


def _ffn(x3, mod3, gain, wg, wu, wd, sb, cast=()):
    n8, _, d = x3.shape
    f = wg.shape[1]
    n_steps = n8 // sb
    per_seq = mod3.ndim == 2
    mod_spec = (pl.BlockSpec((sb, 6 * d), lambda i: (i, 0)) if per_seq
                else pl.BlockSpec((1, 1, 6 * d), lambda i: (0, 0, 0)))
    cast_specs = [_cast_specs(w, n_steps) for w in cast]
    x_spec = pl.BlockSpec((sb, ROWS, d), lambda i: (i, 0, 0))
    out = pl.pallas_call(
        functools.partial(_ffn_body, len(cast)),
        grid=(n_steps,),
        in_specs=[x_spec, mod_spec, _const_spec((1, d)), _const_spec((d, f)), _const_spec((d, f)),
                  _const_spec((f, d))] + [c[0] for c in cast_specs],
        out_specs=[x_spec] + [c[0] for c in cast_specs],
        out_shape=[jax.ShapeDtypeStruct(x3.shape, F32)] + [c[1] for c in cast_specs],
        compiler_params=_params(),
        name="ffn",
    )(x3, mod3, gain.reshape(1, d), wg, wu, wd, *cast)
    return out[0], tuple(out[1:])


def _head_rms(t, g):
    return t * lax.rsqrt(jnp.mean(t * t, axis=-1, keepdims=True) + EPS) * g


def _group_norm_gate(o, gain, gate):
    mu = jnp.mean(o, axis=-1, keepdims=True)
    var = jnp.mean(jnp.square(o - mu), axis=-1, keepdims=True)
    return (o - mu) * lax.rsqrt(var + EPS) * gain * (gate * jax.nn.sigmoid(gate))


def _ret_decay(hb, c):
    lg = _ret_log_gamma(hb)
    ii = lax.broadcasted_iota(jnp.int32, (c, c), 0)
    jj = lax.broadcasted_iota(jnp.int32, (c, c), 1)
    diff = (ii - jj).astype(F32)
    d_in = jnp.where(diff >= 0, jnp.exp(lg * jnp.maximum(diff, 0.0)), 0.0)
    row = lax.broadcasted_iota(jnp.int32, (c, B_KD), 0).astype(F32)
    d_q = jnp.exp(lg * (row + 1.0))
    d_k = jnp.exp(lg * (c - 1.0 - row))
    d_c = math.exp(lg * c)
    return d_in, d_q, d_k, d_c


P_OFF_KA = QA_W
P_OFF_VA = P_OFF_KA + 2 * KA_W
P_OFF_QB = P_OFF_VA + 2 * VA_W
P_OFF_VB = P_OFF_QB + QB_W
P_OFF_GB = P_OFF_VB + VB_W
P_WIDTH = P_OFF_GB + GB_W
PAIR = 2 * A_HD


def _dedup(t):
    low = lax.broadcasted_iota(jnp.int32, (t.shape[0], PAIR), 1) < A_HD
    return jnp.where(low, t[:, 0:PAIR], t[:, PAIR:2 * PAIR])


def _mix_even_prompt_body(n_cast, x_ref, mod_ref, gn_ref, win_ref, wkt_ref, qg_ref, kg_ref, sink_ref, rg_ref, wout_ref,
                          *refs):
    cast_in, refs = refs[:n_cast], refs[n_cast:]
    o_ref, pk_ref, pv_ref, ps_ref = refs[:4]
    cast_out = refs[4:4 + n_cast]
    mix_ref, carry_ref, bias_ref, dec_ref, ones_ref = refs[4 + n_cast:]
    _cast_blocks(cast_in, cast_out)
    step = pl.program_id(0)
    blk = WINDOW
    rows4 = A_GROUP * blk

    @pl.when(step == 0)
    def _():
        carry_ref[...] = jnp.zeros_like(carry_ref)
        ps_ref[...] = jnp.zeros_like(ps_ref)
        er = lax.broadcasted_iota(jnp.int32, ones_ref.shape, 0) // A_HD
        ec = lax.broadcasted_iota(jnp.int32, ones_ref.shape, 1) // A_HD
        ones_ref[...] = jnp.where(er == ec, 1.0 / A_HD, 0.0).astype(BF16)
        row = lax.broadcasted_iota(jnp.int32, (rows4, 2 * blk), 0)
        dist = row % blk + blk - lax.broadcasted_iota(jnp.int32, (rows4, 2 * blk), 1)
        in_window = (dist >= 0) & (dist < WINDOW)
        for kv in range(A_KV):
            slope = jnp.zeros((rows4, 2 * blk), F32)
            for g in range(A_GROUP):
                slope = jnp.where(row // blk == g, _alibi_slope(kv * A_GROUP + g), slope)
            bias_ref[kv] = jnp.where(in_window, slope * dist.astype(F32), -NEG_INF)
        for hb in range(B_HEADS):
            lg = _ret_log_gamma(hb)
            ii = lax.broadcasted_iota(jnp.int32, (blk, blk), 0).astype(F32)
            jj = lax.broadcasted_iota(jnp.int32, (blk, blk), 1).astype(F32)
            diff = ii - jj
            dec_ref[hb, 0] = jnp.where(diff >= 0, jnp.exp(lg * jnp.maximum(diff, 0.0)), 0.0)
            dec_ref[hb, 1] = jnp.exp(lg * (ii + 1.0))
            dec_ref[hb, 2] = jnp.exp(lg * (blk - 1.0 - jj))

    x = x_ref[...]
    sb, _, d = x.shape
    tb = sb * ROWS
    mod = mod_ref[...]
    h = _modulate(x, mod, gn_ref[...], 0).reshape(tb, d).astype(BF16)
    proj = _dot(h, win_ref[...])
    kt_all = _dot_nt(wkt_ref[...], h) * (B_KD ** -0.5)
    rg = rg_ref[...]

    qa = proj[:, 0:QA_W]
    ka = proj[:, P_OFF_KA:P_OFF_KA + 2 * KA_W]
    va = proj[:, P_OFF_VA:P_OFF_VA + 2 * VA_W]
    q_hat = (qa * lax.rsqrt(_dot((qa * qa).astype(BF16), ones_ref[...]) + EPS) * qg_ref[...]).astype(BF16)
    k_hat = ka * lax.rsqrt(_dot((ka * ka).astype(BF16), ones_ref[0:2 * KA_W, 0:2 * KA_W]) + EPS) * kg_ref[...]
    k_hat16 = k_hat.astype(BF16)
    va16 = va.astype(BF16)
    prev = carry_ref[step % 2]
    carry_ref[(step + 1) % 2] = jnp.concatenate([k_hat16[tb - blk:tb], va16[tb - blk:tb]], axis=1)

    row_g = lax.broadcasted_iota(jnp.int32, (rows4, 1), 0) // blk
    key_is_prev = lax.broadcasted_iota(jnp.int32, (rows4, 2 * blk), 1) < blk
    first_penalty = jnp.where(step == 0, -NEG_INF, 0.0)
    lane_low = lax.broadcasted_iota(jnp.int32, (blk, PAIR), 1) < A_HD
    ones_cols = jnp.ones((2 * blk, PAIR), BF16)

    n_blk = tb // blk
    att = [(j, kv) for j in range(n_blk) for kv in range(A_KV)]
    ret = [(j, hb) for j in range(n_blk) for hb in range(B_HEADS)]

    sinks, scores = {}, {}
    for j, kv in att:
        r0 = j * blk
        kcol = slice(kv * PAIR, (kv + 1) * PAIR)
        if j == 0:
            k2 = jnp.concatenate([prev[:, kcol], k_hat16[0:blk, kcol]], axis=0)
        else:
            k2 = k_hat16[r0 - blk:r0 + blk, kcol]
        q4 = jnp.concatenate(
            [jnp.where(lane_low == (g % 2 == 0),
                       q_hat[r0:r0 + blk, (kv * A_GROUP + g - g % 2) * A_HD:(kv * A_GROUP + g - g % 2 + 2) * A_HD],
                       jnp.zeros((), BF16))
             for g in range(A_GROUP)], axis=0)
        scores[j, kv] = _dot_nt(q4, k2)
        sink = jnp.zeros((rows4, 1), F32)
        for g in range(A_GROUP):
            sink = jnp.where(row_g == g, sink_ref[kv * A_GROUP + g], sink)
        sinks[j, kv] = sink
    qb, vb, kt, inner = {}, {}, {}, {}
    for j, hb in ret:
        r0 = j * blk
        qb[j, hb] = proj[r0:r0 + blk, P_OFF_QB + hb * B_KD:P_OFF_QB + (hb + 1) * B_KD].astype(BF16)
        vb[j, hb] = proj[r0:r0 + blk, P_OFF_VB + hb * B_VD:P_OFF_VB + (hb + 1) * B_VD].astype(BF16)
        kt[j, hb] = kt_all[hb * B_KD:(hb + 1) * B_KD, r0:r0 + blk]
        inner[j, hb] = _dot(qb[j, hb], kt[j, hb].astype(BF16))

    probs, maxes = {}, {}
    for j, kv in att:
        s = scores[j, kv] * (A_HD ** -0.5) - bias_ref[kv]
        if j == 0:
            s = s - jnp.where(key_is_prev, first_penalty, 0.0)
        mx = jnp.maximum(jnp.max(s, axis=-1, keepdims=True), sinks[j, kv])
        probs[j, kv] = jnp.exp(s - mx).astype(BF16)
        maxes[j, kv] = mx
    state = {}
    for hb in range(B_HEADS):
        d_c = math.exp(_ret_log_gamma(hb) * blk)
        state[0, hb] = ps_ref[hb]
        for j in range(n_blk):
            state[j + 1, hb] = state[j, hb] * d_c + _dot((kt[j, hb] * dec_ref[hb, 2]).astype(BF16), vb[j, hb])
        ps_ref[hb] = state[n_blk, hb]

    for j, kv in att:
        r0 = j * blk
        kcol = slice(kv * PAIR, (kv + 1) * PAIR)
        vcol = slice(2 * KA_W + kv * PAIR, 2 * KA_W + (kv + 1) * PAIR)
        if j == 0:
            v2 = jnp.concatenate([prev[:, vcol], va16[0:blk, kcol]], axis=0)
        else:
            v2 = va16[r0 - blk:r0 + blk, kcol]
        pv = _dot(probs[j, kv], jnp.concatenate([v2, ones_cols], axis=1))
        o4 = pv[:, 0:PAIR] / (pv[:, PAIR:2 * PAIR] + jnp.exp(sinks[j, kv] - maxes[j, kv]))
        for g in range(A_GROUP):
            hd = kv * A_GROUP + g
            half = slice((hd % 2) * A_HD, (hd % 2 + 1) * A_HD)
            mix_ref[r0:r0 + blk, hd * A_HD:(hd + 1) * A_HD] = o4[g * blk:(g + 1) * blk, half]
    o_ret = {}
    for j, hb in ret:
        o_ret[j, hb] = (_dot((inner[j, hb] * dec_ref[hb, 0]).astype(BF16), vb[j, hb])
                        + _dot(qb[j, hb], state[j, hb].astype(BF16)) * dec_ref[hb, 1])

    cen = {k: o_ret[k] - jnp.mean(o_ret[k], axis=-1, keepdims=True) for k in ret}
    var = {k: jnp.mean(cen[k] * cen[k], axis=-1, keepdims=True) for k in ret}
    for j, hb in ret:
        r0 = j * blk
        gb = proj[r0:r0 + blk, P_OFF_GB + hb * B_VD:P_OFF_GB + (hb + 1) * B_VD]
        mix_ref[r0:r0 + blk, QA_W + hb * B_VD:QA_W + (hb + 1) * B_VD] = (
            cen[j, hb] * lax.rsqrt(var[j, hb] + EPS) * rg[:, hb * B_VD:(hb + 1) * B_VD] * (gb * jax.nn.sigmoid(gb)))

    out = _dot(mix_ref[...].astype(BF16), wout_ref[...])
    o_ref[...] = x + _gate(mod, 0, d) * out.reshape(sb, ROWS, d)

    @pl.when(step == pl.num_programs(0) - 1)
    def _():
        pk_ref[...] = _dedup(k_hat[tb - blk:tb, :])
        pv_ref[...] = _dedup(va[tb - blk:tb, :])


def _dup_heads(t, width):
    lead = t.shape[:-1]
    t = t.reshape(lead + (-1, 1, width))
    return jnp.broadcast_to(t, lead + (t.shape[-3], 2, width)).reshape(lead + (-1,))


def _mix_even_prompt(x3, mod3, gain, w_in, q_gain, k_gain, sinks, ret_gain, w_out, sb, cast=()):
    n8, _, d = x3.shape
    tb = sb * ROWS
    cast_specs = [_cast_specs(w, n8 // sb) for w in cast]
    w_main = jnp.concatenate([w_in[:, :OFF_KA], _dup_heads(w_in[:, OFF_KA:OFF_VA], A_HD),
                              _dup_heads(w_in[:, OFF_VA:OFF_QB], A_HD), w_in[:, OFF_QB:OFF_KB],
                              w_in[:, OFF_VB:]], axis=1)
    wk_t = w_in[:, OFF_KB:OFF_VB].T
    out = pl.pallas_call(
        functools.partial(_mix_even_prompt_body, len(cast)),
        grid=(n8 // sb,),
        in_specs=[pl.BlockSpec((sb, ROWS, d), lambda i: (i, 0, 0)),
                  pl.BlockSpec((1, 1, 6 * d), lambda i: (0, 0, 0)),
                  _const_spec((1, d)), _const_spec((d, P_WIDTH)), _const_spec((KB_W, d)),
                  _const_spec((1, QA_W)), _const_spec((1, 2 * KA_W)),
                  pl.BlockSpec(memory_space=pltpu.SMEM),
                  _const_spec((1, VB_W)), _const_spec((QA_W + VB_W, d))] + [c[0] for c in cast_specs],
        out_specs=[pl.BlockSpec((sb, ROWS, d), lambda i: (i, 0, 0)),
                   pl.BlockSpec((WINDOW, KA_W), lambda i: (0, 0)),
                   pl.BlockSpec((WINDOW, VA_W), lambda i: (0, 0)),
                   pl.BlockSpec((B_HEADS, B_KD, B_VD), lambda i: (0, 0, 0))] + [c[0] for c in cast_specs],
        out_shape=[jax.ShapeDtypeStruct(x3.shape, F32),
                   jax.ShapeDtypeStruct((WINDOW, KA_W), F32),
                   jax.ShapeDtypeStruct((WINDOW, VA_W), F32),
                   jax.ShapeDtypeStruct((B_HEADS, B_KD, B_VD), F32)] + [c[1] for c in cast_specs],
        scratch_shapes=[pltpu.VMEM((tb, QA_W + VB_W), F32),
                        pltpu.VMEM((2, WINDOW, 2 * KA_W + 2 * VA_W), BF16),
                        pltpu.VMEM((A_KV, A_GROUP * WINDOW, 2 * WINDOW), F32),
                        pltpu.VMEM((B_HEADS, 3, WINDOW, WINDOW), F32),
                        pltpu.VMEM((QA_W, QA_W), BF16)],
        compiler_params=_params(),
        name="mix_even_prompt",
    )(x3, mod3, gain.reshape(1, d), w_main, wk_t, jnp.tile(q_gain, A_HEADS).reshape(1, QA_W),
      jnp.tile(k_gain, 2 * A_KV).reshape(1, 2 * KA_W),
      sinks, ret_gain.reshape(1, VB_W), w_out, *cast)
    return out[:4], tuple(out[4:])


def _mix_even_sample_body(x_ref, mod_ref, gn_ref, win_ref, qg_ref, kg_ref, sink_ref, rg_ref, wout_ref,
                          ck_ref, cv_ref, s0_ref,
                          o_ref, nk_ref, nv_ref, ns_ref, mix_ref):
    x = x_ref[...]
    sb, length, d = x.shape
    tb = sb * length
    w = ck_ref.shape[1]
    mod = _load_mod(mod_ref)
    h = _modulate(x, mod, gn_ref[...], 0).reshape(tb, d).astype(BF16)
    proj = _dot(h, win_ref[...])
    qg = qg_ref[...]
    kg = kg_ref[...]
    rg = rg_ref[...]

    rows = A_GROUP * length
    qpos_c = lax.broadcasted_iota(jnp.int32, (rows, w), 0) % length
    kpos_c = lax.broadcasted_iota(jnp.int32, (rows, w), 1)
    dist_c = w + qpos_c - kpos_c
    valid_c = (dist_c >= 0) & (dist_c < WINDOW)
    qpos_n = lax.broadcasted_iota(jnp.int32, (rows, length), 0) % length
    kpos_n = lax.broadcasted_iota(jnp.int32, (rows, length), 1)
    dist_n = qpos_n - kpos_n
    valid_n = (dist_n >= 0) & (dist_n < WINDOW)
    row_g = lax.broadcasted_iota(jnp.int32, (rows, 1), 0) // length

    for kv in range(A_KV):
        lanes = slice(kv * A_HD, (kv + 1) * A_HD)
        kn = _head_rms(proj[:, OFF_KA + kv * A_HD:OFF_KA + (kv + 1) * A_HD], kg).reshape(sb, length, A_HD)
        vn = proj[:, OFF_VA + kv * A_HD:OFF_VA + (kv + 1) * A_HD].reshape(sb, length, A_HD)
        nk_ref[:, 0:w - length, lanes] = ck_ref[:, length:w, lanes]
        nv_ref[:, 0:w - length, lanes] = cv_ref[:, length:w, lanes]
        nk_ref[:, w - length:w, lanes] = kn
        nv_ref[:, w - length:w, lanes] = vn
        kc = ck_ref[:, :, lanes].astype(BF16)
        vc = cv_ref[:, :, lanes].astype(BF16)
        q4 = jnp.concatenate(
            [_head_rms(proj[:, OFF_QA + (kv * A_GROUP + g) * A_HD:OFF_QA + (kv * A_GROUP + g + 1) * A_HD], qg)
             .reshape(sb, length, A_HD) for g in range(A_GROUP)], axis=1).astype(BF16)
        slope = jnp.zeros((rows, 1), F32)
        sink = jnp.zeros((rows, 1), F32)
        for g in range(A_GROUP):
            hd = kv * A_GROUP + g
            slope = jnp.where(row_g == g, _alibi_slope(hd), slope)
            sink = jnp.where(row_g == g, sink_ref[hd], sink)
        scale = A_HD ** -0.5
        s_c = _bmm_nt(q4, kc) * scale - slope * dist_c.astype(F32)
        s_c = jnp.where(valid_c, s_c, NEG_INF)
        s_n = _bmm_nt(q4, kn.astype(BF16)) * scale - slope * dist_n.astype(F32)
        s_n = jnp.where(valid_n, s_n, NEG_INF)
        mx = jnp.maximum(jnp.maximum(jnp.max(s_c, axis=-1, keepdims=True),
                                     jnp.max(s_n, axis=-1, keepdims=True)), sink)
        p_c = jnp.exp(s_c - mx)
        p_n = jnp.exp(s_n - mx)
        den = (jnp.sum(p_c, axis=-1, keepdims=True) + jnp.sum(p_n, axis=-1, keepdims=True)
               + jnp.exp(sink - mx))
        o4 = (_bmm(p_c.astype(BF16), vc) + _bmm(p_n.astype(BF16), vn.astype(BF16))) / den
        for g in range(A_GROUP):
            hd = kv * A_GROUP + g
            mix_ref[:, hd * A_HD:(hd + 1) * A_HD] = o4[:, g * length:(g + 1) * length, :].reshape(tb, A_HD)

    for hb in range(B_HEADS):
        d_in, d_q, d_k, d_c = _ret_decay(hb, length)
        qb = proj[:, OFF_QB + hb * B_KD:OFF_QB + (hb + 1) * B_KD].reshape(sb, length, B_KD).astype(BF16)
        kb = proj[:, OFF_KB + hb * B_KD:OFF_KB + (hb + 1) * B_KD].reshape(sb, length, B_KD) * (B_KD ** -0.5)
        vb = proj[:, OFF_VB + hb * B_VD:OFF_VB + (hb + 1) * B_VD].reshape(sb, length, B_VD).astype(BF16)
        gb = proj[:, OFF_GB + hb * B_VD:OFF_GB + (hb + 1) * B_VD]
        state = s0_ref[:, hb]
        inner = _bmm_nt(qb, kb.astype(BF16)) * d_in
        o = _bmm(inner.astype(BF16), vb) + _bmm(qb, state.astype(BF16)) * d_q
        ns_ref[:, hb] = state * d_c + _bmm_tn((kb * d_k).astype(BF16), vb)
        mix_ref[:, QA_W + hb * B_VD:QA_W + (hb + 1) * B_VD] = _group_norm_gate(
            o.reshape(tb, B_VD), rg[:, hb * B_VD:(hb + 1) * B_VD], gb)

    out = _dot(mix_ref[...].astype(BF16), wout_ref[...])
    o_ref[...] = x + _gate(mod, 0, d) * out.reshape(sb, length, d)


def _mix_even_sample(x3, mod3, gain, w_in, q_gain, k_gain, sinks, ret_gain, w_out, cache_k, cache_v, state, sb):
    n, length, d = x3.shape
    in_w = w_in.shape[1]
    w = cache_k.shape[1]
    seq_spec = lambda shape: pl.BlockSpec((sb,) + shape, lambda i, _n=len(shape): (i,) + (0,) * _n)
    return pl.pallas_call(
        _mix_even_sample_body,
        grid=(n // sb,),
        in_specs=[seq_spec((length, d)), seq_spec((6 * d,)),
                  _const_spec((1, d)), _const_spec((d, in_w)),
                  _const_spec((1, A_HD)), _const_spec((1, A_HD)),
                  pl.BlockSpec(memory_space=pltpu.SMEM),
                  _const_spec((1, VB_W)), _const_spec((QA_W + VB_W, d)),
                  seq_spec((w, KA_W)), seq_spec((w, VA_W)), seq_spec((B_HEADS, B_KD, B_VD))],
        out_specs=[seq_spec((length, d)), seq_spec((w, KA_W)), seq_spec((w, VA_W)),
                   seq_spec((B_HEADS, B_KD, B_VD))],
        out_shape=[jax.ShapeDtypeStruct(x3.shape, F32),
                   jax.ShapeDtypeStruct(cache_k.shape, F32),
                   jax.ShapeDtypeStruct(cache_v.shape, F32),
                   jax.ShapeDtypeStruct(state.shape, F32)],
        scratch_shapes=[pltpu.VMEM((sb * length, QA_W + VB_W), F32)],
        compiler_params=_params(),
        name="mix_even_sample",
    )(x3, mod3, gain.reshape(1, d), w_in, q_gain.reshape(1, A_HD), k_gain.reshape(1, A_HD),
      sinks, ret_gain.reshape(1, VB_W), w_out, cache_k, cache_v, state)


def _cmul(ar, ai, br, bi):
    return ar * br - ai * bi, ar * bi + ai * br


def _s5_lambda(a_re, a_im, log_dt):
    dt = jnp.exp(log_dt)
    mag = jnp.exp(a_re * dt)
    return mag * jnp.cos(a_im * dt), mag * jnp.sin(a_im * dt)


def _s5_prep_body(t_len, n_chunks, are_ref, aim_ref, ldt_ref, btr_ref, bti_ref, ard_ref, aid_ref, ldd_ref,
                  lre_ref, lim_ref, bbr_ref, bbi_ref, pwr_ref, pwi_ref):
    a_re = are_ref[...]
    a_im = aim_ref[...]
    lam_re, lam_im = _s5_lambda(a_re, a_im, ldt_ref[...])
    den = a_re * a_re + a_im * a_im
    n_re = lam_re - 1.0
    n_im = lam_im
    f_re = (n_re * a_re + n_im * a_im) / den
    f_im = (n_im * a_re - n_re * a_im) / den
    br = btr_ref[...]
    bi = bti_ref[...]
    lre_ref[...] = lam_re
    lim_ref[...] = lam_im
    bbr_ref[...] = f_re * br - f_im * bi
    bbi_ref[...] = f_re * bi + f_im * br
    lam_re, lam_im = _s5_lambda(ard_ref[...], aid_ref[...], ldd_ref[...])
    cr, ci = lam_re, lam_im
    for t in range(t_len):
        pwr_ref[t] = cr
        pwi_ref[t] = ci
        if t + 1 < t_len:
            cr, ci = _cmul(cr, ci, lam_re, lam_im)
    tr, ti = cr, ci
    cr, ci = jnp.ones_like(lam_re), jnp.zeros_like(lam_im)
    for m in range(n_chunks + 1):
        pwr_ref[t_len + m] = cr
        pwi_ref[t_len + m] = ci
        cr, ci = _cmul(cr, ci, tr, ti)


def _s5_prep(a_re, a_im, log_dt, b_re, b_im, t_len, n_chunks):
    g, p = a_re.shape
    k = b_re.shape[-1]
    bt_re = jnp.swapaxes(b_re, 1, 2)
    bt_im = jnp.swapaxes(b_im, 1, 2)
    n_pow = t_len + n_chunks + 1
    dense = (g // S5_OCT, S5_OCT * p)
    out = pl.pallas_call(
        functools.partial(_s5_prep_body, t_len, n_chunks),
        out_shape=[jax.ShapeDtypeStruct((g, 1, p), F32), jax.ShapeDtypeStruct((g, 1, p), F32),
                   jax.ShapeDtypeStruct((g, k, p), F32), jax.ShapeDtypeStruct((g, k, p), F32),
                   jax.ShapeDtypeStruct((n_pow,) + dense, F32), jax.ShapeDtypeStruct((n_pow,) + dense, F32)],
        name="s5_prep",
    )(a_re.reshape(g, 1, p), a_im.reshape(g, 1, p), log_dt.reshape(g, 1, 1), bt_re, bt_im,
      a_re.reshape(dense), a_im.reshape(dense), jnp.broadcast_to(log_dt[:, None], (g, p)).reshape(dense))
    return out


def _block_diag(t):
    g, a, b = t.shape
    t = t.reshape(g // S5_OCT, S5_OCT, a, b)
    eye = jnp.eye(S5_OCT, dtype=t.dtype)
    out = t[:, :, :, None, :] * eye[None, :, None, :, None]
    return out.reshape(g // S5_OCT, S5_OCT * a, S5_OCT * b)


def _gelu_glu_out(x, mod, y, u, dskip, glua_ref, glub_ref):
    sb, rows, d = x.shape
    y = y + dskip * u
    yg = jax.nn.gelu(y, approximate=True).astype(BF16)
    out = _dot(yg, glua_ref[...]) * jax.nn.sigmoid(_dot(yg, glub_ref[...]))
    return x + _gate(mod, 0, d) * out.reshape(sb, rows, d)


def _mix_odd_prompt_body(x_ref, xn_ref, mod_ref, gn_ref, bblk_ref, lam_ref, pwb_ref, ltp_ref, cre_ref, cim_ref,
                         dskip_ref, glua_ref, glub_ref, o_ref, hre_ref, him_ref,
                         un_ref, l0_ref, l1_ref, up0_ref, up1_ref, pt_ref):
    step = pl.program_id(0)
    sb, _, d = x_ref.shape
    tm = sb * ROWS
    t_len = tm // ROWS
    n_oct = bblk_ref.shape[0]
    half = bblk_ref.shape[2] // 2
    mod = mod_ref[...]

    def permuted_input(src_ref, up_dst):
        u = _modulate(src_ref[...], mod, gn_ref[...], 0).reshape(tm, d)
        for k in range(d // 128):
            un_ref[k] = u[:, k * 128:(k + 1) * 128]
        up = jnp.concatenate(
            [jnp.concatenate([un_ref[k, pl.ds(t, ROWS, stride=t_len), :] for k in range(d // 128)], axis=1)
             for t in range(t_len)], axis=0)
        up_dst[...] = up
        return up.astype(BF16)

    @pl.when(step == 0)
    def _():
        hre_ref[...] = jnp.zeros_like(hre_ref)
        him_ref[...] = jnp.zeros_like(him_ref)
        nat = lax.broadcasted_iota(jnp.int32, (tm, tm), 0)
        prm = lax.broadcasted_iota(jnp.int32, (tm, tm), 1)
        pt_ref[...] = jnp.where(prm == (nat % t_len) * ROWS + nat // t_len, 1.0, 0.0).astype(BF16)
        up16 = permuted_input(x_ref, up0_ref)
        for s in range(n_oct):
            l0_ref[s] = _dot(up16[:, s * 128:(s + 1) * 128], bblk_ref[s])

    def run(l_ref, l_next, up_ref, up_next):
        for s0 in range(0, n_oct, 2):
            pair = (s0, s0 + 1)
            lam_re = [jnp.broadcast_to(lam_ref[0, s:s + 1, :], (ROWS, half)) for s in pair]
            lam_im = [jnp.broadcast_to(lam_ref[1, s:s + 1, :], (ROWS, half)) for s in pair]

            def local_step(t, carry):
                r = pl.multiple_of(t * ROWS, ROWS)
                out = []
                for i, s in enumerate(pair):
                    h_re, h_im = carry[2 * i], carry[2 * i + 1]
                    n_re = lam_re[i] * h_re - lam_im[i] * h_im + l_ref[s, pl.ds(r, ROWS), 0:half]
                    n_im = lam_re[i] * h_im + lam_im[i] * h_re + l_ref[s, pl.ds(r, ROWS), half:2 * half]
                    l_ref[s, pl.ds(r, ROWS), 0:half] = n_re
                    l_ref[s, pl.ds(r, ROWS), half:2 * half] = n_im
                    out += [n_re, n_im]
                return tuple(out)

            zero = jnp.zeros((ROWS, half), F32)
            lax.fori_loop(0, t_len, local_step, (zero, zero, zero, zero), unroll=2)

        next16 = permuted_input(xn_ref, up_next)
        chunk = lax.broadcasted_iota(jnp.int32, (ROWS, half), 0)
        y_parts = []
        for s in range(n_oct):
            l_next[s] = _dot(next16[:, s * 128:(s + 1) * 128], bblk_ref[s])
            p_re = l_ref[s, tm - ROWS:tm, 0:half]
            p_im = l_ref[s, tm - ROWS:tm, half:2 * half]
            for i, sh in enumerate((1, 2, 4)):
                m_re, m_im = _cmul(ltp_ref[s, i:i + 1, 0:half], ltp_ref[s, i:i + 1, half:2 * half],
                                   jnp.where(chunk >= sh, pltpu.roll(p_re, sh, 0), 0.0),
                                   jnp.where(chunk >= sh, pltpu.roll(p_im, sh, 0), 0.0))
                p_re, p_im = p_re + m_re, p_im + m_im
            hin_re = jnp.broadcast_to(hre_ref[s:s + 1, :], (ROWS, half))
            hin_im = jnp.broadcast_to(him_ref[s:s + 1, :], (ROWS, half))
            m_re, m_im = _cmul(ltp_ref[s, ROWS:2 * ROWS, 0:half], ltp_ref[s, ROWS:2 * ROWS, half:2 * half],
                               hin_re, hin_im)
            st_re = m_re + jnp.where(chunk >= 1, pltpu.roll(p_re, 1, 0), 0.0)
            st_im = m_im + jnp.where(chunk >= 1, pltpu.roll(p_im, 1, 0), 0.0)
            m_re, m_im = _cmul(ltp_ref[s, 3:4, 0:half], ltp_ref[s, 3:4, half:2 * half],
                               hre_ref[s:s + 1, :], him_ref[s:s + 1, :])
            hre_ref[s:s + 1, :] = m_re + p_re[ROWS - 1:ROWS, :]
            him_ref[s:s + 1, :] = m_im + p_im[ROWS - 1:ROWS, :]
            loc = l_ref[s].reshape(t_len, ROWS, 2 * half)
            pw = pwb_ref[s]
            f_re, f_im = _cmul(pw[:, :, 0:half], pw[:, :, half:2 * half], st_re[None], st_im[None])
            hs_re = (loc[:, :, 0:half] + f_re).reshape(tm, half).astype(BF16)
            hs_im = (loc[:, :, half:2 * half] + f_im).reshape(tm, half).astype(BF16)
            y_parts.append(_dot(hs_re, cre_ref[s]) - _dot(hs_im, cim_ref[s]))

        y = jnp.concatenate(y_parts, axis=1) + dskip_ref[...] * up_ref[...]
        yg = jax.nn.gelu(y, approximate=True).astype(BF16)
        yn = _dot(pt_ref[...], yg).astype(BF16)
        out = _dot(yn, glua_ref[...]) * jax.nn.sigmoid(_dot(yn, glub_ref[...]))
        o_ref[...] = x_ref[...] + _gate(mod, 0, d) * out.reshape(sb, ROWS, d)

    @pl.when(step % 2 == 0)
    def _():
        run(l0_ref, l1_ref, up0_ref, up1_ref)

    @pl.when(step % 2 == 1)
    def _():
        run(l1_ref, l0_ref, up1_ref, up0_ref)


def _mix_odd_prompt(x3, mod3, gain, bblk, lam_d, pwb, ltp, cre, cim, dskip, glu_a, glu_b, sb):
    n8, _, d = x3.shape
    n_oct, kin, wid = bblk.shape
    half = wid // 2
    tm = sb * ROWS
    n_tiles = n8 // sb
    proj_buf = pltpu.VMEM((n_oct, tm, wid), F32)
    perm_buf = pltpu.VMEM((tm, d), F32)
    return pl.pallas_call(
        _mix_odd_prompt_body,
        grid=(n_tiles,),
        in_specs=[pl.BlockSpec((sb, ROWS, d), lambda i: (i, 0, 0)),
                  pl.BlockSpec((sb, ROWS, d), lambda i: (jnp.minimum(i + 1, n_tiles - 1), 0, 0)),
                  pl.BlockSpec((1, 1, 6 * d), lambda i: (0, 0, 0)),
                  _const_spec((1, d)), _const_spec(bblk.shape), _const_spec(lam_d.shape),
                  _const_spec(pwb.shape), _const_spec(ltp.shape),
                  _const_spec(cre.shape), _const_spec(cim.shape), _const_spec((1, d)),
                  _const_spec((d, d)), _const_spec((d, d))],
        out_specs=[pl.BlockSpec((sb, ROWS, d), lambda i: (i, 0, 0)),
                   pl.BlockSpec((n_oct, half), lambda i: (0, 0)),
                   pl.BlockSpec((n_oct, half), lambda i: (0, 0))],
        out_shape=[jax.ShapeDtypeStruct(x3.shape, F32),
                   jax.ShapeDtypeStruct((n_oct, half), F32),
                   jax.ShapeDtypeStruct((n_oct, half), F32)],
        scratch_shapes=[pltpu.VMEM((d // 128, tm, 128), F32), proj_buf, proj_buf, perm_buf, perm_buf,
                        pltpu.VMEM((tm, tm), BF16)],
        compiler_params=_params(),
        name="mix_odd_prompt",
    )(x3, x3, mod3, gain.reshape(1, d), bblk, lam_d, pwb, ltp, cre, cim, dskip.reshape(1, d), glu_a, glu_b)


def _mix_odd_sample_body(x_ref, mod_ref, gn_ref, bblk_ref, lam_ref, cre_ref, cim_ref, dskip_ref,
                         glua_ref, glub_ref, sre_ref, sim_ref, o_ref, nre_ref, nim_ref, d_ref, y_ref):
    x = x_ref[...]
    sb, length, d = x.shape
    tm = sb * length
    n_oct = bblk_ref.shape[0]
    half = bblk_ref.shape[2] // 2
    mod = _load_mod(mod_ref)
    u = _modulate(x, mod, gn_ref[...], 0).reshape(tm, d)
    u16 = u.astype(BF16)
    n_ch = bblk_ref.shape[2] // 128
    hc = n_ch // 2
    for s in range(n_oct):
        bu = _dot(u16[:, s * 128:(s + 1) * 128], bblk_ref[s])
        for c in range(n_ch):
            d_ref[c] = bu[:, c * 128:(c + 1) * 128]
        lam_re = lam_ref[0, s:s + 1, :]
        lam_im = lam_ref[1, s:s + 1, :]
        h_re = sre_ref[:, s * half:(s + 1) * half]
        h_im = sim_ref[:, s * half:(s + 1) * half]
        for t in range(length):
            b_re = jnp.concatenate([d_ref[c, pl.ds(t, sb, stride=length), :] for c in range(hc)], axis=1)
            b_im = jnp.concatenate([d_ref[hc + c, pl.ds(t, sb, stride=length), :] for c in range(hc)], axis=1)
            n_re = lam_re * h_re - lam_im * h_im + b_re
            n_im = lam_re * h_im + lam_im * h_re + b_im
            for c in range(hc):
                d_ref[c, pl.ds(t, sb, stride=length), :] = n_re[:, c * 128:(c + 1) * 128]
                d_ref[hc + c, pl.ds(t, sb, stride=length), :] = n_im[:, c * 128:(c + 1) * 128]
            h_re, h_im = n_re, n_im
        nre_ref[:, s * half:(s + 1) * half] = h_re
        nim_ref[:, s * half:(s + 1) * half] = h_im
        hs_re = jnp.concatenate([d_ref[c] for c in range(hc)], axis=1)
        hs_im = jnp.concatenate([d_ref[hc + c] for c in range(hc)], axis=1)
        y_ref[:, s * 128:(s + 1) * 128] = (_dot(hs_re.astype(BF16), cre_ref[s])
                                           - _dot(hs_im.astype(BF16), cim_ref[s]))
    o_ref[...] = _gelu_glu_out(x, mod, y_ref[...], u, dskip_ref[...], glua_ref, glub_ref)


def _mix_odd_sample(x3, mod3, gain, bblk, lam_d, cre, cim, dskip, glu_a, glu_b, s_re, s_im, sb):
    n, length, d = x3.shape
    n_oct, kin, wid = bblk.shape
    tm = sb * length
    nstate = s_re.shape[1]
    return pl.pallas_call(
        _mix_odd_sample_body,
        grid=(n // sb,),
        in_specs=[pl.BlockSpec((sb, length, d), lambda i: (i, 0, 0)),
                  pl.BlockSpec((sb, 6 * d), lambda i: (i, 0)),
                  _const_spec((1, d)), _const_spec(bblk.shape), _const_spec(lam_d.shape),
                  _const_spec(cre.shape), _const_spec(cim.shape), _const_spec((1, d)),
                  _const_spec((d, d)), _const_spec((d, d)),
                  pl.BlockSpec((sb, nstate), lambda i: (i, 0)),
                  pl.BlockSpec((sb, nstate), lambda i: (i, 0))],
        out_specs=[pl.BlockSpec((sb, length, d), lambda i: (i, 0, 0)),
                   pl.BlockSpec((sb, nstate), lambda i: (i, 0)),
                   pl.BlockSpec((sb, nstate), lambda i: (i, 0))],
        out_shape=[jax.ShapeDtypeStruct(x3.shape, F32),
                   jax.ShapeDtypeStruct(s_re.shape, F32),
                   jax.ShapeDtypeStruct(s_im.shape, F32)],
        scratch_shapes=[pltpu.VMEM((wid // 128, tm, 128), F32), pltpu.VMEM((tm, d), F32)],
        compiler_params=_params(),
        name="mix_odd_sample",
    )(x3, mod3, gain.reshape(1, d), bblk, lam_d, cre, cim, dskip.reshape(1, d), glu_a, glu_b, s_re, s_im)


def _pick(n, want):
    while n % want:
        want //= 2
    return max(want, 1)


def kernel(x_prompt, x_sample, cache_win_k, cache_win_v, state_ret, state_s5_re, state_s5_im, c_prompt, c_sample, ada_w, ada_b, norm_mix, norm_ffn, ffn_wg, ffn_wu, ffn_wd, even_w_in, even_q_gain, even_k_gain, even_sinks, even_ret_gain, even_w_out, odd_A_re, odd_A_im, odd_log_dt, odd_B_re, odd_B_im, odd_C_re, odd_C_im, odd_D, odd_glu_a, odd_glu_b):
    bp, lp, d = x_prompt.shape
    ns, ls, _ = x_sample.shape
    assert bp == 1 and ls == ROWS and lp % WINDOW == 0
    w = cache_win_k.shape[2]
    groups, p_state = odd_A_re.shape[1:]

    n_c = bp + ns
    n_pad = -n_c % ROWS
    c_all = jnp.concatenate([c_prompt, c_sample, jnp.zeros((n_pad, d), F32)], axis=0)
    mod = _adaln(c_all, ada_w, ada_b)
    mod_p = [mod[l, 0:1].reshape(1, 1, 6 * d) for l in range(2)]
    mod_s = [mod[l, bp:bp + ns] for l in range(2)]

    bf = lambda t: t.astype(BF16)
    w_in, w_out = bf(even_w_in[0]), bf(even_w_out[0])

    xp = x_prompt.reshape(lp // ROWS, ROWS, d)
    xs = x_sample

    sb_p = _pick(lp // ROWS, 32)
    sb_ffn = _pick(lp // ROWS, 64)
    sb_s = _pick(ns, 32)
    sb_s_even = _pick(ns, 16)

    (xp, p_k, p_v, p_ret), ffn0 = _mix_even_prompt(xp, mod_p[0], norm_mix[0], w_in, even_q_gain[0], even_k_gain[0],
                                                   even_sinks[0], even_ret_gain[0], w_out, sb_ffn,
                                                   cast=(ffn_wg[0], ffn_wu[0], ffn_wd[0]))
    xs, s_k, s_v, s_ret = _mix_even_sample(xs, mod_s[0], norm_mix[0], w_in, even_q_gain[0], even_k_gain[0],
                                           even_sinks[0], even_ret_gain[0], w_out,
                                           cache_win_k[0].reshape(ns, w, KA_W), cache_win_v[0].reshape(ns, w, VA_W),
                                           state_ret[0], sb_s_even)
    xp, (wg1, wu1, wd1, glu_a, glu_b) = _ffn(xp, mod_p[0], norm_ffn[0], *ffn0, sb_ffn,
                                             cast=(ffn_wg[1], ffn_wu[1], ffn_wd[1], odd_glu_a[0], odd_glu_b[0]))
    xs, _ = _ffn(xs, mod_s[0], norm_ffn[0], *ffn0, sb_s)

    t_len = sb_p
    lam_re, lam_im, bbt_re, bbt_im, pw_re, pw_im = _s5_prep(odd_A_re[0], odd_A_im[0], odd_log_dt[0],
                                                            odd_B_re[0], odd_B_im[0], t_len, ROWS)
    n_oct = groups // S5_OCT
    half = S5_OCT * p_state
    bblk = bf(jnp.concatenate([_block_diag(bbt_re), _block_diag(bbt_im)], axis=-1))
    cre = bf(_block_diag(jnp.swapaxes(odd_C_re[0], 1, 2)))
    cim = bf(_block_diag(jnp.swapaxes(odd_C_im[0], 1, 2)))
    lam_d = jnp.stack([lam_re.reshape(n_oct, half), lam_im.reshape(n_oct, half)])

    pw_d = jnp.concatenate([pw_re, pw_im], axis=-1)
    pwb = jnp.broadcast_to(jnp.swapaxes(pw_d[:t_len], 0, 1)[:, :, None, :], (n_oct, t_len, ROWS, 2 * half))
    lt = jnp.swapaxes(pw_d[t_len:], 0, 1)
    ltp = jnp.concatenate([lt[:, 1:2], lt[:, 2:3], lt[:, 4:5], lt[:, 8:9], jnp.zeros_like(lt[:, 0:4]), lt[:, 0:8]],
                          axis=1)
    xp, p_re, p_im = _mix_odd_prompt(xp, mod_p[1], norm_mix[1], bblk, lam_d, pwb, ltp, cre, cim, odd_D[0],
                                     glu_a, glu_b, sb_p)
    xs, s_re, s_im = _mix_odd_sample(xs, mod_s[1], norm_mix[1], bblk, lam_d, cre, cim, odd_D[0], glu_a, glu_b,
                                     state_s5_re[0].reshape(ns, groups * p_state),
                                     state_s5_im[0].reshape(ns, groups * p_state), sb_s)
    xp, _ = _ffn(xp, mod_p[1], norm_ffn[1], wg1, wu1, wd1, sb_ffn)
    xs, _ = _ffn(xs, mod_s[1], norm_ffn[1], wg1, wu1, wd1, sb_s)

    y_prompt = xp.reshape(bp, lp, d)
    y_sample = xs
    return (y_prompt, y_sample,
            p_k.reshape(1, bp, WINDOW, A_KV, A_HD), p_v.reshape(1, bp, WINDOW, A_KV, A_HD),
            p_ret.reshape(1, bp, B_HEADS, B_KD, B_VD),
            p_re.reshape(1, bp, groups, p_state), p_im.reshape(1, bp, groups, p_state),
            s_k.reshape(1, ns, w, A_KV, A_HD), s_v.reshape(1, ns, w, A_KV, A_HD),
            s_ret.reshape(1, ns, B_HEADS, B_KD, B_VD),
            s_re.reshape(1, ns, groups, p_state), s_im.reshape(1, ns, groups, p_state))
```

```python
import functools
import math

import jax
import jax.numpy as jnp
from jax import lax
from jax.experimental import pallas as pl
from jax.experimental.pallas import tpu as pltpu

F32 = jnp.float32
BF16 = jnp.bfloat16

EPS = 1e-6
NEG_INF = -1e30
ROWS = 8

A_HEADS, A_KV, A_GROUP, A_HD = 8, 2, 4, 64
WINDOW = 128
B_HEADS, B_KD, B_VD = 4, 128, 128
S5_GROUP, S5_STATE = 16, 64
S5_OCT = 8

QA_W, KA_W, VA_W = A_HEADS * A_HD, A_KV * A_HD, A_KV * A_HD
QB_W, KB_W, VB_W, GB_W = B_HEADS * B_KD, B_HEADS * B_KD, B_HEADS * B_VD, B_HEADS * B_VD
OFF_QA = 0
OFF_KA = OFF_QA + QA_W
OFF_VA = OFF_KA + KA_W
OFF_QB = OFF_VA + VA_W
OFF_KB = OFF_QB + QB_W
OFF_VB = OFF_KB + KB_W
OFF_GB = OFF_VB + VB_W

VMEM_LIMIT = 56 * 1024 * 1024


def _ret_log_gamma(h):
    return math.log1p(-(2.0 ** (-5.0 - h)))


def _alibi_slope(h):
    return 2.0 ** (-8.0 * (h + 1) / A_HEADS)


def _const_spec(shape):
    nd = len(shape)
    return pl.BlockSpec(shape, lambda i, _n=nd: (0,) * _n, pipeline_mode=pl.Buffered(1))


def _params():
    return pltpu.CompilerParams(dimension_semantics=("arbitrary",), vmem_limit_bytes=VMEM_LIMIT)


def _dot(a, b):
    return jnp.dot(a, b, preferred_element_type=F32)


def _dot_nt(a, b):
    return lax.dot_general(a, b, (((1,), (1,)), ((), ())), preferred_element_type=F32)


def _dot_tn(a, b):
    return lax.dot_general(a, b, (((0,), (0,)), ((), ())), preferred_element_type=F32)


def _bmm(a, b):
    return lax.dot_general(a, b, (((2,), (1,)), ((0,), (0,))), preferred_element_type=F32)


def _bmm_nt(a, b):
    return lax.dot_general(a, b, (((2,), (2,)), ((0,), (0,))), preferred_element_type=F32)


def _bmm_tn(a, b):
    return lax.dot_general(a, b, (((1,), (1,)), ((0,), (0,))), preferred_element_type=F32)


def _rms(x, g):
    return x * lax.rsqrt(jnp.mean(x * x, axis=-1, keepdims=True) + EPS) * g


def _modulate(x3, mod, gain, which):
    d = x3.shape[-1]
    sh = mod[:, :, (3 * which) * d:(3 * which + 1) * d]
    sc = mod[:, :, (3 * which + 1) * d:(3 * which + 2) * d]
    return _rms(x3, gain) * (1.0 + sc) + sh


def _load_mod(mod_ref):
    m = mod_ref[...]
    return m if m.ndim == 3 else m[:, None, :]


def _gate(mod, which, d):
    return mod[:, :, (3 * which + 2) * d:(3 * which + 3) * d]


def _adaln_body(c_ref, w_ref, b_ref, o_ref):
    c = c_ref[...]
    a = (c * jax.nn.sigmoid(c)).astype(BF16)
    o_ref[0] = _dot(a, w_ref[0].astype(BF16)) + b_ref[0]


def _adaln(c_all, ada_w, ada_b):
    depth, d, n = ada_w.shape
    r = c_all.shape[0]
    tn = 1536
    return pl.pallas_call(
        _adaln_body,
        grid=(depth, n // tn),
        in_specs=[pl.BlockSpec((r, d), lambda l, j: (0, 0)),
                  pl.BlockSpec((1, d, tn), lambda l, j: (l, 0, j)),
                  pl.BlockSpec((1, 1, tn), lambda l, j: (l, 0, j))],
        out_specs=pl.BlockSpec((1, r, tn), lambda l, j: (l, 0, j)),
        out_shape=jax.ShapeDtypeStruct((depth, r, n), F32),
        compiler_params=pltpu.CompilerParams(dimension_semantics=("arbitrary", "arbitrary"),
                                             vmem_limit_bytes=VMEM_LIMIT),
        name="adaln",
    )(c_all, ada_w, ada_b.reshape(depth, 1, n))


BF16_ROWS = 16


def _cast_specs(job, n_steps):
    w, layer = job
    _, rows, cols = w.shape
    hold = 1
    while rows % (n_steps // hold) or (rows // (n_steps // hold)) % BF16_ROWS:
        hold *= 2
        assert hold <= n_steps and n_steps % hold == 0
    blk = rows // (n_steps // hold)
    src = pl.BlockSpec((None, blk, cols), lambda i, _h=hold, _l=layer: (_l, i // _h, 0))
    dst = pl.BlockSpec((blk, cols), lambda i, _h=hold: (i // _h, 0))
    return src, dst, jax.ShapeDtypeStruct((rows, cols), BF16)


def _cast_blocks(in_refs, out_refs):
    for src, dst in zip(in_refs, out_refs):
        dst[...] = src[...].astype(BF16)


def _ffn_body(n_cast, x_ref, mod_ref, gn_ref, wg_ref, wu_ref, wd_ref, *refs):
    cast_in, o_ref, cast_out = refs[:n_cast], refs[n_cast], refs[n_cast + 1:]
    x = x_ref[...]
    sb, _, d = x.shape
    mod = _load_mod(mod_ref)
    h = _modulate(x, mod, gn_ref[...], 1).reshape(sb * ROWS, d).astype(BF16)
    a = _dot(h, wg_ref[...])
    b = _dot(h, wu_ref[...])
    act = (a * jax.nn.sigmoid(a) * b).astype(BF16)
    y = _dot(act, wd_ref[...])
    o_ref[...] = x + _gate(mod, 1, d) * y.reshape(sb, ROWS, d)
    _cast_blocks(cast_in, cast_out)


def _ffn(x3, mod3, gain, wg, wu, wd, sb, cast=()):
    n8, _, d = x3.shape
    f = wg.shape[1]
    n_steps = n8 // sb
    per_seq = mod3.ndim == 2
    mod_spec = (pl.BlockSpec((sb, 6 * d), lambda i: (i, 0)) if per_seq
                else pl.BlockSpec((1, 1, 6 * d), lambda i: (0, 0, 0)))
    cast_specs = [_cast_specs(w, n_steps) for w in cast]
    x_spec = pl.BlockSpec((sb, ROWS, d), lambda i: (i, 0, 0))
    out = pl.pallas_call(
        functools.partial(_ffn_body, len(cast)),
        grid=(n_steps,),
        in_specs=[x_spec, mod_spec, _const_spec((1, d)), _const_spec((d, f)), _const_spec((d, f)),
                  _const_spec((f, d))] + [c[0] for c in cast_specs],
        out_specs=[x_spec] + [c[1] for c in cast_specs],
        out_shape=[jax.ShapeDtypeStruct(x3.shape, F32)] + [c[2] for c in cast_specs],
        compiler_params=_params(),
        name="ffn",
    )(x3, mod3, gain.reshape(1, d), wg, wu, wd, *[w for w, _ in cast])
    return out[0], tuple(out[1:])


def _head_rms(t, g):
    return t * lax.rsqrt(jnp.mean(t * t, axis=-1, keepdims=True) + EPS) * g


def _group_norm_gate(o, gain, gate):
    mu = jnp.mean(o, axis=-1, keepdims=True)
    var = jnp.mean(jnp.square(o - mu), axis=-1, keepdims=True)
    return (o - mu) * lax.rsqrt(var + EPS) * gain * (gate * jax.nn.sigmoid(gate))


def _ret_decay(hb, c):
    lg = _ret_log_gamma(hb)
    ii = lax.broadcasted_iota(jnp.int32, (c, c), 0)
    jj = lax.broadcasted_iota(jnp.int32, (c, c), 1)
    diff = (ii - jj).astype(F32)
    d_in = jnp.where(diff >= 0, jnp.exp(lg * jnp.maximum(diff, 0.0)), 0.0)
    row = lax.broadcasted_iota(jnp.int32, (c, B_KD), 0).astype(F32)
    d_q = jnp.exp(lg * (row + 1.0))
    d_k = jnp.exp(lg * (c - 1.0 - row))
    d_c = math.exp(lg * c)
    return d_in, d_q, d_k, d_c


P_OFF_KA = QA_W
P_OFF_VA = P_OFF_KA + 2 * KA_W
P_OFF_QB = P_OFF_VA + 2 * VA_W
P_OFF_VB = P_OFF_QB + QB_W
P_OFF_GB = P_OFF_VB + VB_W
P_WIDTH = P_OFF_GB + GB_W
PAIR = 2 * A_HD


def _dedup(t):
    low = lax.broadcasted_iota(jnp.int32, (t.shape[0], PAIR), 1) < A_HD
    return jnp.where(low, t[:, 0:PAIR], t[:, PAIR:2 * PAIR])


def _mix_even_prompt_body(n_cast, x_ref, mod_ref, gn_ref, win_ref, wkt_ref, qg_ref, kg_ref, sink_ref, rg_ref, wout_ref,
                          *refs):
    cast_in, refs = refs[:n_cast], refs[n_cast:]
    o_ref, pk_ref, pv_ref, ps_ref = refs[:4]
    cast_out = refs[4:4 + n_cast]
    mix_ref, carry_ref, bias_ref, dec_ref, ones_ref = refs[4 + n_cast:]
    _cast_blocks(cast_in, cast_out)
    step = pl.program_id(0)
    blk = WINDOW
    rows4 = A_GROUP * blk

    @pl.when(step == 0)
    def _():
        carry_ref[...] = jnp.zeros_like(carry_ref)
        ps_ref[...] = jnp.zeros_like(ps_ref)
        er = lax.broadcasted_iota(jnp.int32, ones_ref.shape, 0) // A_HD
        ec = lax.broadcasted_iota(jnp.int32, ones_ref.shape, 1) // A_HD
        ones_ref[...] = jnp.where(er == ec, 1.0 / A_HD, 0.0).astype(BF16)
        row = lax.broadcasted_iota(jnp.int32, (rows4, 2 * blk), 0)
        dist = row % blk + blk - lax.broadcasted_iota(jnp.int32, (rows4, 2 * blk), 1)
        in_window = (dist >= 0) & (dist < WINDOW)
        for kv in range(A_KV):
            slope = jnp.zeros((rows4, 2 * blk), F32)
            for g in range(A_GROUP):
                slope = jnp.where(row // blk == g, _alibi_slope(kv * A_GROUP + g), slope)
            bias_ref[kv] = jnp.where(in_window, slope * dist.astype(F32), -NEG_INF)
        for hb in range(B_HEADS):
            lg = _ret_log_gamma(hb)
            ii = lax.broadcasted_iota(jnp.int32, (blk, blk), 0).astype(F32)
            jj = lax.broadcasted_iota(jnp.int32, (blk, blk), 1).astype(F32)
            diff = ii - jj
            dec_ref[hb, 0] = jnp.where(diff >= 0, jnp.exp(lg * jnp.maximum(diff, 0.0)), 0.0)
            dec_ref[hb, 1] = jnp.exp(lg * (ii + 1.0))
            dec_ref[hb, 2] = jnp.exp(lg * (blk - 1.0 - jj))

    x = x_ref[...]
    sb, _, d = x.shape
    tb = sb * ROWS
    mod = mod_ref[...]
    h = _modulate(x, mod, gn_ref[...], 0).reshape(tb, d).astype(BF16)
    proj = _dot(h, win_ref[...])
    kt_all = _dot_nt(wkt_ref[...], h) * (B_KD ** -0.5)
    rg = rg_ref[...]

    qa = proj[:, 0:QA_W]
    ka = proj[:, P_OFF_KA:P_OFF_KA + 2 * KA_W]
    va = proj[:, P_OFF_VA:P_OFF_VA + 2 * VA_W]
    q_hat = (qa * lax.rsqrt(_dot((qa * qa).astype(BF16), ones_ref[...]) + EPS) * qg_ref[...]).astype(BF16)
    k_hat = ka * lax.rsqrt(_dot((ka * ka).astype(BF16), ones_ref[0:2 * KA_W, 0:2 * KA_W]) + EPS) * kg_ref[...]
    k_hat16 = k_hat.astype(BF16)
    va16 = va.astype(BF16)
    prev = carry_ref[step % 2]
    carry_ref[(step + 1) % 2] = jnp.concatenate([k_hat16[tb - blk:tb], va16[tb - blk:tb]], axis=1)

    row_g = lax.broadcasted_iota(jnp.int32, (rows4, 1), 0) // blk
    key_is_prev = lax.broadcasted_iota(jnp.int32, (rows4, 2 * blk), 1) < blk
    first_penalty = jnp.where(step == 0, -NEG_INF, 0.0)
    lane_low = lax.broadcasted_iota(jnp.int32, (blk, PAIR), 1) < A_HD
    ones_cols = jnp.ones((2 * blk, PAIR), BF16)

    n_blk = tb // blk
    att = [(j, kv) for j in range(n_blk) for kv in range(A_KV)]
    ret = [(j, hb) for j in range(n_blk) for hb in range(B_HEADS)]

    sinks, scores = {}, {}
    for j, kv in att:
        r0 = j * blk
        kcol = slice(kv * PAIR, (kv + 1) * PAIR)
        if j == 0:
            k2 = jnp.concatenate([prev[:, kcol], k_hat16[0:blk, kcol]], axis=0)
        else:
            k2 = k_hat16[r0 - blk:r0 + blk, kcol]
        q4 = jnp.concatenate(
            [jnp.where(lane_low == (g % 2 == 0),
                       q_hat[r0:r0 + blk, (kv * A_GROUP + g - g % 2) * A_HD:(kv * A_GROUP + g - g % 2 + 2) * A_HD],
                       jnp.zeros((), BF16))
             for g in range(A_GROUP)], axis=0)
        scores[j, kv] = _dot_nt(q4, k2)
        sink = jnp.zeros((rows4, 1), F32)
        for g in range(A_GROUP):
            sink = jnp.where(row_g == g, sink_ref[kv * A_GROUP + g], sink)
        sinks[j, kv] = sink
    qb, vb, kt, inner = {}, {}, {}, {}
    for j, hb in ret:
        r0 = j * blk
        qb[j, hb] = proj[r0:r0 + blk, P_OFF_QB + hb * B_KD:P_OFF_QB + (hb + 1) * B_KD].astype(BF16)
        vb[j, hb] = proj[r0:r0 + blk, P_OFF_VB + hb * B_VD:P_OFF_VB + (hb + 1) * B_VD].astype(BF16)
        kt[j, hb] = kt_all[hb * B_KD:(hb + 1) * B_KD, r0:r0 + blk]
        inner[j, hb] = _dot(qb[j, hb], kt[j, hb].astype(BF16))

    probs, maxes = {}, {}
    for j, kv in att:
        s = scores[j, kv] * (A_HD ** -0.5) - bias_ref[kv]
        if j == 0:
            s = s - jnp.where(key_is_prev, first_penalty, 0.0)
        mx = jnp.maximum(jnp.max(s, axis=-1, keepdims=True), sinks[j, kv])
        probs[j, kv] = jnp.exp(s - mx).astype(BF16)
        maxes[j, kv] = mx
    state = {}
    for hb in range(B_HEADS):
        d_c = math.exp(_ret_log_gamma(hb) * blk)
        state[0, hb] = ps_ref[hb]
        for j in range(n_blk):
            state[j + 1, hb] = state[j, hb] * d_c + _dot((kt[j, hb] * dec_ref[hb, 2]).astype(BF16), vb[j, hb])
        ps_ref[hb] = state[n_blk, hb]

    for j, kv in att:
        r0 = j * blk
        kcol = slice(kv * PAIR, (kv + 1) * PAIR)
        vcol = slice(2 * KA_W + kv * PAIR, 2 * KA_W + (kv + 1) * PAIR)
        if j == 0:
            v2 = jnp.concatenate([prev[:, vcol], va16[0:blk, kcol]], axis=0)
        else:
            v2 = va16[r0 - blk:r0 + blk, kcol]
        pv = _dot(probs[j, kv], jnp.concatenate([v2, ones_cols], axis=1))
        o4 = pv[:, 0:PAIR] / (pv[:, PAIR:2 * PAIR] + jnp.exp(sinks[j, kv] - maxes[j, kv]))
        for g in range(A_GROUP):
            hd = kv * A_GROUP + g
            half = slice((hd % 2) * A_HD, (hd % 2 + 1) * A_HD)
            mix_ref[r0:r0 + blk, hd * A_HD:(hd + 1) * A_HD] = o4[g * blk:(g + 1) * blk, half]
    o_ret = {}
    for j, hb in ret:
        o_ret[j, hb] = (_dot((inner[j, hb] * dec_ref[hb, 0]).astype(BF16), vb[j, hb])
                        + _dot(qb[j, hb], state[j, hb].astype(BF16)) * dec_ref[hb, 1])

    cen = {k: o_ret[k] - jnp.mean(o_ret[k], axis=-1, keepdims=True) for k in ret}
    var = {k: jnp.mean(cen[k] * cen[k], axis=-1, keepdims=True) for k in ret}
    for j, hb in ret:
        r0 = j * blk
        gb = proj[r0:r0 + blk, P_OFF_GB + hb * B_VD:P_OFF_GB + (hb + 1) * B_VD]
        mix_ref[r0:r0 + blk, QA_W + hb * B_VD:QA_W + (hb + 1) * B_VD] = (
            cen[j, hb] * lax.rsqrt(var[j, hb] + EPS) * rg[:, hb * B_VD:(hb + 1) * B_VD] * (gb * jax.nn.sigmoid(gb)))

    out = _dot(mix_ref[...].astype(BF16), wout_ref[...])
    o_ref[...] = x + _gate(mod, 0, d) * out.reshape(sb, ROWS, d)

    @pl.when(step == pl.num_programs(0) - 1)
    def _():
        pk_ref[...] = _dedup(k_hat[tb - blk:tb, :])
        pv_ref[...] = _dedup(va[tb - blk:tb, :])


def _dup_heads(t, width):
    lead = t.shape[:-1]
    t = t.reshape(lead + (-1, 1, width))
    return jnp.broadcast_to(t, lead + (t.shape[-3], 2, width)).reshape(lead + (-1,))


def _mix_even_prompt(x3, mod3, gain, w_in, q_gain, k_gain, sinks, ret_gain, w_out, sb, cast=()):
    n8, _, d = x3.shape
    tb = sb * ROWS
    cast_specs = [_cast_specs(w, n8 // sb) for w in cast]
    w_main = jnp.concatenate([w_in[:, :OFF_KA], _dup_heads(w_in[:, OFF_KA:OFF_VA], A_HD),
                              _dup_heads(w_in[:, OFF_VA:OFF_QB], A_HD), w_in[:, OFF_QB:OFF_KB],
                              w_in[:, OFF_VB:]], axis=1)
    wk_t = w_in[:, OFF_KB:OFF_VB].T
    out = pl.pallas_call(
        functools.partial(_mix_even_prompt_body, len(cast)),
        grid=(n8 // sb,),
        in_specs=[pl.BlockSpec((sb, ROWS, d), lambda i: (i, 0, 0)),
                  pl.BlockSpec((1, 1, 6 * d), lambda i: (0, 0, 0)),
                  _const_spec((1, d)), _const_spec((d, P_WIDTH)), _const_spec((KB_W, d)),
                  _const_spec((1, QA_W)), _const_spec((1, 2 * KA_W)),
                  pl.BlockSpec(memory_space=pltpu.SMEM),
                  _const_spec((1, VB_W)), _const_spec((QA_W + VB_W, d))] + [c[0] for c in cast_specs],
        out_specs=[pl.BlockSpec((sb, ROWS, d), lambda i: (i, 0, 0)),
                   pl.BlockSpec((WINDOW, KA_W), lambda i: (0, 0)),
                   pl.BlockSpec((WINDOW, VA_W), lambda i: (0, 0)),
                   pl.BlockSpec((B_HEADS, B_KD, B_VD), lambda i: (0, 0, 0))] + [c[1] for c in cast_specs],
        out_shape=[jax.ShapeDtypeStruct(x3.shape, F32),
                   jax.ShapeDtypeStruct((WINDOW, KA_W), F32),
                   jax.ShapeDtypeStruct((WINDOW, VA_W), F32),
                   jax.ShapeDtypeStruct((B_HEADS, B_KD, B_VD), F32)] + [c[2] for c in cast_specs],
        scratch_shapes=[pltpu.VMEM((tb, QA_W + VB_W), F32),
                        pltpu.VMEM((2, WINDOW, 2 * KA_W + 2 * VA_W), BF16),
                        pltpu.VMEM((A_KV, A_GROUP * WINDOW, 2 * WINDOW), F32),
                        pltpu.VMEM((B_HEADS, 3, WINDOW, WINDOW), F32),
                        pltpu.VMEM((QA_W, QA_W), BF16)],
        compiler_params=_params(),
        name="mix_even_prompt",
    )(x3, mod3, gain.reshape(1, d), w_main, wk_t, jnp.tile(q_gain, A_HEADS).reshape(1, QA_W),
      jnp.tile(k_gain, 2 * A_KV).reshape(1, 2 * KA_W),
      sinks, ret_gain.reshape(1, VB_W), w_out, *[w for w, _ in cast])
    return out[:4], tuple(out[4:])


def _mix_even_sample_body(x_ref, mod_ref, gn_ref, win_ref, qg_ref, kg_ref, sink_ref, rg_ref, wout_ref,
                          ck_ref, cv_ref, s0_ref,
                          o_ref, nk_ref, nv_ref, ns_ref, mix_ref):
    x = x_ref[...]
    sb, length, d = x.shape
    tb = sb * length
    w = ck_ref.shape[1]
    mod = _load_mod(mod_ref)
    h = _modulate(x, mod, gn_ref[...], 0).reshape(tb, d).astype(BF16)
    proj = _dot(h, win_ref[...])
    qg = qg_ref[...]
    kg = kg_ref[...]
    rg = rg_ref[...]

    rows = A_GROUP * length
    qpos_c = lax.broadcasted_iota(jnp.int32, (rows, w), 0) % length
    kpos_c = lax.broadcasted_iota(jnp.int32, (rows, w), 1)
    dist_c = w + qpos_c - kpos_c
    valid_c = (dist_c >= 0) & (dist_c < WINDOW)
    qpos_n = lax.broadcasted_iota(jnp.int32, (rows, length), 0) % length
    kpos_n = lax.broadcasted_iota(jnp.int32, (rows, length), 1)
    dist_n = qpos_n - kpos_n
    valid_n = (dist_n >= 0) & (dist_n < WINDOW)
    row_g = lax.broadcasted_iota(jnp.int32, (rows, 1), 0) // length

    for kv in range(A_KV):
        lanes = slice(kv * A_HD, (kv + 1) * A_HD)
        kn = _head_rms(proj[:, OFF_KA + kv * A_HD:OFF_KA + (kv + 1) * A_HD], kg).reshape(sb, length, A_HD)
        vn = proj[:, OFF_VA + kv * A_HD:OFF_VA + (kv + 1) * A_HD].reshape(sb, length, A_HD)
        nk_ref[:, 0:w - length, lanes] = ck_ref[:, length:w, lanes]
        nv_ref[:, 0:w - length, lanes] = cv_ref[:, length:w, lanes]
        nk_ref[:, w - length:w, lanes] = kn
        nv_ref[:, w - length:w, lanes] = vn
        kc = ck_ref[:, :, lanes].astype(BF16)
        vc = cv_ref[:, :, lanes].astype(BF16)
        q4 = jnp.concatenate(
            [_head_rms(proj[:, OFF_QA + (kv * A_GROUP + g) * A_HD:OFF_QA + (kv * A_GROUP + g + 1) * A_HD], qg)
             .reshape(sb, length, A_HD) for g in range(A_GROUP)], axis=1).astype(BF16)
        slope = jnp.zeros((rows, 1), F32)
        sink = jnp.zeros((rows, 1), F32)
        for g in range(A_GROUP):
            hd = kv * A_GROUP + g
            slope = jnp.where(row_g == g, _alibi_slope(hd), slope)
            sink = jnp.where(row_g == g, sink_ref[hd], sink)
        scale = A_HD ** -0.5
        s_c = _bmm_nt(q4, kc) * scale - slope * dist_c.astype(F32)
        s_c = jnp.where(valid_c, s_c, NEG_INF)
        s_n = _bmm_nt(q4, kn.astype(BF16)) * scale - slope * dist_n.astype(F32)
        s_n = jnp.where(valid_n, s_n, NEG_INF)
        mx = jnp.maximum(jnp.maximum(jnp.max(s_c, axis=-1, keepdims=True),
                                     jnp.max(s_n, axis=-1, keepdims=True)), sink)
        p_c = jnp.exp(s_c - mx)
        p_n = jnp.exp(s_n - mx)
        den = (jnp.sum(p_c, axis=-1, keepdims=True) + jnp.sum(p_n, axis=-1, keepdims=True)
               + jnp.exp(sink - mx))
        o4 = (_bmm(p_c.astype(BF16), vc) + _bmm(p_n.astype(BF16), vn.astype(BF16))) / den
        for g in range(A_GROUP):
            hd = kv * A_GROUP + g
            mix_ref[:, hd * A_HD:(hd + 1) * A_HD] = o4[:, g * length:(g + 1) * length, :].reshape(tb, A_HD)

    for hb in range(B_HEADS):
        d_in, d_q, d_k, d_c = _ret_decay(hb, length)
        qb = proj[:, OFF_QB + hb * B_KD:OFF_QB + (hb + 1) * B_KD].reshape(sb, length, B_KD).astype(BF16)
        kb = proj[:, OFF_KB + hb * B_KD:OFF_KB + (hb + 1) * B_KD].reshape(sb, length, B_KD) * (B_KD ** -0.5)
        vb = proj[:, OFF_VB + hb * B_VD:OFF_VB + (hb + 1) * B_VD].reshape(sb, length, B_VD).astype(BF16)
        gb = proj[:, OFF_GB + hb * B_VD:OFF_GB + (hb + 1) * B_VD]
        state = s0_ref[:, hb]
        inner = _bmm_nt(qb, kb.astype(BF16)) * d_in
        o = _bmm(inner.astype(BF16), vb) + _bmm(qb, state.astype(BF16)) * d_q
        ns_ref[:, hb] = state * d_c + _bmm_tn((kb * d_k).astype(BF16), vb)
        mix_ref[:, QA_W + hb * B_VD:QA_W + (hb + 1) * B_VD] = _group_norm_gate(
            o.reshape(tb, B_VD), rg[:, hb * B_VD:(hb + 1) * B_VD], gb)

    out = _dot(mix_ref[...].astype(BF16), wout_ref[...])
    o_ref[...] = x + _gate(mod, 0, d) * out.reshape(sb, length, d)


def _mix_even_sample(x3, mod3, gain, w_in, q_gain, k_gain, sinks, ret_gain, w_out, cache_k, cache_v, state, sb):
    n, length, d = x3.shape
    in_w = w_in.shape[1]
    w = cache_k.shape[1]
    seq_spec = lambda shape: pl.BlockSpec((sb,) + shape, lambda i, _n=len(shape): (i,) + (0,) * _n)
    return pl.pallas_call(
        _mix_even_sample_body,
        grid=(n // sb,),
        in_specs=[seq_spec((length, d)), seq_spec((6 * d,)),
                  _const_spec((1, d)), _const_spec((d, in_w)),
                  _const_spec((1, A_HD)), _const_spec((1, A_HD)),
                  pl.BlockSpec(memory_space=pltpu.SMEM),
                  _const_spec((1, VB_W)), _const_spec((QA_W + VB_W, d)),
                  seq_spec((w, KA_W)), seq_spec((w, VA_W)),
                  pl.BlockSpec((None, sb, B_HEADS, B_KD, B_VD), lambda i: (0, i, 0, 0, 0))],
        out_specs=[seq_spec((length, d)), seq_spec((w, KA_W)), seq_spec((w, VA_W)),
                   seq_spec((B_HEADS, B_KD, B_VD))],
        out_shape=[jax.ShapeDtypeStruct(x3.shape, F32),
                   jax.ShapeDtypeStruct(cache_k.shape, F32),
                   jax.ShapeDtypeStruct(cache_v.shape, F32),
                   jax.ShapeDtypeStruct(state.shape[1:], F32)],
        scratch_shapes=[pltpu.VMEM((sb * length, QA_W + VB_W), F32)],
        compiler_params=_params(),
        name="mix_even_sample",
    )(x3, mod3, gain.reshape(1, d), w_in, q_gain.reshape(1, A_HD), k_gain.reshape(1, A_HD),
      sinks, ret_gain.reshape(1, VB_W), w_out, cache_k, cache_v, state)


def _cmul(ar, ai, br, bi):
    return ar * br - ai * bi, ar * bi + ai * br


def _s5_lambda(a_re, a_im, log_dt):
    dt = jnp.exp(log_dt)
    mag = jnp.exp(a_re * dt)
    return mag * jnp.cos(a_im * dt), mag * jnp.sin(a_im * dt)


def _s5_prep_body(t_len, n_chunks, are_ref, aim_ref, ldt_ref, btr_ref, bti_ref, ard_ref, aid_ref, ldd_ref,
                  lre_ref, lim_ref, bbr_ref, bbi_ref, pwr_ref, pwi_ref):
    a_re = are_ref[...]
    a_im = aim_ref[...]
    lam_re, lam_im = _s5_lambda(a_re, a_im, ldt_ref[...])
    den = a_re * a_re + a_im * a_im
    n_re = lam_re - 1.0
    n_im = lam_im
    f_re = (n_re * a_re + n_im * a_im) / den
    f_im = (n_im * a_re - n_re * a_im) / den
    br = btr_ref[...]
    bi = bti_ref[...]
    lre_ref[...] = lam_re
    lim_ref[...] = lam_im
    bbr_ref[...] = f_re * br - f_im * bi
    bbi_ref[...] = f_re * bi + f_im * br
    lam_re, lam_im = _s5_lambda(ard_ref[...], aid_ref[...], ldd_ref[...])
    cr, ci = lam_re, lam_im
    for t in range(t_len):
        pwr_ref[t] = cr
        pwi_ref[t] = ci
        if t + 1 < t_len:
            cr, ci = _cmul(cr, ci, lam_re, lam_im)
    tr, ti = cr, ci
    cr, ci = jnp.ones_like(lam_re), jnp.zeros_like(lam_im)
    for m in range(n_chunks + 1):
        pwr_ref[t_len + m] = cr
        pwi_ref[t_len + m] = ci
        cr, ci = _cmul(cr, ci, tr, ti)


def _s5_prep(a_re, a_im, log_dt, b_re, b_im, t_len, n_chunks):
    g, p = a_re.shape
    k = b_re.shape[-1]
    bt_re = jnp.swapaxes(b_re, 1, 2)
    bt_im = jnp.swapaxes(b_im, 1, 2)
    n_pow = t_len + n_chunks + 1
    dense = (g // S5_OCT, S5_OCT * p)
    out = pl.pallas_call(
        functools.partial(_s5_prep_body, t_len, n_chunks),
        out_shape=[jax.ShapeDtypeStruct((g, 1, p), F32), jax.ShapeDtypeStruct((g, 1, p), F32),
                   jax.ShapeDtypeStruct((g, k, p), F32), jax.ShapeDtypeStruct((g, k, p), F32),
                   jax.ShapeDtypeStruct((n_pow,) + dense, F32), jax.ShapeDtypeStruct((n_pow,) + dense, F32)],
        name="s5_prep",
    )(a_re.reshape(g, 1, p), a_im.reshape(g, 1, p), log_dt.reshape(g, 1, 1), bt_re, bt_im,
      a_re.reshape(dense), a_im.reshape(dense), jnp.broadcast_to(log_dt[:, None], (g, p)).reshape(dense))
    return out


def _block_diag(t):
    g, a, b = t.shape
    t = t.reshape(g // S5_OCT, S5_OCT, a, b)
    eye = jnp.eye(S5_OCT, dtype=t.dtype)
    out = t[:, :, :, None, :] * eye[None, :, None, :, None]
    return out.reshape(g // S5_OCT, S5_OCT * a, S5_OCT * b)


def _gelu_glu_out(x, mod, y, u, dskip, glua_ref, glub_ref):
    sb, rows, d = x.shape
    y = y + dskip * u
    yg = jax.nn.gelu(y, approximate=True).astype(BF16)
    out = _dot(yg, glua_ref[...]) * jax.nn.sigmoid(_dot(yg, glub_ref[...]))
    return x + _gate(mod, 0, d) * out.reshape(sb, rows, d)


def _mix_odd_prompt_body(x_ref, xn_ref, mod_ref, gn_ref, bblk_ref, lam_ref, pwb_ref, ltp_ref, cre_ref, cim_ref,
                         dskip_ref, glua_ref, glub_ref, o_ref, hre_ref, him_ref,
                         un_ref, l0_ref, l1_ref, up0_ref, up1_ref, pt_ref):
    step = pl.program_id(0)
    sb, _, d = x_ref.shape
    tm = sb * ROWS
    t_len = tm // ROWS
    n_oct = bblk_ref.shape[0]
    half = bblk_ref.shape[2] // 2
    mod = mod_ref[...]

    def permuted_input(src_ref, up_dst):
        u = _modulate(src_ref[...], mod, gn_ref[...], 0).reshape(tm, d)
        for k in range(d // 128):
            un_ref[k] = u[:, k * 128:(k + 1) * 128]
        up = jnp.concatenate(
            [jnp.concatenate([un_ref[k, pl.ds(t, ROWS, stride=t_len), :] for k in range(d // 128)], axis=1)
             for t in range(t_len)], axis=0)
        up_dst[...] = up
        return up.astype(BF16)

    @pl.when(step == 0)
    def _():
        hre_ref[...] = jnp.zeros_like(hre_ref)
        him_ref[...] = jnp.zeros_like(him_ref)
        nat = lax.broadcasted_iota(jnp.int32, (tm, tm), 0)
        prm = lax.broadcasted_iota(jnp.int32, (tm, tm), 1)
        pt_ref[...] = jnp.where(prm == (nat % t_len) * ROWS + nat // t_len, 1.0, 0.0).astype(BF16)
        up16 = permuted_input(x_ref, up0_ref)
        for s in range(n_oct):
            l0_ref[s] = _dot(up16[:, s * 128:(s + 1) * 128], bblk_ref[s])

    def run(l_ref, l_next, up_ref, up_next):
        for s0 in range(0, n_oct, 2):
            pair = (s0, s0 + 1)
            lam_re = [jnp.broadcast_to(lam_ref[0, s:s + 1, :], (ROWS, half)) for s in pair]
            lam_im = [jnp.broadcast_to(lam_ref[1, s:s + 1, :], (ROWS, half)) for s in pair]

            def local_step(t, carry):
                r = pl.multiple_of(t * ROWS, ROWS)
                out = []
                for i, s in enumerate(pair):
                    h_re, h_im = carry[2 * i], carry[2 * i + 1]
                    n_re = lam_re[i] * h_re - lam_im[i] * h_im + l_ref[s, pl.ds(r, ROWS), 0:half]
                    n_im = lam_re[i] * h_im + lam_im[i] * h_re + l_ref[s, pl.ds(r, ROWS), half:2 * half]
                    l_ref[s, pl.ds(r, ROWS), 0:half] = n_re
                    l_ref[s, pl.ds(r, ROWS), half:2 * half] = n_im
                    out += [n_re, n_im]
                return tuple(out)

            zero = jnp.zeros((ROWS, half), F32)
            lax.fori_loop(0, t_len, local_step, (zero, zero, zero, zero), unroll=2)

        next16 = permuted_input(xn_ref, up_next)
        chunk = lax.broadcasted_iota(jnp.int32, (ROWS, half), 0)
        y_parts = []
        for s in range(n_oct):
            l_next[s] = _dot(next16[:, s * 128:(s + 1) * 128], bblk_ref[s])
            p_re = l_ref[s, tm - ROWS:tm, 0:half]
            p_im = l_ref[s, tm - ROWS:tm, half:2 * half]
            for i, sh in enumerate((1, 2, 4)):
                m_re, m_im = _cmul(ltp_ref[s, i:i + 1, 0:half], ltp_ref[s, i:i + 1, half:2 * half],
                                   jnp.where(chunk >= sh, pltpu.roll(p_re, sh, 0), 0.0),
                                   jnp.where(chunk >= sh, pltpu.roll(p_im, sh, 0), 0.0))
                p_re, p_im = p_re + m_re, p_im + m_im
            hin_re = jnp.broadcast_to(hre_ref[s:s + 1, :], (ROWS, half))
            hin_im = jnp.broadcast_to(him_ref[s:s + 1, :], (ROWS, half))
            m_re, m_im = _cmul(ltp_ref[s, ROWS:2 * ROWS, 0:half], ltp_ref[s, ROWS:2 * ROWS, half:2 * half],
                               hin_re, hin_im)
            st_re = m_re + jnp.where(chunk >= 1, pltpu.roll(p_re, 1, 0), 0.0)
            st_im = m_im + jnp.where(chunk >= 1, pltpu.roll(p_im, 1, 0), 0.0)
            m_re, m_im = _cmul(ltp_ref[s, 3:4, 0:half], ltp_ref[s, 3:4, half:2 * half],
                               hre_ref[s:s + 1, :], him_ref[s:s + 1, :])
            hre_ref[s:s + 1, :] = m_re + p_re[ROWS - 1:ROWS, :]
            him_ref[s:s + 1, :] = m_im + p_im[ROWS - 1:ROWS, :]
            loc = l_ref[s].reshape(t_len, ROWS, 2 * half)
            pw = pwb_ref[s]
            f_re, f_im = _cmul(pw[:, :, 0:half], pw[:, :, half:2 * half], st_re[None], st_im[None])
            hs_re = (loc[:, :, 0:half] + f_re).reshape(tm, half).astype(BF16)
            hs_im = (loc[:, :, half:2 * half] + f_im).reshape(tm, half).astype(BF16)
            y_parts.append(_dot(hs_re, cre_ref[s]) - _dot(hs_im, cim_ref[s]))

        y = jnp.concatenate(y_parts, axis=1) + dskip_ref[...] * up_ref[...]
        yg = jax.nn.gelu(y, approximate=True).astype(BF16)
        yn = _dot(pt_ref[...], yg).astype(BF16)
        out = _dot(yn, glua_ref[...]) * jax.nn.sigmoid(_dot(yn, glub_ref[...]))
        o_ref[...] = x_ref[...] + _gate(mod, 0, d) * out.reshape(sb, ROWS, d)

    @pl.when(step % 2 == 0)
    def _():
        run(l0_ref, l1_ref, up0_ref, up1_ref)

    @pl.when(step % 2 == 1)
    def _():
        run(l1_ref, l0_ref, up1_ref, up0_ref)


def _mix_odd_prompt(x3, mod3, gain, bblk, lam_d, pwb, ltp, cre, cim, dskip, glu_a, glu_b, sb):
    n8, _, d = x3.shape
    n_oct, kin, wid = bblk.shape
    half = wid // 2
    tm = sb * ROWS
    n_tiles = n8 // sb
    proj_buf = pltpu.VMEM((n_oct, tm, wid), F32)
    perm_buf = pltpu.VMEM((tm, d), F32)
    return pl.pallas_call(
        _mix_odd_prompt_body,
        grid=(n_tiles,),
        in_specs=[pl.BlockSpec((sb, ROWS, d), lambda i: (i, 0, 0)),
                  pl.BlockSpec((sb, ROWS, d), lambda i: (jnp.minimum(i + 1, n_tiles - 1), 0, 0)),
                  pl.BlockSpec((1, 1, 6 * d), lambda i: (0, 0, 0)),
                  _const_spec((1, d)), _const_spec(bblk.shape), _const_spec(lam_d.shape),
                  _const_spec(pwb.shape), _const_spec(ltp.shape),
                  _const_spec(cre.shape), _const_spec(cim.shape), _const_spec((1, d)),
                  _const_spec((d, d)), _const_spec((d, d))],
        out_specs=[pl.BlockSpec((sb, ROWS, d), lambda i: (i, 0, 0)),
                   pl.BlockSpec((n_oct, half), lambda i: (0, 0)),
                   pl.BlockSpec((n_oct, half), lambda i: (0, 0))],
        out_shape=[jax.ShapeDtypeStruct(x3.shape, F32),
                   jax.ShapeDtypeStruct((n_oct, half), F32),
                   jax.ShapeDtypeStruct((n_oct, half), F32)],
        scratch_shapes=[pltpu.VMEM((d // 128, tm, 128), F32), proj_buf, proj_buf, perm_buf, perm_buf,
                        pltpu.VMEM((tm, tm), BF16)],
        compiler_params=_params(),
        name="mix_odd_prompt",
    )(x3, x3, mod3, gain.reshape(1, d), bblk, lam_d, pwb, ltp, cre, cim, dskip.reshape(1, d), glu_a, glu_b)


def _mix_odd_sample_body(x_ref, mod_ref, gn_ref, bblk_ref, lam_ref, cre_ref, cim_ref, dskip_ref,
                         glua_ref, glub_ref, sre_ref, sim_ref, o_ref, nre_ref, nim_ref, d_ref, y_ref):
    x = x_ref[...]
    sb, length, d = x.shape
    tm = sb * length
    n_oct = bblk_ref.shape[0]
    half = bblk_ref.shape[2] // 2
    mod = _load_mod(mod_ref)
    u = _modulate(x, mod, gn_ref[...], 0).reshape(tm, d)
    u16 = u.astype(BF16)
    n_ch = bblk_ref.shape[2] // 128
    hc = n_ch // 2
    for s in range(n_oct):
        bu = _dot(u16[:, s * 128:(s + 1) * 128], bblk_ref[s])
        for c in range(n_ch):
            d_ref[c] = bu[:, c * 128:(c + 1) * 128]
        lam_re = lam_ref[0, s:s + 1, :]
        lam_im = lam_ref[1, s:s + 1, :]
        h_re = sre_ref[:, s * half:(s + 1) * half]
        h_im = sim_ref[:, s * half:(s + 1) * half]
        for t in range(length):
            b_re = jnp.concatenate([d_ref[c, pl.ds(t, sb, stride=length), :] for c in range(hc)], axis=1)
            b_im = jnp.concatenate([d_ref[hc + c, pl.ds(t, sb, stride=length), :] for c in range(hc)], axis=1)
            n_re = lam_re * h_re - lam_im * h_im + b_re
            n_im = lam_re * h_im + lam_im * h_re + b_im
            for c in range(hc):
                d_ref[c, pl.ds(t, sb, stride=length), :] = n_re[:, c * 128:(c + 1) * 128]
                d_ref[hc + c, pl.ds(t, sb, stride=length), :] = n_im[:, c * 128:(c + 1) * 128]
            h_re, h_im = n_re, n_im
        nre_ref[:, s * half:(s + 1) * half] = h_re
        nim_ref[:, s * half:(s + 1) * half] = h_im
        hs_re = jnp.concatenate([d_ref[c] for c in range(hc)], axis=1)
        hs_im = jnp.concatenate([d_ref[hc + c] for c in range(hc)], axis=1)
        y_ref[:, s * 128:(s + 1) * 128] = (_dot(hs_re.astype(BF16), cre_ref[s])
                                           - _dot(hs_im.astype(BF16), cim_ref[s]))
    o_ref[...] = _gelu_glu_out(x, mod, y_ref[...], u, dskip_ref[...], glua_ref, glub_ref)


def _mix_odd_sample(x3, mod3, gain, bblk, lam_d, cre, cim, dskip, glu_a, glu_b, s_re, s_im, sb):
    n, length, d = x3.shape
    n_oct, kin, wid = bblk.shape
    tm = sb * length
    nstate = s_re.shape[1]
    return pl.pallas_call(
        _mix_odd_sample_body,
        grid=(n // sb,),
        in_specs=[pl.BlockSpec((sb, length, d), lambda i: (i, 0, 0)),
                  pl.BlockSpec((sb, 6 * d), lambda i: (i, 0)),
                  _const_spec((1, d)), _const_spec(bblk.shape), _const_spec(lam_d.shape),
                  _const_spec(cre.shape), _const_spec(cim.shape), _const_spec((1, d)),
                  _const_spec((d, d)), _const_spec((d, d)),
                  pl.BlockSpec((sb, nstate), lambda i: (i, 0)),
                  pl.BlockSpec((sb, nstate), lambda i: (i, 0))],
        out_specs=[pl.BlockSpec((sb, length, d), lambda i: (i, 0, 0)),
                   pl.BlockSpec((sb, nstate), lambda i: (i, 0)),
                   pl.BlockSpec((sb, nstate), lambda i: (i, 0))],
        out_shape=[jax.ShapeDtypeStruct(x3.shape, F32),
                   jax.ShapeDtypeStruct(s_re.shape, F32),
                   jax.ShapeDtypeStruct(s_im.shape, F32)],
        scratch_shapes=[pltpu.VMEM((wid // 128, tm, 128), F32), pltpu.VMEM((tm, d), F32)],
        compiler_params=_params(),
        name="mix_odd_sample",
    )(x3, mod3, gain.reshape(1, d), bblk, lam_d, cre, cim, dskip.reshape(1, d), glu_a, glu_b, s_re, s_im)


def _pick(n, want):
    while n % want:
        want //= 2
    return max(want, 1)


def kernel(x_prompt, x_sample, cache_win_k, cache_win_v, state_ret, state_s5_re, state_s5_im, c_prompt, c_sample, ada_w, ada_b, norm_mix, norm_ffn, ffn_wg, ffn_wu, ffn_wd, even_w_in, even_q_gain, even_k_gain, even_sinks, even_ret_gain, even_w_out, odd_A_re, odd_A_im, odd_log_dt, odd_B_re, odd_B_im, odd_C_re, odd_C_im, odd_D, odd_glu_a, odd_glu_b):
    bp, lp, d = x_prompt.shape
    ns, ls, _ = x_sample.shape
    assert bp == 1 and ls == ROWS and lp % WINDOW == 0
    w = cache_win_k.shape[2]
    groups, p_state = odd_A_re.shape[1:]

    n_c = bp + ns
    n_pad = -n_c % ROWS
    c_all = jnp.concatenate([c_prompt, c_sample, jnp.zeros((n_pad, d), F32)], axis=0)
    mod = _adaln(c_all, ada_w, ada_b)
    mod_p = [mod[l, 0:1].reshape(1, 1, 6 * d) for l in range(2)]
    mod_s = [mod[l, bp:bp + ns] for l in range(2)]

    bf = lambda t: t.astype(BF16)
    w_in, w_out = bf(even_w_in[0]), bf(even_w_out[0])

    xp = x_prompt.reshape(lp // ROWS, ROWS, d)
    xs = x_sample

    sb_p = _pick(lp // ROWS, 32)
    sb_ffn = _pick(lp // ROWS, 64)
    sb_s = _pick(ns, 32)
    sb_s_even = _pick(ns, 16)

    (xp, p_k, p_v, p_ret), ffn0 = _mix_even_prompt(xp, mod_p[0], norm_mix[0], w_in, even_q_gain[0], even_k_gain[0],
                                                   even_sinks[0], even_ret_gain[0], w_out, sb_ffn,
                                                   cast=((ffn_wg, 0), (ffn_wu, 0), (ffn_wd, 0)))
    xs, s_k, s_v, s_ret = _mix_even_sample(xs, mod_s[0], norm_mix[0], w_in, even_q_gain[0], even_k_gain[0],
                                           even_sinks[0], even_ret_gain[0], w_out,
                                           cache_win_k[0].reshape(ns, w, KA_W), cache_win_v[0].reshape(ns, w, VA_W),
                                           state_ret, sb_s_even)
    xp, (wg1, wu1, wd1, glu_a, glu_b) = _ffn(xp, mod_p[0], norm_ffn[0], *ffn0, sb_ffn,
                                             cast=((ffn_wg, 1), (ffn_wu, 1), (ffn_wd, 1), (odd_glu_a, 0), (odd_glu_b, 0)))
    xs, _ = _ffn(xs, mod_s[0], norm_ffn[0], *ffn0, sb_s)

    t_len = sb_p
    lam_re, lam_im, bbt_re, bbt_im, pw_re, pw_im = _s5_prep(odd_A_re[0], odd_A_im[0], odd_log_dt[0],
                                                            odd_B_re[0], odd_B_im[0], t_len, ROWS)
    n_oct = groups // S5_OCT
    half = S5_OCT * p_state
    bblk = bf(jnp.concatenate([_block_diag(bbt_re), _block_diag(bbt_im)], axis=-1))
    cre = bf(_block_diag(jnp.swapaxes(odd_C_re[0], 1, 2)))
    cim = bf(_block_diag(jnp.swapaxes(odd_C_im[0], 1, 2)))
    lam_d = jnp.stack([lam_re.reshape(n_oct, half), lam_im.reshape(n_oct, half)])

    pw_d = jnp.concatenate([pw_re, pw_im], axis=-1)
    pwb = jnp.broadcast_to(jnp.swapaxes(pw_d[:t_len], 0, 1)[:, :, None, :], (n_oct, t_len, ROWS, 2 * half))
    lt = jnp.swapaxes(pw_d[t_len:], 0, 1)
    ltp = jnp.concatenate([lt[:, 1:2], lt[:, 2:3], lt[:, 4:5], lt[:, 8:9], jnp.zeros_like(lt[:, 0:4]), lt[:, 0:8]],
                          axis=1)
    xp, p_re, p_im = _mix_odd_prompt(xp, mod_p[1], norm_mix[1], bblk, lam_d, pwb, ltp, cre, cim, odd_D[0],
                                     glu_a, glu_b, sb_p)
    xs, s_re, s_im = _mix_odd_sample(xs, mod_s[1], norm_mix[1], bblk, lam_d, cre, cim, odd_D[0], glu_a, glu_b,
                                     state_s5_re[0].reshape(ns, groups * p_state),
                                     state_s5_im[0].reshape(ns, groups * p_state), sb_s)
    xp, _ = _ffn(xp, mod_p[1], norm_ffn[1], wg1, wu1, wd1, sb_ffn)
    xs, _ = _ffn(xs, mod_s[1], norm_ffn[1], wg1, wu1, wd1, sb_s)

    y_prompt = xp.reshape(bp, lp, d)
    y_sample = xs
    return (y_prompt, y_sample,
            p_k.reshape(1, bp, WINDOW, A_KV, A_HD), p_v.reshape(1, bp, WINDOW, A_KV, A_HD),
            p_ret.reshape(1, bp, B_HEADS, B_KD, B_VD),
            p_re.reshape(1, bp, groups, p_state), p_im.reshape(1, bp, groups, p_state),
            s_k.reshape(1, ns, w, A_KV, A_HD), s_v.reshape(1, ns, w, A_KV, A_HD),
            s_ret.reshape(1, ns, B_HEADS, B_KD, B_VD),
            s_re.reshape(1, ns, groups, p_state), s_im.reshape(1, ns, groups, p_state))
```

```python
import functools
import math

import jax
import jax.numpy as jnp
from jax import lax
from jax.experimental import pallas as pl
from jax.experimental.pallas import tpu as pltpu

F32 = jnp.float32
BF16 = jnp.bfloat16

EPS = 1e-6
NEG_INF = -1e30
ROWS = 8

A_HEADS, A_KV, A_GROUP, A_HD = 8, 2, 4, 64
WINDOW = 128
B_HEADS, B_KD, B_VD = 4, 128, 128
S5_GROUP, S5_STATE = 16, 64
S5_OCT = 8

QA_W, KA_W, VA_W = A_HEADS * A_HD, A_KV * A_HD, A_KV * A_HD
QB_W, KB_W, VB_W, GB_W = B_HEADS * B_KD, B_HEADS * B_KD, B_HEADS * B_VD, B_HEADS * B_VD
OFF_QA = 0
OFF_KA = OFF_QA + QA_W
OFF_VA = OFF_KA + KA_W
OFF_QB = OFF_VA + VA_W
OFF_KB = OFF_QB + QB_W
OFF_VB = OFF_KB + KB_W
OFF_GB = OFF_VB + VB_W

VMEM_LIMIT = 56 * 1024 * 1024


def _ret_log_gamma(h):
    return math.log1p(-(2.0 ** (-5.0 - h)))


def _alibi_slope(h):
    return 2.0 ** (-8.0 * (h + 1) / A_HEADS)


def _const_spec(shape):
    nd = len(shape)
    return pl.BlockSpec(shape, lambda i, _n=nd: (0,) * _n, pipeline_mode=pl.Buffered(1))


def _params():
    return pltpu.CompilerParams(dimension_semantics=("arbitrary",), vmem_limit_bytes=VMEM_LIMIT)


def _dot(a, b):
    return jnp.dot(a, b, preferred_element_type=F32)


def _dot_nt(a, b):
    return lax.dot_general(a, b, (((1,), (1,)), ((), ())), preferred_element_type=F32)


def _dot_tn(a, b):
    return lax.dot_general(a, b, (((0,), (0,)), ((), ())), preferred_element_type=F32)


def _bmm(a, b):
    return lax.dot_general(a, b, (((2,), (1,)), ((0,), (0,))), preferred_element_type=F32)


def _bmm_nt(a, b):
    return lax.dot_general(a, b, (((2,), (2,)), ((0,), (0,))), preferred_element_type=F32)


def _bmm_tn(a, b):
    return lax.dot_general(a, b, (((1,), (1,)), ((0,), (0,))), preferred_element_type=F32)


def _rms(x, g):
    return x * lax.rsqrt(jnp.mean(x * x, axis=-1, keepdims=True) + EPS) * g


def _modulate(x3, mod, gain, which):
    d = x3.shape[-1]
    sh = mod[:, :, (3 * which) * d:(3 * which + 1) * d]
    sc = mod[:, :, (3 * which + 1) * d:(3 * which + 2) * d]
    return _rms(x3, gain) * (1.0 + sc) + sh


def _load_mod(mod_ref):
    m = mod_ref[...]
    return m if m.ndim == 3 else m[:, None, :]


def _gate(mod, which, d):
    return mod[:, :, (3 * which + 2) * d:(3 * which + 3) * d]


def _adaln_body(c_ref, w_ref, b_ref, o_ref):
    c = c_ref[...]
    a = (c * jax.nn.sigmoid(c)).astype(BF16)
    o_ref[0] = _dot(a, w_ref[0].astype(BF16)) + b_ref[0]


def _adaln(c_all, ada_w, ada_b):
    depth, d, n = ada_w.shape
    r = c_all.shape[0]
    tn = 1536
    return pl.pallas_call(
        _adaln_body,
        grid=(depth, n // tn),
        in_specs=[pl.BlockSpec((r, d), lambda l, j: (0, 0)),
                  pl.BlockSpec((1, d, tn), lambda l, j: (l, 0, j)),
                  pl.BlockSpec((1, 1, tn), lambda l, j: (l, 0, j))],
        out_specs=pl.BlockSpec((1, r, tn), lambda l, j: (l, 0, j)),
        out_shape=jax.ShapeDtypeStruct((depth, r, n), F32),
        compiler_params=pltpu.CompilerParams(dimension_semantics=("arbitrary", "arbitrary"),
                                             vmem_limit_bytes=VMEM_LIMIT),
        name="adaln",
    )(c_all, ada_w, ada_b.reshape(depth, 1, n))


BF16_ROWS = 16


def _cast_specs(job, n_steps):
    w, layer = job
    _, rows, cols = w.shape
    hold = 1
    while rows % (n_steps // hold) or (rows // (n_steps // hold)) % BF16_ROWS:
        hold *= 2
        assert hold <= n_steps and n_steps % hold == 0
    blk = rows // (n_steps // hold)
    src = pl.BlockSpec((None, blk, cols),
                       lambda i, _h=hold, _l=layer: (_l, jnp.minimum(i, n_steps - 1) // _h, 0))
    dst = pl.BlockSpec((blk, cols), lambda i, _h=hold: (jnp.minimum(i, n_steps - 1) // _h, 0))
    return src, dst, jax.ShapeDtypeStruct((rows, cols), BF16)


def _cast_blocks(in_refs, out_refs):
    for src, dst in zip(in_refs, out_refs):
        dst[...] = src[...].astype(BF16)


def _ffn_body(n_cast, n_prompt, xp_ref, mp_ref, xs_ref, ms_ref, gn_ref, wg_ref, wu_ref, wd_ref, *refs):
    cast_in, (op_ref, os_ref), cast_out = refs[:n_cast], refs[n_cast:n_cast + 2], refs[n_cast + 2:]
    step = pl.program_id(0)

    def tile(x_ref, mod_ref, o_ref):
        x = x_ref[...]
        sb, _, d = x.shape
        mod = _load_mod(mod_ref)
        h = _modulate(x, mod, gn_ref[...], 1).reshape(sb * ROWS, d).astype(BF16)
        a = _dot(h, wg_ref[...])
        b = _dot(h, wu_ref[...])
        act = (a * jax.nn.sigmoid(a) * b).astype(BF16)
        y = _dot(act, wd_ref[...])
        o_ref[...] = x + _gate(mod, 1, d) * y.reshape(sb, ROWS, d)

    @pl.when(step < n_prompt)
    def _():
        tile(xp_ref, mp_ref, op_ref)

    @pl.when(step >= n_prompt)
    def _():
        tile(xs_ref, ms_ref, os_ref)

    _cast_blocks(cast_in, cast_out)


def _ffn(xp3, mod_p, xs3, mod_s, gain, wg, wu, wd, sb, cast=()):
    n8, _, d = xp3.shape
    f = wg.shape[1]
    n_prompt = n8 // sb
    n_sample = xs3.shape[0] // sb
    assert n8 % sb == 0 and xs3.shape[0] % sb == 0
    cast_specs = [_cast_specs(w, n_prompt) for w in cast]
    p_idx = lambda i: (jnp.minimum(i, n_prompt - 1), 0, 0)
    s_idx = lambda i: (jnp.maximum(i - n_prompt, 0), 0, 0)
    xp_spec = pl.BlockSpec((sb, ROWS, d), p_idx)
    xs_spec = pl.BlockSpec((sb, ROWS, d), s_idx)
    out = pl.pallas_call(
        functools.partial(_ffn_body, len(cast), n_prompt),
        grid=(n_prompt + n_sample,),
        in_specs=[xp_spec, pl.BlockSpec((1, 1, 6 * d), lambda i: (0, 0, 0)),
                  xs_spec, pl.BlockSpec((sb, 6 * d), lambda i: s_idx(i)[:2]),
                  _const_spec((1, d)), _const_spec((d, f)), _const_spec((d, f)),
                  _const_spec((f, d))] + [c[0] for c in cast_specs],
        out_specs=[xp_spec, xs_spec] + [c[1] for c in cast_specs],
        out_shape=[jax.ShapeDtypeStruct(xp3.shape, F32), jax.ShapeDtypeStruct(xs3.shape, F32)]
        + [c[2] for c in cast_specs],
        compiler_params=_params(),
        name="ffn",
    )(xp3, mod_p, xs3, mod_s, gain.reshape(1, d), wg, wu, wd, *[w for w, _ in cast])
    return out[0], out[1], tuple(out[2:])


def _head_rms(t, g):
    return t * lax.rsqrt(jnp.mean(t * t, axis=-1, keepdims=True) + EPS) * g


def _group_norm_gate(o, gain, gate):
    mu = jnp.mean(o, axis=-1, keepdims=True)
    var = jnp.mean(jnp.square(o - mu), axis=-1, keepdims=True)
    return (o - mu) * lax.rsqrt(var + EPS) * gain * (gate * jax.nn.sigmoid(gate))


def _ret_decay(hb, c):
    lg = _ret_log_gamma(hb)
    ii = lax.broadcasted_iota(jnp.int32, (c, c), 0)
    jj = lax.broadcasted_iota(jnp.int32, (c, c), 1)
    diff = (ii - jj).astype(F32)
    d_in = jnp.where(diff >= 0, jnp.exp(lg * jnp.maximum(diff, 0.0)), 0.0)
    row = lax.broadcasted_iota(jnp.int32, (c, B_KD), 0).astype(F32)
    d_q = jnp.exp(lg * (row + 1.0))
    d_k = jnp.exp(lg * (c - 1.0 - row))
    d_c = math.exp(lg * c)
    return d_in, d_q, d_k, d_c


P_OFF_KA = QA_W
P_OFF_VA = P_OFF_KA + 2 * KA_W
P_OFF_QB = P_OFF_VA + 2 * VA_W
P_OFF_VB = P_OFF_QB + QB_W
P_OFF_GB = P_OFF_VB + VB_W
P_WIDTH = P_OFF_GB + GB_W
PAIR = 2 * A_HD


def _dedup(t):
    low = lax.broadcasted_iota(jnp.int32, (t.shape[0], PAIR), 1) < A_HD
    return jnp.where(low, t[:, 0:PAIR], t[:, PAIR:2 * PAIR])


def _mix_even_prompt_body(n_cast, x_ref, mod_ref, gn_ref, win_ref, wkt_ref, qg_ref, kg_ref, sink_ref, rg_ref, wout_ref,
                          *refs):
    cast_in, refs = refs[:n_cast], refs[n_cast:]
    o_ref, pk_ref, pv_ref, ps_ref = refs[:4]
    cast_out = refs[4:4 + n_cast]
    mix_ref, carry_ref, bias_ref, dec_ref, ones_ref = refs[4 + n_cast:]
    _cast_blocks(cast_in, cast_out)
    step = pl.program_id(0)
    blk = WINDOW
    rows4 = A_GROUP * blk

    @pl.when(step == 0)
    def _():
        carry_ref[...] = jnp.zeros_like(carry_ref)
        ps_ref[...] = jnp.zeros_like(ps_ref)
        er = lax.broadcasted_iota(jnp.int32, ones_ref.shape, 0) // A_HD
        ec = lax.broadcasted_iota(jnp.int32, ones_ref.shape, 1) // A_HD
        ones_ref[...] = jnp.where(er == ec, 1.0 / A_HD, 0.0).astype(BF16)
        row = lax.broadcasted_iota(jnp.int32, (rows4, 2 * blk), 0)
        dist = row % blk + blk - lax.broadcasted_iota(jnp.int32, (rows4, 2 * blk), 1)
        in_window = (dist >= 0) & (dist < WINDOW)
        for kv in range(A_KV):
            slope = jnp.zeros((rows4, 2 * blk), F32)
            for g in range(A_GROUP):
                slope = jnp.where(row // blk == g, _alibi_slope(kv * A_GROUP + g), slope)
            bias_ref[kv] = jnp.where(in_window, slope * dist.astype(F32), -NEG_INF)
        for hb in range(B_HEADS):
            lg = _ret_log_gamma(hb)
            ii = lax.broadcasted_iota(jnp.int32, (blk, blk), 0).astype(F32)
            jj = lax.broadcasted_iota(jnp.int32, (blk, blk), 1).astype(F32)
            diff = ii - jj
            dec_ref[hb, 0] = jnp.where(diff >= 0, jnp.exp(lg * jnp.maximum(diff, 0.0)), 0.0)
            dec_ref[hb, 1] = jnp.exp(lg * (ii + 1.0))
            dec_ref[hb, 2] = jnp.exp(lg * (blk - 1.0 - jj))

    x = x_ref[...]
    sb, _, d = x.shape
    tb = sb * ROWS
    mod = mod_ref[...]
    h = _modulate(x, mod, gn_ref[...], 0).reshape(tb, d).astype(BF16)
    proj = _dot(h, win_ref[...])
    kt_all = _dot_nt(wkt_ref[...], h) * (B_KD ** -0.5)
    rg = rg_ref[...]

    qa = proj[:, 0:QA_W]
    ka = proj[:, P_OFF_KA:P_OFF_KA + 2 * KA_W]
    va = proj[:, P_OFF_VA:P_OFF_VA + 2 * VA_W]
    q_hat = (qa * lax.rsqrt(_dot((qa * qa).astype(BF16), ones_ref[...]) + EPS) * qg_ref[...]).astype(BF16)
    k_hat = ka * lax.rsqrt(_dot((ka * ka).astype(BF16), ones_ref[0:2 * KA_W, 0:2 * KA_W]) + EPS) * kg_ref[...]
    k_hat16 = k_hat.astype(BF16)
    va16 = va.astype(BF16)
    prev = carry_ref[step % 2]
    carry_ref[(step + 1) % 2] = jnp.concatenate([k_hat16[tb - blk:tb], va16[tb - blk:tb]], axis=1)

    row_g = lax.broadcasted_iota(jnp.int32, (rows4, 1), 0) // blk
    key_is_prev = lax.broadcasted_iota(jnp.int32, (rows4, 2 * blk), 1) < blk
    first_penalty = jnp.where(step == 0, -NEG_INF, 0.0)
    lane_low = lax.broadcasted_iota(jnp.int32, (blk, PAIR), 1) < A_HD
    ones_cols = jnp.ones((2 * blk, PAIR), BF16)

    n_blk = tb // blk
    att = [(j, kv) for j in range(n_blk) for kv in range(A_KV)]
    ret = [(j, hb) for j in range(n_blk) for hb in range(B_HEADS)]

    sinks, scores = {}, {}
    for j, kv in att:
        r0 = j * blk
        kcol = slice(kv * PAIR, (kv + 1) * PAIR)
        if j == 0:
            k2 = jnp.concatenate([prev[:, kcol], k_hat16[0:blk, kcol]], axis=0)
        else:
            k2 = k_hat16[r0 - blk:r0 + blk, kcol]
        q4 = jnp.concatenate(
            [jnp.where(lane_low == (g % 2 == 0),
                       q_hat[r0:r0 + blk, (kv * A_GROUP + g - g % 2) * A_HD:(kv * A_GROUP + g - g % 2 + 2) * A_HD],
                       jnp.zeros((), BF16))
             for g in range(A_GROUP)], axis=0)
        scores[j, kv] = _dot_nt(q4, k2)
        sink = jnp.zeros((rows4, 1), F32)
        for g in range(A_GROUP):
            sink = jnp.where(row_g == g, sink_ref[kv * A_GROUP + g], sink)
        sinks[j, kv] = sink
    qb, vb, kt, inner = {}, {}, {}, {}
    for j, hb in ret:
        r0 = j * blk
        qb[j, hb] = proj[r0:r0 + blk, P_OFF_QB + hb * B_KD:P_OFF_QB + (hb + 1) * B_KD].astype(BF16)
        vb[j, hb] = proj[r0:r0 + blk, P_OFF_VB + hb * B_VD:P_OFF_VB + (hb + 1) * B_VD].astype(BF16)
        kt[j, hb] = kt_all[hb * B_KD:(hb + 1) * B_KD, r0:r0 + blk]
        inner[j, hb] = _dot(qb[j, hb], kt[j, hb].astype(BF16))

    probs, maxes = {}, {}
    for j, kv in att:
        s = scores[j, kv] * (A_HD ** -0.5) - bias_ref[kv]
        if j == 0:
            s = s - jnp.where(key_is_prev, first_penalty, 0.0)
        mx = jnp.maximum(jnp.max(s, axis=-1, keepdims=True), sinks[j, kv])
        probs[j, kv] = jnp.exp(s - mx).astype(BF16)
        maxes[j, kv] = mx
    state = {}
    for hb in range(B_HEADS):
        d_c = math.exp(_ret_log_gamma(hb) * blk)
        state[0, hb] = ps_ref[hb]
        for j in range(n_blk):
            state[j + 1, hb] = state[j, hb] * d_c + _dot((kt[j, hb] * dec_ref[hb, 2]).astype(BF16), vb[j, hb])
        ps_ref[hb] = state[n_blk, hb]

    for j, kv in att:
        r0 = j * blk
        kcol = slice(kv * PAIR, (kv + 1) * PAIR)
        vcol = slice(2 * KA_W + kv * PAIR, 2 * KA_W + (kv + 1) * PAIR)
        if j == 0:
            v2 = jnp.concatenate([prev[:, vcol], va16[0:blk, kcol]], axis=0)
        else:
            v2 = va16[r0 - blk:r0 + blk, kcol]
        pv = _dot(probs[j, kv], jnp.concatenate([v2, ones_cols], axis=1))
        o4 = pv[:, 0:PAIR] / (pv[:, PAIR:2 * PAIR] + jnp.exp(sinks[j, kv] - maxes[j, kv]))
        for g in range(A_GROUP):
            hd = kv * A_GROUP + g
            half = slice((hd % 2) * A_HD, (hd % 2 + 1) * A_HD)
            mix_ref[r0:r0 + blk, hd * A_HD:(hd + 1) * A_HD] = o4[g * blk:(g + 1) * blk, half]
    o_ret = {}
    for j, hb in ret:
        o_ret[j, hb] = (_dot((inner[j, hb] * dec_ref[hb, 0]).astype(BF16), vb[j, hb])
                        + _dot(qb[j, hb], state[j, hb].astype(BF16)) * dec_ref[hb, 1])

    cen = {k: o_ret[k] - jnp.mean(o_ret[k], axis=-1, keepdims=True) for k in ret}
    var = {k: jnp.mean(cen[k] * cen[k], axis=-1, keepdims=True) for k in ret}
    for j, hb in ret:
        r0 = j * blk
        gb = proj[r0:r0 + blk, P_OFF_GB + hb * B_VD:P_OFF_GB + (hb + 1) * B_VD]
        mix_ref[r0:r0 + blk, QA_W + hb * B_VD:QA_W + (hb + 1) * B_VD] = (
            cen[j, hb] * lax.rsqrt(var[j, hb] + EPS) * rg[:, hb * B_VD:(hb + 1) * B_VD] * (gb * jax.nn.sigmoid(gb)))

    out = _dot(mix_ref[...].astype(BF16), wout_ref[...])
    o_ref[...] = x + _gate(mod, 0, d) * out.reshape(sb, ROWS, d)

    @pl.when(step == pl.num_programs(0) - 1)
    def _():
        pk_ref[...] = _dedup(k_hat[tb - blk:tb, :])
        pv_ref[...] = _dedup(va[tb - blk:tb, :])


def _dup_heads(t, width):
    lead = t.shape[:-1]
    t = t.reshape(lead + (-1, 1, width))
    return jnp.broadcast_to(t, lead + (t.shape[-3], 2, width)).reshape(lead + (-1,))


def _mix_even_prompt(x3, mod3, gain, w_in, q_gain, k_gain, sinks, ret_gain, w_out, sb, cast=()):
    n8, _, d = x3.shape
    tb = sb * ROWS
    cast_specs = [_cast_specs(w, n8 // sb) for w in cast]
    w_main = jnp.concatenate([w_in[:, :OFF_KA], _dup_heads(w_in[:, OFF_KA:OFF_VA], A_HD),
                              _dup_heads(w_in[:, OFF_VA:OFF_QB], A_HD), w_in[:, OFF_QB:OFF_KB],
                              w_in[:, OFF_VB:]], axis=1)
    wk_t = w_in[:, OFF_KB:OFF_VB].T
    out = pl.pallas_call(
        functools.partial(_mix_even_prompt_body, len(cast)),
        grid=(n8 // sb,),
        in_specs=[pl.BlockSpec((sb, ROWS, d), lambda i: (i, 0, 0)),
                  pl.BlockSpec((1, 1, 6 * d), lambda i: (0, 0, 0)),
                  _const_spec((1, d)), _const_spec((d, P_WIDTH)), _const_spec((KB_W, d)),
                  _const_spec((1, QA_W)), _const_spec((1, 2 * KA_W)),
                  pl.BlockSpec(memory_space=pltpu.SMEM),
                  _const_spec((1, VB_W)), _const_spec((QA_W + VB_W, d))] + [c[0] for c in cast_specs],
        out_specs=[pl.BlockSpec((sb, ROWS, d), lambda i: (i, 0, 0)),
                   pl.BlockSpec((WINDOW, KA_W), lambda i: (0, 0)),
                   pl.BlockSpec((WINDOW, VA_W), lambda i: (0, 0)),
                   pl.BlockSpec((B_HEADS, B_KD, B_VD), lambda i: (0, 0, 0))] + [c[1] for c in cast_specs],
        out_shape=[jax.ShapeDtypeStruct(x3.shape, F32),
                   jax.ShapeDtypeStruct((WINDOW, KA_W), F32),
                   jax.ShapeDtypeStruct((WINDOW, VA_W), F32),
                   jax.ShapeDtypeStruct((B_HEADS, B_KD, B_VD), F32)] + [c[2] for c in cast_specs],
        scratch_shapes=[pltpu.VMEM((tb, QA_W + VB_W), F32),
                        pltpu.VMEM((2, WINDOW, 2 * KA_W + 2 * VA_W), BF16),
                        pltpu.VMEM((A_KV, A_GROUP * WINDOW, 2 * WINDOW), F32),
                        pltpu.VMEM((B_HEADS, 3, WINDOW, WINDOW), F32),
                        pltpu.VMEM((QA_W, QA_W), BF16)],
        compiler_params=_params(),
        name="mix_even_prompt",
    )(x3, mod3, gain.reshape(1, d), w_main, wk_t, jnp.tile(q_gain, A_HEADS).reshape(1, QA_W),
      jnp.tile(k_gain, 2 * A_KV).reshape(1, 2 * KA_W),
      sinks, ret_gain.reshape(1, VB_W), w_out, *[w for w, _ in cast])
    return out[:4], tuple(out[4:])


def _mix_even_sample_body(x_ref, mod_ref, gn_ref, win_ref, qg_ref, kg_ref, sink_ref, rg_ref, wout_ref,
                          ck_ref, cv_ref, s0_ref,
                          o_ref, nk_ref, nv_ref, ns_ref, mix_ref):
    x = x_ref[...]
    sb, length, d = x.shape
    tb = sb * length
    w = ck_ref.shape[1]
    mod = _load_mod(mod_ref)
    h = _modulate(x, mod, gn_ref[...], 0).reshape(tb, d).astype(BF16)
    proj = _dot(h, win_ref[...])
    qg = qg_ref[...]
    kg = kg_ref[...]
    rg = rg_ref[...]

    rows = A_GROUP * length
    qpos_c = lax.broadcasted_iota(jnp.int32, (rows, w), 0) % length
    kpos_c = lax.broadcasted_iota(jnp.int32, (rows, w), 1)
    dist_c = w + qpos_c - kpos_c
    valid_c = (dist_c >= 0) & (dist_c < WINDOW)
    qpos_n = lax.broadcasted_iota(jnp.int32, (rows, length), 0) % length
    kpos_n = lax.broadcasted_iota(jnp.int32, (rows, length), 1)
    dist_n = qpos_n - kpos_n
    valid_n = (dist_n >= 0) & (dist_n < WINDOW)
    row_g = lax.broadcasted_iota(jnp.int32, (rows, 1), 0) // length

    for kv in range(A_KV):
        lanes = slice(kv * A_HD, (kv + 1) * A_HD)
        kn = _head_rms(proj[:, OFF_KA + kv * A_HD:OFF_KA + (kv + 1) * A_HD], kg).reshape(sb, length, A_HD)
        vn = proj[:, OFF_VA + kv * A_HD:OFF_VA + (kv + 1) * A_HD].reshape(sb, length, A_HD)
        nk_ref[:, 0:w - length, lanes] = ck_ref[:, length:w, lanes]
        nv_ref[:, 0:w - length, lanes] = cv_ref[:, length:w, lanes]
        nk_ref[:, w - length:w, lanes] = kn
        nv_ref[:, w - length:w, lanes] = vn
        kc = ck_ref[:, :, lanes].astype(BF16)
        vc = cv_ref[:, :, lanes].astype(BF16)
        q4 = jnp.concatenate(
            [_head_rms(proj[:, OFF_QA + (kv * A_GROUP + g) * A_HD:OFF_QA + (kv * A_GROUP + g + 1) * A_HD], qg)
             .reshape(sb, length, A_HD) for g in range(A_GROUP)], axis=1).astype(BF16)
        slope = jnp.zeros((rows, 1), F32)
        sink = jnp.zeros((rows, 1), F32)
        for g in range(A_GROUP):
            hd = kv * A_GROUP + g
            slope = jnp.where(row_g == g, _alibi_slope(hd), slope)
            sink = jnp.where(row_g == g, sink_ref[hd], sink)
        scale = A_HD ** -0.5
        s_c = _bmm_nt(q4, kc) * scale - slope * dist_c.astype(F32)
        s_c = jnp.where(valid_c, s_c, NEG_INF)
        s_n = _bmm_nt(q4, kn.astype(BF16)) * scale - slope * dist_n.astype(F32)
        s_n = jnp.where(valid_n, s_n, NEG_INF)
        mx = jnp.maximum(jnp.maximum(jnp.max(s_c, axis=-1, keepdims=True),
                                     jnp.max(s_n, axis=-1, keepdims=True)), sink)
        p_c = jnp.exp(s_c - mx)
        p_n = jnp.exp(s_n - mx)
        den = (jnp.sum(p_c, axis=-1, keepdims=True) + jnp.sum(p_n, axis=-1, keepdims=True)
               + jnp.exp(sink - mx))
        o4 = (_bmm(p_c.astype(BF16), vc) + _bmm(p_n.astype(BF16), vn.astype(BF16))) / den
        for g in range(A_GROUP):
            hd = kv * A_GROUP + g
            mix_ref[:, hd * A_HD:(hd + 1) * A_HD] = o4[:, g * length:(g + 1) * length, :].reshape(tb, A_HD)

    for hb in range(B_HEADS):
        d_in, d_q, d_k, d_c = _ret_decay(hb, length)
        qb = proj[:, OFF_QB + hb * B_KD:OFF_QB + (hb + 1) * B_KD].reshape(sb, length, B_KD).astype(BF16)
        kb = proj[:, OFF_KB + hb * B_KD:OFF_KB + (hb + 1) * B_KD].reshape(sb, length, B_KD) * (B_KD ** -0.5)
        vb = proj[:, OFF_VB + hb * B_VD:OFF_VB + (hb + 1) * B_VD].reshape(sb, length, B_VD).astype(BF16)
        gb = proj[:, OFF_GB + hb * B_VD:OFF_GB + (hb + 1) * B_VD]
        state = s0_ref[:, hb]
        inner = _bmm_nt(qb, kb.astype(BF16)) * d_in
        o = _bmm(inner.astype(BF16), vb) + _bmm(qb, state.astype(BF16)) * d_q
        ns_ref[:, hb] = state * d_c + _bmm_tn((kb * d_k).astype(BF16), vb)
        mix_ref[:, QA_W + hb * B_VD:QA_W + (hb + 1) * B_VD] = _group_norm_gate(
            o.reshape(tb, B_VD), rg[:, hb * B_VD:(hb + 1) * B_VD], gb)

    out = _dot(mix_ref[...].astype(BF16), wout_ref[...])
    o_ref[...] = x + _gate(mod, 0, d) * out.reshape(sb, length, d)


def _mix_even_sample(x3, mod3, gain, w_in, q_gain, k_gain, sinks, ret_gain, w_out, cache_k, cache_v, state, sb):
    n, length, d = x3.shape
    in_w = w_in.shape[1]
    w = cache_k.shape[1]
    seq_spec = lambda shape: pl.BlockSpec((sb,) + shape, lambda i, _n=len(shape): (i,) + (0,) * _n)
    return pl.pallas_call(
        _mix_even_sample_body,
        grid=(n // sb,),
        in_specs=[seq_spec((length, d)), seq_spec((6 * d,)),
                  _const_spec((1, d)), _const_spec((d, in_w)),
                  _const_spec((1, A_HD)), _const_spec((1, A_HD)),
                  pl.BlockSpec(memory_space=pltpu.SMEM),
                  _const_spec((1, VB_W)), _const_spec((QA_W + VB_W, d)),
                  seq_spec((w, KA_W)), seq_spec((w, VA_W)),
                  pl.BlockSpec((None, sb, B_HEADS, B_KD, B_VD), lambda i: (0, i, 0, 0, 0))],
        out_specs=[seq_spec((length, d)), seq_spec((w, KA_W)), seq_spec((w, VA_W)),
                   seq_spec((B_HEADS, B_KD, B_VD))],
        out_shape=[jax.ShapeDtypeStruct(x3.shape, F32),
                   jax.ShapeDtypeStruct(cache_k.shape, F32),
                   jax.ShapeDtypeStruct(cache_v.shape, F32),
                   jax.ShapeDtypeStruct(state.shape[1:], F32)],
        scratch_shapes=[pltpu.VMEM((sb * length, QA_W + VB_W), F32)],
        compiler_params=_params(),
        name="mix_even_sample",
    )(x3, mod3, gain.reshape(1, d), w_in, q_gain.reshape(1, A_HD), k_gain.reshape(1, A_HD),
      sinks, ret_gain.reshape(1, VB_W), w_out, cache_k, cache_v, state)


def _cmul(ar, ai, br, bi):
    return ar * br - ai * bi, ar * bi + ai * br


def _s5_lambda(a_re, a_im, log_dt):
    dt = jnp.exp(log_dt)
    mag = jnp.exp(a_re * dt)
    return mag * jnp.cos(a_im * dt), mag * jnp.sin(a_im * dt)


def _s5_prep_body(t_len, n_chunks, are_ref, aim_ref, ldt_ref, btr_ref, bti_ref, ard_ref, aid_ref, ldd_ref,
                  lre_ref, lim_ref, bbr_ref, bbi_ref, pwr_ref, pwi_ref):
    a_re = are_ref[...]
    a_im = aim_ref[...]
    lam_re, lam_im = _s5_lambda(a_re, a_im, ldt_ref[...])
    den = a_re * a_re + a_im * a_im
    n_re = lam_re - 1.0
    n_im = lam_im
    f_re = (n_re * a_re + n_im * a_im) / den
    f_im = (n_im * a_re - n_re * a_im) / den
    br = btr_ref[...]
    bi = bti_ref[...]
    lre_ref[...] = lam_re
    lim_ref[...] = lam_im
    bbr_ref[...] = f_re * br - f_im * bi
    bbi_ref[...] = f_re * bi + f_im * br
    lam_re, lam_im = _s5_lambda(ard_ref[...], aid_ref[...], ldd_ref[...])
    cr, ci = lam_re, lam_im
    for t in range(t_len):
        pwr_ref[t] = cr
        pwi_ref[t] = ci
        if t + 1 < t_len:
            cr, ci = _cmul(cr, ci, lam_re, lam_im)
    tr, ti = cr, ci
    cr, ci = jnp.ones_like(lam_re), jnp.zeros_like(lam_im)
    for m in range(n_chunks + 1):
        pwr_ref[t_len + m] = cr
        pwi_ref[t_len + m] = ci
        cr, ci = _cmul(cr, ci, tr, ti)


def _s5_prep(a_re, a_im, log_dt, b_re, b_im, t_len, n_chunks):
    g, p = a_re.shape
    k = b_re.shape[-1]
    bt_re = jnp.swapaxes(b_re, 1, 2)
    bt_im = jnp.swapaxes(b_im, 1, 2)
    n_pow = t_len + n_chunks + 1
    dense = (g // S5_OCT, S5_OCT * p)
    out = pl.pallas_call(
        functools.partial(_s5_prep_body, t_len, n_chunks),
        out_shape=[jax.ShapeDtypeStruct((g, 1, p), F32), jax.ShapeDtypeStruct((g, 1, p), F32),
                   jax.ShapeDtypeStruct((g, k, p), F32), jax.ShapeDtypeStruct((g, k, p), F32),
                   jax.ShapeDtypeStruct((n_pow,) + dense, F32), jax.ShapeDtypeStruct((n_pow,) + dense, F32)],
        name="s5_prep",
    )(a_re.reshape(g, 1, p), a_im.reshape(g, 1, p), log_dt.reshape(g, 1, 1), bt_re, bt_im,
      a_re.reshape(dense), a_im.reshape(dense), jnp.broadcast_to(log_dt[:, None], (g, p)).reshape(dense))
    return out


def _block_diag(t):
    g, a, b = t.shape
    t = t.reshape(g // S5_OCT, S5_OCT, a, b)
    eye = jnp.eye(S5_OCT, dtype=t.dtype)
    out = t[:, :, :, None, :] * eye[None, :, None, :, None]
    return out.reshape(g // S5_OCT, S5_OCT * a, S5_OCT * b)


def _gelu_glu_out(x, mod, y, u, dskip, glua_ref, glub_ref):
    sb, rows, d = x.shape
    y = y + dskip * u
    yg = jax.nn.gelu(y, approximate=True).astype(BF16)
    out = _dot(yg, glua_ref[...]) * jax.nn.sigmoid(_dot(yg, glub_ref[...]))
    return x + _gate(mod, 0, d) * out.reshape(sb, rows, d)


def _mix_odd_prompt_body(x_ref, xn_ref, mod_ref, gn_ref, bblk_ref, lam_ref, pwb_ref, ltp_ref, cre_ref, cim_ref,
                         dskip_ref, glua_ref, glub_ref, o_ref, hre_ref, him_ref,
                         un_ref, l0_ref, l1_ref, up0_ref, up1_ref, pt_ref):
    step = pl.program_id(0)
    sb, _, d = x_ref.shape
    tm = sb * ROWS
    t_len = tm // ROWS
    n_oct = bblk_ref.shape[0]
    half = bblk_ref.shape[2] // 2
    mod = mod_ref[...]

    def permuted_input(src_ref, up_dst):
        u = _modulate(src_ref[...], mod, gn_ref[...], 0).reshape(tm, d)
        for k in range(d // 128):
            un_ref[k] = u[:, k * 128:(k + 1) * 128]
        up = jnp.concatenate(
            [jnp.concatenate([un_ref[k, pl.ds(t, ROWS, stride=t_len), :] for k in range(d // 128)], axis=1)
             for t in range(t_len)], axis=0)
        up_dst[...] = up
        return up.astype(BF16)

    @pl.when(step == 0)
    def _():
        hre_ref[...] = jnp.zeros_like(hre_ref)
        him_ref[...] = jnp.zeros_like(him_ref)
        nat = lax.broadcasted_iota(jnp.int32, (tm, tm), 0)
        prm = lax.broadcasted_iota(jnp.int32, (tm, tm), 1)
        pt_ref[...] = jnp.where(prm == (nat % t_len) * ROWS + nat // t_len, 1.0, 0.0).astype(BF16)
        up16 = permuted_input(x_ref, up0_ref)
        for s in range(n_oct):
            l0_ref[s] = _dot(up16[:, s * 128:(s + 1) * 128], bblk_ref[s])

    def run(l_ref, l_next, up_ref, up_next):
        for s0 in range(0, n_oct, 2):
            pair = (s0, s0 + 1)
            lam_re = [jnp.broadcast_to(lam_ref[0, s:s + 1, :], (ROWS, half)) for s in pair]
            lam_im = [jnp.broadcast_to(lam_ref[1, s:s + 1, :], (ROWS, half)) for s in pair]

            def local_step(t, carry):
                r = pl.multiple_of(t * ROWS, ROWS)
                out = []
                for i, s in enumerate(pair):
                    h_re, h_im = carry[2 * i], carry[2 * i + 1]
                    n_re = lam_re[i] * h_re - lam_im[i] * h_im + l_ref[s, pl.ds(r, ROWS), 0:half]
                    n_im = lam_re[i] * h_im + lam_im[i] * h_re + l_ref[s, pl.ds(r, ROWS), half:2 * half]
                    l_ref[s, pl.ds(r, ROWS), 0:half] = n_re
                    l_ref[s, pl.ds(r, ROWS), half:2 * half] = n_im
                    out += [n_re, n_im]
                return tuple(out)

            zero = jnp.zeros((ROWS, half), F32)
            lax.fori_loop(0, t_len, local_step, (zero, zero, zero, zero), unroll=2)

        next16 = permuted_input(xn_ref, up_next)
        chunk = lax.broadcasted_iota(jnp.int32, (ROWS, half), 0)
        y_parts = []
        for s in range(n_oct):
            l_next[s] = _dot(next16[:, s * 128:(s + 1) * 128], bblk_ref[s])
            p_re = l_ref[s, tm - ROWS:tm, 0:half]
            p_im = l_ref[s, tm - ROWS:tm, half:2 * half]
            for i, sh in enumerate((1, 2, 4)):
                m_re, m_im = _cmul(ltp_ref[s, i:i + 1, 0:half], ltp_ref[s, i:i + 1, half:2 * half],
                                   jnp.where(chunk >= sh, pltpu.roll(p_re, sh, 0), 0.0),
                                   jnp.where(chunk >= sh, pltpu.roll(p_im, sh, 0), 0.0))
                p_re, p_im = p_re + m_re, p_im + m_im
            hin_re = jnp.broadcast_to(hre_ref[s:s + 1, :], (ROWS, half))
            hin_im = jnp.broadcast_to(him_ref[s:s + 1, :], (ROWS, half))
            m_re, m_im = _cmul(ltp_ref[s, ROWS:2 * ROWS, 0:half], ltp_ref[s, ROWS:2 * ROWS, half:2 * half],
                               hin_re, hin_im)
            st_re = m_re + jnp.where(chunk >= 1, pltpu.roll(p_re, 1, 0), 0.0)
            st_im = m_im + jnp.where(chunk >= 1, pltpu.roll(p_im, 1, 0), 0.0)
            m_re, m_im = _cmul(ltp_ref[s, 3:4, 0:half], ltp_ref[s, 3:4, half:2 * half],
                               hre_ref[s:s + 1, :], him_ref[s:s + 1, :])
            hre_ref[s:s + 1, :] = m_re + p_re[ROWS - 1:ROWS, :]
            him_ref[s:s + 1, :] = m_im + p_im[ROWS - 1:ROWS, :]
            loc = l_ref[s].reshape(t_len, ROWS, 2 * half)
            pw = pwb_ref[s]
            f_re, f_im = _cmul(pw[:, :, 0:half], pw[:, :, half:2 * half], st_re[None], st_im[None])
            hs_re = (loc[:, :, 0:half] + f_re).reshape(tm, half).astype(BF16)
            hs_im = (loc[:, :, half:2 * half] + f_im).reshape(tm, half).astype(BF16)
            y_parts.append(_dot(hs_re, cre_ref[s]) - _dot(hs_im, cim_ref[s]))

        y = jnp.concatenate(y_parts, axis=1) + dskip_ref[...] * up_ref[...]
        yg = jax.nn.gelu(y, approximate=True).astype(BF16)
        yn = _dot(pt_ref[...], yg).astype(BF16)
        out = _dot(yn, glua_ref[...]) * jax.nn.sigmoid(_dot(yn, glub_ref[...]))
        o_ref[...] = x_ref[...] + _gate(mod, 0, d) * out.reshape(sb, ROWS, d)

    @pl.when(step % 2 == 0)
    def _():
        run(l0_ref, l1_ref, up0_ref, up1_ref)

    @pl.when(step % 2 == 1)
    def _():
        run(l1_ref, l0_ref, up1_ref, up0_ref)


def _mix_odd_prompt(x3, mod3, gain, bblk, lam_d, pwb, ltp, cre, cim, dskip, glu_a, glu_b, sb):
    n8, _, d = x3.shape
    n_oct, kin, wid = bblk.shape
    half = wid // 2
    tm = sb * ROWS
    n_tiles = n8 // sb
    proj_buf = pltpu.VMEM((n_oct, tm, wid), F32)
    perm_buf = pltpu.VMEM((tm, d), F32)
    return pl.pallas_call(
        _mix_odd_prompt_body,
        grid=(n_tiles,),
        in_specs=[pl.BlockSpec((sb, ROWS, d), lambda i: (i, 0, 0)),
                  pl.BlockSpec((sb, ROWS, d), lambda i: (jnp.minimum(i + 1, n_tiles - 1), 0, 0)),
                  pl.BlockSpec((1, 1, 6 * d), lambda i: (0, 0, 0)),
                  _const_spec((1, d)), _const_spec(bblk.shape), _const_spec(lam_d.shape),
                  _const_spec(pwb.shape), _const_spec(ltp.shape),
                  _const_spec(cre.shape), _const_spec(cim.shape), _const_spec((1, d)),
                  _const_spec((d, d)), _const_spec((d, d))],
        out_specs=[pl.BlockSpec((sb, ROWS, d), lambda i: (i, 0, 0)),
                   pl.BlockSpec((n_oct, half), lambda i: (0, 0)),
                   pl.BlockSpec((n_oct, half), lambda i: (0, 0))],
        out_shape=[jax.ShapeDtypeStruct(x3.shape, F32),
                   jax.ShapeDtypeStruct((n_oct, half), F32),
                   jax.ShapeDtypeStruct((n_oct, half), F32)],
        scratch_shapes=[pltpu.VMEM((d // 128, tm, 128), F32), proj_buf, proj_buf, perm_buf, perm_buf,
                        pltpu.VMEM((tm, tm), BF16)],
        compiler_params=_params(),
        name="mix_odd_prompt",
    )(x3, x3, mod3, gain.reshape(1, d), bblk, lam_d, pwb, ltp, cre, cim, dskip.reshape(1, d), glu_a, glu_b)


def _mix_odd_sample_body(x_ref, mod_ref, gn_ref, bblk_ref, lam_ref, cre_ref, cim_ref, dskip_ref,
                         glua_ref, glub_ref, sre_ref, sim_ref, o_ref, nre_ref, nim_ref, d_ref, y_ref):
    x = x_ref[...]
    sb, length, d = x.shape
    tm = sb * length
    n_oct = bblk_ref.shape[0]
    half = bblk_ref.shape[2] // 2
    mod = _load_mod(mod_ref)
    u = _modulate(x, mod, gn_ref[...], 0).reshape(tm, d)
    u16 = u.astype(BF16)
    n_ch = bblk_ref.shape[2] // 128
    hc = n_ch // 2
    for s in range(n_oct):
        bu = _dot(u16[:, s * 128:(s + 1) * 128], bblk_ref[s])
        for c in range(n_ch):
            d_ref[c] = bu[:, c * 128:(c + 1) * 128]
        lam_re = lam_ref[0, s:s + 1, :]
        lam_im = lam_ref[1, s:s + 1, :]
        h_re = sre_ref[:, s * half:(s + 1) * half]
        h_im = sim_ref[:, s * half:(s + 1) * half]
        for t in range(length):
            b_re = jnp.concatenate([d_ref[c, pl.ds(t, sb, stride=length), :] for c in range(hc)], axis=1)
            b_im = jnp.concatenate([d_ref[hc + c, pl.ds(t, sb, stride=length), :] for c in range(hc)], axis=1)
            n_re = lam_re * h_re - lam_im * h_im + b_re
            n_im = lam_re * h_im + lam_im * h_re + b_im
            for c in range(hc):
                d_ref[c, pl.ds(t, sb, stride=length), :] = n_re[:, c * 128:(c + 1) * 128]
                d_ref[hc + c, pl.ds(t, sb, stride=length), :] = n_im[:, c * 128:(c + 1) * 128]
            h_re, h_im = n_re, n_im
        nre_ref[:, s * half:(s + 1) * half] = h_re
        nim_ref[:, s * half:(s + 1) * half] = h_im
        hs_re = jnp.concatenate([d_ref[c] for c in range(hc)], axis=1)
        hs_im = jnp.concatenate([d_ref[hc + c] for c in range(hc)], axis=1)
        y_ref[:, s * 128:(s + 1) * 128] = (_dot(hs_re.astype(BF16), cre_ref[s])
                                           - _dot(hs_im.astype(BF16), cim_ref[s]))
    o_ref[...] = _gelu_glu_out(x, mod, y_ref[...], u, dskip_ref[...], glua_ref, glub_ref)


def _mix_odd_sample(x3, mod3, gain, bblk, lam_d, cre, cim, dskip, glu_a, glu_b, s_re, s_im, sb):
    n, length, d = x3.shape
    n_oct, kin, wid = bblk.shape
    tm = sb * length
    nstate = s_re.shape[1]
    return pl.pallas_call(
        _mix_odd_sample_body,
        grid=(n // sb,),
        in_specs=[pl.BlockSpec((sb, length, d), lambda i: (i, 0, 0)),
                  pl.BlockSpec((sb, 6 * d), lambda i: (i, 0)),
                  _const_spec((1, d)), _const_spec(bblk.shape), _const_spec(lam_d.shape),
                  _const_spec(cre.shape), _const_spec(cim.shape), _const_spec((1, d)),
                  _const_spec((d, d)), _const_spec((d, d)),
                  pl.BlockSpec((sb, nstate), lambda i: (i, 0)),
                  pl.BlockSpec((sb, nstate), lambda i: (i, 0))],
        out_specs=[pl.BlockSpec((sb, length, d), lambda i: (i, 0, 0)),
                   pl.BlockSpec((sb, nstate), lambda i: (i, 0)),
                   pl.BlockSpec((sb, nstate), lambda i: (i, 0))],
        out_shape=[jax.ShapeDtypeStruct(x3.shape, F32),
                   jax.ShapeDtypeStruct(s_re.shape, F32),
                   jax.ShapeDtypeStruct(s_im.shape, F32)],
        scratch_shapes=[pltpu.VMEM((wid // 128, tm, 128), F32), pltpu.VMEM((tm, d), F32)],
        compiler_params=_params(),
        name="mix_odd_sample",
    )(x3, mod3, gain.reshape(1, d), bblk, lam_d, cre, cim, dskip.reshape(1, d), glu_a, glu_b, s_re, s_im)


def _pick(n, want):
    while n % want:
        want //= 2
    return max(want, 1)


def kernel(x_prompt, x_sample, cache_win_k, cache_win_v, state_ret, state_s5_re, state_s5_im, c_prompt, c_sample, ada_w, ada_b, norm_mix, norm_ffn, ffn_wg, ffn_wu, ffn_wd, even_w_in, even_q_gain, even_k_gain, even_sinks, even_ret_gain, even_w_out, odd_A_re, odd_A_im, odd_log_dt, odd_B_re, odd_B_im, odd_C_re, odd_C_im, odd_D, odd_glu_a, odd_glu_b):
    bp, lp, d = x_prompt.shape
    ns, ls, _ = x_sample.shape
    assert bp == 1 and ls == ROWS and lp % WINDOW == 0
    w = cache_win_k.shape[2]
    groups, p_state = odd_A_re.shape[1:]

    n_c = bp + ns
    n_pad = -n_c % ROWS
    c_all = jnp.concatenate([c_prompt, c_sample, jnp.zeros((n_pad, d), F32)], axis=0)
    mod = _adaln(c_all, ada_w, ada_b)
    mod_p = [mod[l, 0:1].reshape(1, 1, 6 * d) for l in range(2)]
    mod_s = [mod[l, bp:bp + ns] for l in range(2)]

    bf = lambda t: t.astype(BF16)
    w_in, w_out = bf(even_w_in[0]), bf(even_w_out[0])

    xp = x_prompt.reshape(lp // ROWS, ROWS, d)
    xs = x_sample

    sb_p = _pick(lp // ROWS, 32)
    sb_ffn = _pick(lp // ROWS, 64)
    sb_s = _pick(ns, 32)
    sb_s_even = _pick(ns, 16)

    (xp, p_k, p_v, p_ret), ffn0 = _mix_even_prompt(xp, mod_p[0], norm_mix[0], w_in, even_q_gain[0], even_k_gain[0],
                                                   even_sinks[0], even_ret_gain[0], w_out, sb_ffn,
                                                   cast=((ffn_wg, 0), (ffn_wu, 0), (ffn_wd, 0)))
    xs, s_k, s_v, s_ret = _mix_even_sample(xs, mod_s[0], norm_mix[0], w_in, even_q_gain[0], even_k_gain[0],
                                           even_sinks[0], even_ret_gain[0], w_out,
                                           cache_win_k[0].reshape(ns, w, KA_W), cache_win_v[0].reshape(ns, w, VA_W),
                                           state_ret, sb_s_even)
    xp, xs, (wg1, wu1, wd1, glu_a, glu_b) = _ffn(
        xp, mod_p[0], xs, mod_s[0], norm_ffn[0], *ffn0, sb_ffn,
        cast=((ffn_wg, 1), (ffn_wu, 1), (ffn_wd, 1), (odd_glu_a, 0), (odd_glu_b, 0)))

    t_len = sb_p
    lam_re, lam_im, bbt_re, bbt_im, pw_re, pw_im = _s5_prep(odd_A_re[0], odd_A_im[0], odd_log_dt[0],
                                                            odd_B_re[0], odd_B_im[0], t_len, ROWS)
    n_oct = groups // S5_OCT
    half = S5_OCT * p_state
    bblk = bf(jnp.concatenate([_block_diag(bbt_re), _block_diag(bbt_im)], axis=-1))
    cre = bf(_block_diag(jnp.swapaxes(odd_C_re[0], 1, 2)))
    cim = bf(_block_diag(jnp.swapaxes(odd_C_im[0], 1, 2)))
    lam_d = jnp.stack([lam_re.reshape(n_oct, half), lam_im.reshape(n_oct, half)])

    pw_d = jnp.concatenate([pw_re, pw_im], axis=-1)
    pwb = jnp.broadcast_to(jnp.swapaxes(pw_d[:t_len], 0, 1)[:, :, None, :], (n_oct, t_len, ROWS, 2 * half))
    lt = jnp.swapaxes(pw_d[t_len:], 0, 1)
    ltp = jnp.concatenate([lt[:, 1:2], lt[:, 2:3], lt[:, 4:5], lt[:, 8:9], jnp.zeros_like(lt[:, 0:4]), lt[:, 0:8]],
                          axis=1)
    xp, p_re, p_im = _mix_odd_prompt(xp, mod_p[1], norm_mix[1], bblk, lam_d, pwb, ltp, cre, cim, odd_D[0],
                                     glu_a, glu_b, sb_p)
    xs, s_re, s_im = _mix_odd_sample(xs, mod_s[1], norm_mix[1], bblk, lam_d, cre, cim, odd_D[0], glu_a, glu_b,
                                     state_s5_re[0].reshape(ns, groups * p_state),
                                     state_s5_im[0].reshape(ns, groups * p_state), sb_s)
    xp, xs, _ = _ffn(xp, mod_p[1], xs, mod_s[1], norm_ffn[1], wg1, wu1, wd1, sb_ffn)

    y_prompt = xp.reshape(bp, lp, d)
    y_sample = xs
    return (y_prompt, y_sample,
            p_k.reshape(1, bp, WINDOW, A_KV, A_HD), p_v.reshape(1, bp, WINDOW, A_KV, A_HD),
            p_ret.reshape(1, bp, B_HEADS, B_KD, B_VD),
            p_re.reshape(1, bp, groups, p_state), p_im.reshape(1, bp, groups, p_state),
            s_k.reshape(1, ns, w, A_KV, A_HD), s_v.reshape(1, ns, w, A_KV, A_HD),
            s_ret.reshape(1, ns, B_HEADS, B_KD, B_VD),
            s_re.reshape(1, ns, groups, p_state), s_im.reshape(1, ns, groups, p_state))
```

```python
import functools
import math

import jax
import jax.numpy as jnp
from jax import lax
from jax.experimental import pallas as pl
from jax.experimental.pallas import tpu as pltpu

F32 = jnp.float32
BF16 = jnp.bfloat16

EPS = 1e-6
NEG_INF = -1e30
ROWS = 8

A_HEADS, A_KV, A_GROUP, A_HD = 8, 2, 4, 64
WINDOW = 128
B_HEADS, B_KD, B_VD = 4, 128, 128
S5_GROUP, S5_STATE = 16, 64
S5_OCT = 8

QA_W, KA_W, VA_W = A_HEADS * A_HD, A_KV * A_HD, A_KV * A_HD
QB_W, KB_W, VB_W, GB_W = B_HEADS * B_KD, B_HEADS * B_KD, B_HEADS * B_VD, B_HEADS * B_VD
OFF_QA = 0
OFF_KA = OFF_QA + QA_W
OFF_VA = OFF_KA + KA_W
OFF_QB = OFF_VA + VA_W
OFF_KB = OFF_QB + QB_W
OFF_VB = OFF_KB + KB_W
OFF_GB = OFF_VB + VB_W

VMEM_LIMIT = 56 * 1024 * 1024


def _ret_log_gamma(h):
    return math.log1p(-(2.0 ** (-5.0 - h)))


def _alibi_slope(h):
    return 2.0 ** (-8.0 * (h + 1) / A_HEADS)


def _const_spec(shape):
    nd = len(shape)
    return pl.BlockSpec(shape, lambda i, _n=nd: (0,) * _n, pipeline_mode=pl.Buffered(1))


def _params():
    return pltpu.CompilerParams(dimension_semantics=("arbitrary",), vmem_limit_bytes=VMEM_LIMIT)


def _dot(a, b):
    return jnp.dot(a, b, preferred_element_type=F32)


def _dot_nt(a, b):
    return lax.dot_general(a, b, (((1,), (1,)), ((), ())), preferred_element_type=F32)


def _dot_tn(a, b):
    return lax.dot_general(a, b, (((0,), (0,)), ((), ())), preferred_element_type=F32)


def _bmm(a, b):
    return lax.dot_general(a, b, (((2,), (1,)), ((0,), (0,))), preferred_element_type=F32)


def _bmm_nt(a, b):
    return lax.dot_general(a, b, (((2,), (2,)), ((0,), (0,))), preferred_element_type=F32)


def _bmm_tn(a, b):
    return lax.dot_general(a, b, (((1,), (1,)), ((0,), (0,))), preferred_element_type=F32)


def _rms(x, g):
    return x * lax.rsqrt(jnp.mean(x * x, axis=-1, keepdims=True) + EPS) * g


def _modulate(x3, mod, gain, which):
    d = x3.shape[-1]
    sh = mod[:, :, (3 * which) * d:(3 * which + 1) * d]
    sc = mod[:, :, (3 * which + 1) * d:(3 * which + 2) * d]
    return _rms(x3, gain) * (1.0 + sc) + sh


def _load_mod(mod_ref):
    m = mod_ref[...]
    return m if m.ndim == 3 else m[:, None, :]


def _gate(mod, which, d):
    return mod[:, :, (3 * which + 2) * d:(3 * which + 3) * d]


def _adaln_body(c_ref, w_ref, b_ref, o_ref):
    c = c_ref[...]
    a = (c * jax.nn.sigmoid(c)).astype(BF16)
    o_ref[0] = _dot(a, w_ref[0].astype(BF16)) + b_ref[0]


def _adaln(c_all, ada_w, ada_b):
    depth, d, n = ada_w.shape
    r = c_all.shape[0]
    tn = 1536
    return pl.pallas_call(
        _adaln_body,
        grid=(depth, n // tn),
        in_specs=[pl.BlockSpec((r, d), lambda l, j: (0, 0)),
                  pl.BlockSpec((1, d, tn), lambda l, j: (l, 0, j)),
                  pl.BlockSpec((1, 1, tn), lambda l, j: (l, 0, j))],
        out_specs=pl.BlockSpec((1, r, tn), lambda l, j: (l, 0, j)),
        out_shape=jax.ShapeDtypeStruct((depth, r, n), F32),
        compiler_params=pltpu.CompilerParams(dimension_semantics=("arbitrary", "arbitrary"),
                                             vmem_limit_bytes=VMEM_LIMIT),
        name="adaln",
    )(c_all, ada_w, ada_b.reshape(depth, 1, n))


BF16_ROWS = 16


def _cast_specs(job, n_steps):
    w, layer = job
    _, rows, cols = w.shape
    hold = 1
    while rows % (n_steps // hold) or (rows // (n_steps // hold)) % BF16_ROWS:
        hold *= 2
        assert hold <= n_steps and n_steps % hold == 0
    blk = rows // (n_steps // hold)
    src = pl.BlockSpec((None, blk, cols),
                       lambda i, _h=hold, _l=layer: (_l, jnp.minimum(i, n_steps - 1) // _h, 0))
    dst = pl.BlockSpec((blk, cols), lambda i, _h=hold: (jnp.minimum(i, n_steps - 1) // _h, 0))
    return src, dst, jax.ShapeDtypeStruct((rows, cols), BF16)


def _cast_blocks(in_refs, out_refs):
    for src, dst in zip(in_refs, out_refs):
        dst[...] = src[...].astype(BF16)


def _ffn_body(n_cast, n_prompt, xp_ref, mp_ref, xs_ref, ms_ref, gn_ref, wg_ref, wu_ref, wd_ref, *refs):
    cast_in, (op_ref, os_ref), cast_out = refs[:n_cast], refs[n_cast:n_cast + 2], refs[n_cast + 2:]
    step = pl.program_id(0)

    def tile(x_ref, mod_ref, o_ref):
        x = x_ref[...]
        sb, _, d = x.shape
        mod = _load_mod(mod_ref)
        h = _modulate(x, mod, gn_ref[...], 1).reshape(sb * ROWS, d).astype(BF16)
        a = _dot(h, wg_ref[...])
        b = _dot(h, wu_ref[...])
        act = (a * jax.nn.sigmoid(a) * b).astype(BF16)
        y = _dot(act, wd_ref[...])
        o_ref[...] = x + _gate(mod, 1, d) * y.reshape(sb, ROWS, d)

    @pl.when(step < n_prompt)
    def _():
        tile(xp_ref, mp_ref, op_ref)

    @pl.when(step >= n_prompt)
    def _():
        tile(xs_ref, ms_ref, os_ref)

    _cast_blocks(cast_in, cast_out)


def _ffn(xp3, mod_p, xs3, mod_s, gain, wg, wu, wd, sb, cast=()):
    n8, _, d = xp3.shape
    f = wg.shape[1]
    n_prompt = n8 // sb
    n_sample = xs3.shape[0] // sb
    assert n8 % sb == 0 and xs3.shape[0] % sb == 0
    cast_specs = [_cast_specs(w, n_prompt) for w in cast]
    p_idx = lambda i: (jnp.minimum(i, n_prompt - 1), 0, 0)
    s_idx = lambda i: (jnp.maximum(i - n_prompt, 0), 0, 0)
    xp_spec = pl.BlockSpec((sb, ROWS, d), p_idx)
    xs_spec = pl.BlockSpec((sb, ROWS, d), s_idx)
    out = pl.pallas_call(
        functools.partial(_ffn_body, len(cast), n_prompt),
        grid=(n_prompt + n_sample,),
        in_specs=[xp_spec, pl.BlockSpec((1, 1, 6 * d), lambda i: (0, 0, 0)),
                  xs_spec, pl.BlockSpec((sb, 6 * d), lambda i: s_idx(i)[:2]),
                  _const_spec((1, d)), _const_spec((d, f)), _const_spec((d, f)),
                  _const_spec((f, d))] + [c[0] for c in cast_specs],
        out_specs=[xp_spec, xs_spec] + [c[1] for c in cast_specs],
        out_shape=[jax.ShapeDtypeStruct(xp3.shape, F32), jax.ShapeDtypeStruct(xs3.shape, F32)]
        + [c[2] for c in cast_specs],
        compiler_params=_params(),
        name="ffn",
    )(xp3, mod_p, xs3, mod_s, gain.reshape(1, d), wg, wu, wd, *[w for w, _ in cast])
    return out[0], out[1], tuple(out[2:])


def _head_rms(t, g):
    return t * lax.rsqrt(jnp.mean(t * t, axis=-1, keepdims=True) + EPS) * g


def _group_norm_gate(o, gain, gate):
    mu = jnp.mean(o, axis=-1, keepdims=True)
    var = jnp.mean(jnp.square(o - mu), axis=-1, keepdims=True)
    return (o - mu) * lax.rsqrt(var + EPS) * gain * (gate * jax.nn.sigmoid(gate))


def _ret_decay(hb, c):
    lg = _ret_log_gamma(hb)
    ii = lax.broadcasted_iota(jnp.int32, (c, c), 0)
    jj = lax.broadcasted_iota(jnp.int32, (c, c), 1)
    diff = (ii - jj).astype(F32)
    d_in = jnp.where(diff >= 0, jnp.exp(lg * jnp.maximum(diff, 0.0)), 0.0)
    row = lax.broadcasted_iota(jnp.int32, (c, B_KD), 0).astype(F32)
    d_q = jnp.exp(lg * (row + 1.0))
    d_k = jnp.exp(lg * (c - 1.0 - row))
    d_c = math.exp(lg * c)
    return d_in, d_q, d_k, d_c


P_OFF_KA = QA_W
P_OFF_VA = P_OFF_KA + 2 * KA_W
P_OFF_QB = P_OFF_VA + 2 * VA_W
P_OFF_VB = P_OFF_QB + QB_W
P_OFF_GB = P_OFF_VB + VB_W
P_WIDTH = P_OFF_GB + GB_W
PAIR = 2 * A_HD


def _dedup(t):
    low = lax.broadcasted_iota(jnp.int32, (t.shape[0], PAIR), 1) < A_HD
    return jnp.where(low, t[:, 0:PAIR], t[:, PAIR:2 * PAIR])


def _mix_even_prompt_body(n_cast, x_ref, mod_ref, gn_ref, win_ref, wkt_ref, qg_ref, kg_ref, sink_ref, rg_ref, wout_ref,
                          *refs):
    cast_in, refs = refs[:n_cast], refs[n_cast:]
    o_ref, pk_ref, pv_ref, ps_ref = refs[:4]
    cast_out = refs[4:4 + n_cast]
    mix_ref, carry_ref, bias_ref, dec_ref, ones_ref = refs[4 + n_cast:]
    _cast_blocks(cast_in, cast_out)
    step = pl.program_id(0)
    blk = WINDOW
    rows4 = A_GROUP * blk

    @pl.when(step == 0)
    def _():
        carry_ref[...] = jnp.zeros_like(carry_ref)
        ps_ref[...] = jnp.zeros_like(ps_ref)
        er = lax.broadcasted_iota(jnp.int32, ones_ref.shape, 0) // A_HD
        ec = lax.broadcasted_iota(jnp.int32, ones_ref.shape, 1) // A_HD
        ones_ref[...] = jnp.where(er == ec, 1.0 / A_HD, 0.0).astype(BF16)
        row = lax.broadcasted_iota(jnp.int32, (rows4, 2 * blk), 0)
        dist = row % blk + blk - lax.broadcasted_iota(jnp.int32, (rows4, 2 * blk), 1)
        in_window = (dist >= 0) & (dist < WINDOW)
        for kv in range(A_KV):
            slope = jnp.zeros((rows4, 2 * blk), F32)
            for g in range(A_GROUP):
                slope = jnp.where(row // blk == g, _alibi_slope(kv * A_GROUP + g), slope)
            bias_ref[kv] = jnp.where(in_window, slope * dist.astype(F32), -NEG_INF)
        for hb in range(B_HEADS):
            lg = _ret_log_gamma(hb)
            ii = lax.broadcasted_iota(jnp.int32, (blk, blk), 0).astype(F32)
            jj = lax.broadcasted_iota(jnp.int32, (blk, blk), 1).astype(F32)
            diff = ii - jj
            dec_ref[hb, 0] = jnp.where(diff >= 0, jnp.exp(lg * jnp.maximum(diff, 0.0)), 0.0)
            dec_ref[hb, 1] = jnp.exp(lg * (ii + 1.0))
            dec_ref[hb, 2] = jnp.exp(lg * (blk - 1.0 - jj))

    x = x_ref[...]
    sb, _, d = x.shape
    tb = sb * ROWS
    mod = mod_ref[...]
    h = _modulate(x, mod, gn_ref[...], 0).reshape(tb, d).astype(BF16)
    proj = _dot(h, win_ref[...])
    kt_all = _dot_nt(wkt_ref[...], h) * (B_KD ** -0.5)
    rg = rg_ref[...]

    qa = proj[:, 0:QA_W]
    ka = proj[:, P_OFF_KA:P_OFF_KA + 2 * KA_W]
    va = proj[:, P_OFF_VA:P_OFF_VA + 2 * VA_W]
    q_hat = (qa * lax.rsqrt(_dot((qa * qa).astype(BF16), ones_ref[...]) + EPS) * qg_ref[...]).astype(BF16)
    k_hat = ka * lax.rsqrt(_dot((ka * ka).astype(BF16), ones_ref[0:2 * KA_W, 0:2 * KA_W]) + EPS) * kg_ref[...]
    k_hat16 = k_hat.astype(BF16)
    va16 = va.astype(BF16)
    prev = carry_ref[step % 2]
    carry_ref[(step + 1) % 2] = jnp.concatenate([k_hat16[tb - blk:tb], va16[tb - blk:tb]], axis=1)

    row_g = lax.broadcasted_iota(jnp.int32, (rows4, 1), 0) // blk
    key_is_prev = lax.broadcasted_iota(jnp.int32, (rows4, 2 * blk), 1) < blk
    first_penalty = jnp.where(step == 0, -NEG_INF, 0.0)
    lane_low = lax.broadcasted_iota(jnp.int32, (blk, PAIR), 1) < A_HD
    ones_cols = jnp.ones((2 * blk, PAIR), BF16)

    n_blk = tb // blk
    att = [(j, kv) for j in range(n_blk) for kv in range(A_KV)]
    ret = [(j, hb) for j in range(n_blk) for hb in range(B_HEADS)]

    sinks, scores = {}, {}
    for j, kv in att:
        r0 = j * blk
        kcol = slice(kv * PAIR, (kv + 1) * PAIR)
        if j == 0:
            k2 = jnp.concatenate([prev[:, kcol], k_hat16[0:blk, kcol]], axis=0)
        else:
            k2 = k_hat16[r0 - blk:r0 + blk, kcol]
        q4 = jnp.concatenate(
            [jnp.where(lane_low == (g % 2 == 0),
                       q_hat[r0:r0 + blk, (kv * A_GROUP + g - g % 2) * A_HD:(kv * A_GROUP + g - g % 2 + 2) * A_HD],
                       jnp.zeros((), BF16))
             for g in range(A_GROUP)], axis=0)
        scores[j, kv] = _dot_nt(q4, k2)
        sink = jnp.zeros((rows4, 1), F32)
        for g in range(A_GROUP):
            sink = jnp.where(row_g == g, sink_ref[kv * A_GROUP + g], sink)
        sinks[j, kv] = sink
    qb, vb, kt, inner = {}, {}, {}, {}
    for j, hb in ret:
        r0 = j * blk
        qb[j, hb] = proj[r0:r0 + blk, P_OFF_QB + hb * B_KD:P_OFF_QB + (hb + 1) * B_KD].astype(BF16)
        vb[j, hb] = proj[r0:r0 + blk, P_OFF_VB + hb * B_VD:P_OFF_VB + (hb + 1) * B_VD].astype(BF16)
        kt[j, hb] = kt_all[hb * B_KD:(hb + 1) * B_KD, r0:r0 + blk]
        inner[j, hb] = _dot(qb[j, hb], kt[j, hb].astype(BF16))

    probs, maxes = {}, {}
    for j, kv in att:
        s = scores[j, kv] * (A_HD ** -0.5) - bias_ref[kv]
        if j == 0:
            s = s - jnp.where(key_is_prev, first_penalty, 0.0)
        mx = jnp.maximum(jnp.max(s, axis=-1, keepdims=True), sinks[j, kv])
        probs[j, kv] = jnp.exp(s - mx).astype(BF16)
        maxes[j, kv] = mx
    state = {}
    for hb in range(B_HEADS):
        d_c = math.exp(_ret_log_gamma(hb) * blk)
        state[0, hb] = ps_ref[hb]
        for j in range(n_blk):
            state[j + 1, hb] = state[j, hb] * d_c + _dot((kt[j, hb] * dec_ref[hb, 2]).astype(BF16), vb[j, hb])
        ps_ref[hb] = state[n_blk, hb]

    for j, kv in att:
        r0 = j * blk
        kcol = slice(kv * PAIR, (kv + 1) * PAIR)
        vcol = slice(2 * KA_W + kv * PAIR, 2 * KA_W + (kv + 1) * PAIR)
        if j == 0:
            v2 = jnp.concatenate([prev[:, vcol], va16[0:blk, kcol]], axis=0)
        else:
            v2 = va16[r0 - blk:r0 + blk, kcol]
        pv = _dot(probs[j, kv], jnp.concatenate([v2, ones_cols], axis=1))
        o4 = pv[:, 0:PAIR] / (pv[:, PAIR:2 * PAIR] + jnp.exp(sinks[j, kv] - maxes[j, kv]))
        for g in range(A_GROUP):
            hd = kv * A_GROUP + g
            half = slice((hd % 2) * A_HD, (hd % 2 + 1) * A_HD)
            mix_ref[r0:r0 + blk, hd * A_HD:(hd + 1) * A_HD] = o4[g * blk:(g + 1) * blk, half]
    o_ret = {}
    for j, hb in ret:
        o_ret[j, hb] = (_dot((inner[j, hb] * dec_ref[hb, 0]).astype(BF16), vb[j, hb])
                        + _dot(qb[j, hb], state[j, hb].astype(BF16)) * dec_ref[hb, 1])

    cen = {k: o_ret[k] - jnp.mean(o_ret[k], axis=-1, keepdims=True) for k in ret}
    var = {k: jnp.mean(cen[k] * cen[k], axis=-1, keepdims=True) for k in ret}
    for j, hb in ret:
        r0 = j * blk
        gb = proj[r0:r0 + blk, P_OFF_GB + hb * B_VD:P_OFF_GB + (hb + 1) * B_VD]
        mix_ref[r0:r0 + blk, QA_W + hb * B_VD:QA_W + (hb + 1) * B_VD] = (
            cen[j, hb] * lax.rsqrt(var[j, hb] + EPS) * rg[:, hb * B_VD:(hb + 1) * B_VD] * (gb * jax.nn.sigmoid(gb)))

    out = _dot(mix_ref[...].astype(BF16), wout_ref[...])
    o_ref[...] = x + _gate(mod, 0, d) * out.reshape(sb, ROWS, d)

    @pl.when(step == pl.num_programs(0) - 1)
    def _():
        pk_ref[...] = _dedup(k_hat[tb - blk:tb, :])
        pv_ref[...] = _dedup(va[tb - blk:tb, :])


def _dup_heads(t, width):
    lead = t.shape[:-1]
    t = t.reshape(lead + (-1, 1, width))
    return jnp.broadcast_to(t, lead + (t.shape[-3], 2, width)).reshape(lead + (-1,))


def _mix_even_prompt(x3, mod3, gain, w_in, q_gain, k_gain, sinks, ret_gain, w_out, sb, cast=()):
    n8, _, d = x3.shape
    tb = sb * ROWS
    cast_specs = [_cast_specs(w, n8 // sb) for w in cast]
    w_main = jnp.concatenate([w_in[:, :OFF_KA], _dup_heads(w_in[:, OFF_KA:OFF_VA], A_HD),
                              _dup_heads(w_in[:, OFF_VA:OFF_QB], A_HD), w_in[:, OFF_QB:OFF_KB],
                              w_in[:, OFF_VB:]], axis=1)
    wk_t = w_in[:, OFF_KB:OFF_VB].T
    out = pl.pallas_call(
        functools.partial(_mix_even_prompt_body, len(cast)),
        grid=(n8 // sb,),
        in_specs=[pl.BlockSpec((sb, ROWS, d), lambda i: (i, 0, 0)),
                  pl.BlockSpec((1, 1, 6 * d), lambda i: (0, 0, 0)),
                  _const_spec((1, d)), _const_spec((d, P_WIDTH)), _const_spec((KB_W, d)),
                  _const_spec((1, QA_W)), _const_spec((1, 2 * KA_W)),
                  pl.BlockSpec(memory_space=pltpu.SMEM),
                  _const_spec((1, VB_W)), _const_spec((QA_W + VB_W, d))] + [c[0] for c in cast_specs],
        out_specs=[pl.BlockSpec((sb, ROWS, d), lambda i: (i, 0, 0)),
                   pl.BlockSpec((WINDOW, KA_W), lambda i: (0, 0)),
                   pl.BlockSpec((WINDOW, VA_W), lambda i: (0, 0)),
                   pl.BlockSpec((B_HEADS, B_KD, B_VD), lambda i: (0, 0, 0))] + [c[1] for c in cast_specs],
        out_shape=[jax.ShapeDtypeStruct(x3.shape, F32),
                   jax.ShapeDtypeStruct((WINDOW, KA_W), F32),
                   jax.ShapeDtypeStruct((WINDOW, VA_W), F32),
                   jax.ShapeDtypeStruct((B_HEADS, B_KD, B_VD), F32)] + [c[2] for c in cast_specs],
        scratch_shapes=[pltpu.VMEM((tb, QA_W + VB_W), F32),
                        pltpu.VMEM((2, WINDOW, 2 * KA_W + 2 * VA_W), BF16),
                        pltpu.VMEM((A_KV, A_GROUP * WINDOW, 2 * WINDOW), F32),
                        pltpu.VMEM((B_HEADS, 3, WINDOW, WINDOW), F32),
                        pltpu.VMEM((QA_W, QA_W), BF16)],
        compiler_params=_params(),
        name="mix_even_prompt",
    )(x3, mod3, gain.reshape(1, d), w_main, wk_t, jnp.tile(q_gain, A_HEADS).reshape(1, QA_W),
      jnp.tile(k_gain, 2 * A_KV).reshape(1, 2 * KA_W),
      sinks, ret_gain.reshape(1, VB_W), w_out, *[w for w, _ in cast])
    return out[:4], tuple(out[4:])


def _mix_even_sample_body(x_ref, mod_ref, gn_ref, win_ref, qg_ref, kg_ref, sink_ref, rg_ref, wout_ref,
                          ck_ref, cv_ref, s0_ref,
                          o_ref, nk_ref, nv_ref, ns_ref, mix_ref):
    x = x_ref[...]
    sb, length, d = x.shape
    tb = sb * length
    w = ck_ref.shape[1]
    mod = _load_mod(mod_ref)
    h = _modulate(x, mod, gn_ref[...], 0).reshape(tb, d).astype(BF16)
    proj = _dot(h, win_ref[...])
    qg = qg_ref[...]
    kg = kg_ref[...]
    rg = rg_ref[...]

    rows = A_GROUP * length
    qpos_c = lax.broadcasted_iota(jnp.int32, (rows, w), 0) % length
    kpos_c = lax.broadcasted_iota(jnp.int32, (rows, w), 1)
    dist_c = w + qpos_c - kpos_c
    valid_c = (dist_c >= 0) & (dist_c < WINDOW)
    qpos_n = lax.broadcasted_iota(jnp.int32, (rows, length), 0) % length
    kpos_n = lax.broadcasted_iota(jnp.int32, (rows, length), 1)
    dist_n = qpos_n - kpos_n
    valid_n = (dist_n >= 0) & (dist_n < WINDOW)
    row_g = lax.broadcasted_iota(jnp.int32, (rows, 1), 0) // length

    for kv in range(A_KV):
        lanes = slice(kv * A_HD, (kv + 1) * A_HD)
        kn = _head_rms(proj[:, OFF_KA + kv * A_HD:OFF_KA + (kv + 1) * A_HD], kg).reshape(sb, length, A_HD)
        vn = proj[:, OFF_VA + kv * A_HD:OFF_VA + (kv + 1) * A_HD].reshape(sb, length, A_HD)
        nk_ref[:, 0:w - length, lanes] = ck_ref[:, length:w, lanes]
        nv_ref[:, 0:w - length, lanes] = cv_ref[:, length:w, lanes]
        nk_ref[:, w - length:w, lanes] = kn
        nv_ref[:, w - length:w, lanes] = vn
        kc = ck_ref[:, :, lanes].astype(BF16)
        vc = cv_ref[:, :, lanes].astype(BF16)
        q4 = jnp.concatenate(
            [_head_rms(proj[:, OFF_QA + (kv * A_GROUP + g) * A_HD:OFF_QA + (kv * A_GROUP + g + 1) * A_HD], qg)
             .reshape(sb, length, A_HD) for g in range(A_GROUP)], axis=1).astype(BF16)
        slope = jnp.zeros((rows, 1), F32)
        sink = jnp.zeros((rows, 1), F32)
        for g in range(A_GROUP):
            hd = kv * A_GROUP + g
            slope = jnp.where(row_g == g, _alibi_slope(hd), slope)
            sink = jnp.where(row_g == g, sink_ref[hd], sink)
        scale = A_HD ** -0.5
        s_c = _bmm_nt(q4, kc) * scale - slope * dist_c.astype(F32)
        s_c = jnp.where(valid_c, s_c, NEG_INF)
        s_n = _bmm_nt(q4, kn.astype(BF16)) * scale - slope * dist_n.astype(F32)
        s_n = jnp.where(valid_n, s_n, NEG_INF)
        mx = jnp.maximum(jnp.maximum(jnp.max(s_c, axis=-1, keepdims=True),
                                     jnp.max(s_n, axis=-1, keepdims=True)), sink)
        p_c = jnp.exp(s_c - mx)
        p_n = jnp.exp(s_n - mx)
        den = (jnp.sum(p_c, axis=-1, keepdims=True) + jnp.sum(p_n, axis=-1, keepdims=True)
               + jnp.exp(sink - mx))
        o4 = (_bmm(p_c.astype(BF16), vc) + _bmm(p_n.astype(BF16), vn.astype(BF16))) / den
        for g in range(A_GROUP):
            hd = kv * A_GROUP + g
            mix_ref[:, hd * A_HD:(hd + 1) * A_HD] = o4[:, g * length:(g + 1) * length, :].reshape(tb, A_HD)

    for hb in range(B_HEADS):
        d_in, d_q, d_k, d_c = _ret_decay(hb, length)
        qb = proj[:, OFF_QB + hb * B_KD:OFF_QB + (hb + 1) * B_KD].reshape(sb, length, B_KD).astype(BF16)
        kb = proj[:, OFF_KB + hb * B_KD:OFF_KB + (hb + 1) * B_KD].reshape(sb, length, B_KD) * (B_KD ** -0.5)
        vb = proj[:, OFF_VB + hb * B_VD:OFF_VB + (hb + 1) * B_VD].reshape(sb, length, B_VD).astype(BF16)
        gb = proj[:, OFF_GB + hb * B_VD:OFF_GB + (hb + 1) * B_VD]
        state = s0_ref[:, hb]
        inner = _bmm_nt(qb, kb.astype(BF16)) * d_in
        o = _bmm(inner.astype(BF16), vb) + _bmm(qb, state.astype(BF16)) * d_q
        ns_ref[:, hb] = state * d_c + _bmm_tn((kb * d_k).astype(BF16), vb)
        mix_ref[:, QA_W + hb * B_VD:QA_W + (hb + 1) * B_VD] = _group_norm_gate(
            o.reshape(tb, B_VD), rg[:, hb * B_VD:(hb + 1) * B_VD], gb)

    out = _dot(mix_ref[...].astype(BF16), wout_ref[...])
    o_ref[...] = x + _gate(mod, 0, d) * out.reshape(sb, length, d)


def _mix_even_sample(x3, mod3, gain, w_in, q_gain, k_gain, sinks, ret_gain, w_out, cache_k, cache_v, state, sb):
    n, length, d = x3.shape
    in_w = w_in.shape[1]
    w = cache_k.shape[1]
    seq_spec = lambda shape: pl.BlockSpec((sb,) + shape, lambda i, _n=len(shape): (i,) + (0,) * _n)
    return pl.pallas_call(
        _mix_even_sample_body,
        grid=(n // sb,),
        in_specs=[seq_spec((length, d)), seq_spec((6 * d,)),
                  _const_spec((1, d)), _const_spec((d, in_w)),
                  _const_spec((1, A_HD)), _const_spec((1, A_HD)),
                  pl.BlockSpec(memory_space=pltpu.SMEM),
                  _const_spec((1, VB_W)), _const_spec((QA_W + VB_W, d)),
                  seq_spec((w, KA_W)), seq_spec((w, VA_W)),
                  pl.BlockSpec((None, sb, B_HEADS, B_KD, B_VD), lambda i: (0, i, 0, 0, 0))],
        out_specs=[seq_spec((length, d)), seq_spec((w, KA_W)), seq_spec((w, VA_W)),
                   seq_spec((B_HEADS, B_KD, B_VD))],
        out_shape=[jax.ShapeDtypeStruct(x3.shape, F32),
                   jax.ShapeDtypeStruct(cache_k.shape, F32),
                   jax.ShapeDtypeStruct(cache_v.shape, F32),
                   jax.ShapeDtypeStruct(state.shape[1:], F32)],
        scratch_shapes=[pltpu.VMEM((sb * length, QA_W + VB_W), F32)],
        compiler_params=_params(),
        name="mix_even_sample",
    )(x3, mod3, gain.reshape(1, d), w_in, q_gain.reshape(1, A_HD), k_gain.reshape(1, A_HD),
      sinks, ret_gain.reshape(1, VB_W), w_out, cache_k, cache_v, state)


def _cmul(ar, ai, br, bi):
    return ar * br - ai * bi, ar * bi + ai * br


def _s5_lambda(a_re, a_im, log_dt):
    dt = jnp.exp(log_dt)
    mag = jnp.exp(a_re * dt)
    return mag * jnp.cos(a_im * dt), mag * jnp.sin(a_im * dt)


def _s5_prep_body(t_len, n_chunks, are_ref, aim_ref, ldt_ref, btr_ref, bti_ref, ctr_ref, cti_ref,
                  ard_ref, aid_ref, ldd_ref,
                  bblk_ref, cre_ref, cim_ref, lam_ref, pwb_ref, ltp_ref, b32_ref, cr32_ref, ci32_ref):
    groups = btr_ref.shape[0]
    k_in, p = btr_ref.shape[1:]
    half = S5_OCT * p
    a_re = are_ref[...]
    a_im = aim_ref[...]
    lam_re, lam_im = _s5_lambda(a_re, a_im, ldt_ref[...])
    den = a_re * a_re + a_im * a_im
    n_re = lam_re - 1.0
    n_im = lam_im
    f_re = (n_re * a_re + n_im * a_im) / den
    f_im = (n_im * a_re - n_re * a_im) / den
    br = btr_ref[...]
    bi = bti_ref[...]
    bb_re = f_re * br - f_im * bi
    bb_im = f_re * bi + f_im * br
    b32_ref[...] = jnp.zeros_like(b32_ref)
    cr32_ref[...] = jnp.zeros_like(cr32_ref)
    ci32_ref[...] = jnp.zeros_like(ci32_ref)
    for g in range(groups):
        s, gl = divmod(g, S5_OCT)
        rows = slice(gl * k_in, (gl + 1) * k_in)
        cols = slice(gl * p, (gl + 1) * p)
        b32_ref[s, rows, cols] = bb_re[g]
        b32_ref[s, rows, half + gl * p:half + (gl + 1) * p] = bb_im[g]
        cr32_ref[s, cols, rows] = ctr_ref[g]
        ci32_ref[s, cols, rows] = cti_ref[g]
    bblk_ref[...] = b32_ref[...].astype(BF16)
    cre_ref[...] = cr32_ref[...].astype(BF16)
    cim_ref[...] = ci32_ref[...].astype(BF16)
    lam_re, lam_im = _s5_lambda(ard_ref[...], aid_ref[...], ldd_ref[...])
    lam_ref[0] = lam_re
    lam_ref[1] = lam_im
    n_oct = lam_re.shape[0]

    def both(cr, ci, s, rows):
        return jnp.concatenate([jnp.broadcast_to(cr[s:s + 1, :], (rows, half)),
                                jnp.broadcast_to(ci[s:s + 1, :], (rows, half))], axis=1)

    cr, ci = lam_re, lam_im
    for t in range(t_len):
        for s in range(n_oct):
            pwb_ref[s, t] = both(cr, ci, s, ROWS)
        if t + 1 < t_len:
            cr, ci = _cmul(cr, ci, lam_re, lam_im)
    tr, ti = cr, ci
    ltp_ref[...] = jnp.zeros_like(ltp_ref)
    cr, ci = jnp.ones_like(lam_re), jnp.zeros_like(lam_im)
    for m in range(n_chunks + 1):
        for s in range(n_oct):
            if m < n_chunks:
                ltp_ref[s, n_chunks + m:n_chunks + m + 1, :] = both(cr, ci, s, 1)
            if m in (1, 2, 4, 8):
                i = (1, 2, 4, 8).index(m)
                ltp_ref[s, i:i + 1, :] = both(cr, ci, s, 1)
        cr, ci = _cmul(cr, ci, tr, ti)


def _s5_prep(a_re, a_im, log_dt, b_re, b_im, c_re, c_im, t_len, n_chunks):
    g, p = a_re.shape
    k = b_re.shape[-1]
    assert n_chunks == ROWS
    n_oct = g // S5_OCT
    half = S5_OCT * p
    dense = (n_oct, half)
    return pl.pallas_call(
        functools.partial(_s5_prep_body, t_len, n_chunks),
        out_shape=[jax.ShapeDtypeStruct((n_oct, S5_OCT * k, 2 * half), BF16),
                   jax.ShapeDtypeStruct((n_oct, half, S5_OCT * k), BF16),
                   jax.ShapeDtypeStruct((n_oct, half, S5_OCT * k), BF16),
                   jax.ShapeDtypeStruct((2,) + dense, F32),
                   jax.ShapeDtypeStruct((n_oct, t_len, ROWS, 2 * half), F32),
                   jax.ShapeDtypeStruct((n_oct, 2 * n_chunks, 2 * half), F32)],
        scratch_shapes=[pltpu.VMEM((n_oct, S5_OCT * k, 2 * half), F32),
                        pltpu.VMEM((n_oct, half, S5_OCT * k), F32),
                        pltpu.VMEM((n_oct, half, S5_OCT * k), F32)],
        compiler_params=pltpu.CompilerParams(vmem_limit_bytes=VMEM_LIMIT),
        name="s5_prep",
    )(a_re.reshape(g, 1, p), a_im.reshape(g, 1, p), log_dt.reshape(g, 1, 1),
      jnp.swapaxes(b_re, 1, 2), jnp.swapaxes(b_im, 1, 2), jnp.swapaxes(c_re, 1, 2), jnp.swapaxes(c_im, 1, 2),
      a_re.reshape(dense), a_im.reshape(dense), jnp.broadcast_to(log_dt[:, None], (g, p)).reshape(dense))


def _gelu_glu_out(x, mod, y, u, dskip, glua_ref, glub_ref):
    sb, rows, d = x.shape
    y = y + dskip * u
    yg = jax.nn.gelu(y, approximate=True).astype(BF16)
    out = _dot(yg, glua_ref[...]) * jax.nn.sigmoid(_dot(yg, glub_ref[...]))
    return x + _gate(mod, 0, d) * out.reshape(sb, rows, d)


def _mix_odd_prompt_body(x_ref, xn_ref, mod_ref, gn_ref, bblk_ref, lam_ref, pwb_ref, ltp_ref, cre_ref, cim_ref,
                         dskip_ref, glua_ref, glub_ref, o_ref, hre_ref, him_ref,
                         un_ref, l0_ref, l1_ref, up0_ref, up1_ref, pt_ref):
    step = pl.program_id(0)
    sb, _, d = x_ref.shape
    tm = sb * ROWS
    t_len = tm // ROWS
    n_oct = bblk_ref.shape[0]
    half = bblk_ref.shape[2] // 2
    mod = mod_ref[...]

    def permuted_input(src_ref, up_dst):
        u = _modulate(src_ref[...], mod, gn_ref[...], 0).reshape(tm, d)
        for k in range(d // 128):
            un_ref[k] = u[:, k * 128:(k + 1) * 128]
        up = jnp.concatenate(
            [jnp.concatenate([un_ref[k, pl.ds(t, ROWS, stride=t_len), :] for k in range(d // 128)], axis=1)
             for t in range(t_len)], axis=0)
        up_dst[...] = up
        return up.astype(BF16)

    @pl.when(step == 0)
    def _():
        hre_ref[...] = jnp.zeros_like(hre_ref)
        him_ref[...] = jnp.zeros_like(him_ref)
        nat = lax.broadcasted_iota(jnp.int32, (tm, tm), 0)
        prm = lax.broadcasted_iota(jnp.int32, (tm, tm), 1)
        pt_ref[...] = jnp.where(prm == (nat % t_len) * ROWS + nat // t_len, 1.0, 0.0).astype(BF16)
        up16 = permuted_input(x_ref, up0_ref)
        for s in range(n_oct):
            l0_ref[s] = _dot(up16[:, s * 128:(s + 1) * 128], bblk_ref[s])

    def run(l_ref, l_next, up_ref, up_next):
        for s0 in range(0, n_oct, 2):
            pair = (s0, s0 + 1)
            lam_re = [jnp.broadcast_to(lam_ref[0, s:s + 1, :], (ROWS, half)) for s in pair]
            lam_im = [jnp.broadcast_to(lam_ref[1, s:s + 1, :], (ROWS, half)) for s in pair]

            def local_step(t, carry):
                r = pl.multiple_of(t * ROWS, ROWS)
                out = []
                for i, s in enumerate(pair):
                    h_re, h_im = carry[2 * i], carry[2 * i + 1]
                    n_re = lam_re[i] * h_re - lam_im[i] * h_im + l_ref[s, pl.ds(r, ROWS), 0:half]
                    n_im = lam_re[i] * h_im + lam_im[i] * h_re + l_ref[s, pl.ds(r, ROWS), half:2 * half]
                    l_ref[s, pl.ds(r, ROWS), 0:half] = n_re
                    l_ref[s, pl.ds(r, ROWS), half:2 * half] = n_im
                    out += [n_re, n_im]
                return tuple(out)

            zero = jnp.zeros((ROWS, half), F32)
            lax.fori_loop(0, t_len, local_step, (zero, zero, zero, zero), unroll=2)

        next16 = permuted_input(xn_ref, up_next)
        chunk = lax.broadcasted_iota(jnp.int32, (ROWS, half), 0)
        y_parts = []
        for s in range(n_oct):
            l_next[s] = _dot(next16[:, s * 128:(s + 1) * 128], bblk_ref[s])
            p_re = l_ref[s, tm - ROWS:tm, 0:half]
            p_im = l_ref[s, tm - ROWS:tm, half:2 * half]
            for i, sh in enumerate((1, 2, 4)):
                m_re, m_im = _cmul(ltp_ref[s, i:i + 1, 0:half], ltp_ref[s, i:i + 1, half:2 * half],
                                   jnp.where(chunk >= sh, pltpu.roll(p_re, sh, 0), 0.0),
                                   jnp.where(chunk >= sh, pltpu.roll(p_im, sh, 0), 0.0))
                p_re, p_im = p_re + m_re, p_im + m_im
            hin_re = jnp.broadcast_to(hre_ref[s:s + 1, :], (ROWS, half))
            hin_im = jnp.broadcast_to(him_ref[s:s + 1, :], (ROWS, half))
            m_re, m_im = _cmul(ltp_ref[s, ROWS:2 * ROWS, 0:half], ltp_ref[s, ROWS:2 * ROWS, half:2 * half],
                               hin_re, hin_im)
            st_re = m_re + jnp.where(chunk >= 1, pltpu.roll(p_re, 1, 0), 0.0)
            st_im = m_im + jnp.where(chunk >= 1, pltpu.roll(p_im, 1, 0), 0.0)
            m_re, m_im = _cmul(ltp_ref[s, 3:4, 0:half], ltp_ref[s, 3:4, half:2 * half],
                               hre_ref[s:s + 1, :], him_ref[s:s + 1, :])
            hre_ref[s:s + 1, :] = m_re + p_re[ROWS - 1:ROWS, :]
            him_ref[s:s + 1, :] = m_im + p_im[ROWS - 1:ROWS, :]
            loc = l_ref[s].reshape(t_len, ROWS, 2 * half)
            pw = pwb_ref[s]
            f_re, f_im = _cmul(pw[:, :, 0:half], pw[:, :, half:2 * half], st_re[None], st_im[None])
            hs_re = (loc[:, :, 0:half] + f_re).reshape(tm, half).astype(BF16)
            hs_im = (loc[:, :, half:2 * half] + f_im).reshape(tm, half).astype(BF16)
            y_parts.append(_dot(hs_re, cre_ref[s]) - _dot(hs_im, cim_ref[s]))

        y = jnp.concatenate(y_parts, axis=1) + dskip_ref[...] * up_ref[...]
        yg = jax.nn.gelu(y, approximate=True).astype(BF16)
        yn = _dot(pt_ref[...], yg).astype(BF16)
        out = _dot(yn, glua_ref[...]) * jax.nn.sigmoid(_dot(yn, glub_ref[...]))
        o_ref[...] = x_ref[...] + _gate(mod, 0, d) * out.reshape(sb, ROWS, d)

    @pl.when(step % 2 == 0)
    def _():
        run(l0_ref, l1_ref, up0_ref, up1_ref)

    @pl.when(step % 2 == 1)
    def _():
        run(l1_ref, l0_ref, up1_ref, up0_ref)


def _mix_odd_prompt(x3, mod3, gain, bblk, lam_d, pwb, ltp, cre, cim, dskip, glu_a, glu_b, sb):
    n8, _, d = x3.shape
    n_oct, kin, wid = bblk.shape
    half = wid // 2
    tm = sb * ROWS
    n_tiles = n8 // sb
    proj_buf = pltpu.VMEM((n_oct, tm, wid), F32)
    perm_buf = pltpu.VMEM((tm, d), F32)
    return pl.pallas_call(
        _mix_odd_prompt_body,
        grid=(n_tiles,),
        in_specs=[pl.BlockSpec((sb, ROWS, d), lambda i: (i, 0, 0)),
                  pl.BlockSpec((sb, ROWS, d), lambda i: (jnp.minimum(i + 1, n_tiles - 1), 0, 0)),
                  pl.BlockSpec((1, 1, 6 * d), lambda i: (0, 0, 0)),
                  _const_spec((1, d)), _const_spec(bblk.shape), _const_spec(lam_d.shape),
                  _const_spec(pwb.shape), _const_spec(ltp.shape),
                  _const_spec(cre.shape), _const_spec(cim.shape), _const_spec((1, d)),
                  _const_spec((d, d)), _const_spec((d, d))],
        out_specs=[pl.BlockSpec((sb, ROWS, d), lambda i: (i, 0, 0)),
                   pl.BlockSpec((n_oct, half), lambda i: (0, 0)),
                   pl.BlockSpec((n_oct, half), lambda i: (0, 0))],
        out_shape=[jax.ShapeDtypeStruct(x3.shape, F32),
                   jax.ShapeDtypeStruct((n_oct, half), F32),
                   jax.ShapeDtypeStruct((n_oct, half), F32)],
        scratch_shapes=[pltpu.VMEM((d // 128, tm, 128), F32), proj_buf, proj_buf, perm_buf, perm_buf,
                        pltpu.VMEM((tm, tm), BF16)],
        compiler_params=_params(),
        name="mix_odd_prompt",
    )(x3, x3, mod3, gain.reshape(1, d), bblk, lam_d, pwb, ltp, cre, cim, dskip.reshape(1, d), glu_a, glu_b)


def _mix_odd_sample_body(x_ref, mod_ref, gn_ref, bblk_ref, lam_ref, cre_ref, cim_ref, dskip_ref,
                         glua_ref, glub_ref, sre_ref, sim_ref, o_ref, nre_ref, nim_ref, d_ref, y_ref):
    x = x_ref[...]
    sb, length, d = x.shape
    tm = sb * length
    n_oct = bblk_ref.shape[0]
    half = bblk_ref.shape[2] // 2
    mod = _load_mod(mod_ref)
    u = _modulate(x, mod, gn_ref[...], 0).reshape(tm, d)
    u16 = u.astype(BF16)
    n_ch = bblk_ref.shape[2] // 128
    hc = n_ch // 2
    for s in range(n_oct):
        bu = _dot(u16[:, s * 128:(s + 1) * 128], bblk_ref[s])
        for c in range(n_ch):
            d_ref[c] = bu[:, c * 128:(c + 1) * 128]
        lam_re = lam_ref[0, s:s + 1, :]
        lam_im = lam_ref[1, s:s + 1, :]
        h_re = sre_ref[:, s * half:(s + 1) * half]
        h_im = sim_ref[:, s * half:(s + 1) * half]
        for t in range(length):
            b_re = jnp.concatenate([d_ref[c, pl.ds(t, sb, stride=length), :] for c in range(hc)], axis=1)
            b_im = jnp.concatenate([d_ref[hc + c, pl.ds(t, sb, stride=length), :] for c in range(hc)], axis=1)
            n_re = lam_re * h_re - lam_im * h_im + b_re
            n_im = lam_re * h_im + lam_im * h_re + b_im
            for c in range(hc):
                d_ref[c, pl.ds(t, sb, stride=length), :] = n_re[:, c * 128:(c + 1) * 128]
                d_ref[hc + c, pl.ds(t, sb, stride=length), :] = n_im[:, c * 128:(c + 1) * 128]
            h_re, h_im = n_re, n_im
        nre_ref[:, s * half:(s + 1) * half] = h_re
        nim_ref[:, s * half:(s + 1) * half] = h_im
        hs_re = jnp.concatenate([d_ref[c] for c in range(hc)], axis=1)
        hs_im = jnp.concatenate([d_ref[hc + c] for c in range(hc)], axis=1)
        y_ref[:, s * 128:(s + 1) * 128] = (_dot(hs_re.astype(BF16), cre_ref[s])
                                           - _dot(hs_im.astype(BF16), cim_ref[s]))
    o_ref[...] = _gelu_glu_out(x, mod, y_ref[...], u, dskip_ref[...], glua_ref, glub_ref)


def _mix_odd_sample(x3, mod3, gain, bblk, lam_d, cre, cim, dskip, glu_a, glu_b, s_re, s_im, sb):
    n, length, d = x3.shape
    n_oct, kin, wid = bblk.shape
    tm = sb * length
    nstate = s_re.shape[1]
    return pl.pallas_call(
        _mix_odd_sample_body,
        grid=(n // sb,),
        in_specs=[pl.BlockSpec((sb, length, d), lambda i: (i, 0, 0)),
                  pl.BlockSpec((sb, 6 * d), lambda i: (i, 0)),
                  _const_spec((1, d)), _const_spec(bblk.shape), _const_spec(lam_d.shape),
                  _const_spec(cre.shape), _const_spec(cim.shape), _const_spec((1, d)),
                  _const_spec((d, d)), _const_spec((d, d)),
                  pl.BlockSpec((sb, nstate), lambda i: (i, 0)),
                  pl.BlockSpec((sb, nstate), lambda i: (i, 0))],
        out_specs=[pl.BlockSpec((sb, length, d), lambda i: (i, 0, 0)),
                   pl.BlockSpec((sb, nstate), lambda i: (i, 0)),
                   pl.BlockSpec((sb, nstate), lambda i: (i, 0))],
        out_shape=[jax.ShapeDtypeStruct(x3.shape, F32),
                   jax.ShapeDtypeStruct(s_re.shape, F32),
                   jax.ShapeDtypeStruct(s_im.shape, F32)],
        scratch_shapes=[pltpu.VMEM((wid // 128, tm, 128), F32), pltpu.VMEM((tm, d), F32)],
        compiler_params=_params(),
        name="mix_odd_sample",
    )(x3, mod3, gain.reshape(1, d), bblk, lam_d, cre, cim, dskip.reshape(1, d), glu_a, glu_b, s_re, s_im)


def _pick(n, want):
    while n % want:
        want //= 2
    return max(want, 1)


def kernel(x_prompt, x_sample, cache_win_k, cache_win_v, state_ret, state_s5_re, state_s5_im, c_prompt, c_sample, ada_w, ada_b, norm_mix, norm_ffn, ffn_wg, ffn_wu, ffn_wd, even_w_in, even_q_gain, even_k_gain, even_sinks, even_ret_gain, even_w_out, odd_A_re, odd_A_im, odd_log_dt, odd_B_re, odd_B_im, odd_C_re, odd_C_im, odd_D, odd_glu_a, odd_glu_b):
    bp, lp, d = x_prompt.shape
    ns, ls, _ = x_sample.shape
    assert bp == 1 and ls == ROWS and lp % WINDOW == 0
    w = cache_win_k.shape[2]
    groups, p_state = odd_A_re.shape[1:]

    n_c = bp + ns
    n_pad = -n_c % ROWS
    c_all = jnp.concatenate([c_prompt, c_sample, jnp.zeros((n_pad, d), F32)], axis=0)
    mod = _adaln(c_all, ada_w, ada_b)
    mod_p = [mod[l, 0:1].reshape(1, 1, 6 * d) for l in range(2)]
    mod_s = [mod[l, bp:bp + ns] for l in range(2)]

    bf = lambda t: t.astype(BF16)
    w_in, w_out = bf(even_w_in[0]), bf(even_w_out[0])

    xp = x_prompt.reshape(lp // ROWS, ROWS, d)
    xs = x_sample

    sb_p = _pick(lp // ROWS, 32)
    sb_ffn = _pick(lp // ROWS, 64)
    sb_s = _pick(ns, 32)
    sb_s_even = _pick(ns, 16)

    (xp, p_k, p_v, p_ret), ffn0 = _mix_even_prompt(xp, mod_p[0], norm_mix[0], w_in, even_q_gain[0], even_k_gain[0],
                                                   even_sinks[0], even_ret_gain[0], w_out, sb_ffn,
                                                   cast=((ffn_wg, 0), (ffn_wu, 0), (ffn_wd, 0)))
    xs, s_k, s_v, s_ret = _mix_even_sample(xs, mod_s[0], norm_mix[0], w_in, even_q_gain[0], even_k_gain[0],
                                           even_sinks[0], even_ret_gain[0], w_out,
                                           cache_win_k[0].reshape(ns, w, KA_W), cache_win_v[0].reshape(ns, w, VA_W),
                                           state_ret, sb_s_even)
    xp, xs, (wg1, wu1, wd1, glu_a, glu_b) = _ffn(
        xp, mod_p[0], xs, mod_s[0], norm_ffn[0], *ffn0, sb_ffn,
        cast=((ffn_wg, 1), (ffn_wu, 1), (ffn_wd, 1), (odd_glu_a, 0), (odd_glu_b, 0)))

    t_len = sb_p
    bblk, cre, cim, lam_d, pwb, ltp = _s5_prep(odd_A_re[0], odd_A_im[0], odd_log_dt[0], odd_B_re[0], odd_B_im[0],
                                               odd_C_re[0], odd_C_im[0], t_len, ROWS)
    xp, p_re, p_im = _mix_odd_prompt(xp, mod_p[1], norm_mix[1], bblk, lam_d, pwb, ltp, cre, cim, odd_D[0],
                                     glu_a, glu_b, sb_p)
    xs, s_re, s_im = _mix_odd_sample(xs, mod_s[1], norm_mix[1], bblk, lam_d, cre, cim, odd_D[0], glu_a, glu_b,
                                     state_s5_re[0].reshape(ns, groups * p_state),
                                     state_s5_im[0].reshape(ns, groups * p_state), sb_s)
    xp, xs, _ = _ffn(xp, mod_p[1], xs, mod_s[1], norm_ffn[1], wg1, wu1, wd1, sb_ffn)

    y_prompt = xp.reshape(bp, lp, d)
    y_sample = xs
    return (y_prompt, y_sample,
            p_k.reshape(1, bp, WINDOW, A_KV, A_HD), p_v.reshape(1, bp, WINDOW, A_KV, A_HD),
            p_ret.reshape(1, bp, B_HEADS, B_KD, B_VD),
            p_re.reshape(1, bp, groups, p_state), p_im.reshape(1, bp, groups, p_state),
            s_k.reshape(1, ns, w, A_KV, A_HD), s_v.reshape(1, ns, w, A_KV, A_HD),
            s_ret.reshape(1, ns, B_HEADS, B_KD, B_VD),
            s_re.reshape(1, ns, groups, p_state), s_im.reshape(1, ns, groups, p_state))
```

```python
import functools
import math

import jax
import jax.numpy as jnp
from jax import lax
from jax.experimental import pallas as pl
from jax.experimental.pallas import tpu as pltpu

F32 = jnp.float32
BF16 = jnp.bfloat16

EPS = 1e-6
NEG_INF = -1e30
ROWS = 8

A_HEADS, A_KV, A_GROUP, A_HD = 8, 2, 4, 64
WINDOW = 128
B_HEADS, B_KD, B_VD = 4, 128, 128
S5_GROUP, S5_STATE = 16, 64
S5_OCT = 8

QA_W, KA_W, VA_W = A_HEADS * A_HD, A_KV * A_HD, A_KV * A_HD
QB_W, KB_W, VB_W, GB_W = B_HEADS * B_KD, B_HEADS * B_KD, B_HEADS * B_VD, B_HEADS * B_VD
OFF_QA = 0
OFF_KA = OFF_QA + QA_W
OFF_VA = OFF_KA + KA_W
OFF_QB = OFF_VA + VA_W
OFF_KB = OFF_QB + QB_W
OFF_VB = OFF_KB + KB_W
OFF_GB = OFF_VB + VB_W

VMEM_LIMIT = 56 * 1024 * 1024


def _ret_log_gamma(h):
    return math.log1p(-(2.0 ** (-5.0 - h)))


def _alibi_slope(h):
    return 2.0 ** (-8.0 * (h + 1) / A_HEADS)


def _const_spec(shape):
    nd = len(shape)
    return pl.BlockSpec(shape, lambda i, _n=nd: (0,) * _n, pipeline_mode=pl.Buffered(1))


def _params():
    return pltpu.CompilerParams(dimension_semantics=("arbitrary",), vmem_limit_bytes=VMEM_LIMIT)


def _dot(a, b):
    return jnp.dot(a, b, preferred_element_type=F32)


def _dot_nt(a, b):
    return lax.dot_general(a, b, (((1,), (1,)), ((), ())), preferred_element_type=F32)


def _dot_tn(a, b):
    return lax.dot_general(a, b, (((0,), (0,)), ((), ())), preferred_element_type=F32)


def _bmm(a, b):
    return lax.dot_general(a, b, (((2,), (1,)), ((0,), (0,))), preferred_element_type=F32)


def _bmm_nt(a, b):
    return lax.dot_general(a, b, (((2,), (2,)), ((0,), (0,))), preferred_element_type=F32)


def _bmm_tn(a, b):
    return lax.dot_general(a, b, (((1,), (1,)), ((0,), (0,))), preferred_element_type=F32)


def _rms(x, g):
    return x * lax.rsqrt(jnp.mean(x * x, axis=-1, keepdims=True) + EPS) * g


def _modulate(x3, mod, gain, which):
    d = x3.shape[-1]
    sh = mod[:, :, (3 * which) * d:(3 * which + 1) * d]
    sc = mod[:, :, (3 * which + 1) * d:(3 * which + 2) * d]
    return _rms(x3, gain) * (1.0 + sc) + sh


def _mod_row_spec(mod, layer, row):
    return pl.BlockSpec((None, 1, 1, mod.shape[-1]), lambda i: (layer, row, 0, 0))


def _mod_seq_spec(mod, layer, sb, first_step=0):
    return pl.BlockSpec((None, sb, 1, mod.shape[-1]), lambda i: (layer, jnp.maximum(i - first_step, 0), 0, 0))


def _gate(mod, which, d):
    return mod[:, :, (3 * which + 2) * d:(3 * which + 3) * d]


def _adaln_body(c_ref, w_ref, b_ref, o_ref):
    c = c_ref[...]
    a = (c * jax.nn.sigmoid(c)).astype(BF16)
    o_ref[0, :, 0, :] = _dot(a, w_ref[0].astype(BF16)) + b_ref[0]


def _adaln(c_all, ada_w, ada_b):
    depth, d, n = ada_w.shape
    r = c_all.shape[0]
    tn = 1536
    return pl.pallas_call(
        _adaln_body,
        grid=(depth, n // tn),
        in_specs=[pl.BlockSpec((r, d), lambda l, j: (0, 0)),
                  pl.BlockSpec((1, d, tn), lambda l, j: (l, 0, j)),
                  pl.BlockSpec((1, 1, tn), lambda l, j: (l, 0, j))],
        out_specs=pl.BlockSpec((1, r, 1, tn), lambda l, j: (l, 0, 0, j)),
        out_shape=jax.ShapeDtypeStruct((depth, r, 1, n), F32),
        compiler_params=pltpu.CompilerParams(dimension_semantics=("arbitrary", "arbitrary"),
                                             vmem_limit_bytes=VMEM_LIMIT),
        name="adaln",
    )(c_all, ada_w, ada_b.reshape(depth, 1, n))


BF16_ROWS = 16


def _cast_specs(job, n_steps):
    w, layer = job
    _, rows, cols = w.shape
    hold = 1
    while rows % (n_steps // hold) or (rows // (n_steps // hold)) % BF16_ROWS:
        hold *= 2
        assert hold <= n_steps and n_steps % hold == 0
    blk = rows // (n_steps // hold)
    src = pl.BlockSpec((None, blk, cols),
                       lambda i, _h=hold, _l=layer: (_l, jnp.minimum(i, n_steps - 1) // _h, 0))
    dst = pl.BlockSpec((blk, cols), lambda i, _h=hold: (jnp.minimum(i, n_steps - 1) // _h, 0))
    return src, dst, jax.ShapeDtypeStruct((rows, cols), BF16)


def _cast_blocks(in_refs, out_refs):
    for src, dst in zip(in_refs, out_refs):
        dst[...] = src[...].astype(BF16)


def _ffn_body(n_cast, n_prompt, xp_ref, mp_ref, xs_ref, ms_ref, gn_ref, wg_ref, wu_ref, wd_ref, *refs):
    cast_in, (op_ref, os_ref), cast_out = refs[:n_cast], refs[n_cast:n_cast + 2], refs[n_cast + 2:]
    step = pl.program_id(0)

    def tile(x_ref, mod_ref, o_ref):
        x = x_ref[...]
        sb, _, d = x.shape
        mod = mod_ref[...]
        h = _modulate(x, mod, gn_ref[...], 1).reshape(sb * ROWS, d).astype(BF16)
        a = _dot(h, wg_ref[...])
        b = _dot(h, wu_ref[...])
        act = (a * jax.nn.sigmoid(a) * b).astype(BF16)
        y = _dot(act, wd_ref[...])
        o_ref[...] = x + _gate(mod, 1, d) * y.reshape(sb, ROWS, d)

    @pl.when(step < n_prompt)
    def _():
        tile(xp_ref, mp_ref, op_ref)

    @pl.when(step >= n_prompt)
    def _():
        tile(xs_ref, ms_ref, os_ref)

    _cast_blocks(cast_in, cast_out)


def _ffn(xp3, xs3, mod, layer, gain, wg, wu, wd, sb, cast=()):
    n8, _, d = xp3.shape
    f = wg.shape[1]
    n_prompt = n8 // sb
    n_sample = xs3.shape[0] // sb
    prompt_row = xs3.shape[0]
    assert n8 % sb == 0 and xs3.shape[0] % sb == 0
    cast_specs = [_cast_specs(w, n_prompt) for w in cast]
    p_idx = lambda i: (jnp.minimum(i, n_prompt - 1), 0, 0)
    s_idx = lambda i: (jnp.maximum(i - n_prompt, 0), 0, 0)
    xp_spec = pl.BlockSpec((sb, ROWS, d), p_idx)
    xs_spec = pl.BlockSpec((sb, ROWS, d), s_idx)
    out = pl.pallas_call(
        functools.partial(_ffn_body, len(cast), n_prompt),
        grid=(n_prompt + n_sample,),
        in_specs=[xp_spec, _mod_row_spec(mod, layer, prompt_row),
                  xs_spec, _mod_seq_spec(mod, layer, sb, n_prompt),
                  _const_spec((1, d)), _const_spec((d, f)), _const_spec((d, f)),
                  _const_spec((f, d))] + [c[0] for c in cast_specs],
        out_specs=[xp_spec, xs_spec] + [c[1] for c in cast_specs],
        out_shape=[jax.ShapeDtypeStruct(xp3.shape, F32), jax.ShapeDtypeStruct(xs3.shape, F32)]
        + [c[2] for c in cast_specs],
        compiler_params=_params(),
        name="ffn",
    )(xp3, mod, xs3, mod, gain.reshape(1, d), wg, wu, wd, *[w for w, _ in cast])
    return out[0], out[1], tuple(out[2:])


def _head_rms(t, g):
    return t * lax.rsqrt(jnp.mean(t * t, axis=-1, keepdims=True) + EPS) * g


def _group_norm_gate(o, gain, gate):
    mu = jnp.mean(o, axis=-1, keepdims=True)
    var = jnp.mean(jnp.square(o - mu), axis=-1, keepdims=True)
    return (o - mu) * lax.rsqrt(var + EPS) * gain * (gate * jax.nn.sigmoid(gate))


def _ret_decay(hb, c):
    lg = _ret_log_gamma(hb)
    ii = lax.broadcasted_iota(jnp.int32, (c, c), 0)
    jj = lax.broadcasted_iota(jnp.int32, (c, c), 1)
    diff = (ii - jj).astype(F32)
    d_in = jnp.where(diff >= 0, jnp.exp(lg * jnp.maximum(diff, 0.0)), 0.0)
    row = lax.broadcasted_iota(jnp.int32, (c, B_KD), 0).astype(F32)
    d_q = jnp.exp(lg * (row + 1.0))
    d_k = jnp.exp(lg * (c - 1.0 - row))
    d_c = math.exp(lg * c)
    return d_in, d_q, d_k, d_c


P_OFF_KA = QA_W
P_OFF_VA = P_OFF_KA + 2 * KA_W
P_OFF_QB = P_OFF_VA + 2 * VA_W
P_OFF_VB = P_OFF_QB + QB_W
P_OFF_GB = P_OFF_VB + VB_W
P_WIDTH = P_OFF_GB + GB_W
PAIR = 2 * A_HD


def _dedup(t):
    low = lax.broadcasted_iota(jnp.int32, (t.shape[0], PAIR), 1) < A_HD
    return jnp.where(low, t[:, 0:PAIR], t[:, PAIR:2 * PAIR])


def _mix_even_prompt_body(n_cast, x_ref, mod_ref, gn_ref, win_ref, wkt_ref, qg_ref, kg_ref, sink_ref, rg_ref, wout_ref,
                          *refs):
    cast_in, refs = refs[:n_cast], refs[n_cast:]
    o_ref, pk_ref, pv_ref, ps_ref = refs[:4]
    cast_out = refs[4:4 + n_cast]
    mix_ref, carry_ref, bias_ref, dec_ref, ones_ref = refs[4 + n_cast:]
    _cast_blocks(cast_in, cast_out)
    step = pl.program_id(0)
    blk = WINDOW
    rows4 = A_GROUP * blk

    @pl.when(step == 0)
    def _():
        carry_ref[...] = jnp.zeros_like(carry_ref)
        ps_ref[...] = jnp.zeros_like(ps_ref)
        er = lax.broadcasted_iota(jnp.int32, ones_ref.shape, 0) // A_HD
        ec = lax.broadcasted_iota(jnp.int32, ones_ref.shape, 1) // A_HD
        ones_ref[...] = jnp.where(er == ec, 1.0 / A_HD, 0.0).astype(BF16)
        row = lax.broadcasted_iota(jnp.int32, (rows4, 2 * blk), 0)
        dist = row % blk + blk - lax.broadcasted_iota(jnp.int32, (rows4, 2 * blk), 1)
        in_window = (dist >= 0) & (dist < WINDOW)
        for kv in range(A_KV):
            slope = jnp.zeros((rows4, 2 * blk), F32)
            for g in range(A_GROUP):
                slope = jnp.where(row // blk == g, _alibi_slope(kv * A_GROUP + g), slope)
            bias_ref[kv] = jnp.where(in_window, slope * dist.astype(F32), -NEG_INF)
        for hb in range(B_HEADS):
            lg = _ret_log_gamma(hb)
            ii = lax.broadcasted_iota(jnp.int32, (blk, blk), 0).astype(F32)
            jj = lax.broadcasted_iota(jnp.int32, (blk, blk), 1).astype(F32)
            diff = ii - jj
            dec_ref[hb, 0] = jnp.where(diff >= 0, jnp.exp(lg * jnp.maximum(diff, 0.0)), 0.0)
            dec_ref[hb, 1] = jnp.exp(lg * (ii + 1.0))
            dec_ref[hb, 2] = jnp.exp(lg * (blk - 1.0 - jj))

    x = x_ref[...]
    sb, _, d = x.shape
    tb = sb * ROWS
    mod = mod_ref[...]
    h = _modulate(x, mod, gn_ref[...], 0).reshape(tb, d).astype(BF16)
    proj = _dot(h, win_ref[...])
    kt_all = _dot_nt(wkt_ref[...], h) * (B_KD ** -0.5)
    rg = rg_ref[...]

    qa = proj[:, 0:QA_W]
    ka = proj[:, P_OFF_KA:P_OFF_KA + 2 * KA_W]
    va = proj[:, P_OFF_VA:P_OFF_VA + 2 * VA_W]
    q_hat = (qa * lax.rsqrt(_dot((qa * qa).astype(BF16), ones_ref[...]) + EPS) * qg_ref[...]).astype(BF16)
    k_hat = ka * lax.rsqrt(_dot((ka * ka).astype(BF16), ones_ref[0:2 * KA_W, 0:2 * KA_W]) + EPS) * kg_ref[...]
    k_hat16 = k_hat.astype(BF16)
    va16 = va.astype(BF16)
    prev = carry_ref[step % 2]
    carry_ref[(step + 1) % 2] = jnp.concatenate([k_hat16[tb - blk:tb], va16[tb - blk:tb]], axis=1)

    row_g = lax.broadcasted_iota(jnp.int32, (rows4, 1), 0) // blk
    key_is_prev = lax.broadcasted_iota(jnp.int32, (rows4, 2 * blk), 1) < blk
    first_penalty = jnp.where(step == 0, -NEG_INF, 0.0)
    lane_low = lax.broadcasted_iota(jnp.int32, (blk, PAIR), 1) < A_HD
    ones_cols = jnp.ones((2 * blk, PAIR), BF16)

    n_blk = tb // blk
    att = [(j, kv) for j in range(n_blk) for kv in range(A_KV)]
    ret = [(j, hb) for j in range(n_blk) for hb in range(B_HEADS)]

    sinks, scores = {}, {}
    for j, kv in att:
        r0 = j * blk
        kcol = slice(kv * PAIR, (kv + 1) * PAIR)
        if j == 0:
            k2 = jnp.concatenate([prev[:, kcol], k_hat16[0:blk, kcol]], axis=0)
        else:
            k2 = k_hat16[r0 - blk:r0 + blk, kcol]
        q4 = jnp.concatenate(
            [jnp.where(lane_low == (g % 2 == 0),
                       q_hat[r0:r0 + blk, (kv * A_GROUP + g - g % 2) * A_HD:(kv * A_GROUP + g - g % 2 + 2) * A_HD],
                       jnp.zeros((), BF16))
             for g in range(A_GROUP)], axis=0)
        scores[j, kv] = _dot_nt(q4, k2)
        sink = jnp.zeros((rows4, 1), F32)
        for g in range(A_GROUP):
            sink = jnp.where(row_g == g, sink_ref[kv * A_GROUP + g], sink)
        sinks[j, kv] = sink
    qb, vb, kt, inner = {}, {}, {}, {}
    for j, hb in ret:
        r0 = j * blk
        qb[j, hb] = proj[r0:r0 + blk, P_OFF_QB + hb * B_KD:P_OFF_QB + (hb + 1) * B_KD].astype(BF16)
        vb[j, hb] = proj[r0:r0 + blk, P_OFF_VB + hb * B_VD:P_OFF_VB + (hb + 1) * B_VD].astype(BF16)
        kt[j, hb] = kt_all[hb * B_KD:(hb + 1) * B_KD, r0:r0 + blk]
        inner[j, hb] = _dot(qb[j, hb], kt[j, hb].astype(BF16))

    probs, maxes = {}, {}
    for j, kv in att:
        s = scores[j, kv] * (A_HD ** -0.5) - bias_ref[kv]
        if j == 0:
            s = s - jnp.where(key_is_prev, first_penalty, 0.0)
        mx = jnp.maximum(jnp.max(s, axis=-1, keepdims=True), sinks[j, kv])
        probs[j, kv] = jnp.exp(s - mx).astype(BF16)
        maxes[j, kv] = mx
    state = {}
    for hb in range(B_HEADS):
        d_c = math.exp(_ret_log_gamma(hb) * blk)
        state[0, hb] = ps_ref[hb]
        for j in range(n_blk):
            state[j + 1, hb] = state[j, hb] * d_c + _dot((kt[j, hb] * dec_ref[hb, 2]).astype(BF16), vb[j, hb])
        ps_ref[hb] = state[n_blk, hb]

    for j, kv in att:
        r0 = j * blk
        kcol = slice(kv * PAIR, (kv + 1) * PAIR)
        vcol = slice(2 * KA_W + kv * PAIR, 2 * KA_W + (kv + 1) * PAIR)
        if j == 0:
            v2 = jnp.concatenate([prev[:, vcol], va16[0:blk, kcol]], axis=0)
        else:
            v2 = va16[r0 - blk:r0 + blk, kcol]
        pv = _dot(probs[j, kv], jnp.concatenate([v2, ones_cols], axis=1))
        o4 = pv[:, 0:PAIR] / (pv[:, PAIR:2 * PAIR] + jnp.exp(sinks[j, kv] - maxes[j, kv]))
        for g in range(A_GROUP):
            hd = kv * A_GROUP + g
            half = slice((hd % 2) * A_HD, (hd % 2 + 1) * A_HD)
            mix_ref[r0:r0 + blk, hd * A_HD:(hd + 1) * A_HD] = o4[g * blk:(g + 1) * blk, half]
    o_ret = {}
    for j, hb in ret:
        o_ret[j, hb] = (_dot((inner[j, hb] * dec_ref[hb, 0]).astype(BF16), vb[j, hb])
                        + _dot(qb[j, hb], state[j, hb].astype(BF16)) * dec_ref[hb, 1])

    cen = {k: o_ret[k] - jnp.mean(o_ret[k], axis=-1, keepdims=True) for k in ret}
    var = {k: jnp.mean(cen[k] * cen[k], axis=-1, keepdims=True) for k in ret}
    for j, hb in ret:
        r0 = j * blk
        gb = proj[r0:r0 + blk, P_OFF_GB + hb * B_VD:P_OFF_GB + (hb + 1) * B_VD]
        mix_ref[r0:r0 + blk, QA_W + hb * B_VD:QA_W + (hb + 1) * B_VD] = (
            cen[j, hb] * lax.rsqrt(var[j, hb] + EPS) * rg[:, hb * B_VD:(hb + 1) * B_VD] * (gb * jax.nn.sigmoid(gb)))

    out = _dot(mix_ref[...].astype(BF16), wout_ref[...])
    o_ref[...] = x + _gate(mod, 0, d) * out.reshape(sb, ROWS, d)

    @pl.when(step == pl.num_programs(0) - 1)
    def _():
        pk_ref[...] = _dedup(k_hat[tb - blk:tb, :])
        pv_ref[...] = _dedup(va[tb - blk:tb, :])


def _dup_heads(t, width):
    lead = t.shape[:-1]
    t = t.reshape(lead + (-1, 1, width))
    return jnp.broadcast_to(t, lead + (t.shape[-3], 2, width)).reshape(lead + (-1,))


def _mix_even_prompt(x3, mod, layer, prompt_row, gain, w_in, q_gain, k_gain, sinks, ret_gain, w_out, sb, cast=()):
    n8, _, d = x3.shape
    tb = sb * ROWS
    cast_specs = [_cast_specs(w, n8 // sb) for w in cast]
    w_main = jnp.concatenate([w_in[:, :OFF_KA], _dup_heads(w_in[:, OFF_KA:OFF_VA], A_HD),
                              _dup_heads(w_in[:, OFF_VA:OFF_QB], A_HD), w_in[:, OFF_QB:OFF_KB],
                              w_in[:, OFF_VB:]], axis=1)
    wk_t = w_in[:, OFF_KB:OFF_VB].T
    out = pl.pallas_call(
        functools.partial(_mix_even_prompt_body, len(cast)),
        grid=(n8 // sb,),
        in_specs=[pl.BlockSpec((sb, ROWS, d), lambda i: (i, 0, 0)),
                  _mod_row_spec(mod, layer, prompt_row),
                  _const_spec((1, d)), _const_spec((d, P_WIDTH)), _const_spec((KB_W, d)),
                  _const_spec((1, QA_W)), _const_spec((1, 2 * KA_W)),
                  pl.BlockSpec(memory_space=pltpu.SMEM),
                  _const_spec((1, VB_W)), _const_spec((QA_W + VB_W, d))] + [c[0] for c in cast_specs],
        out_specs=[pl.BlockSpec((sb, ROWS, d), lambda i: (i, 0, 0)),
                   pl.BlockSpec((WINDOW, KA_W), lambda i: (0, 0)),
                   pl.BlockSpec((WINDOW, VA_W), lambda i: (0, 0)),
                   pl.BlockSpec((B_HEADS, B_KD, B_VD), lambda i: (0, 0, 0))] + [c[1] for c in cast_specs],
        out_shape=[jax.ShapeDtypeStruct(x3.shape, F32),
                   jax.ShapeDtypeStruct((WINDOW, KA_W), F32),
                   jax.ShapeDtypeStruct((WINDOW, VA_W), F32),
                   jax.ShapeDtypeStruct((B_HEADS, B_KD, B_VD), F32)] + [c[2] for c in cast_specs],
        scratch_shapes=[pltpu.VMEM((tb, QA_W + VB_W), F32),
                        pltpu.VMEM((2, WINDOW, 2 * KA_W + 2 * VA_W), BF16),
                        pltpu.VMEM((A_KV, A_GROUP * WINDOW, 2 * WINDOW), F32),
                        pltpu.VMEM((B_HEADS, 3, WINDOW, WINDOW), F32),
                        pltpu.VMEM((QA_W, QA_W), BF16)],
        compiler_params=_params(),
        name="mix_even_prompt",
    )(x3, mod, gain.reshape(1, d), w_main, wk_t, jnp.tile(q_gain, A_HEADS).reshape(1, QA_W),
      jnp.tile(k_gain, 2 * A_KV).reshape(1, 2 * KA_W),
      sinks, ret_gain.reshape(1, VB_W), w_out, *[w for w, _ in cast])
    return out[:4], tuple(out[4:])


def _mix_even_sample_body(x_ref, mod_ref, gn_ref, win_ref, qg_ref, kg_ref, sink_ref, rg_ref, wout_ref,
                          ck_ref, cv_ref, s0_ref,
                          o_ref, nk_ref, nv_ref, ns_ref, mix_ref):
    x = x_ref[...]
    sb, length, d = x.shape
    tb = sb * length
    w = ck_ref.shape[1]
    mod = mod_ref[...]
    h = _modulate(x, mod, gn_ref[...], 0).reshape(tb, d).astype(BF16)
    proj = _dot(h, win_ref[...])
    qg = qg_ref[...]
    kg = kg_ref[...]
    rg = rg_ref[...]

    rows = A_GROUP * length
    qpos_c = lax.broadcasted_iota(jnp.int32, (rows, w), 0) % length
    kpos_c = lax.broadcasted_iota(jnp.int32, (rows, w), 1)
    dist_c = w + qpos_c - kpos_c
    valid_c = (dist_c >= 0) & (dist_c < WINDOW)
    qpos_n = lax.broadcasted_iota(jnp.int32, (rows, length), 0) % length
    kpos_n = lax.broadcasted_iota(jnp.int32, (rows, length), 1)
    dist_n = qpos_n - kpos_n
    valid_n = (dist_n >= 0) & (dist_n < WINDOW)
    row_g = lax.broadcasted_iota(jnp.int32, (rows, 1), 0) // length

    for kv in range(A_KV):
        lanes = slice(kv * A_HD, (kv + 1) * A_HD)
        kn = _head_rms(proj[:, OFF_KA + kv * A_HD:OFF_KA + (kv + 1) * A_HD], kg).reshape(sb, length, A_HD)
        vn = proj[:, OFF_VA + kv * A_HD:OFF_VA + (kv + 1) * A_HD].reshape(sb, length, A_HD)
        nk_ref[:, 0:w - length, lanes] = ck_ref[:, length:w, lanes]
        nv_ref[:, 0:w - length, lanes] = cv_ref[:, length:w, lanes]
        nk_ref[:, w - length:w, lanes] = kn
        nv_ref[:, w - length:w, lanes] = vn
        kc = ck_ref[:, :, lanes].astype(BF16)
        vc = cv_ref[:, :, lanes].astype(BF16)
        q4 = jnp.concatenate(
            [_head_rms(proj[:, OFF_QA + (kv * A_GROUP + g) * A_HD:OFF_QA + (kv * A_GROUP + g + 1) * A_HD], qg)
             .reshape(sb, length, A_HD) for g in range(A_GROUP)], axis=1).astype(BF16)
        slope = jnp.zeros((rows, 1), F32)
        sink = jnp.zeros((rows, 1), F32)
        for g in range(A_GROUP):
            hd = kv * A_GROUP + g
            slope = jnp.where(row_g == g, _alibi_slope(hd), slope)
            sink = jnp.where(row_g == g, sink_ref[hd], sink)
        scale = A_HD ** -0.5
        s_c = _bmm_nt(q4, kc) * scale - slope * dist_c.astype(F32)
        s_c = jnp.where(valid_c, s_c, NEG_INF)
        s_n = _bmm_nt(q4, kn.astype(BF16)) * scale - slope * dist_n.astype(F32)
        s_n = jnp.where(valid_n, s_n, NEG_INF)
        mx = jnp.maximum(jnp.maximum(jnp.max(s_c, axis=-1, keepdims=True),
                                     jnp.max(s_n, axis=-1, keepdims=True)), sink)
        p_c = jnp.exp(s_c - mx)
        p_n = jnp.exp(s_n - mx)
        den = (jnp.sum(p_c, axis=-1, keepdims=True) + jnp.sum(p_n, axis=-1, keepdims=True)
               + jnp.exp(sink - mx))
        o4 = (_bmm(p_c.astype(BF16), vc) + _bmm(p_n.astype(BF16), vn.astype(BF16))) / den
        for g in range(A_GROUP):
            hd = kv * A_GROUP + g
            mix_ref[:, hd * A_HD:(hd + 1) * A_HD] = o4[:, g * length:(g + 1) * length, :].reshape(tb, A_HD)

    for hb in range(B_HEADS):
        d_in, d_q, d_k, d_c = _ret_decay(hb, length)
        qb = proj[:, OFF_QB + hb * B_KD:OFF_QB + (hb + 1) * B_KD].reshape(sb, length, B_KD).astype(BF16)
        kb = proj[:, OFF_KB + hb * B_KD:OFF_KB + (hb + 1) * B_KD].reshape(sb, length, B_KD) * (B_KD ** -0.5)
        vb = proj[:, OFF_VB + hb * B_VD:OFF_VB + (hb + 1) * B_VD].reshape(sb, length, B_VD).astype(BF16)
        gb = proj[:, OFF_GB + hb * B_VD:OFF_GB + (hb + 1) * B_VD]
        state = s0_ref[:, hb]
        inner = _bmm_nt(qb, kb.astype(BF16)) * d_in
        o = _bmm(inner.astype(BF16), vb) + _bmm(qb, state.astype(BF16)) * d_q
        ns_ref[:, hb] = state * d_c + _bmm_tn((kb * d_k).astype(BF16), vb)
        mix_ref[:, QA_W + hb * B_VD:QA_W + (hb + 1) * B_VD] = _group_norm_gate(
            o.reshape(tb, B_VD), rg[:, hb * B_VD:(hb + 1) * B_VD], gb)

    out = _dot(mix_ref[...].astype(BF16), wout_ref[...])
    o_ref[...] = x + _gate(mod, 0, d) * out.reshape(sb, length, d)


def _mix_even_sample(x3, mod, layer, gain, w_in, q_gain, k_gain, sinks, ret_gain, w_out, cache_k, cache_v, state, sb):
    n, length, d = x3.shape
    in_w = w_in.shape[1]
    w = cache_k.shape[1]
    seq_spec = lambda shape: pl.BlockSpec((sb,) + shape, lambda i, _n=len(shape): (i,) + (0,) * _n)
    return pl.pallas_call(
        _mix_even_sample_body,
        grid=(n // sb,),
        in_specs=[seq_spec((length, d)), _mod_seq_spec(mod, layer, sb),
                  _const_spec((1, d)), _const_spec((d, in_w)),
                  _const_spec((1, A_HD)), _const_spec((1, A_HD)),
                  pl.BlockSpec(memory_space=pltpu.SMEM),
                  _const_spec((1, VB_W)), _const_spec((QA_W + VB_W, d)),
                  seq_spec((w, KA_W)), seq_spec((w, VA_W)),
                  pl.BlockSpec((None, sb, B_HEADS, B_KD, B_VD), lambda i: (0, i, 0, 0, 0))],
        out_specs=[seq_spec((length, d)), seq_spec((w, KA_W)), seq_spec((w, VA_W)),
                   seq_spec((B_HEADS, B_KD, B_VD))],
        out_shape=[jax.ShapeDtypeStruct(x3.shape, F32),
                   jax.ShapeDtypeStruct(cache_k.shape, F32),
                   jax.ShapeDtypeStruct(cache_v.shape, F32),
                   jax.ShapeDtypeStruct(state.shape[1:], F32)],
        scratch_shapes=[pltpu.VMEM((sb * length, QA_W + VB_W), F32)],
        compiler_params=_params(),
        name="mix_even_sample",
    )(x3, mod, gain.reshape(1, d), w_in, q_gain.reshape(1, A_HD), k_gain.reshape(1, A_HD),
      sinks, ret_gain.reshape(1, VB_W), w_out, cache_k, cache_v, state)


def _cmul(ar, ai, br, bi):
    return ar * br - ai * bi, ar * bi + ai * br


def _s5_lambda(a_re, a_im, log_dt):
    dt = jnp.exp(log_dt)
    mag = jnp.exp(a_re * dt)
    return mag * jnp.cos(a_im * dt), mag * jnp.sin(a_im * dt)


def _s5_prep_body(t_len, n_chunks, are_ref, aim_ref, ldt_ref, btr_ref, bti_ref, ctr_ref, cti_ref,
                  ard_ref, aid_ref, ldd_ref,
                  bblk_ref, cre_ref, cim_ref, lam_ref, pwb_ref, ltp_ref, b32_ref, cr32_ref, ci32_ref):
    groups = btr_ref.shape[0]
    k_in, p = btr_ref.shape[1:]
    half = S5_OCT * p
    a_re = are_ref[...]
    a_im = aim_ref[...]
    lam_re, lam_im = _s5_lambda(a_re, a_im, ldt_ref[...])
    den = a_re * a_re + a_im * a_im
    n_re = lam_re - 1.0
    n_im = lam_im
    f_re = (n_re * a_re + n_im * a_im) / den
    f_im = (n_im * a_re - n_re * a_im) / den
    br = btr_ref[...]
    bi = bti_ref[...]
    bb_re = f_re * br - f_im * bi
    bb_im = f_re * bi + f_im * br
    b32_ref[...] = jnp.zeros_like(b32_ref)
    cr32_ref[...] = jnp.zeros_like(cr32_ref)
    ci32_ref[...] = jnp.zeros_like(ci32_ref)
    for g in range(groups):
        s, gl = divmod(g, S5_OCT)
        rows = slice(gl * k_in, (gl + 1) * k_in)
        cols = slice(gl * p, (gl + 1) * p)
        b32_ref[s, rows, cols] = bb_re[g]
        b32_ref[s, rows, half + gl * p:half + (gl + 1) * p] = bb_im[g]
        cr32_ref[s, cols, rows] = ctr_ref[g]
        ci32_ref[s, cols, rows] = cti_ref[g]
    bblk_ref[...] = b32_ref[...].astype(BF16)
    cre_ref[...] = cr32_ref[...].astype(BF16)
    cim_ref[...] = ci32_ref[...].astype(BF16)
    lam_re, lam_im = _s5_lambda(ard_ref[...], aid_ref[...], ldd_ref[...])
    lam_ref[0] = lam_re
    lam_ref[1] = lam_im
    n_oct = lam_re.shape[0]

    def both(cr, ci, s, rows):
        return jnp.concatenate([jnp.broadcast_to(cr[s:s + 1, :], (rows, half)),
                                jnp.broadcast_to(ci[s:s + 1, :], (rows, half))], axis=1)

    cr, ci = lam_re, lam_im
    for t in range(t_len):
        for s in range(n_oct):
            pwb_ref[s, t] = both(cr, ci, s, ROWS)
        if t + 1 < t_len:
            cr, ci = _cmul(cr, ci, lam_re, lam_im)
    tr, ti = cr, ci
    ltp_ref[...] = jnp.zeros_like(ltp_ref)
    cr, ci = jnp.ones_like(lam_re), jnp.zeros_like(lam_im)
    for m in range(n_chunks + 1):
        for s in range(n_oct):
            if m < n_chunks:
                ltp_ref[s, n_chunks + m:n_chunks + m + 1, :] = both(cr, ci, s, 1)
            if m in (1, 2, 4, 8):
                i = (1, 2, 4, 8).index(m)
                ltp_ref[s, i:i + 1, :] = both(cr, ci, s, 1)
        cr, ci = _cmul(cr, ci, tr, ti)


def _s5_prep(a_re, a_im, log_dt, b_re, b_im, c_re, c_im, t_len, n_chunks):
    g, p = a_re.shape
    k = b_re.shape[-1]
    assert n_chunks == ROWS
    n_oct = g // S5_OCT
    half = S5_OCT * p
    dense = (n_oct, half)
    return pl.pallas_call(
        functools.partial(_s5_prep_body, t_len, n_chunks),
        out_shape=[jax.ShapeDtypeStruct((n_oct, S5_OCT * k, 2 * half), BF16),
                   jax.ShapeDtypeStruct((n_oct, half, S5_OCT * k), BF16),
                   jax.ShapeDtypeStruct((n_oct, half, S5_OCT * k), BF16),
                   jax.ShapeDtypeStruct((2,) + dense, F32),
                   jax.ShapeDtypeStruct((n_oct, t_len, ROWS, 2 * half), F32),
                   jax.ShapeDtypeStruct((n_oct, 2 * n_chunks, 2 * half), F32)],
        scratch_shapes=[pltpu.VMEM((n_oct, S5_OCT * k, 2 * half), F32),
                        pltpu.VMEM((n_oct, half, S5_OCT * k), F32),
                        pltpu.VMEM((n_oct, half, S5_OCT * k), F32)],
        compiler_params=pltpu.CompilerParams(vmem_limit_bytes=VMEM_LIMIT),
        name="s5_prep",
    )(a_re.reshape(g, 1, p), a_im.reshape(g, 1, p), log_dt.reshape(g, 1, 1),
      jnp.swapaxes(b_re, 1, 2), jnp.swapaxes(b_im, 1, 2), jnp.swapaxes(c_re, 1, 2), jnp.swapaxes(c_im, 1, 2),
      a_re.reshape(dense), a_im.reshape(dense), jnp.broadcast_to(log_dt[:, None], (g, p)).reshape(dense))


def _gelu_glu_out(x, mod, y, u, dskip, glua_ref, glub_ref):
    sb, rows, d = x.shape
    y = y + dskip * u
    yg = jax.nn.gelu(y, approximate=True).astype(BF16)
    out = _dot(yg, glua_ref[...]) * jax.nn.sigmoid(_dot(yg, glub_ref[...]))
    return x + _gate(mod, 0, d) * out.reshape(sb, rows, d)


def _mix_odd_prompt_body(x_ref, xn_ref, mod_ref, gn_ref, bblk_ref, lam_ref, pwb_ref, ltp_ref, cre_ref, cim_ref,
                         dskip_ref, glua_ref, glub_ref, o_ref, hre_ref, him_ref,
                         un_ref, l0_ref, l1_ref, up0_ref, up1_ref, pt_ref):
    step = pl.program_id(0)
    sb, _, d = x_ref.shape
    tm = sb * ROWS
    t_len = tm // ROWS
    n_oct = bblk_ref.shape[0]
    half = bblk_ref.shape[2] // 2
    mod = mod_ref[...]

    def permuted_input(src_ref, up_dst):
        u = _modulate(src_ref[...], mod, gn_ref[...], 0).reshape(tm, d)
        for k in range(d // 128):
            un_ref[k] = u[:, k * 128:(k + 1) * 128]
        up = jnp.concatenate(
            [jnp.concatenate([un_ref[k, pl.ds(t, ROWS, stride=t_len), :] for k in range(d // 128)], axis=1)
             for t in range(t_len)], axis=0)
        up_dst[...] = up
        return up.astype(BF16)

    @pl.when(step == 0)
    def _():
        hre_ref[...] = jnp.zeros_like(hre_ref)
        him_ref[...] = jnp.zeros_like(him_ref)
        nat = lax.broadcasted_iota(jnp.int32, (tm, tm), 0)
        prm = lax.broadcasted_iota(jnp.int32, (tm, tm), 1)
        pt_ref[...] = jnp.where(prm == (nat % t_len) * ROWS + nat // t_len, 1.0, 0.0).astype(BF16)
        up16 = permuted_input(x_ref, up0_ref)
        for s in range(n_oct):
            l0_ref[s] = _dot(up16[:, s * 128:(s + 1) * 128], bblk_ref[s])

    def run(l_ref, l_next, up_ref, up_next):
        for s0 in range(0, n_oct, 2):
            pair = (s0, s0 + 1)
            lam_re = [jnp.broadcast_to(lam_ref[0, s:s + 1, :], (ROWS, half)) for s in pair]
            lam_im = [jnp.broadcast_to(lam_ref[1, s:s + 1, :], (ROWS, half)) for s in pair]

            def local_step(t, carry):
                r = pl.multiple_of(t * ROWS, ROWS)
                out = []
                for i, s in enumerate(pair):
                    h_re, h_im = carry[2 * i], carry[2 * i + 1]
                    n_re = lam_re[i] * h_re - lam_im[i] * h_im + l_ref[s, pl.ds(r, ROWS), 0:half]
                    n_im = lam_re[i] * h_im + lam_im[i] * h_re + l_ref[s, pl.ds(r, ROWS), half:2 * half]
                    l_ref[s, pl.ds(r, ROWS), 0:half] = n_re
                    l_ref[s, pl.ds(r, ROWS), half:2 * half] = n_im
                    out += [n_re, n_im]
                return tuple(out)

            zero = jnp.zeros((ROWS, half), F32)
            lax.fori_loop(0, t_len, local_step, (zero, zero, zero, zero), unroll=2)

        next16 = permuted_input(xn_ref, up_next)
        chunk = lax.broadcasted_iota(jnp.int32, (ROWS, half), 0)
        y_parts = []
        for s in range(n_oct):
            l_next[s] = _dot(next16[:, s * 128:(s + 1) * 128], bblk_ref[s])
            p_re = l_ref[s, tm - ROWS:tm, 0:half]
            p_im = l_ref[s, tm - ROWS:tm, half:2 * half]
            for i, sh in enumerate((1, 2, 4)):
                m_re, m_im = _cmul(ltp_ref[s, i:i + 1, 0:half], ltp_ref[s, i:i + 1, half:2 * half],
                                   jnp.where(chunk >= sh, pltpu.roll(p_re, sh, 0), 0.0),
                                   jnp.where(chunk >= sh, pltpu.roll(p_im, sh, 0), 0.0))
                p_re, p_im = p_re + m_re, p_im + m_im
            hin_re = jnp.broadcast_to(hre_ref[s:s + 1, :], (ROWS, half))
            hin_im = jnp.broadcast_to(him_ref[s:s + 1, :], (ROWS, half))
            m_re, m_im = _cmul(ltp_ref[s, ROWS:2 * ROWS, 0:half], ltp_ref[s, ROWS:2 * ROWS, half:2 * half],
                               hin_re, hin_im)
            st_re = m_re + jnp.where(chunk >= 1, pltpu.roll(p_re, 1, 0), 0.0)
            st_im = m_im + jnp.where(chunk >= 1, pltpu.roll(p_im, 1, 0), 0.0)
            m_re, m_im = _cmul(ltp_ref[s, 3:4, 0:half], ltp_ref[s, 3:4, half:2 * half],
                               hre_ref[s:s + 1, :], him_ref[s:s + 1, :])
            hre_ref[s:s + 1, :] = m_re + p_re[ROWS - 1:ROWS, :]
            him_ref[s:s + 1, :] = m_im + p_im[ROWS - 1:ROWS, :]
            loc = l_ref[s].reshape(t_len, ROWS, 2 * half)
            pw = pwb_ref[s]
            f_re, f_im = _cmul(pw[:, :, 0:half], pw[:, :, half:2 * half], st_re[None], st_im[None])
            hs_re = (loc[:, :, 0:half] + f_re).reshape(tm, half).astype(BF16)
            hs_im = (loc[:, :, half:2 * half] + f_im).reshape(tm, half).astype(BF16)
            y_parts.append(_dot(hs_re, cre_ref[s]) - _dot(hs_im, cim_ref[s]))

        y = jnp.concatenate(y_parts, axis=1) + dskip_ref[...] * up_ref[...]
        yg = jax.nn.gelu(y, approximate=True).astype(BF16)
        yn = _dot(pt_ref[...], yg).astype(BF16)
        out = _dot(yn, glua_ref[...]) * jax.nn.sigmoid(_dot(yn, glub_ref[...]))
        o_ref[...] = x_ref[...] + _gate(mod, 0, d) * out.reshape(sb, ROWS, d)

    @pl.when(step % 2 == 0)
    def _():
        run(l0_ref, l1_ref, up0_ref, up1_ref)

    @pl.when(step % 2 == 1)
    def _():
        run(l1_ref, l0_ref, up1_ref, up0_ref)


def _mix_odd_prompt(x3, mod, layer, prompt_row, gain, bblk, lam_d, pwb, ltp, cre, cim, dskip, glu_a, glu_b, sb):
    n8, _, d = x3.shape
    n_oct, kin, wid = bblk.shape
    half = wid // 2
    tm = sb * ROWS
    n_tiles = n8 // sb
    proj_buf = pltpu.VMEM((n_oct, tm, wid), F32)
    perm_buf = pltpu.VMEM((tm, d), F32)
    return pl.pallas_call(
        _mix_odd_prompt_body,
        grid=(n_tiles,),
        in_specs=[pl.BlockSpec((sb, ROWS, d), lambda i: (i, 0, 0)),
                  pl.BlockSpec((sb, ROWS, d), lambda i: (jnp.minimum(i + 1, n_tiles - 1), 0, 0)),
                  _mod_row_spec(mod, layer, prompt_row),
                  _const_spec((1, d)), _const_spec(bblk.shape), _const_spec(lam_d.shape),
                  _const_spec(pwb.shape), _const_spec(ltp.shape),
                  _const_spec(cre.shape), _const_spec(cim.shape), _const_spec((1, d)),
                  _const_spec((d, d)), _const_spec((d, d))],
        out_specs=[pl.BlockSpec((sb, ROWS, d), lambda i: (i, 0, 0)),
                   pl.BlockSpec((n_oct, half), lambda i: (0, 0)),
                   pl.BlockSpec((n_oct, half), lambda i: (0, 0))],
        out_shape=[jax.ShapeDtypeStruct(x3.shape, F32),
                   jax.ShapeDtypeStruct((n_oct, half), F32),
                   jax.ShapeDtypeStruct((n_oct, half), F32)],
        scratch_shapes=[pltpu.VMEM((d // 128, tm, 128), F32), proj_buf, proj_buf, perm_buf, perm_buf,
                        pltpu.VMEM((tm, tm), BF16)],
        compiler_params=_params(),
        name="mix_odd_prompt",
    )(x3, x3, mod, gain.reshape(1, d), bblk, lam_d, pwb, ltp, cre, cim, dskip.reshape(1, d), glu_a, glu_b)


def _mix_odd_sample_body(x_ref, mod_ref, gn_ref, bblk_ref, lam_ref, cre_ref, cim_ref, dskip_ref,
                         glua_ref, glub_ref, sre_ref, sim_ref, o_ref, nre_ref, nim_ref, d_ref, y_ref):
    x = x_ref[...]
    sb, length, d = x.shape
    tm = sb * length
    n_oct = bblk_ref.shape[0]
    half = bblk_ref.shape[2] // 2
    mod = mod_ref[...]
    u = _modulate(x, mod, gn_ref[...], 0).reshape(tm, d)
    u16 = u.astype(BF16)
    n_ch = bblk_ref.shape[2] // 128
    hc = n_ch // 2
    for s in range(n_oct):
        bu = _dot(u16[:, s * 128:(s + 1) * 128], bblk_ref[s])
        for c in range(n_ch):
            d_ref[c] = bu[:, c * 128:(c + 1) * 128]
        lam_re = lam_ref[0, s:s + 1, :]
        lam_im = lam_ref[1, s:s + 1, :]
        h_re = sre_ref[:, s * half:(s + 1) * half]
        h_im = sim_ref[:, s * half:(s + 1) * half]
        for t in range(length):
            b_re = jnp.concatenate([d_ref[c, pl.ds(t, sb, stride=length), :] for c in range(hc)], axis=1)
            b_im = jnp.concatenate([d_ref[hc + c, pl.ds(t, sb, stride=length), :] for c in range(hc)], axis=1)
            n_re = lam_re * h_re - lam_im * h_im + b_re
            n_im = lam_re * h_im + lam_im * h_re + b_im
            for c in range(hc):
                d_ref[c, pl.ds(t, sb, stride=length), :] = n_re[:, c * 128:(c + 1) * 128]
                d_ref[hc + c, pl.ds(t, sb, stride=length), :] = n_im[:, c * 128:(c + 1) * 128]
            h_re, h_im = n_re, n_im
        nre_ref[:, s * half:(s + 1) * half] = h_re
        nim_ref[:, s * half:(s + 1) * half] = h_im
        hs_re = jnp.concatenate([d_ref[c] for c in range(hc)], axis=1)
        hs_im = jnp.concatenate([d_ref[hc + c] for c in range(hc)], axis=1)
        y_ref[:, s * 128:(s + 1) * 128] = (_dot(hs_re.astype(BF16), cre_ref[s])
                                           - _dot(hs_im.astype(BF16), cim_ref[s]))
    o_ref[...] = _gelu_glu_out(x, mod, y_ref[...], u, dskip_ref[...], glua_ref, glub_ref)


def _mix_odd_sample(x3, mod, layer, gain, bblk, lam_d, cre, cim, dskip, glu_a, glu_b, s_re, s_im, sb):
    n, length, d = x3.shape
    n_oct, kin, wid = bblk.shape
    tm = sb * length
    nstate = s_re.shape[1]
    return pl.pallas_call(
        _mix_odd_sample_body,
        grid=(n // sb,),
        in_specs=[pl.BlockSpec((sb, length, d), lambda i: (i, 0, 0)),
                  _mod_seq_spec(mod, layer, sb),
                  _const_spec((1, d)), _const_spec(bblk.shape), _const_spec(lam_d.shape),
                  _const_spec(cre.shape), _const_spec(cim.shape), _const_spec((1, d)),
                  _const_spec((d, d)), _const_spec((d, d)),
                  pl.BlockSpec((sb, nstate), lambda i: (i, 0)),
                  pl.BlockSpec((sb, nstate), lambda i: (i, 0))],
        out_specs=[pl.BlockSpec((sb, length, d), lambda i: (i, 0, 0)),
                   pl.BlockSpec((sb, nstate), lambda i: (i, 0)),
                   pl.BlockSpec((sb, nstate), lambda i: (i, 0))],
        out_shape=[jax.ShapeDtypeStruct(x3.shape, F32),
                   jax.ShapeDtypeStruct(s_re.shape, F32),
                   jax.ShapeDtypeStruct(s_im.shape, F32)],
        scratch_shapes=[pltpu.VMEM((wid // 128, tm, 128), F32), pltpu.VMEM((tm, d), F32)],
        compiler_params=_params(),
        name="mix_odd_sample",
    )(x3, mod, gain.reshape(1, d), bblk, lam_d, cre, cim, dskip.reshape(1, d), glu_a, glu_b, s_re, s_im)


def _pick(n, want):
    while n % want:
        want //= 2
    return max(want, 1)


def kernel(x_prompt, x_sample, cache_win_k, cache_win_v, state_ret, state_s5_re, state_s5_im, c_prompt, c_sample, ada_w, ada_b, norm_mix, norm_ffn, ffn_wg, ffn_wu, ffn_wd, even_w_in, even_q_gain, even_k_gain, even_sinks, even_ret_gain, even_w_out, odd_A_re, odd_A_im, odd_log_dt, odd_B_re, odd_B_im, odd_C_re, odd_C_im, odd_D, odd_glu_a, odd_glu_b):
    bp, lp, d = x_prompt.shape
    ns, ls, _ = x_sample.shape
    assert bp == 1 and ls == ROWS and lp % WINDOW == 0
    w = cache_win_k.shape[2]
    groups, p_state = odd_A_re.shape[1:]

    n_c = bp + ns
    n_pad = -n_c % ROWS
    c_all = jnp.concatenate([c_sample, c_prompt, jnp.zeros((n_pad, d), F32)], axis=0)
    mod = _adaln(c_all, ada_w, ada_b)

    bf = lambda t: t.astype(BF16)
    w_in, w_out = bf(even_w_in[0]), bf(even_w_out[0])

    xp = x_prompt.reshape(lp // ROWS, ROWS, d)
    xs = x_sample

    sb_p = _pick(lp // ROWS, 32)
    sb_ffn = _pick(lp // ROWS, 64)
    sb_s = _pick(ns, 32)
    sb_s_even = _pick(ns, 16)

    (xp, p_k, p_v, p_ret), ffn0 = _mix_even_prompt(xp, mod, 0, ns, norm_mix[0], w_in, even_q_gain[0], even_k_gain[0],
                                                   even_sinks[0], even_ret_gain[0], w_out, sb_ffn,
                                                   cast=((ffn_wg, 0), (ffn_wu, 0), (ffn_wd, 0)))
    xs, s_k, s_v, s_ret = _mix_even_sample(xs, mod, 0, norm_mix[0], w_in, even_q_gain[0], even_k_gain[0],
                                           even_sinks[0], even_ret_gain[0], w_out,
                                           cache_win_k[0].reshape(ns, w, KA_W), cache_win_v[0].reshape(ns, w, VA_W),
                                           state_ret, sb_s_even)
    xp, xs, (wg1, wu1, wd1, glu_a, glu_b) = _ffn(
        xp, xs, mod, 0, norm_ffn[0], *ffn0, sb_ffn,
        cast=((ffn_wg, 1), (ffn_wu, 1), (ffn_wd, 1), (odd_glu_a, 0), (odd_glu_b, 0)))

    t_len = sb_p
    bblk, cre, cim, lam_d, pwb, ltp = _s5_prep(odd_A_re[0], odd_A_im[0], odd_log_dt[0], odd_B_re[0], odd_B_im[0],
                                               odd_C_re[0], odd_C_im[0], t_len, ROWS)
    xp, p_re, p_im = _mix_odd_prompt(xp, mod, 1, ns, norm_mix[1], bblk, lam_d, pwb, ltp, cre, cim, odd_D[0],
                                     glu_a, glu_b, sb_p)
    xs, s_re, s_im = _mix_odd_sample(xs, mod, 1, norm_mix[1], bblk, lam_d, cre, cim, odd_D[0], glu_a, glu_b,
                                     state_s5_re[0].reshape(ns, groups * p_state),
                                     state_s5_im[0].reshape(ns, groups * p_state), sb_s)
    xp, xs, _ = _ffn(xp, xs, mod, 1, norm_ffn[1], wg1, wu1, wd1, sb_ffn)

    y_prompt = xp.reshape(bp, lp, d)
    y_sample = xs
    return (y_prompt, y_sample,
            p_k.reshape(1, bp, WINDOW, A_KV, A_HD), p_v.reshape(1, bp, WINDOW, A_KV, A_HD),
            p_ret.reshape(1, bp, B_HEADS, B_KD, B_VD),
            p_re.reshape(1, bp, groups, p_state), p_im.reshape(1, bp, groups, p_state),
            s_k.reshape(1, ns, w, A_KV, A_HD), s_v.reshape(1, ns, w, A_KV, A_HD),
            s_ret.reshape(1, ns, B_HEADS, B_KD, B_VD),
            s_re.reshape(1, ns, groups, p_state), s_im.reshape(1, ns, groups, p_state))
```

```python
import functools
import math

import jax
import jax.numpy as jnp
from jax import lax
from jax.experimental import pallas as pl
from jax.experimental.pallas import tpu as pltpu

F32 = jnp.float32
BF16 = jnp.bfloat16

EPS = 1e-6
NEG_INF = -1e30
ROWS = 8

A_HEADS, A_KV, A_GROUP, A_HD = 8, 2, 4, 64
WINDOW = 128
B_HEADS, B_KD, B_VD = 4, 128, 128
S5_GROUP, S5_STATE = 16, 64
S5_OCT = 8

QA_W, KA_W, VA_W = A_HEADS * A_HD, A_KV * A_HD, A_KV * A_HD
QB_W, KB_W, VB_W, GB_W = B_HEADS * B_KD, B_HEADS * B_KD, B_HEADS * B_VD, B_HEADS * B_VD
OFF_QA = 0
OFF_KA = OFF_QA + QA_W
OFF_VA = OFF_KA + KA_W
OFF_QB = OFF_VA + VA_W
OFF_KB = OFF_QB + QB_W
OFF_VB = OFF_KB + KB_W
OFF_GB = OFF_VB + VB_W

VMEM_LIMIT = 56 * 1024 * 1024


def _ret_log_gamma(h):
    return math.log1p(-(2.0 ** (-5.0 - h)))


def _alibi_slope(h):
    return 2.0 ** (-8.0 * (h + 1) / A_HEADS)


def _const_spec(shape):
    nd = len(shape)
    return pl.BlockSpec(shape, lambda i, _n=nd: (0,) * _n, pipeline_mode=pl.Buffered(1))


def _params():
    return pltpu.CompilerParams(dimension_semantics=("arbitrary",), vmem_limit_bytes=VMEM_LIMIT)


def _dot(a, b):
    return jnp.dot(a, b, preferred_element_type=F32)


def _dot_nt(a, b):
    return lax.dot_general(a, b, (((1,), (1,)), ((), ())), preferred_element_type=F32)


def _dot_tn(a, b):
    return lax.dot_general(a, b, (((0,), (0,)), ((), ())), preferred_element_type=F32)


def _bmm(a, b):
    return lax.dot_general(a, b, (((2,), (1,)), ((0,), (0,))), preferred_element_type=F32)


def _bmm_nt(a, b):
    return lax.dot_general(a, b, (((2,), (2,)), ((0,), (0,))), preferred_element_type=F32)


def _bmm_tn(a, b):
    return lax.dot_general(a, b, (((1,), (1,)), ((0,), (0,))), preferred_element_type=F32)


def _rms(x, g):
    return x * lax.rsqrt(jnp.mean(x * x, axis=-1, keepdims=True) + EPS) * g


def _modulate(x3, mod, gain, which):
    d = x3.shape[-1]
    sh = mod[:, :, (3 * which) * d:(3 * which + 1) * d]
    sc = mod[:, :, (3 * which + 1) * d:(3 * which + 2) * d]
    return _rms(x3, gain) * (1.0 + sc) + sh


def _mod_row_spec(mod, layer, row):
    return pl.BlockSpec((None, 1, 1, mod.shape[-1]), lambda i: (layer, row, 0, 0))


def _mod_seq_spec(mod, layer, sb, first_step=0):
    return pl.BlockSpec((None, sb, 1, mod.shape[-1]), lambda i: (layer, jnp.maximum(i - first_step, 0), 0, 0))


def _gate(mod, which, d):
    return mod[:, :, (3 * which + 2) * d:(3 * which + 3) * d]


def _adaln_body(c_ref, w_ref, b_ref, o_ref):
    c = c_ref[...]
    a = (c * jax.nn.sigmoid(c)).astype(BF16)
    o_ref[0, :, 0, :] = _dot(a, w_ref[0].astype(BF16)) + b_ref[0]


def _adaln(c_all, ada_w, ada_b):
    depth, d, n = ada_w.shape
    r = c_all.shape[0]
    tn = 1536
    return pl.pallas_call(
        _adaln_body,
        grid=(depth, n // tn),
        in_specs=[pl.BlockSpec((r, d), lambda l, j: (0, 0)),
                  pl.BlockSpec((1, d, tn), lambda l, j: (l, 0, j)),
                  pl.BlockSpec((1, 1, tn), lambda l, j: (l, 0, j))],
        out_specs=pl.BlockSpec((1, r, 1, tn), lambda l, j: (l, 0, 0, j)),
        out_shape=jax.ShapeDtypeStruct((depth, r, 1, n), F32),
        compiler_params=pltpu.CompilerParams(dimension_semantics=("arbitrary", "arbitrary"),
                                             vmem_limit_bytes=VMEM_LIMIT),
        name="adaln",
    )(c_all, ada_w, ada_b.reshape(depth, 1, n))


BF16_ROWS = 16


def _cast_specs(job, n_steps):
    w, layer = job
    _, rows, cols = w.shape
    hold = 1
    while rows % (n_steps // hold) or (rows // (n_steps // hold)) % BF16_ROWS:
        hold *= 2
        assert hold <= n_steps and n_steps % hold == 0
    blk = rows // (n_steps // hold)
    src = pl.BlockSpec((None, blk, cols),
                       lambda i, _h=hold, _l=layer: (_l, jnp.minimum(i, n_steps - 1) // _h, 0))
    dst = pl.BlockSpec((blk, cols), lambda i, _h=hold: (jnp.minimum(i, n_steps - 1) // _h, 0))
    return src, dst, jax.ShapeDtypeStruct((rows, cols), BF16)


def _cast_blocks(in_refs, out_refs):
    for src, dst in zip(in_refs, out_refs):
        dst[...] = src[...].astype(BF16)


def _ffn_body(n_cast, n_prompt, xp_ref, mp_ref, xs_ref, ms_ref, gn_ref, wg_ref, wu_ref, wd_ref, *refs):
    cast_in, (op_ref, os_ref), cast_out = refs[:n_cast], refs[n_cast:n_cast + 2], refs[n_cast + 2:]
    step = pl.program_id(0)

    def tile(x_ref, mod_ref, o_ref):
        x = x_ref[...]
        sb, _, d = x.shape
        mod = mod_ref[...]
        h = _modulate(x, mod, gn_ref[...], 1).reshape(sb * ROWS, d).astype(BF16)
        a = _dot(h, wg_ref[...])
        b = _dot(h, wu_ref[...])
        act = (a * jax.nn.sigmoid(a) * b).astype(BF16)
        y = _dot(act, wd_ref[...])
        o_ref[...] = x + _gate(mod, 1, d) * y.reshape(sb, ROWS, d)

    @pl.when(step < n_prompt)
    def _():
        tile(xp_ref, mp_ref, op_ref)

    @pl.when(step >= n_prompt)
    def _():
        tile(xs_ref, ms_ref, os_ref)

    _cast_blocks(cast_in, cast_out)


def _ffn(xp3, xs3, mod, layer, gain, wg, wu, wd, sb, cast=()):
    n8, _, d = xp3.shape
    f = wg.shape[1]
    n_prompt = n8 // sb
    n_sample = xs3.shape[0] // sb
    prompt_row = xs3.shape[0]
    assert n8 % sb == 0 and xs3.shape[0] % sb == 0
    cast_specs = [_cast_specs(w, n_prompt) for w in cast]
    p_idx = lambda i: (jnp.minimum(i, n_prompt - 1), 0, 0)
    s_idx = lambda i: (jnp.maximum(i - n_prompt, 0), 0, 0)
    xp_spec = pl.BlockSpec((sb, ROWS, d), p_idx)
    xs_spec = pl.BlockSpec((sb, ROWS, d), s_idx)
    out = pl.pallas_call(
        functools.partial(_ffn_body, len(cast), n_prompt),
        grid=(n_prompt + n_sample,),
        in_specs=[xp_spec, _mod_row_spec(mod, layer, prompt_row),
                  xs_spec, _mod_seq_spec(mod, layer, sb, n_prompt),
                  _const_spec((1, d)), _const_spec((d, f)), _const_spec((d, f)),
                  _const_spec((f, d))] + [c[0] for c in cast_specs],
        out_specs=[xp_spec, xs_spec] + [c[1] for c in cast_specs],
        out_shape=[jax.ShapeDtypeStruct(xp3.shape, F32), jax.ShapeDtypeStruct(xs3.shape, F32)]
        + [c[2] for c in cast_specs],
        compiler_params=_params(),
        name="ffn",
    )(xp3, mod, xs3, mod, gain.reshape(1, d), wg, wu, wd, *[w for w, _ in cast])
    return out[0], out[1], tuple(out[2:])


def _head_rms(t, g):
    return t * lax.rsqrt(jnp.mean(t * t, axis=-1, keepdims=True) + EPS) * g


def _group_norm_gate(o, gain, gate):
    mu = jnp.mean(o, axis=-1, keepdims=True)
    var = jnp.mean(jnp.square(o - mu), axis=-1, keepdims=True)
    return (o - mu) * lax.rsqrt(var + EPS) * gain * (gate * jax.nn.sigmoid(gate))


def _ret_decay(hb, c):
    lg = _ret_log_gamma(hb)
    ii = lax.broadcasted_iota(jnp.int32, (c, c), 0)
    jj = lax.broadcasted_iota(jnp.int32, (c, c), 1)
    diff = (ii - jj).astype(F32)
    d_in = jnp.where(diff >= 0, jnp.exp(lg * jnp.maximum(diff, 0.0)), 0.0)
    row = lax.broadcasted_iota(jnp.int32, (c, B_KD), 0).astype(F32)
    d_q = jnp.exp(lg * (row + 1.0))
    d_k = jnp.exp(lg * (c - 1.0 - row))
    d_c = math.exp(lg * c)
    return d_in, d_q, d_k, d_c


P_OFF_KA = QA_W
P_OFF_VA = P_OFF_KA + 2 * KA_W
P_OFF_QB = P_OFF_VA + 2 * VA_W
P_OFF_VB = P_OFF_QB + QB_W
P_OFF_GB = P_OFF_VB + VB_W
P_WIDTH = P_OFF_GB + GB_W
PAIR = 2 * A_HD


def _dedup(t):
    low = lax.broadcasted_iota(jnp.int32, (t.shape[0], PAIR), 1) < A_HD
    return jnp.where(low, t[:, 0:PAIR], t[:, PAIR:2 * PAIR])


def _mix_even_prompt_body(n_cast, x_ref, mod_ref, gn_ref, win_ref, wkt_ref, qg_ref, kg_ref, sink_ref, rg_ref, wout_ref,
                          *refs):
    cast_in, refs = refs[:n_cast], refs[n_cast:]
    o_ref, pk_ref, pv_ref, ps_ref = refs[:4]
    cast_out = refs[4:4 + n_cast]
    mix_ref, carry_ref, bias_ref, dec_ref, ones_ref = refs[4 + n_cast:]
    _cast_blocks(cast_in, cast_out)
    step = pl.program_id(0)
    blk = WINDOW
    rows4 = A_GROUP * blk

    @pl.when(step == 0)
    def _():
        carry_ref[...] = jnp.zeros_like(carry_ref)
        ps_ref[...] = jnp.zeros_like(ps_ref)
        er = lax.broadcasted_iota(jnp.int32, ones_ref.shape, 0) // A_HD
        ec = lax.broadcasted_iota(jnp.int32, ones_ref.shape, 1) // A_HD
        ones_ref[...] = jnp.where(er == ec, 1.0 / A_HD, 0.0).astype(BF16)
        row = lax.broadcasted_iota(jnp.int32, (rows4, 2 * blk), 0)
        dist = row % blk + blk - lax.broadcasted_iota(jnp.int32, (rows4, 2 * blk), 1)
        in_window = (dist >= 0) & (dist < WINDOW)
        for kv in range(A_KV):
            slope = jnp.zeros((rows4, 2 * blk), F32)
            for g in range(A_GROUP):
                slope = jnp.where(row // blk == g, _alibi_slope(kv * A_GROUP + g), slope)
            bias_ref[kv] = jnp.where(in_window, slope * dist.astype(F32), -NEG_INF)
        for hb in range(B_HEADS):
            lg = _ret_log_gamma(hb)
            ii = lax.broadcasted_iota(jnp.int32, (blk, blk), 0).astype(F32)
            jj = lax.broadcasted_iota(jnp.int32, (blk, blk), 1).astype(F32)
            diff = ii - jj
            dec_ref[hb, 0] = jnp.where(diff >= 0, jnp.exp(lg * jnp.maximum(diff, 0.0)), 0.0)
            dec_ref[hb, 1] = jnp.exp(lg * (ii + 1.0))
            dec_ref[hb, 2] = jnp.exp(lg * (blk - 1.0 - jj))

    x = x_ref[...]
    sb, _, d = x.shape
    tb = sb * ROWS
    mod = mod_ref[...]
    h = _modulate(x, mod, gn_ref[...], 0).reshape(tb, d).astype(BF16)
    proj = _dot(h, win_ref[...])
    kt_all = _dot_nt(wkt_ref[...], h) * (B_KD ** -0.5)
    rg = rg_ref[...]

    qa = proj[:, 0:QA_W]
    ka = proj[:, P_OFF_KA:P_OFF_KA + 2 * KA_W]
    va = proj[:, P_OFF_VA:P_OFF_VA + 2 * VA_W]
    q_hat = (qa * lax.rsqrt(_dot((qa * qa).astype(BF16), ones_ref[...]) + EPS) * qg_ref[...]).astype(BF16)
    k_hat = ka * lax.rsqrt(_dot((ka * ka).astype(BF16), ones_ref[0:2 * KA_W, 0:2 * KA_W]) + EPS) * kg_ref[...]
    k_hat16 = k_hat.astype(BF16)
    va16 = va.astype(BF16)
    prev = carry_ref[step % 2]
    carry_ref[(step + 1) % 2] = jnp.concatenate([k_hat16[tb - blk:tb], va16[tb - blk:tb]], axis=1)

    row_g = lax.broadcasted_iota(jnp.int32, (rows4, 1), 0) // blk
    key_is_prev = lax.broadcasted_iota(jnp.int32, (rows4, 2 * blk), 1) < blk
    first_penalty = jnp.where(step == 0, -NEG_INF, 0.0)
    lane_low = lax.broadcasted_iota(jnp.int32, (blk, PAIR), 1) < A_HD
    ones_cols = jnp.ones((2 * blk, PAIR), BF16)

    n_blk = tb // blk
    att = [(j, kv) for j in range(n_blk) for kv in range(A_KV)]
    ret = [(j, hb) for j in range(n_blk) for hb in range(B_HEADS)]

    sinks, scores = {}, {}
    for j, kv in att:
        r0 = j * blk
        kcol = slice(kv * PAIR, (kv + 1) * PAIR)
        if j == 0:
            k2 = jnp.concatenate([prev[:, kcol], k_hat16[0:blk, kcol]], axis=0)
        else:
            k2 = k_hat16[r0 - blk:r0 + blk, kcol]
        q4 = jnp.concatenate(
            [jnp.where(lane_low == (g % 2 == 0),
                       q_hat[r0:r0 + blk, (kv * A_GROUP + g - g % 2) * A_HD:(kv * A_GROUP + g - g % 2 + 2) * A_HD],
                       jnp.zeros((), BF16))
             for g in range(A_GROUP)], axis=0)
        scores[j, kv] = _dot_nt(q4, k2)
        sink = jnp.zeros((rows4, 1), F32)
        for g in range(A_GROUP):
            sink = jnp.where(row_g == g, sink_ref[kv * A_GROUP + g], sink)
        sinks[j, kv] = sink
    qb, vb, kt, inner = {}, {}, {}, {}
    for j, hb in ret:
        r0 = j * blk
        qb[j, hb] = proj[r0:r0 + blk, P_OFF_QB + hb * B_KD:P_OFF_QB + (hb + 1) * B_KD].astype(BF16)
        vb[j, hb] = proj[r0:r0 + blk, P_OFF_VB + hb * B_VD:P_OFF_VB + (hb + 1) * B_VD].astype(BF16)
        kt[j, hb] = kt_all[hb * B_KD:(hb + 1) * B_KD, r0:r0 + blk]
        inner[j, hb] = _dot(qb[j, hb], kt[j, hb].astype(BF16))

    probs, maxes = {}, {}
    for j, kv in att:
        s = scores[j, kv] * (A_HD ** -0.5) - bias_ref[kv]
        if j == 0:
            s = s - jnp.where(key_is_prev, first_penalty, 0.0)
        mx = jnp.maximum(jnp.max(s, axis=-1, keepdims=True), sinks[j, kv])
        probs[j, kv] = jnp.exp(s - mx).astype(BF16)
        maxes[j, kv] = mx
    state = {}
    for hb in range(B_HEADS):
        d_c = math.exp(_ret_log_gamma(hb) * blk)
        state[0, hb] = ps_ref[hb]
        for j in range(n_blk):
            state[j + 1, hb] = state[j, hb] * d_c + _dot((kt[j, hb] * dec_ref[hb, 2]).astype(BF16), vb[j, hb])
        ps_ref[hb] = state[n_blk, hb]

    for j, kv in att:
        r0 = j * blk
        kcol = slice(kv * PAIR, (kv + 1) * PAIR)
        vcol = slice(2 * KA_W + kv * PAIR, 2 * KA_W + (kv + 1) * PAIR)
        if j == 0:
            v2 = jnp.concatenate([prev[:, vcol], va16[0:blk, kcol]], axis=0)
        else:
            v2 = va16[r0 - blk:r0 + blk, kcol]
        pv = _dot(probs[j, kv], jnp.concatenate([v2, ones_cols], axis=1))
        o4 = pv[:, 0:PAIR] / (pv[:, PAIR:2 * PAIR] + jnp.exp(sinks[j, kv] - maxes[j, kv]))
        for g in range(A_GROUP):
            hd = kv * A_GROUP + g
            half = slice((hd % 2) * A_HD, (hd % 2 + 1) * A_HD)
            mix_ref[r0:r0 + blk, hd * A_HD:(hd + 1) * A_HD] = o4[g * blk:(g + 1) * blk, half]
    o_ret = {}
    for j, hb in ret:
        o_ret[j, hb] = (_dot((inner[j, hb] * dec_ref[hb, 0]).astype(BF16), vb[j, hb])
                        + _dot(qb[j, hb], state[j, hb].astype(BF16)) * dec_ref[hb, 1])

    cen = {k: o_ret[k] - jnp.mean(o_ret[k], axis=-1, keepdims=True) for k in ret}
    var = {k: jnp.mean(cen[k] * cen[k], axis=-1, keepdims=True) for k in ret}
    for j, hb in ret:
        r0 = j * blk
        gb = proj[r0:r0 + blk, P_OFF_GB + hb * B_VD:P_OFF_GB + (hb + 1) * B_VD]
        mix_ref[r0:r0 + blk, QA_W + hb * B_VD:QA_W + (hb + 1) * B_VD] = (
            cen[j, hb] * lax.rsqrt(var[j, hb] + EPS) * rg[:, hb * B_VD:(hb + 1) * B_VD] * (gb * jax.nn.sigmoid(gb)))

    out = _dot(mix_ref[...].astype(BF16), wout_ref[...])
    o_ref[...] = x + _gate(mod, 0, d) * out.reshape(sb, ROWS, d)

    @pl.when(step == pl.num_programs(0) - 1)
    def _():
        pk_ref[...] = _dedup(k_hat[tb - blk:tb, :])
        pv_ref[...] = _dedup(va[tb - blk:tb, :])


def _dup_heads(t, width):
    lead = t.shape[:-1]
    t = t.reshape(lead + (-1, 1, width))
    return jnp.broadcast_to(t, lead + (t.shape[-3], 2, width)).reshape(lead + (-1,))


def _mix_even_prompt(x3, mod, layer, prompt_row, gain, w_in, q_gain, k_gain, sinks, ret_gain, w_out, sb, cast=()):
    n8, _, d = x3.shape
    tb = sb * ROWS
    cast_specs = [_cast_specs(w, n8 // sb) for w in cast]
    w_main = jnp.concatenate([w_in[:, :OFF_KA], _dup_heads(w_in[:, OFF_KA:OFF_VA], A_HD),
                              _dup_heads(w_in[:, OFF_VA:OFF_QB], A_HD), w_in[:, OFF_QB:OFF_KB],
                              w_in[:, OFF_VB:]], axis=1)
    wk_t = w_in[:, OFF_KB:OFF_VB].T
    out = pl.pallas_call(
        functools.partial(_mix_even_prompt_body, len(cast)),
        grid=(n8 // sb,),
        in_specs=[pl.BlockSpec((sb, ROWS, d), lambda i: (i, 0, 0)),
                  _mod_row_spec(mod, layer, prompt_row),
                  _const_spec((1, d)), _const_spec((d, P_WIDTH)), _const_spec((KB_W, d)),
                  _const_spec((1, QA_W)), _const_spec((1, 2 * KA_W)),
                  pl.BlockSpec(memory_space=pltpu.SMEM),
                  _const_spec((1, VB_W)), _const_spec((QA_W + VB_W, d))] + [c[0] for c in cast_specs],
        out_specs=[pl.BlockSpec((sb, ROWS, d), lambda i: (i, 0, 0)),
                   pl.BlockSpec((WINDOW, KA_W), lambda i: (0, 0)),
                   pl.BlockSpec((WINDOW, VA_W), lambda i: (0, 0)),
                   pl.BlockSpec((B_HEADS, B_KD, B_VD), lambda i: (0, 0, 0))] + [c[1] for c in cast_specs],
        out_shape=[jax.ShapeDtypeStruct(x3.shape, F32),
                   jax.ShapeDtypeStruct((WINDOW, KA_W), F32),
                   jax.ShapeDtypeStruct((WINDOW, VA_W), F32),
                   jax.ShapeDtypeStruct((B_HEADS, B_KD, B_VD), F32)] + [c[2] for c in cast_specs],
        scratch_shapes=[pltpu.VMEM((tb, QA_W + VB_W), F32),
                        pltpu.VMEM((2, WINDOW, 2 * KA_W + 2 * VA_W), BF16),
                        pltpu.VMEM((A_KV, A_GROUP * WINDOW, 2 * WINDOW), F32),
                        pltpu.VMEM((B_HEADS, 3, WINDOW, WINDOW), F32),
                        pltpu.VMEM((QA_W, QA_W), BF16)],
        compiler_params=_params(),
        name="mix_even_prompt",
    )(x3, mod, gain.reshape(1, d), w_main, wk_t, jnp.tile(q_gain, A_HEADS).reshape(1, QA_W),
      jnp.tile(k_gain, 2 * A_KV).reshape(1, 2 * KA_W),
      sinks, ret_gain.reshape(1, VB_W), w_out, *[w for w, _ in cast])
    return out[:4], tuple(out[4:])


def _mix_even_sample_body(x_ref, mod_ref, gn_ref, win_ref, qg_ref, kg_ref, sink_ref, rg_ref, wout_ref,
                          ck_ref, cv_ref, s0_ref,
                          o_ref, nk_ref, nv_ref, ns_ref, mix_ref):
    x = x_ref[...]
    sb, length, d = x.shape
    tb = sb * length
    w = ck_ref.shape[1]
    mod = mod_ref[...]
    h = _modulate(x, mod, gn_ref[...], 0).reshape(tb, d).astype(BF16)
    proj = _dot(h, win_ref[...])
    qg = qg_ref[...]
    kg = kg_ref[...]
    rg = rg_ref[...]

    rows = A_GROUP * length
    qpos_c = lax.broadcasted_iota(jnp.int32, (rows, w), 0) % length
    kpos_c = lax.broadcasted_iota(jnp.int32, (rows, w), 1)
    dist_c = w + qpos_c - kpos_c
    valid_c = (dist_c >= 0) & (dist_c < WINDOW)
    qpos_n = lax.broadcasted_iota(jnp.int32, (rows, length), 0) % length
    kpos_n = lax.broadcasted_iota(jnp.int32, (rows, length), 1)
    dist_n = qpos_n - kpos_n
    valid_n = (dist_n >= 0) & (dist_n < WINDOW)
    row_g = lax.broadcasted_iota(jnp.int32, (rows, 1), 0) // length

    for kv in range(A_KV):
        lanes = slice(kv * A_HD, (kv + 1) * A_HD)
        kn = _head_rms(proj[:, OFF_KA + kv * A_HD:OFF_KA + (kv + 1) * A_HD], kg).reshape(sb, length, A_HD)
        vn = proj[:, OFF_VA + kv * A_HD:OFF_VA + (kv + 1) * A_HD].reshape(sb, length, A_HD)
        nk_ref[:, 0:w - length, lanes] = ck_ref[:, length:w, lanes]
        nv_ref[:, 0:w - length, lanes] = cv_ref[:, length:w, lanes]
        nk_ref[:, w - length:w, lanes] = kn
        nv_ref[:, w - length:w, lanes] = vn
        kc = ck_ref[:, :, lanes].astype(BF16)
        vc = cv_ref[:, :, lanes].astype(BF16)
        q4 = jnp.concatenate(
            [_head_rms(proj[:, OFF_QA + (kv * A_GROUP + g) * A_HD:OFF_QA + (kv * A_GROUP + g + 1) * A_HD], qg)
             .reshape(sb, length, A_HD) for g in range(A_GROUP)], axis=1).astype(BF16)
        slope = jnp.zeros((rows, 1), F32)
        sink = jnp.zeros((rows, 1), F32)
        for g in range(A_GROUP):
            hd = kv * A_GROUP + g
            slope = jnp.where(row_g == g, _alibi_slope(hd), slope)
            sink = jnp.where(row_g == g, sink_ref[hd], sink)
        scale = A_HD ** -0.5
        s_c = _bmm_nt(q4, kc) * scale - slope * dist_c.astype(F32)
        s_c = jnp.where(valid_c, s_c, NEG_INF)
        s_n = _bmm_nt(q4, kn.astype(BF16)) * scale - slope * dist_n.astype(F32)
        s_n = jnp.where(valid_n, s_n, NEG_INF)
        mx = jnp.maximum(jnp.maximum(jnp.max(s_c, axis=-1, keepdims=True),
                                     jnp.max(s_n, axis=-1, keepdims=True)), sink)
        p_c = jnp.exp(s_c - mx)
        p_n = jnp.exp(s_n - mx)
        den = (jnp.sum(p_c, axis=-1, keepdims=True) + jnp.sum(p_n, axis=-1, keepdims=True)
               + jnp.exp(sink - mx))
        o4 = (_bmm(p_c.astype(BF16), vc) + _bmm(p_n.astype(BF16), vn.astype(BF16))) / den
        for g in range(A_GROUP):
            hd = kv * A_GROUP + g
            mix_ref[:, hd * A_HD:(hd + 1) * A_HD] = o4[:, g * length:(g + 1) * length, :].reshape(tb, A_HD)

    for hb in range(B_HEADS):
        d_in, d_q, d_k, d_c = _ret_decay(hb, length)
        qb = proj[:, OFF_QB + hb * B_KD:OFF_QB + (hb + 1) * B_KD].reshape(sb, length, B_KD).astype(BF16)
        kb = proj[:, OFF_KB + hb * B_KD:OFF_KB + (hb + 1) * B_KD].reshape(sb, length, B_KD) * (B_KD ** -0.5)
        vb = proj[:, OFF_VB + hb * B_VD:OFF_VB + (hb + 1) * B_VD].reshape(sb, length, B_VD).astype(BF16)
        gb = proj[:, OFF_GB + hb * B_VD:OFF_GB + (hb + 1) * B_VD]
        state = s0_ref[:, hb]
        inner = _bmm_nt(qb, kb.astype(BF16)) * d_in
        o = _bmm(inner.astype(BF16), vb) + _bmm(qb, state.astype(BF16)) * d_q
        ns_ref[:, hb] = state * d_c + _bmm_tn((kb * d_k).astype(BF16), vb)
        mix_ref[:, QA_W + hb * B_VD:QA_W + (hb + 1) * B_VD] = _group_norm_gate(
            o.reshape(tb, B_VD), rg[:, hb * B_VD:(hb + 1) * B_VD], gb)

    out = _dot(mix_ref[...].astype(BF16), wout_ref[...])
    o_ref[...] = x + _gate(mod, 0, d) * out.reshape(sb, length, d)


def _mix_even_sample(x3, mod, layer, gain, w_in, q_gain, k_gain, sinks, ret_gain, w_out, cache_k, cache_v, state, sb):
    n, length, d = x3.shape
    in_w = w_in.shape[1]
    w = cache_k.shape[1]
    seq_spec = lambda shape: pl.BlockSpec((sb,) + shape, lambda i, _n=len(shape): (i,) + (0,) * _n)
    return pl.pallas_call(
        _mix_even_sample_body,
        grid=(n // sb,),
        in_specs=[seq_spec((length, d)), _mod_seq_spec(mod, layer, sb),
                  _const_spec((1, d)), _const_spec((d, in_w)),
                  _const_spec((1, A_HD)), _const_spec((1, A_HD)),
                  pl.BlockSpec(memory_space=pltpu.SMEM),
                  _const_spec((1, VB_W)), _const_spec((QA_W + VB_W, d)),
                  seq_spec((w, KA_W)), seq_spec((w, VA_W)),
                  pl.BlockSpec((None, sb, B_HEADS, B_KD, B_VD), lambda i: (0, i, 0, 0, 0))],
        out_specs=[seq_spec((length, d)), seq_spec((w, KA_W)), seq_spec((w, VA_W)),
                   seq_spec((B_HEADS, B_KD, B_VD))],
        out_shape=[jax.ShapeDtypeStruct(x3.shape, F32),
                   jax.ShapeDtypeStruct(cache_k.shape, F32),
                   jax.ShapeDtypeStruct(cache_v.shape, F32),
                   jax.ShapeDtypeStruct(state.shape[1:], F32)],
        scratch_shapes=[pltpu.VMEM((sb * length, QA_W + VB_W), F32)],
        compiler_params=_params(),
        name="mix_even_sample",
    )(x3, mod, gain.reshape(1, d), w_in, q_gain.reshape(1, A_HD), k_gain.reshape(1, A_HD),
      sinks, ret_gain.reshape(1, VB_W), w_out, cache_k, cache_v, state)


def _cmul(ar, ai, br, bi):
    return ar * br - ai * bi, ar * bi + ai * br


def _s5_lambda(a_re, a_im, log_dt):
    dt = jnp.exp(log_dt)
    mag = jnp.exp(a_re * dt)
    return mag * jnp.cos(a_im * dt), mag * jnp.sin(a_im * dt)


def _s5_prep_body(t_len, n_chunks, are_ref, aim_ref, ldt_ref, btr_ref, bti_ref, ctr_ref, cti_ref,
                  ard_ref, aid_ref, ldd_ref,
                  bblk_ref, cre_ref, cim_ref, lam_ref, pwb_ref, ltp_ref, b32_ref, cr32_ref, ci32_ref, pw32_ref):
    groups = btr_ref.shape[0]
    k_in, p = btr_ref.shape[1:]
    half = S5_OCT * p
    a_re = are_ref[...]
    a_im = aim_ref[...]
    lam_re, lam_im = _s5_lambda(a_re, a_im, ldt_ref[...])
    den = a_re * a_re + a_im * a_im
    n_re = lam_re - 1.0
    n_im = lam_im
    f_re = (n_re * a_re + n_im * a_im) / den
    f_im = (n_im * a_re - n_re * a_im) / den
    br = btr_ref[...]
    bi = bti_ref[...]
    bb_re = f_re * br - f_im * bi
    bb_im = f_re * bi + f_im * br
    b32_ref[...] = jnp.zeros_like(b32_ref)
    cr32_ref[...] = jnp.zeros_like(cr32_ref)
    ci32_ref[...] = jnp.zeros_like(ci32_ref)
    for g in range(groups):
        s, gl = divmod(g, S5_OCT)
        rows = slice(gl * k_in, (gl + 1) * k_in)
        cols = slice(gl * p, (gl + 1) * p)
        b32_ref[s, rows, cols] = bb_re[g]
        b32_ref[s, rows, half + gl * p:half + (gl + 1) * p] = bb_im[g]
        cr32_ref[s, cols, rows] = ctr_ref[g]
        ci32_ref[s, cols, rows] = cti_ref[g]
    bblk_ref[...] = b32_ref[...].astype(BF16)
    cre_ref[...] = cr32_ref[...].astype(BF16)
    cim_ref[...] = ci32_ref[...].astype(BF16)
    lam_re, lam_im = _s5_lambda(ard_ref[...], aid_ref[...], ldd_ref[...])
    lam_ref[0] = lam_re
    lam_ref[1] = lam_im
    n_oct = lam_re.shape[0]

    def both(cr, ci, s, rows):
        return jnp.concatenate([jnp.broadcast_to(cr[s:s + 1, :], (rows, half)),
                                jnp.broadcast_to(ci[s:s + 1, :], (rows, half))], axis=1)

    cr, ci = lam_re, lam_im
    for t in range(t_len):
        for s in range(n_oct):
            pw32_ref[s, t * ROWS:(t + 1) * ROWS, :] = both(cr, ci, s, ROWS)
        if t + 1 < t_len:
            cr, ci = _cmul(cr, ci, lam_re, lam_im)
    pwb_ref[...] = pw32_ref[...].astype(BF16)
    tr, ti = cr, ci
    ltp_ref[...] = jnp.zeros_like(ltp_ref)
    cr, ci = jnp.ones_like(lam_re), jnp.zeros_like(lam_im)
    for m in range(n_chunks + 1):
        for s in range(n_oct):
            if m < n_chunks:
                ltp_ref[s, n_chunks + m:n_chunks + m + 1, :] = both(cr, ci, s, 1)
            if m in (1, 2, 4, 8):
                i = (1, 2, 4, 8).index(m)
                ltp_ref[s, i:i + 1, :] = both(cr, ci, s, 1)
        cr, ci = _cmul(cr, ci, tr, ti)


def _s5_prep(a_re, a_im, log_dt, b_re, b_im, c_re, c_im, t_len, n_chunks):
    g, p = a_re.shape
    k = b_re.shape[-1]
    assert n_chunks == ROWS
    n_oct = g // S5_OCT
    half = S5_OCT * p
    dense = (n_oct, half)
    return pl.pallas_call(
        functools.partial(_s5_prep_body, t_len, n_chunks),
        out_shape=[jax.ShapeDtypeStruct((n_oct, S5_OCT * k, 2 * half), BF16),
                   jax.ShapeDtypeStruct((n_oct, half, S5_OCT * k), BF16),
                   jax.ShapeDtypeStruct((n_oct, half, S5_OCT * k), BF16),
                   jax.ShapeDtypeStruct((2,) + dense, F32),
                   jax.ShapeDtypeStruct((n_oct, t_len * ROWS, 2 * half), BF16),
                   jax.ShapeDtypeStruct((n_oct, 2 * n_chunks, 2 * half), F32)],
        scratch_shapes=[pltpu.VMEM((n_oct, S5_OCT * k, 2 * half), F32),
                        pltpu.VMEM((n_oct, half, S5_OCT * k), F32),
                        pltpu.VMEM((n_oct, half, S5_OCT * k), F32),
                        pltpu.VMEM((n_oct, t_len * ROWS, 2 * half), F32)],
        compiler_params=pltpu.CompilerParams(vmem_limit_bytes=VMEM_LIMIT),
        name="s5_prep",
    )(a_re.reshape(g, 1, p), a_im.reshape(g, 1, p), log_dt.reshape(g, 1, 1),
      jnp.swapaxes(b_re, 1, 2), jnp.swapaxes(b_im, 1, 2), jnp.swapaxes(c_re, 1, 2), jnp.swapaxes(c_im, 1, 2),
      a_re.reshape(dense), a_im.reshape(dense), jnp.broadcast_to(log_dt[:, None], (g, p)).reshape(dense))


def _gelu_glu_out(x, mod, y, u, dskip, glua_ref, glub_ref):
    sb, rows, d = x.shape
    y = y + dskip * u
    yg = jax.nn.gelu(y, approximate=True).astype(BF16)
    out = _dot(yg, glua_ref[...]) * jax.nn.sigmoid(_dot(yg, glub_ref[...]))
    return x + _gate(mod, 0, d) * out.reshape(sb, rows, d)


def _mix_odd_prompt_body(x_ref, xn_ref, mod_ref, gn_ref, bblk_ref, lam_ref, pwb_ref, ltp_ref, cre_ref, cim_ref,
                         dskip_ref, glua_ref, glub_ref, o_ref, hre_ref, him_ref,
                         un_ref, l0_ref, l1_ref, up0_ref, up1_ref, pt_ref):
    step = pl.program_id(0)
    sb, _, d = x_ref.shape
    tm = sb * ROWS
    t_len = tm // ROWS
    n_oct = bblk_ref.shape[0]
    half = bblk_ref.shape[2] // 2
    mod = mod_ref[...]

    def permuted_input(src_ref, up_dst):
        u = _modulate(src_ref[...], mod, gn_ref[...], 0).reshape(tm, d)
        for k in range(d // 128):
            un_ref[k] = u[:, k * 128:(k + 1) * 128]
        up = jnp.concatenate(
            [jnp.concatenate([un_ref[k, pl.ds(t, ROWS, stride=t_len), :] for k in range(d // 128)], axis=1)
             for t in range(t_len)], axis=0)
        up_dst[...] = up
        return up.astype(BF16)

    @pl.when(step == 0)
    def _():
        hre_ref[...] = jnp.zeros_like(hre_ref)
        him_ref[...] = jnp.zeros_like(him_ref)
        nat = lax.broadcasted_iota(jnp.int32, (tm, tm), 0)
        prm = lax.broadcasted_iota(jnp.int32, (tm, tm), 1)
        pt_ref[...] = jnp.where(prm == (nat % t_len) * ROWS + nat // t_len, 1.0, 0.0).astype(BF16)
        up16 = permuted_input(x_ref, up0_ref)
        for s in range(n_oct):
            l0_ref[s] = _dot(up16[:, s * 128:(s + 1) * 128], bblk_ref[s])

    def run(l_ref, l_next, up_ref, up_next):
        for s0 in range(0, n_oct, 2):
            pair = (s0, s0 + 1)
            lam_re = [jnp.broadcast_to(lam_ref[0, s:s + 1, :], (ROWS, half)) for s in pair]
            lam_im = [jnp.broadcast_to(lam_ref[1, s:s + 1, :], (ROWS, half)) for s in pair]

            def local_step(t, carry):
                r = pl.multiple_of(t * ROWS, ROWS)
                out = []
                for i, s in enumerate(pair):
                    h_re, h_im = carry[2 * i], carry[2 * i + 1]
                    n_re = lam_re[i] * h_re - lam_im[i] * h_im + l_ref[s, pl.ds(r, ROWS), 0:half]
                    n_im = lam_re[i] * h_im + lam_im[i] * h_re + l_ref[s, pl.ds(r, ROWS), half:2 * half]
                    l_ref[s, pl.ds(r, ROWS), 0:half] = n_re
                    l_ref[s, pl.ds(r, ROWS), half:2 * half] = n_im
                    out += [n_re, n_im]
                return tuple(out)

            zero = jnp.zeros((ROWS, half), F32)
            lax.fori_loop(0, t_len, local_step, (zero, zero, zero, zero), unroll=2)

        next16 = permuted_input(xn_ref, up_next)
        chunk = lax.broadcasted_iota(jnp.int32, (ROWS, half), 0)
        y_parts = []
        for s in range(n_oct):
            l_next[s] = _dot(next16[:, s * 128:(s + 1) * 128], bblk_ref[s])
            p_re = l_ref[s, tm - ROWS:tm, 0:half]
            p_im = l_ref[s, tm - ROWS:tm, half:2 * half]
            for i, sh in enumerate((1, 2, 4)):
                m_re, m_im = _cmul(ltp_ref[s, i:i + 1, 0:half], ltp_ref[s, i:i + 1, half:2 * half],
                                   jnp.where(chunk >= sh, pltpu.roll(p_re, sh, 0), 0.0),
                                   jnp.where(chunk >= sh, pltpu.roll(p_im, sh, 0), 0.0))
                p_re, p_im = p_re + m_re, p_im + m_im
            hin_re = jnp.broadcast_to(hre_ref[s:s + 1, :], (ROWS, half))
            hin_im = jnp.broadcast_to(him_ref[s:s + 1, :], (ROWS, half))
            m_re, m_im = _cmul(ltp_ref[s, ROWS:2 * ROWS, 0:half], ltp_ref[s, ROWS:2 * ROWS, half:2 * half],
                               hin_re, hin_im)
            st_re = m_re + jnp.where(chunk >= 1, pltpu.roll(p_re, 1, 0), 0.0)
            st_im = m_im + jnp.where(chunk >= 1, pltpu.roll(p_im, 1, 0), 0.0)
            m_re, m_im = _cmul(ltp_ref[s, 3:4, 0:half], ltp_ref[s, 3:4, half:2 * half],
                               hre_ref[s:s + 1, :], him_ref[s:s + 1, :])
            hre_ref[s:s + 1, :] = m_re + p_re[ROWS - 1:ROWS, :]
            him_ref[s:s + 1, :] = m_im + p_im[ROWS - 1:ROWS, :]
            loc = l_ref[s].astype(BF16).reshape(tm // BF16_ROWS, BF16_ROWS, 2 * half)
            pw = pwb_ref[s].reshape(tm // BF16_ROWS, BF16_ROWS, 2 * half)
            pair_re = jnp.concatenate([st_re, st_re], axis=0).astype(BF16)[None]
            pair_im = jnp.concatenate([st_im, st_im], axis=0).astype(BF16)[None]
            f_re, f_im = _cmul(pw[:, :, 0:half], pw[:, :, half:2 * half], pair_re, pair_im)
            hs_re = (loc[:, :, 0:half] + f_re).reshape(tm, half)
            hs_im = (loc[:, :, half:2 * half] + f_im).reshape(tm, half)
            y_parts.append(_dot(hs_re, cre_ref[s]) - _dot(hs_im, cim_ref[s]))

        y = jnp.concatenate(y_parts, axis=1) + dskip_ref[...] * up_ref[...]
        yg = jax.nn.gelu(y, approximate=True).astype(BF16)
        yn = _dot(pt_ref[...], yg).astype(BF16)
        out = _dot(yn, glua_ref[...]) * jax.nn.sigmoid(_dot(yn, glub_ref[...]))
        o_ref[...] = x_ref[...] + _gate(mod, 0, d) * out.reshape(sb, ROWS, d)

    @pl.when(step % 2 == 0)
    def _():
        run(l0_ref, l1_ref, up0_ref, up1_ref)

    @pl.when(step % 2 == 1)
    def _():
        run(l1_ref, l0_ref, up1_ref, up0_ref)


def _mix_odd_prompt(x3, mod, layer, prompt_row, gain, bblk, lam_d, pwb, ltp, cre, cim, dskip, glu_a, glu_b, sb):
    n8, _, d = x3.shape
    n_oct, kin, wid = bblk.shape
    half = wid // 2
    tm = sb * ROWS
    n_tiles = n8 // sb
    proj_buf = pltpu.VMEM((n_oct, tm, wid), F32)
    perm_buf = pltpu.VMEM((tm, d), F32)
    return pl.pallas_call(
        _mix_odd_prompt_body,
        grid=(n_tiles,),
        in_specs=[pl.BlockSpec((sb, ROWS, d), lambda i: (i, 0, 0)),
                  pl.BlockSpec((sb, ROWS, d), lambda i: (jnp.minimum(i + 1, n_tiles - 1), 0, 0)),
                  _mod_row_spec(mod, layer, prompt_row),
                  _const_spec((1, d)), _const_spec(bblk.shape), _const_spec(lam_d.shape),
                  _const_spec(pwb.shape), _const_spec(ltp.shape),
                  _const_spec(cre.shape), _const_spec(cim.shape), _const_spec((1, d)),
                  _const_spec((d, d)), _const_spec((d, d))],
        out_specs=[pl.BlockSpec((sb, ROWS, d), lambda i: (i, 0, 0)),
                   pl.BlockSpec((n_oct, half), lambda i: (0, 0)),
                   pl.BlockSpec((n_oct, half), lambda i: (0, 0))],
        out_shape=[jax.ShapeDtypeStruct(x3.shape, F32),
                   jax.ShapeDtypeStruct((n_oct, half), F32),
                   jax.ShapeDtypeStruct((n_oct, half), F32)],
        scratch_shapes=[pltpu.VMEM((d // 128, tm, 128), F32), proj_buf, proj_buf, perm_buf, perm_buf,
                        pltpu.VMEM((tm, tm), BF16)],
        compiler_params=_params(),
        name="mix_odd_prompt",
    )(x3, x3, mod, gain.reshape(1, d), bblk, lam_d, pwb, ltp, cre, cim, dskip.reshape(1, d), glu_a, glu_b)


def _mix_odd_sample_body(x_ref, mod_ref, gn_ref, bblk_ref, lam_ref, cre_ref, cim_ref, dskip_ref,
                         glua_ref, glub_ref, sre_ref, sim_ref, o_ref, nre_ref, nim_ref, d_ref, y_ref):
    x = x_ref[...]
    sb, length, d = x.shape
    tm = sb * length
    n_oct = bblk_ref.shape[0]
    half = bblk_ref.shape[2] // 2
    mod = mod_ref[...]
    u = _modulate(x, mod, gn_ref[...], 0).reshape(tm, d)
    u16 = u.astype(BF16)
    n_ch = bblk_ref.shape[2] // 128
    hc = n_ch // 2
    for s in range(n_oct):
        bu = _dot(u16[:, s * 128:(s + 1) * 128], bblk_ref[s])
        for c in range(n_ch):
            d_ref[c] = bu[:, c * 128:(c + 1) * 128]
        lam_re = lam_ref[0, s:s + 1, :]
        lam_im = lam_ref[1, s:s + 1, :]
        h_re = sre_ref[:, s * half:(s + 1) * half]
        h_im = sim_ref[:, s * half:(s + 1) * half]
        for t in range(length):
            b_re = jnp.concatenate([d_ref[c, pl.ds(t, sb, stride=length), :] for c in range(hc)], axis=1)
            b_im = jnp.concatenate([d_ref[hc + c, pl.ds(t, sb, stride=length), :] for c in range(hc)], axis=1)
            n_re = lam_re * h_re - lam_im * h_im + b_re
            n_im = lam_re * h_im + lam_im * h_re + b_im
            for c in range(hc):
                d_ref[c, pl.ds(t, sb, stride=length), :] = n_re[:, c * 128:(c + 1) * 128]
                d_ref[hc + c, pl.ds(t, sb, stride=length), :] = n_im[:, c * 128:(c + 1) * 128]
            h_re, h_im = n_re, n_im
        nre_ref[:, s * half:(s + 1) * half] = h_re
        nim_ref[:, s * half:(s + 1) * half] = h_im
        hs_re = jnp.concatenate([d_ref[c] for c in range(hc)], axis=1)
        hs_im = jnp.concatenate([d_ref[hc + c] for c in range(hc)], axis=1)
        y_ref[:, s * 128:(s + 1) * 128] = (_dot(hs_re.astype(BF16), cre_ref[s])
                                           - _dot(hs_im.astype(BF16), cim_ref[s]))
    o_ref[...] = _gelu_glu_out(x, mod, y_ref[...], u, dskip_ref[...], glua_ref, glub_ref)


def _mix_odd_sample(x3, mod, layer, gain, bblk, lam_d, cre, cim, dskip, glu_a, glu_b, s_re, s_im, sb):
    n, length, d = x3.shape
    n_oct, kin, wid = bblk.shape
    tm = sb * length
    nstate = s_re.shape[1]
    return pl.pallas_call(
        _mix_odd_sample_body,
        grid=(n // sb,),
        in_specs=[pl.BlockSpec((sb, length, d), lambda i: (i, 0, 0)),
                  _mod_seq_spec(mod, layer, sb),
                  _const_spec((1, d)), _const_spec(bblk.shape), _const_spec(lam_d.shape),
                  _const_spec(cre.shape), _const_spec(cim.shape), _const_spec((1, d)),
                  _const_spec((d, d)), _const_spec((d, d)),
                  pl.BlockSpec((sb, nstate), lambda i: (i, 0)),
                  pl.BlockSpec((sb, nstate), lambda i: (i, 0))],
        out_specs=[pl.BlockSpec((sb, length, d), lambda i: (i, 0, 0)),
                   pl.BlockSpec((sb, nstate), lambda i: (i, 0)),
                   pl.BlockSpec((sb, nstate), lambda i: (i, 0))],
        out_shape=[jax.ShapeDtypeStruct(x3.shape, F32),
                   jax.ShapeDtypeStruct(s_re.shape, F32),
                   jax.ShapeDtypeStruct(s_im.shape, F32)],
        scratch_shapes=[pltpu.VMEM((wid // 128, tm, 128), F32), pltpu.VMEM((tm, d), F32)],
        compiler_params=_params(),
        name="mix_odd_sample",
    )(x3, mod, gain.reshape(1, d), bblk, lam_d, cre, cim, dskip.reshape(1, d), glu_a, glu_b, s_re, s_im)


def _pick(n, want):
    while n % want:
        want //= 2
    return max(want, 1)


def kernel(x_prompt, x_sample, cache_win_k, cache_win_v, state_ret, state_s5_re, state_s5_im, c_prompt, c_sample, ada_w, ada_b, norm_mix, norm_ffn, ffn_wg, ffn_wu, ffn_wd, even_w_in, even_q_gain, even_k_gain, even_sinks, even_ret_gain, even_w_out, odd_A_re, odd_A_im, odd_log_dt, odd_B_re, odd_B_im, odd_C_re, odd_C_im, odd_D, odd_glu_a, odd_glu_b):
    bp, lp, d = x_prompt.shape
    ns, ls, _ = x_sample.shape
    assert bp == 1 and ls == ROWS and lp % WINDOW == 0
    w = cache_win_k.shape[2]
    groups, p_state = odd_A_re.shape[1:]

    n_c = bp + ns
    n_pad = -n_c % ROWS
    c_all = jnp.concatenate([c_sample, c_prompt, jnp.zeros((n_pad, d), F32)], axis=0)
    mod = _adaln(c_all, ada_w, ada_b)

    bf = lambda t: t.astype(BF16)
    w_in, w_out = bf(even_w_in[0]), bf(even_w_out[0])

    xp = x_prompt.reshape(lp // ROWS, ROWS, d)
    xs = x_sample

    sb_p = _pick(lp // ROWS, 32)
    sb_ffn = _pick(lp // ROWS, 64)
    sb_s = _pick(ns, 32)
    sb_s_even = _pick(ns, 16)

    (xp, p_k, p_v, p_ret), ffn0 = _mix_even_prompt(xp, mod, 0, ns, norm_mix[0], w_in, even_q_gain[0], even_k_gain[0],
                                                   even_sinks[0], even_ret_gain[0], w_out, sb_ffn,
                                                   cast=((ffn_wg, 0), (ffn_wu, 0), (ffn_wd, 0)))
    xs, s_k, s_v, s_ret = _mix_even_sample(xs, mod, 0, norm_mix[0], w_in, even_q_gain[0], even_k_gain[0],
                                           even_sinks[0], even_ret_gain[0], w_out,
                                           cache_win_k[0].reshape(ns, w, KA_W), cache_win_v[0].reshape(ns, w, VA_W),
                                           state_ret, sb_s_even)
    xp, xs, (wg1, wu1, wd1, glu_a, glu_b) = _ffn(
        xp, xs, mod, 0, norm_ffn[0], *ffn0, sb_ffn,
        cast=((ffn_wg, 1), (ffn_wu, 1), (ffn_wd, 1), (odd_glu_a, 0), (odd_glu_b, 0)))

    t_len = sb_p
    bblk, cre, cim, lam_d, pwb, ltp = _s5_prep(odd_A_re[0], odd_A_im[0], odd_log_dt[0], odd_B_re[0], odd_B_im[0],
                                               odd_C_re[0], odd_C_im[0], t_len, ROWS)
    xp, p_re, p_im = _mix_odd_prompt(xp, mod, 1, ns, norm_mix[1], bblk, lam_d, pwb, ltp, cre, cim, odd_D[0],
                                     glu_a, glu_b, sb_p)
    xs, s_re, s_im = _mix_odd_sample(xs, mod, 1, norm_mix[1], bblk, lam_d, cre, cim, odd_D[0], glu_a, glu_b,
                                     state_s5_re[0].reshape(ns, groups * p_state),
                                     state_s5_im[0].reshape(ns, groups * p_state), sb_s)
    xp, xs, _ = _ffn(xp, xs, mod, 1, norm_ffn[1], wg1, wu1, wd1, sb_ffn)

    y_prompt = xp.reshape(bp, lp, d)
    y_sample = xs
    return (y_prompt, y_sample,
            p_k.reshape(1, bp, WINDOW, A_KV, A_HD), p_v.reshape(1, bp, WINDOW, A_KV, A_HD),
            p_ret.reshape(1, bp, B_HEADS, B_KD, B_VD),
            p_re.reshape(1, bp, groups, p_state), p_im.reshape(1, bp, groups, p_state),
            s_k.reshape(1, ns, w, A_KV, A_HD), s_v.reshape(1, ns, w, A_KV, A_HD),
            s_ret.reshape(1, ns, B_HEADS, B_KD, B_VD),
            s_re.reshape(1, ns, groups, p_state), s_im.reshape(1, ns, groups, p_state))
```

```python
import functools
import math

import jax
import jax.numpy as jnp
from jax import lax
from jax.experimental import pallas as pl
from jax.experimental.pallas import tpu as pltpu

F32 = jnp.float32
BF16 = jnp.bfloat16

EPS = 1e-6
NEG_INF = -1e30
ROWS = 8

A_HEADS, A_KV, A_GROUP, A_HD = 8, 2, 4, 64
WINDOW = 128
B_HEADS, B_KD, B_VD = 4, 128, 128
S5_GROUP, S5_STATE = 16, 64
S5_OCT = 8

QA_W, KA_W, VA_W = A_HEADS * A_HD, A_KV * A_HD, A_KV * A_HD
QB_W, KB_W, VB_W, GB_W = B_HEADS * B_KD, B_HEADS * B_KD, B_HEADS * B_VD, B_HEADS * B_VD
OFF_QA = 0
OFF_KA = OFF_QA + QA_W
OFF_VA = OFF_KA + KA_W
OFF_QB = OFF_VA + VA_W
OFF_KB = OFF_QB + QB_W
OFF_VB = OFF_KB + KB_W
OFF_GB = OFF_VB + VB_W

VMEM_LIMIT = 56 * 1024 * 1024


def _ret_log_gamma(h):
    return math.log1p(-(2.0 ** (-5.0 - h)))


def _alibi_slope(h):
    return 2.0 ** (-8.0 * (h + 1) / A_HEADS)


def _const_spec(shape):
    nd = len(shape)
    return pl.BlockSpec(shape, lambda i, _n=nd: (0,) * _n, pipeline_mode=pl.Buffered(1))


def _params():
    return pltpu.CompilerParams(dimension_semantics=("arbitrary",), vmem_limit_bytes=VMEM_LIMIT)


def _dot(a, b):
    return jnp.dot(a, b, preferred_element_type=F32)


def _dot_nt(a, b):
    return lax.dot_general(a, b, (((1,), (1,)), ((), ())), preferred_element_type=F32)


def _bmm(a, b):
    return lax.dot_general(a, b, (((2,), (1,)), ((0,), (0,))), preferred_element_type=F32)


def _bmm_nt(a, b):
    return lax.dot_general(a, b, (((2,), (2,)), ((0,), (0,))), preferred_element_type=F32)


def _bmm_tn(a, b):
    return lax.dot_general(a, b, (((1,), (1,)), ((0,), (0,))), preferred_element_type=F32)


def _rms(x, g):
    return x * lax.rsqrt(jnp.mean(x * x, axis=-1, keepdims=True) + EPS) * g


def _modulate(x3, mod, gain, which):
    d = x3.shape[-1]
    sh = mod[:, :, (3 * which) * d:(3 * which + 1) * d]
    sc = mod[:, :, (3 * which + 1) * d:(3 * which + 2) * d]
    return _rms(x3, gain) * (1.0 + sc) + sh


def _mod_row_spec(mod, layer, row):
    assert row % ROWS == 0
    return pl.BlockSpec((None, ROWS, mod.shape[-1]), lambda i: (layer, row // ROWS, 0))


def _mod_seq_spec(mod, layer, sb, first_step=0):
    return pl.BlockSpec((None, sb, mod.shape[-1]), lambda i: (layer, jnp.maximum(i - first_step, 0), 0))


def _row_mod(mod_ref):
    return mod_ref[0:1, :][:, None, :]


def _seq_mod(mod_ref):
    return mod_ref[...][:, None, :]


def _gate(mod, which, d):
    return mod[:, :, (3 * which + 2) * d:(3 * which + 3) * d]


def _adaln_body(c_ref, w_ref, b_ref, o_ref):
    c = c_ref[...]
    a = (c * jax.nn.sigmoid(c)).astype(BF16)
    o_ref[0] = _dot(a, w_ref[0].astype(BF16)) + b_ref[0]


def _adaln(c_all, ada_w, ada_b):
    depth, d, n = ada_w.shape
    r = c_all.shape[0]
    tn = 1536
    return pl.pallas_call(
        _adaln_body,
        grid=(depth, n // tn),
        in_specs=[pl.BlockSpec((r, d), lambda l, j: (0, 0)),
                  pl.BlockSpec((1, d, tn), lambda l, j: (l, 0, j)),
                  pl.BlockSpec((1, 1, tn), lambda l, j: (l, 0, j))],
        out_specs=pl.BlockSpec((1, r, tn), lambda l, j: (l, 0, j)),
        out_shape=jax.ShapeDtypeStruct((depth, r, n), F32),
        compiler_params=pltpu.CompilerParams(dimension_semantics=("arbitrary", "arbitrary"),
                                             vmem_limit_bytes=VMEM_LIMIT),
        name="adaln",
    )(c_all, ada_w, ada_b.reshape(depth, 1, n))


BF16_ROWS = 16


def _cast_specs(job, n_steps):
    w, layer = job
    _, rows, cols = w.shape
    hold = 1
    while rows % (n_steps // hold) or (rows // (n_steps // hold)) % BF16_ROWS:
        hold *= 2
        assert hold <= n_steps and n_steps % hold == 0
    blk = rows // (n_steps // hold)
    src = pl.BlockSpec((None, blk, cols),
                       lambda i, _h=hold, _l=layer: (_l, jnp.minimum(i, n_steps - 1) // _h, 0))
    dst = pl.BlockSpec((blk, cols), lambda i, _h=hold: (jnp.minimum(i, n_steps - 1) // _h, 0))
    return src, dst, jax.ShapeDtypeStruct((rows, cols), BF16)


def _cast_blocks(in_refs, out_refs):
    for src, dst in zip(in_refs, out_refs):
        dst[...] = src[...].astype(BF16)


def _ffn_body(n_cast, n_prompt, xp_ref, mp_ref, xs_ref, ms_ref, gn_ref, wg_ref, wu_ref, wd_ref, *refs):
    cast_in, (op_ref, os_ref), cast_out = refs[:n_cast], refs[n_cast:n_cast + 2], refs[n_cast + 2:]
    step = pl.program_id(0)

    def tile(x_ref, mod, o_ref):
        x = x_ref[...]
        sb, _, d = x.shape
        h = _modulate(x, mod, gn_ref[...], 1).reshape(sb * ROWS, d).astype(BF16)
        a = _dot(h, wg_ref[...])
        b = _dot(h, wu_ref[...])
        act = (a * jax.nn.sigmoid(a) * b).astype(BF16)
        y = _dot(act, wd_ref[...])
        o_ref[...] = x + _gate(mod, 1, d) * y.reshape(sb, ROWS, d)

    @pl.when(step < n_prompt)
    def _():
        tile(xp_ref, _row_mod(mp_ref), op_ref)

    @pl.when(step >= n_prompt)
    def _():
        tile(xs_ref, _seq_mod(ms_ref), os_ref)

    _cast_blocks(cast_in, cast_out)


def _ffn(xp3, xs3, mod, layer, gain, wg, wu, wd, sb, cast=()):
    n8, _, d = xp3.shape
    f = wg.shape[1]
    n_prompt = n8 // sb
    n_sample = xs3.shape[0] // sb
    prompt_row = xs3.shape[0]
    assert n8 % sb == 0 and xs3.shape[0] % sb == 0
    cast_specs = [_cast_specs(w, n_prompt) for w in cast]
    p_idx = lambda i: (jnp.minimum(i, n_prompt - 1), 0, 0)
    s_idx = lambda i: (jnp.maximum(i - n_prompt, 0), 0, 0)
    xp_spec = pl.BlockSpec((sb, ROWS, d), p_idx)
    xs_spec = pl.BlockSpec((sb, ROWS, d), s_idx)
    out = pl.pallas_call(
        functools.partial(_ffn_body, len(cast), n_prompt),
        grid=(n_prompt + n_sample,),
        in_specs=[xp_spec, _mod_row_spec(mod, layer, prompt_row),
                  xs_spec, _mod_seq_spec(mod, layer, sb, n_prompt),
                  _const_spec((1, d)), _const_spec((d, f)), _const_spec((d, f)),
                  _const_spec((f, d))] + [c[0] for c in cast_specs],
        out_specs=[xp_spec, xs_spec] + [c[1] for c in cast_specs],
        out_shape=[jax.ShapeDtypeStruct(xp3.shape, F32), jax.ShapeDtypeStruct(xs3.shape, F32)]
        + [c[2] for c in cast_specs],
        compiler_params=_params(),
        name="ffn",
    )(xp3, mod, xs3, mod, gain.reshape(1, d), wg, wu, wd, *[w for w, _ in cast])
    return out[0], out[1], tuple(out[2:])


def _head_rms(t, g):
    return t * lax.rsqrt(jnp.mean(t * t, axis=-1, keepdims=True) + EPS) * g


def _group_norm_gate(o, gain, gate):
    mu = jnp.mean(o, axis=-1, keepdims=True)
    var = jnp.mean(jnp.square(o - mu), axis=-1, keepdims=True)
    return (o - mu) * lax.rsqrt(var + EPS) * gain * (gate * jax.nn.sigmoid(gate))


def _ret_decay(hb, c):
    lg = _ret_log_gamma(hb)
    ii = lax.broadcasted_iota(jnp.int32, (c, c), 0)
    jj = lax.broadcasted_iota(jnp.int32, (c, c), 1)
    diff = (ii - jj).astype(F32)
    d_in = jnp.where(diff >= 0, jnp.exp(lg * jnp.maximum(diff, 0.0)), 0.0)
    row = lax.broadcasted_iota(jnp.int32, (c, B_KD), 0).astype(F32)
    d_q = jnp.exp(lg * (row + 1.0))
    d_k = jnp.exp(lg * (c - 1.0 - row))
    d_c = math.exp(lg * c)
    return d_in, d_q, d_k, d_c


P_OFF_KA = QA_W
P_OFF_VA = P_OFF_KA + 2 * KA_W
P_OFF_QB = P_OFF_VA + 2 * VA_W
P_OFF_VB = P_OFF_QB + QB_W
P_OFF_GB = P_OFF_VB + VB_W
P_WIDTH = P_OFF_GB + GB_W
PAIR = 2 * A_HD


def _dedup(t):
    low = lax.broadcasted_iota(jnp.int32, (t.shape[0], PAIR), 1) < A_HD
    return jnp.where(low, t[:, 0:PAIR], t[:, PAIR:2 * PAIR])


def _mix_even_prompt_body(n_cast, x_ref, mod_ref, gn_ref, win_ref, wkt_ref, qg_ref, kg_ref, sink_ref, rg_ref, wout_ref,
                          *refs):
    cast_in, refs = refs[:n_cast], refs[n_cast:]
    o_ref, pk_ref, pv_ref, ps_ref = refs[:4]
    cast_out = refs[4:4 + n_cast]
    mix_ref, carry_ref, bias_ref, dec_ref, ones_ref = refs[4 + n_cast:]
    _cast_blocks(cast_in, cast_out)
    step = pl.program_id(0)
    blk = WINDOW
    rows4 = A_GROUP * blk

    @pl.when(step == 0)
    def _():
        carry_ref[...] = jnp.zeros_like(carry_ref)
        ps_ref[...] = jnp.zeros_like(ps_ref)
        er = lax.broadcasted_iota(jnp.int32, ones_ref.shape, 0) // A_HD
        ec = lax.broadcasted_iota(jnp.int32, ones_ref.shape, 1) // A_HD
        ones_ref[...] = jnp.where(er == ec, 1.0 / A_HD, 0.0).astype(BF16)
        row = lax.broadcasted_iota(jnp.int32, (rows4, 2 * blk), 0)
        dist = row % blk + blk - lax.broadcasted_iota(jnp.int32, (rows4, 2 * blk), 1)
        in_window = (dist >= 0) & (dist < WINDOW)
        for kv in range(A_KV):
            slope = jnp.zeros((rows4, 2 * blk), F32)
            for g in range(A_GROUP):
                slope = jnp.where(row // blk == g, _alibi_slope(kv * A_GROUP + g), slope)
            bias_ref[kv] = jnp.where(in_window, slope * dist.astype(F32), -NEG_INF)
        for hb in range(B_HEADS):
            lg = _ret_log_gamma(hb)
            ii = lax.broadcasted_iota(jnp.int32, (blk, blk), 0).astype(F32)
            jj = lax.broadcasted_iota(jnp.int32, (blk, blk), 1).astype(F32)
            diff = ii - jj
            dec_ref[hb, 0] = jnp.where(diff >= 0, jnp.exp(lg * jnp.maximum(diff, 0.0)), 0.0)
            dec_ref[hb, 1] = jnp.exp(lg * (ii + 1.0))
            dec_ref[hb, 2] = jnp.exp(lg * (blk - 1.0 - jj))

    x = x_ref[...]
    sb, _, d = x.shape
    tb = sb * ROWS
    mod = _row_mod(mod_ref)
    h = _modulate(x, mod, gn_ref[...], 0).reshape(tb, d).astype(BF16)
    proj = _dot(h, win_ref[...])
    kt_all = _dot_nt(wkt_ref[...], h) * (B_KD ** -0.5)
    rg = rg_ref[...]

    qa = proj[:, 0:QA_W]
    ka = proj[:, P_OFF_KA:P_OFF_KA + 2 * KA_W]
    va = proj[:, P_OFF_VA:P_OFF_VA + 2 * VA_W]
    q_hat = (qa * lax.rsqrt(_dot((qa * qa).astype(BF16), ones_ref[...]) + EPS) * qg_ref[...]).astype(BF16)
    k_hat = ka * lax.rsqrt(_dot((ka * ka).astype(BF16), ones_ref[0:2 * KA_W, 0:2 * KA_W]) + EPS) * kg_ref[...]
    k_hat16 = k_hat.astype(BF16)
    va16 = va.astype(BF16)
    prev = carry_ref[step % 2]
    carry_ref[(step + 1) % 2] = jnp.concatenate([k_hat16[tb - blk:tb], va16[tb - blk:tb]], axis=1)

    row_g = lax.broadcasted_iota(jnp.int32, (rows4, 1), 0) // blk
    key_is_prev = lax.broadcasted_iota(jnp.int32, (rows4, 2 * blk), 1) < blk
    first_penalty = jnp.where(step == 0, -NEG_INF, 0.0)
    lane_low = lax.broadcasted_iota(jnp.int32, (blk, PAIR), 1) < A_HD
    ones_cols = jnp.ones((2 * blk, PAIR), BF16)

    n_blk = tb // blk
    att = [(j, kv) for j in range(n_blk) for kv in range(A_KV)]
    ret = [(j, hb) for j in range(n_blk) for hb in range(B_HEADS)]

    sinks, scores = {}, {}
    for j, kv in att:
        r0 = j * blk
        kcol = slice(kv * PAIR, (kv + 1) * PAIR)
        if j == 0:
            k2 = jnp.concatenate([prev[:, kcol], k_hat16[0:blk, kcol]], axis=0)
        else:
            k2 = k_hat16[r0 - blk:r0 + blk, kcol]
        q4 = jnp.concatenate(
            [jnp.where(lane_low == (g % 2 == 0),
                       q_hat[r0:r0 + blk, (kv * A_GROUP + g - g % 2) * A_HD:(kv * A_GROUP + g - g % 2 + 2) * A_HD],
                       jnp.zeros((), BF16))
             for g in range(A_GROUP)], axis=0)
        scores[j, kv] = _dot_nt(q4, k2)
        sink = jnp.zeros((rows4, 1), F32)
        for g in range(A_GROUP):
            sink = jnp.where(row_g == g, sink_ref[kv * A_GROUP + g], sink)
        sinks[j, kv] = sink
    qb, vb, kt, inner = {}, {}, {}, {}
    for j, hb in ret:
        r0 = j * blk
        qb[j, hb] = proj[r0:r0 + blk, P_OFF_QB + hb * B_KD:P_OFF_QB + (hb + 1) * B_KD].astype(BF16)
        vb[j, hb] = proj[r0:r0 + blk, P_OFF_VB + hb * B_VD:P_OFF_VB + (hb + 1) * B_VD].astype(BF16)
        kt[j, hb] = kt_all[hb * B_KD:(hb + 1) * B_KD, r0:r0 + blk]
        inner[j, hb] = _dot(qb[j, hb], kt[j, hb].astype(BF16))

    probs, maxes = {}, {}
    for j, kv in att:
        s = scores[j, kv] * (A_HD ** -0.5) - bias_ref[kv]
        if j == 0:
            s = s - jnp.where(key_is_prev, first_penalty, 0.0)
        mx = jnp.maximum(jnp.max(s, axis=-1, keepdims=True), sinks[j, kv])
        probs[j, kv] = jnp.exp(s - mx).astype(BF16)
        maxes[j, kv] = mx
    state = {}
    for hb in range(B_HEADS):
        d_c = math.exp(_ret_log_gamma(hb) * blk)
        state[0, hb] = ps_ref[hb]
        for j in range(n_blk):
            state[j + 1, hb] = state[j, hb] * d_c + _dot((kt[j, hb] * dec_ref[hb, 2]).astype(BF16), vb[j, hb])
        ps_ref[hb] = state[n_blk, hb]

    for j, kv in att:
        r0 = j * blk
        kcol = slice(kv * PAIR, (kv + 1) * PAIR)
        vcol = slice(2 * KA_W + kv * PAIR, 2 * KA_W + (kv + 1) * PAIR)
        if j == 0:
            v2 = jnp.concatenate([prev[:, vcol], va16[0:blk, kcol]], axis=0)
        else:
            v2 = va16[r0 - blk:r0 + blk, kcol]
        pv = _dot(probs[j, kv], jnp.concatenate([v2, ones_cols], axis=1))
        o4 = pv[:, 0:PAIR] / (pv[:, PAIR:2 * PAIR] + jnp.exp(sinks[j, kv] - maxes[j, kv]))
        for g in range(A_GROUP):
            hd = kv * A_GROUP + g
            half = slice((hd % 2) * A_HD, (hd % 2 + 1) * A_HD)
            mix_ref[r0:r0 + blk, hd * A_HD:(hd + 1) * A_HD] = o4[g * blk:(g + 1) * blk, half]
    o_ret = {}
    for j, hb in ret:
        o_ret[j, hb] = (_dot((inner[j, hb] * dec_ref[hb, 0]).astype(BF16), vb[j, hb])
                        + _dot(qb[j, hb], state[j, hb].astype(BF16)) * dec_ref[hb, 1])

    cen = {k: o_ret[k] - jnp.mean(o_ret[k], axis=-1, keepdims=True) for k in ret}
    var = {k: jnp.mean(cen[k] * cen[k], axis=-1, keepdims=True) for k in ret}
    for j, hb in ret:
        r0 = j * blk
        gb = proj[r0:r0 + blk, P_OFF_GB + hb * B_VD:P_OFF_GB + (hb + 1) * B_VD]
        mix_ref[r0:r0 + blk, QA_W + hb * B_VD:QA_W + (hb + 1) * B_VD] = (
            cen[j, hb] * lax.rsqrt(var[j, hb] + EPS) * rg[:, hb * B_VD:(hb + 1) * B_VD] * (gb * jax.nn.sigmoid(gb)))

    out = _dot(mix_ref[...].astype(BF16), wout_ref[...])
    o_ref[...] = x + _gate(mod, 0, d) * out.reshape(sb, ROWS, d)

    @pl.when(step == pl.num_programs(0) - 1)
    def _():
        pk_ref[...] = _dedup(k_hat[tb - blk:tb, :])
        pv_ref[...] = _dedup(va[tb - blk:tb, :])


def _dup_heads(t, width):
    lead = t.shape[:-1]
    t = t.reshape(lead + (-1, 1, width))
    return jnp.broadcast_to(t, lead + (t.shape[-3], 2, width)).reshape(lead + (-1,))


def _mix_even_prompt(x3, mod, layer, prompt_row, gain, w_in, q_gain, k_gain, sinks, ret_gain, w_out, sb, cast=()):
    n8, _, d = x3.shape
    tb = sb * ROWS
    cast_specs = [_cast_specs(w, n8 // sb) for w in cast]
    w_main = jnp.concatenate([w_in[:, :OFF_KA], _dup_heads(w_in[:, OFF_KA:OFF_VA], A_HD),
                              _dup_heads(w_in[:, OFF_VA:OFF_QB], A_HD), w_in[:, OFF_QB:OFF_KB],
                              w_in[:, OFF_VB:]], axis=1)
    wk_t = w_in[:, OFF_KB:OFF_VB].T
    out = pl.pallas_call(
        functools.partial(_mix_even_prompt_body, len(cast)),
        grid=(n8 // sb,),
        in_specs=[pl.BlockSpec((sb, ROWS, d), lambda i: (i, 0, 0)),
                  _mod_row_spec(mod, layer, prompt_row),
                  _const_spec((1, d)), _const_spec((d, P_WIDTH)), _const_spec((KB_W, d)),
                  _const_spec((1, QA_W)), _const_spec((1, 2 * KA_W)),
                  pl.BlockSpec(memory_space=pltpu.SMEM),
                  _const_spec((1, VB_W)), _const_spec((QA_W + VB_W, d))] + [c[0] for c in cast_specs],
        out_specs=[pl.BlockSpec((sb, ROWS, d), lambda i: (i, 0, 0)),
                   pl.BlockSpec((WINDOW, KA_W), lambda i: (0, 0)),
                   pl.BlockSpec((WINDOW, VA_W), lambda i: (0, 0)),
                   pl.BlockSpec((B_HEADS, B_KD, B_VD), lambda i: (0, 0, 0))] + [c[1] for c in cast_specs],
        out_shape=[jax.ShapeDtypeStruct(x3.shape, F32),
                   jax.ShapeDtypeStruct((WINDOW, KA_W), F32),
                   jax.ShapeDtypeStruct((WINDOW, VA_W), F32),
                   jax.ShapeDtypeStruct((B_HEADS, B_KD, B_VD), F32)] + [c[2] for c in cast_specs],
        scratch_shapes=[pltpu.VMEM((tb, QA_W + VB_W), F32),
                        pltpu.VMEM((2, WINDOW, 2 * KA_W + 2 * VA_W), BF16),
                        pltpu.VMEM((A_KV, A_GROUP * WINDOW, 2 * WINDOW), F32),
                        pltpu.VMEM((B_HEADS, 3, WINDOW, WINDOW), F32),
                        pltpu.VMEM((QA_W, QA_W), BF16)],
        compiler_params=_params(),
        name="mix_even_prompt",
    )(x3, mod, gain.reshape(1, d), w_main, wk_t, jnp.tile(q_gain, A_HEADS).reshape(1, QA_W),
      jnp.tile(k_gain, 2 * A_KV).reshape(1, 2 * KA_W),
      sinks, ret_gain.reshape(1, VB_W), w_out, *[w for w, _ in cast])
    return out[:4], tuple(out[4:])


def _mix_even_sample_body(x_ref, mod_ref, gn_ref, win_ref, qg_ref, kg_ref, sink_ref, rg_ref, wout_ref,
                          ck_ref, cv_ref, s0_ref,
                          o_ref, nk_ref, nv_ref, ns_ref, mix_ref):
    x = x_ref[...]
    sb, length, d = x.shape
    tb = sb * length
    w = ck_ref.shape[1]
    mod = _seq_mod(mod_ref)
    h = _modulate(x, mod, gn_ref[...], 0).reshape(tb, d).astype(BF16)
    proj = _dot(h, win_ref[...])
    qg = qg_ref[...]
    kg = kg_ref[...]
    rg = rg_ref[...]

    rows = A_GROUP * length
    qpos_c = lax.broadcasted_iota(jnp.int32, (rows, w), 0) % length
    kpos_c = lax.broadcasted_iota(jnp.int32, (rows, w), 1)
    dist_c = w + qpos_c - kpos_c
    valid_c = (dist_c >= 0) & (dist_c < WINDOW)
    qpos_n = lax.broadcasted_iota(jnp.int32, (rows, length), 0) % length
    kpos_n = lax.broadcasted_iota(jnp.int32, (rows, length), 1)
    dist_n = qpos_n - kpos_n
    valid_n = (dist_n >= 0) & (dist_n < WINDOW)
    row_g = lax.broadcasted_iota(jnp.int32, (rows, 1), 0) // length

    for kv in range(A_KV):
        lanes = slice(kv * A_HD, (kv + 1) * A_HD)
        kn = _head_rms(proj[:, OFF_KA + kv * A_HD:OFF_KA + (kv + 1) * A_HD], kg).reshape(sb, length, A_HD)
        vn = proj[:, OFF_VA + kv * A_HD:OFF_VA + (kv + 1) * A_HD].reshape(sb, length, A_HD)
        nk_ref[:, 0:w - length, lanes] = ck_ref[:, length:w, lanes]
        nv_ref[:, 0:w - length, lanes] = cv_ref[:, length:w, lanes]
        nk_ref[:, w - length:w, lanes] = kn
        nv_ref[:, w - length:w, lanes] = vn
        kc = ck_ref[:, :, lanes].astype(BF16)
        vc = cv_ref[:, :, lanes].astype(BF16)
        q4 = jnp.concatenate(
            [_head_rms(proj[:, OFF_QA + (kv * A_GROUP + g) * A_HD:OFF_QA + (kv * A_GROUP + g + 1) * A_HD], qg)
             .reshape(sb, length, A_HD) for g in range(A_GROUP)], axis=1).astype(BF16)
        slope = jnp.zeros((rows, 1), F32)
        sink = jnp.zeros((rows, 1), F32)
        for g in range(A_GROUP):
            hd = kv * A_GROUP + g
            slope = jnp.where(row_g == g, _alibi_slope(hd), slope)
            sink = jnp.where(row_g == g, sink_ref[hd], sink)
        scale = A_HD ** -0.5
        s_c = _bmm_nt(q4, kc) * scale - slope * dist_c.astype(F32)
        s_c = jnp.where(valid_c, s_c, NEG_INF)
        s_n = _bmm_nt(q4, kn.astype(BF16)) * scale - slope * dist_n.astype(F32)
        s_n = jnp.where(valid_n, s_n, NEG_INF)
        mx = jnp.maximum(jnp.maximum(jnp.max(s_c, axis=-1, keepdims=True),
                                     jnp.max(s_n, axis=-1, keepdims=True)), sink)
        p_c = jnp.exp(s_c - mx)
        p_n = jnp.exp(s_n - mx)
        den = (jnp.sum(p_c, axis=-1, keepdims=True) + jnp.sum(p_n, axis=-1, keepdims=True)
               + jnp.exp(sink - mx))
        o4 = (_bmm(p_c.astype(BF16), vc) + _bmm(p_n.astype(BF16), vn.astype(BF16))) / den
        for g in range(A_GROUP):
            hd = kv * A_GROUP + g
            mix_ref[:, hd * A_HD:(hd + 1) * A_HD] = o4[:, g * length:(g + 1) * length, :].reshape(tb, A_HD)

    for hb in range(B_HEADS):
        d_in, d_q, d_k, d_c = _ret_decay(hb, length)
        qb = proj[:, OFF_QB + hb * B_KD:OFF_QB + (hb + 1) * B_KD].reshape(sb, length, B_KD).astype(BF16)
        kb = proj[:, OFF_KB + hb * B_KD:OFF_KB + (hb + 1) * B_KD].reshape(sb, length, B_KD) * (B_KD ** -0.5)
        vb = proj[:, OFF_VB + hb * B_VD:OFF_VB + (hb + 1) * B_VD].reshape(sb, length, B_VD).astype(BF16)
        gb = proj[:, OFF_GB + hb * B_VD:OFF_GB + (hb + 1) * B_VD]
        state = s0_ref[:, hb]
        inner = _bmm_nt(qb, kb.astype(BF16)) * d_in
        o = _bmm(inner.astype(BF16), vb) + _bmm(qb, state.astype(BF16)) * d_q
        ns_ref[:, hb] = state * d_c + _bmm_tn((kb * d_k).astype(BF16), vb)
        mix_ref[:, QA_W + hb * B_VD:QA_W + (hb + 1) * B_VD] = _group_norm_gate(
            o.reshape(tb, B_VD), rg[:, hb * B_VD:(hb + 1) * B_VD], gb)

    out = _dot(mix_ref[...].astype(BF16), wout_ref[...])
    o_ref[...] = x + _gate(mod, 0, d) * out.reshape(sb, length, d)


def _mix_even_sample(x3, mod, layer, gain, w_in, q_gain, k_gain, sinks, ret_gain, w_out, cache_k, cache_v, state, sb):
    n, length, d = x3.shape
    in_w = w_in.shape[1]
    w = cache_k.shape[1]
    seq_spec = lambda shape: pl.BlockSpec((sb,) + shape, lambda i, _n=len(shape): (i,) + (0,) * _n)
    return pl.pallas_call(
        _mix_even_sample_body,
        grid=(n // sb,),
        in_specs=[seq_spec((length, d)), _mod_seq_spec(mod, layer, sb),
                  _const_spec((1, d)), _const_spec((d, in_w)),
                  _const_spec((1, A_HD)), _const_spec((1, A_HD)),
                  pl.BlockSpec(memory_space=pltpu.SMEM),
                  _const_spec((1, VB_W)), _const_spec((QA_W + VB_W, d)),
                  seq_spec((w, KA_W)), seq_spec((w, VA_W)),
                  pl.BlockSpec((None, sb, B_HEADS, B_KD, B_VD), lambda i: (0, i, 0, 0, 0))],
        out_specs=[seq_spec((length, d)), seq_spec((w, KA_W)), seq_spec((w, VA_W)),
                   seq_spec((B_HEADS, B_KD, B_VD))],
        out_shape=[jax.ShapeDtypeStruct(x3.shape, F32),
                   jax.ShapeDtypeStruct(cache_k.shape, F32),
                   jax.ShapeDtypeStruct(cache_v.shape, F32),
                   jax.ShapeDtypeStruct(state.shape[1:], F32)],
        scratch_shapes=[pltpu.VMEM((sb * length, QA_W + VB_W), F32)],
        compiler_params=_params(),
        name="mix_even_sample",
    )(x3, mod, gain.reshape(1, d), w_in, q_gain.reshape(1, A_HD), k_gain.reshape(1, A_HD),
      sinks, ret_gain.reshape(1, VB_W), w_out, cache_k, cache_v, state)


def _cmul(ar, ai, br, bi):
    return ar * br - ai * bi, ar * bi + ai * br


def _s5_lambda(a_re, a_im, log_dt):
    dt = jnp.exp(log_dt)
    mag = jnp.exp(a_re * dt)
    return mag * jnp.cos(a_im * dt), mag * jnp.sin(a_im * dt)


def _s5_prep_body(t_len, n_chunks, are_ref, aim_ref, ldt_ref, btr_ref, bti_ref, ctr_ref, cti_ref,
                  ard_ref, aid_ref, ldd_ref,
                  bblk_ref, cre_ref, cim_ref, lam_ref, pwb_ref, ltp_ref, b32_ref, cr32_ref, ci32_ref, pw32_ref):
    groups = btr_ref.shape[0]
    k_in, p = btr_ref.shape[1:]
    half = S5_OCT * p
    a_re = are_ref[...]
    a_im = aim_ref[...]
    lam_re, lam_im = _s5_lambda(a_re, a_im, ldt_ref[...])
    den = a_re * a_re + a_im * a_im
    n_re = lam_re - 1.0
    n_im = lam_im
    f_re = (n_re * a_re + n_im * a_im) / den
    f_im = (n_im * a_re - n_re * a_im) / den
    br = btr_ref[...]
    bi = bti_ref[...]
    bb_re = f_re * br - f_im * bi
    bb_im = f_re * bi + f_im * br
    b32_ref[...] = jnp.zeros_like(b32_ref)
    cr32_ref[...] = jnp.zeros_like(cr32_ref)
    ci32_ref[...] = jnp.zeros_like(ci32_ref)
    for g in range(groups):
        s, gl = divmod(g, S5_OCT)
        rows = slice(gl * k_in, (gl + 1) * k_in)
        cols = slice(gl * p, (gl + 1) * p)
        b32_ref[s, rows, cols] = bb_re[g]
        b32_ref[s, rows, half + gl * p:half + (gl + 1) * p] = bb_im[g]
        cr32_ref[s, cols, rows] = ctr_ref[g]
        ci32_ref[s, cols, rows] = cti_ref[g]
    bblk_ref[...] = b32_ref[...].astype(BF16)
    cre_ref[...] = cr32_ref[...].astype(BF16)
    cim_ref[...] = ci32_ref[...].astype(BF16)
    lam_re, lam_im = _s5_lambda(ard_ref[...], aid_ref[...], ldd_ref[...])
    lam_ref[0] = lam_re
    lam_ref[1] = lam_im
    n_oct = lam_re.shape[0]

    def both(cr, ci, s, rows):
        return jnp.concatenate([jnp.broadcast_to(cr[s:s + 1, :], (rows, half)),
                                jnp.broadcast_to(ci[s:s + 1, :], (rows, half))], axis=1)

    cr, ci = lam_re, lam_im
    for t in range(t_len):
        for s in range(n_oct):
            pw32_ref[s, t * ROWS:(t + 1) * ROWS, :] = both(cr, ci, s, ROWS)
        if t + 1 < t_len:
            cr, ci = _cmul(cr, ci, lam_re, lam_im)
    pwb_ref[...] = pw32_ref[...].astype(BF16)
    tr, ti = cr, ci
    ltp_ref[...] = jnp.zeros_like(ltp_ref)
    cr, ci = jnp.ones_like(lam_re), jnp.zeros_like(lam_im)
    for m in range(n_chunks + 1):
        for s in range(n_oct):
            if m < n_chunks:
                ltp_ref[s, n_chunks + m:n_chunks + m + 1, :] = both(cr, ci, s, 1)
            if m in (1, 2, 4, 8):
                i = (1, 2, 4, 8).index(m)
                ltp_ref[s, i:i + 1, :] = both(cr, ci, s, 1)
        cr, ci = _cmul(cr, ci, tr, ti)


def _s5_prep(a_re, a_im, log_dt, b_re, b_im, c_re, c_im, t_len, n_chunks):
    g, p = a_re.shape
    k = b_re.shape[-1]
    assert n_chunks == ROWS
    n_oct = g // S5_OCT
    half = S5_OCT * p
    dense = (n_oct, half)
    return pl.pallas_call(
        functools.partial(_s5_prep_body, t_len, n_chunks),
        out_shape=[jax.ShapeDtypeStruct((n_oct, S5_OCT * k, 2 * half), BF16),
                   jax.ShapeDtypeStruct((n_oct, half, S5_OCT * k), BF16),
                   jax.ShapeDtypeStruct((n_oct, half, S5_OCT * k), BF16),
                   jax.ShapeDtypeStruct((2,) + dense, F32),
                   jax.ShapeDtypeStruct((n_oct, t_len * ROWS, 2 * half), BF16),
                   jax.ShapeDtypeStruct((n_oct, 2 * n_chunks, 2 * half), F32)],
        scratch_shapes=[pltpu.VMEM((n_oct, S5_OCT * k, 2 * half), F32),
                        pltpu.VMEM((n_oct, half, S5_OCT * k), F32),
                        pltpu.VMEM((n_oct, half, S5_OCT * k), F32),
                        pltpu.VMEM((n_oct, t_len * ROWS, 2 * half), F32)],
        compiler_params=pltpu.CompilerParams(vmem_limit_bytes=VMEM_LIMIT),
        name="s5_prep",
    )(a_re.reshape(g, 1, p), a_im.reshape(g, 1, p), log_dt.reshape(g, 1, 1),
      jnp.swapaxes(b_re, 1, 2), jnp.swapaxes(b_im, 1, 2), jnp.swapaxes(c_re, 1, 2), jnp.swapaxes(c_im, 1, 2),
      a_re.reshape(dense), a_im.reshape(dense), jnp.broadcast_to(log_dt[:, None], (g, p)).reshape(dense))


def _gelu_glu_out(x, mod, y, u, dskip, glua_ref, glub_ref):
    sb, rows, d = x.shape
    y = y + dskip * u
    yg = jax.nn.gelu(y, approximate=True).astype(BF16)
    out = _dot(yg, glua_ref[...]) * jax.nn.sigmoid(_dot(yg, glub_ref[...]))
    return x + _gate(mod, 0, d) * out.reshape(sb, rows, d)


def _mix_odd_prompt_body(x_ref, xn_ref, mod_ref, gn_ref, bblk_ref, lam_ref, pwb_ref, ltp_ref, cre_ref, cim_ref,
                         dskip_ref, glua_ref, glub_ref, o_ref, hre_ref, him_ref,
                         un_ref, l0_ref, l1_ref, up0_ref, up1_ref, pt_ref):
    step = pl.program_id(0)
    sb, _, d = x_ref.shape
    tm = sb * ROWS
    t_len = tm // ROWS
    n_oct = bblk_ref.shape[0]
    half = bblk_ref.shape[2] // 2
    mod = _row_mod(mod_ref)

    def permuted_input(src_ref, up_dst):
        u = _modulate(src_ref[...], mod, gn_ref[...], 0).reshape(tm, d)
        for k in range(d // 128):
            un_ref[k] = u[:, k * 128:(k + 1) * 128]
        up = jnp.concatenate(
            [jnp.concatenate([un_ref[k, pl.ds(t, ROWS, stride=t_len), :] for k in range(d // 128)], axis=1)
             for t in range(t_len)], axis=0)
        up_dst[...] = up
        return up.astype(BF16)

    @pl.when(step == 0)
    def _():
        hre_ref[...] = jnp.zeros_like(hre_ref)
        him_ref[...] = jnp.zeros_like(him_ref)
        nat = lax.broadcasted_iota(jnp.int32, (tm, tm), 0)
        prm = lax.broadcasted_iota(jnp.int32, (tm, tm), 1)
        pt_ref[...] = jnp.where(prm == (nat % t_len) * ROWS + nat // t_len, 1.0, 0.0).astype(BF16)
        up16 = permuted_input(x_ref, up0_ref)
        for s in range(n_oct):
            l0_ref[s] = _dot(up16[:, s * 128:(s + 1) * 128], bblk_ref[s])

    def run(l_ref, l_next, up_ref, up_next):
        for s0 in range(0, n_oct, 2):
            pair = (s0, s0 + 1)
            lam_re = [jnp.broadcast_to(lam_ref[0, s:s + 1, :], (ROWS, half)) for s in pair]
            lam_im = [jnp.broadcast_to(lam_ref[1, s:s + 1, :], (ROWS, half)) for s in pair]

            def local_step(t, carry):
                r = pl.multiple_of(t * ROWS, ROWS)
                out = []
                for i, s in enumerate(pair):
                    h_re, h_im = carry[2 * i], carry[2 * i + 1]
                    n_re = lam_re[i] * h_re - lam_im[i] * h_im + l_ref[s, pl.ds(r, ROWS), 0:half]
                    n_im = lam_re[i] * h_im + lam_im[i] * h_re + l_ref[s, pl.ds(r, ROWS), half:2 * half]
                    l_ref[s, pl.ds(r, ROWS), 0:half] = n_re
                    l_ref[s, pl.ds(r, ROWS), half:2 * half] = n_im
                    out += [n_re, n_im]
                return tuple(out)

            zero = jnp.zeros((ROWS, half), F32)
            lax.fori_loop(0, t_len, local_step, (zero, zero, zero, zero), unroll=2)

        next16 = permuted_input(xn_ref, up_next)
        chunk = lax.broadcasted_iota(jnp.int32, (ROWS, half), 0)
        y_parts = []
        for s in range(n_oct):
            l_next[s] = _dot(next16[:, s * 128:(s + 1) * 128], bblk_ref[s])
            p_re = l_ref[s, tm - ROWS:tm, 0:half]
            p_im = l_ref[s, tm - ROWS:tm, half:2 * half]
            for i, sh in enumerate((1, 2, 4)):
                m_re, m_im = _cmul(ltp_ref[s, i:i + 1, 0:half], ltp_ref[s, i:i + 1, half:2 * half],
                                   jnp.where(chunk >= sh, pltpu.roll(p_re, sh, 0), 0.0),
                                   jnp.where(chunk >= sh, pltpu.roll(p_im, sh, 0), 0.0))
                p_re, p_im = p_re + m_re, p_im + m_im
            hin_re = jnp.broadcast_to(hre_ref[s:s + 1, :], (ROWS, half))
            hin_im = jnp.broadcast_to(him_ref[s:s + 1, :], (ROWS, half))
            m_re, m_im = _cmul(ltp_ref[s, ROWS:2 * ROWS, 0:half], ltp_ref[s, ROWS:2 * ROWS, half:2 * half],
                               hin_re, hin_im)
            st_re = m_re + jnp.where(chunk >= 1, pltpu.roll(p_re, 1, 0), 0.0)
            st_im = m_im + jnp.where(chunk >= 1, pltpu.roll(p_im, 1, 0), 0.0)
            m_re, m_im = _cmul(ltp_ref[s, 3:4, 0:half], ltp_ref[s, 3:4, half:2 * half],
                               hre_ref[s:s + 1, :], him_ref[s:s + 1, :])
            hre_ref[s:s + 1, :] = m_re + p_re[ROWS - 1:ROWS, :]
            him_ref[s:s + 1, :] = m_im + p_im[ROWS - 1:ROWS, :]
            loc = l_ref[s].astype(BF16).reshape(tm // BF16_ROWS, BF16_ROWS, 2 * half)
            pw = pwb_ref[s].reshape(tm // BF16_ROWS, BF16_ROWS, 2 * half)
            pair_re = jnp.concatenate([st_re, st_re], axis=0).astype(BF16)[None]
            pair_im = jnp.concatenate([st_im, st_im], axis=0).astype(BF16)[None]
            f_re, f_im = _cmul(pw[:, :, 0:half], pw[:, :, half:2 * half], pair_re, pair_im)
            hs_re = (loc[:, :, 0:half] + f_re).reshape(tm, half)
            hs_im = (loc[:, :, half:2 * half] + f_im).reshape(tm, half)
            y_parts.append(_dot(hs_re, cre_ref[s]) - _dot(hs_im, cim_ref[s]))

        y = jnp.concatenate(y_parts, axis=1) + dskip_ref[...] * up_ref[...]
        yg = jax.nn.gelu(y, approximate=True).astype(BF16)
        yn = _dot(pt_ref[...], yg).astype(BF16)
        out = _dot(yn, glua_ref[...]) * jax.nn.sigmoid(_dot(yn, glub_ref[...]))
        o_ref[...] = x_ref[...] + _gate(mod, 0, d) * out.reshape(sb, ROWS, d)

    @pl.when(step % 2 == 0)
    def _():
        run(l0_ref, l1_ref, up0_ref, up1_ref)

    @pl.when(step % 2 == 1)
    def _():
        run(l1_ref, l0_ref, up1_ref, up0_ref)


def _mix_odd_prompt(x3, mod, layer, prompt_row, gain, bblk, lam_d, pwb, ltp, cre, cim, dskip, glu_a, glu_b, sb):
    n8, _, d = x3.shape
    n_oct, kin, wid = bblk.shape
    half = wid // 2
    tm = sb * ROWS
    n_tiles = n8 // sb
    proj_buf = pltpu.VMEM((n_oct, tm, wid), F32)
    perm_buf = pltpu.VMEM((tm, d), F32)
    return pl.pallas_call(
        _mix_odd_prompt_body,
        grid=(n_tiles,),
        in_specs=[pl.BlockSpec((sb, ROWS, d), lambda i: (i, 0, 0)),
                  pl.BlockSpec((sb, ROWS, d), lambda i: (jnp.minimum(i + 1, n_tiles - 1), 0, 0)),
                  _mod_row_spec(mod, layer, prompt_row),
                  _const_spec((1, d)), _const_spec(bblk.shape), _const_spec(lam_d.shape),
                  _const_spec(pwb.shape), _const_spec(ltp.shape),
                  _const_spec(cre.shape), _const_spec(cim.shape), _const_spec((1, d)),
                  _const_spec((d, d)), _const_spec((d, d))],
        out_specs=[pl.BlockSpec((sb, ROWS, d), lambda i: (i, 0, 0)),
                   pl.BlockSpec((n_oct, half), lambda i: (0, 0)),
                   pl.BlockSpec((n_oct, half), lambda i: (0, 0))],
        out_shape=[jax.ShapeDtypeStruct(x3.shape, F32),
                   jax.ShapeDtypeStruct((n_oct, half), F32),
                   jax.ShapeDtypeStruct((n_oct, half), F32)],
        scratch_shapes=[pltpu.VMEM((d // 128, tm, 128), F32), proj_buf, proj_buf, perm_buf, perm_buf,
                        pltpu.VMEM((tm, tm), BF16)],
        compiler_params=_params(),
        name="mix_odd_prompt",
    )(x3, x3, mod, gain.reshape(1, d), bblk, lam_d, pwb, ltp, cre, cim, dskip.reshape(1, d), glu_a, glu_b)


def _mix_odd_sample_body(x_ref, mod_ref, gn_ref, bblk_ref, lam_ref, cre_ref, cim_ref, dskip_ref,
                         glua_ref, glub_ref, sre_ref, sim_ref, o_ref, nre_ref, nim_ref, d_ref, y_ref):
    x = x_ref[...]
    sb, length, d = x.shape
    tm = sb * length
    n_oct = bblk_ref.shape[0]
    half = bblk_ref.shape[2] // 2
    mod = _seq_mod(mod_ref)
    u = _modulate(x, mod, gn_ref[...], 0).reshape(tm, d)
    u16 = u.astype(BF16)
    n_ch = bblk_ref.shape[2] // 128
    hc = n_ch // 2
    for s in range(n_oct):
        bu = _dot(u16[:, s * 128:(s + 1) * 128], bblk_ref[s])
        for c in range(n_ch):
            d_ref[c] = bu[:, c * 128:(c + 1) * 128]
        lam_re = lam_ref[0, s:s + 1, :]
        lam_im = lam_ref[1, s:s + 1, :]
        h_re = sre_ref[:, s * half:(s + 1) * half]
        h_im = sim_ref[:, s * half:(s + 1) * half]
        for t in range(length):
            b_re = jnp.concatenate([d_ref[c, pl.ds(t, sb, stride=length), :] for c in range(hc)], axis=1)
            b_im = jnp.concatenate([d_ref[hc + c, pl.ds(t, sb, stride=length), :] for c in range(hc)], axis=1)
            n_re = lam_re * h_re - lam_im * h_im + b_re
            n_im = lam_re * h_im + lam_im * h_re + b_im
            for c in range(hc):
                d_ref[c, pl.ds(t, sb, stride=length), :] = n_re[:, c * 128:(c + 1) * 128]
                d_ref[hc + c, pl.ds(t, sb, stride=length), :] = n_im[:, c * 128:(c + 1) * 128]
            h_re, h_im = n_re, n_im
        nre_ref[:, s * half:(s + 1) * half] = h_re
        nim_ref[:, s * half:(s + 1) * half] = h_im
        hs_re = jnp.concatenate([d_ref[c] for c in range(hc)], axis=1)
        hs_im = jnp.concatenate([d_ref[hc + c] for c in range(hc)], axis=1)
        y_ref[:, s * 128:(s + 1) * 128] = (_dot(hs_re.astype(BF16), cre_ref[s])
                                           - _dot(hs_im.astype(BF16), cim_ref[s]))
    o_ref[...] = _gelu_glu_out(x, mod, y_ref[...], u, dskip_ref[...], glua_ref, glub_ref)


def _mix_odd_sample(x3, mod, layer, gain, bblk, lam_d, cre, cim, dskip, glu_a, glu_b, s_re, s_im, sb):
    n, length, d = x3.shape
    n_oct, kin, wid = bblk.shape
    tm = sb * length
    nstate = s_re.shape[1]
    return pl.pallas_call(
        _mix_odd_sample_body,
        grid=(n // sb,),
        in_specs=[pl.BlockSpec((sb, length, d), lambda i: (i, 0, 0)),
                  _mod_seq_spec(mod, layer, sb),
                  _const_spec((1, d)), _const_spec(bblk.shape), _const_spec(lam_d.shape),
                  _const_spec(cre.shape), _const_spec(cim.shape), _const_spec((1, d)),
                  _const_spec((d, d)), _const_spec((d, d)),
                  pl.BlockSpec((sb, nstate), lambda i: (i, 0)),
                  pl.BlockSpec((sb, nstate), lambda i: (i, 0))],
        out_specs=[pl.BlockSpec((sb, length, d), lambda i: (i, 0, 0)),
                   pl.BlockSpec((sb, nstate), lambda i: (i, 0)),
                   pl.BlockSpec((sb, nstate), lambda i: (i, 0))],
        out_shape=[jax.ShapeDtypeStruct(x3.shape, F32),
                   jax.ShapeDtypeStruct(s_re.shape, F32),
                   jax.ShapeDtypeStruct(s_im.shape, F32)],
        scratch_shapes=[pltpu.VMEM((wid // 128, tm, 128), F32), pltpu.VMEM((tm, d), F32)],
        compiler_params=_params(),
        name="mix_odd_sample",
    )(x3, mod, gain.reshape(1, d), bblk, lam_d, cre, cim, dskip.reshape(1, d), glu_a, glu_b, s_re, s_im)


def _pick(n, want):
    while n % want:
        want //= 2
    return max(want, 1)


def kernel(x_prompt, x_sample, cache_win_k, cache_win_v, state_ret, state_s5_re, state_s5_im, c_prompt, c_sample, ada_w, ada_b, norm_mix, norm_ffn, ffn_wg, ffn_wu, ffn_wd, even_w_in, even_q_gain, even_k_gain, even_sinks, even_ret_gain, even_w_out, odd_A_re, odd_A_im, odd_log_dt, odd_B_re, odd_B_im, odd_C_re, odd_C_im, odd_D, odd_glu_a, odd_glu_b):
    bp, lp, d = x_prompt.shape
    ns, ls, _ = x_sample.shape
    assert bp == 1 and ls == ROWS and lp % WINDOW == 0
    w = cache_win_k.shape[2]
    groups, p_state = odd_A_re.shape[1:]

    n_c = bp + ns
    n_pad = -n_c % ROWS
    c_all = jnp.concatenate([c_sample, c_prompt, jnp.zeros((n_pad, d), F32)], axis=0)
    mod = _adaln(c_all, ada_w, ada_b)

    bf = lambda t: t.astype(BF16)
    w_in, w_out = bf(even_w_in[0]), bf(even_w_out[0])

    xp = x_prompt.reshape(lp // ROWS, ROWS, d)
    xs = x_sample

    sb_p = _pick(lp // ROWS, 32)
    sb_ffn = _pick(lp // ROWS, 64)
    sb_s = _pick(ns, 32)
    sb_s_even = _pick(ns, 16)

    (xp, p_k, p_v, p_ret), ffn0 = _mix_even_prompt(xp, mod, 0, ns, norm_mix[0], w_in, even_q_gain[0], even_k_gain[0],
                                                   even_sinks[0], even_ret_gain[0], w_out, sb_ffn,
                                                   cast=((ffn_wg, 0), (ffn_wu, 0), (ffn_wd, 0)))
    xs, s_k, s_v, s_ret = _mix_even_sample(xs, mod, 0, norm_mix[0], w_in, even_q_gain[0], even_k_gain[0],
                                           even_sinks[0], even_ret_gain[0], w_out,
                                           cache_win_k[0].reshape(ns, w, KA_W), cache_win_v[0].reshape(ns, w, VA_W),
                                           state_ret, sb_s_even)
    xp, xs, (wg1, wu1, wd1, glu_a, glu_b) = _ffn(
        xp, xs, mod, 0, norm_ffn[0], *ffn0, sb_ffn,
        cast=((ffn_wg, 1), (ffn_wu, 1), (ffn_wd, 1), (odd_glu_a, 0), (odd_glu_b, 0)))

    t_len = sb_p
    bblk, cre, cim, lam_d, pwb, ltp = _s5_prep(odd_A_re[0], odd_A_im[0], odd_log_dt[0], odd_B_re[0], odd_B_im[0],
                                               odd_C_re[0], odd_C_im[0], t_len, ROWS)
    xp, p_re, p_im = _mix_odd_prompt(xp, mod, 1, ns, norm_mix[1], bblk, lam_d, pwb, ltp, cre, cim, odd_D[0],
                                     glu_a, glu_b, sb_p)
    xs, s_re, s_im = _mix_odd_sample(xs, mod, 1, norm_mix[1], bblk, lam_d, cre, cim, odd_D[0], glu_a, glu_b,
                                     state_s5_re[0].reshape(ns, groups * p_state),
                                     state_s5_im[0].reshape(ns, groups * p_state), sb_s)
    xp, xs, _ = _ffn(xp, xs, mod, 1, norm_ffn[1], wg1, wu1, wd1, sb_ffn)

    y_prompt = xp.reshape(bp, lp, d)
    y_sample = xs
    return (y_prompt, y_sample,
            p_k.reshape(1, bp, WINDOW, A_KV, A_HD), p_v.reshape(1, bp, WINDOW, A_KV, A_HD),
            p_ret.reshape(1, bp, B_HEADS, B_KD, B_VD),
            p_re.reshape(1, bp, groups, p_state), p_im.reshape(1, bp, groups, p_state),
            s_k.reshape(1, ns, w, A_KV, A_HD), s_v.reshape(1, ns, w, A_KV, A_HD),
            s_ret.reshape(1, ns, B_HEADS, B_KD, B_VD),
            s_re.reshape(1, ns, groups, p_state), s_im.reshape(1, ns, groups, p_state))
```

```python
import functools
import math

import jax
import jax.numpy as jnp
from jax import lax
from jax.experimental import pallas as pl
from jax.experimental.pallas import tpu as pltpu

F32 = jnp.float32
BF16 = jnp.bfloat16

EPS = 1e-6
NEG_INF = -1e30
ROWS = 8

A_HEADS, A_KV, A_GROUP, A_HD = 8, 2, 4, 64
WINDOW = 128
B_HEADS, B_KD, B_VD = 4, 128, 128
S5_GROUP, S5_STATE = 16, 64
S5_OCT = 8

QA_W, KA_W, VA_W = A_HEADS * A_HD, A_KV * A_HD, A_KV * A_HD
QB_W, KB_W, VB_W, GB_W = B_HEADS * B_KD, B_HEADS * B_KD, B_HEADS * B_VD, B_HEADS * B_VD
OFF_QA = 0
OFF_KA = OFF_QA + QA_W
OFF_VA = OFF_KA + KA_W
OFF_QB = OFF_VA + VA_W
OFF_KB = OFF_QB + QB_W
OFF_VB = OFF_KB + KB_W
OFF_GB = OFF_VB + VB_W

VMEM_LIMIT = 56 * 1024 * 1024


def _ret_log_gamma(h):
    return math.log1p(-(2.0 ** (-5.0 - h)))


def _alibi_slope(h):
    return 2.0 ** (-8.0 * (h + 1) / A_HEADS)


def _const_spec(shape):
    nd = len(shape)
    return pl.BlockSpec(shape, lambda i, _n=nd: (0,) * _n, pipeline_mode=pl.Buffered(1))


def _params():
    return pltpu.CompilerParams(dimension_semantics=("arbitrary",), vmem_limit_bytes=VMEM_LIMIT)


def _dot(a, b):
    return jnp.dot(a, b, preferred_element_type=F32)


def _dot_nt(a, b):
    return lax.dot_general(a, b, (((1,), (1,)), ((), ())), preferred_element_type=F32)


def _bmm(a, b):
    return lax.dot_general(a, b, (((2,), (1,)), ((0,), (0,))), preferred_element_type=F32)


def _bmm_nt(a, b):
    return lax.dot_general(a, b, (((2,), (2,)), ((0,), (0,))), preferred_element_type=F32)


def _bmm_tn(a, b):
    return lax.dot_general(a, b, (((1,), (1,)), ((0,), (0,))), preferred_element_type=F32)


def _rms(x, g):
    return x * lax.rsqrt(jnp.mean(x * x, axis=-1, keepdims=True) + EPS) * g


def _modulate(x3, mod, gain, which):
    d = x3.shape[-1]
    sh = mod[:, :, (3 * which) * d:(3 * which + 1) * d]
    sc = mod[:, :, (3 * which + 1) * d:(3 * which + 2) * d]
    return _rms(x3, gain) * (1.0 + sc) + sh


def _mod_row_spec(mod, layer, row):
    assert row % ROWS == 0
    return pl.BlockSpec((None, ROWS, mod.shape[-1]), lambda i: (layer, row // ROWS, 0))


def _mod_seq_spec(mod, layer, sb, first_step=0):
    return pl.BlockSpec((None, sb, mod.shape[-1]), lambda i: (layer, jnp.maximum(i - first_step, 0), 0))


def _row_mod(mod_ref):
    return mod_ref[0:1, :][:, None, :]


def _seq_mod(mod_ref):
    return mod_ref[...][:, None, :]


def _gate(mod, which, d):
    return mod[:, :, (3 * which + 2) * d:(3 * which + 3) * d]


def _adaln_body(c_ref, w_ref, b_ref, o_ref):
    c = c_ref[...]
    a = (c * jax.nn.sigmoid(c)).astype(BF16)
    o_ref[0] = _dot(a, w_ref[0].astype(BF16)) + b_ref[0]


def _adaln(c_all, ada_w, ada_b):
    depth, d, n = ada_w.shape
    r = c_all.shape[0]
    tn = 1536
    return pl.pallas_call(
        _adaln_body,
        grid=(depth, n // tn),
        in_specs=[pl.BlockSpec((r, d), lambda l, j: (0, 0)),
                  pl.BlockSpec((1, d, tn), lambda l, j: (l, 0, j)),
                  pl.BlockSpec((1, 1, tn), lambda l, j: (l, 0, j))],
        out_specs=pl.BlockSpec((1, r, tn), lambda l, j: (l, 0, j)),
        out_shape=jax.ShapeDtypeStruct((depth, r, n), F32),
        compiler_params=pltpu.CompilerParams(dimension_semantics=("arbitrary", "arbitrary"),
                                             vmem_limit_bytes=VMEM_LIMIT),
        name="adaln",
    )(c_all, ada_w, ada_b.reshape(depth, 1, n))


BF16_ROWS = 16


def _cast_specs(job, n_steps):
    w, layer = job
    _, rows, cols = w.shape
    hold = 1
    while rows % (n_steps // hold) or (rows // (n_steps // hold)) % BF16_ROWS:
        hold *= 2
        assert hold <= n_steps and n_steps % hold == 0
    blk = rows // (n_steps // hold)
    src = pl.BlockSpec((None, blk, cols),
                       lambda i, _h=hold, _l=layer: (_l, jnp.minimum(i, n_steps - 1) // _h, 0))
    dst = pl.BlockSpec((blk, cols), lambda i, _h=hold: (jnp.minimum(i, n_steps - 1) // _h, 0))
    return src, dst, jax.ShapeDtypeStruct((rows, cols), BF16)


def _cast_blocks(in_refs, out_refs):
    for src, dst in zip(in_refs, out_refs):
        dst[...] = src[...].astype(BF16)


def _ffn_body(n_cast, n_prompt, xp_ref, mp_ref, xs_ref, ms_ref, gn_ref, wg_ref, wu_ref, wd_ref, *refs):
    cast_in, (op_ref, os_ref), cast_out = refs[:n_cast], refs[n_cast:n_cast + 2], refs[n_cast + 2:]
    step = pl.program_id(0)

    def tile(x_ref, mod, o_ref):
        x = x_ref[...]
        sb, _, d = x.shape
        h = _modulate(x, mod, gn_ref[...], 1).reshape(sb * ROWS, d).astype(BF16)
        a = _dot(h, wg_ref[...])
        b = _dot(h, wu_ref[...])
        act = (a * jax.nn.sigmoid(a) * b).astype(BF16)
        y = _dot(act, wd_ref[...])
        o_ref[...] = x + _gate(mod, 1, d) * y.reshape(sb, ROWS, d)

    @pl.when(step < n_prompt)
    def _():
        tile(xp_ref, _row_mod(mp_ref), op_ref)

    @pl.when(step >= n_prompt)
    def _():
        tile(xs_ref, _seq_mod(ms_ref), os_ref)

    _cast_blocks(cast_in, cast_out)


def _ffn(xp3, xs3, mod, layer, gain, wg, wu, wd, sb, cast=()):
    n8, _, d = xp3.shape
    f = wg.shape[1]
    n_prompt = n8 // sb
    n_sample = xs3.shape[0] // sb
    prompt_row = xs3.shape[0]
    assert n8 % sb == 0 and xs3.shape[0] % sb == 0
    cast_specs = [_cast_specs(w, n_prompt) for w in cast]
    p_idx = lambda i: (jnp.minimum(i, n_prompt - 1), 0, 0)
    s_idx = lambda i: (jnp.maximum(i - n_prompt, 0), 0, 0)
    xp_spec = pl.BlockSpec((sb, ROWS, d), p_idx)
    xs_spec = pl.BlockSpec((sb, ROWS, d), s_idx)
    out = pl.pallas_call(
        functools.partial(_ffn_body, len(cast), n_prompt),
        grid=(n_prompt + n_sample,),
        in_specs=[xp_spec, _mod_row_spec(mod, layer, prompt_row),
                  xs_spec, _mod_seq_spec(mod, layer, sb, n_prompt),
                  _const_spec((1, d)), _const_spec((d, f)), _const_spec((d, f)),
                  _const_spec((f, d))] + [c[0] for c in cast_specs],
        out_specs=[xp_spec, xs_spec] + [c[1] for c in cast_specs],
        out_shape=[jax.ShapeDtypeStruct(xp3.shape, F32), jax.ShapeDtypeStruct(xs3.shape, F32)]
        + [c[2] for c in cast_specs],
        compiler_params=_params(),
        name="ffn",
    )(xp3, mod, xs3, mod, gain.reshape(1, d), wg, wu, wd, *[w for w, _ in cast])
    return out[0], out[1], tuple(out[2:])


def _head_rms(t, g):
    return t * lax.rsqrt(jnp.mean(t * t, axis=-1, keepdims=True) + EPS) * g


def _group_norm_gate(o, gain, gate):
    mu = jnp.mean(o, axis=-1, keepdims=True)
    var = jnp.mean(jnp.square(o - mu), axis=-1, keepdims=True)
    return (o - mu) * lax.rsqrt(var + EPS) * gain * (gate * jax.nn.sigmoid(gate))


def _ret_decay(hb, c):
    lg = _ret_log_gamma(hb)
    ii = lax.broadcasted_iota(jnp.int32, (c, c), 0)
    jj = lax.broadcasted_iota(jnp.int32, (c, c), 1)
    diff = (ii - jj).astype(F32)
    d_in = jnp.where(diff >= 0, jnp.exp(lg * jnp.maximum(diff, 0.0)), 0.0)
    row = lax.broadcasted_iota(jnp.int32, (c, B_KD), 0).astype(F32)
    d_q = jnp.exp(lg * (row + 1.0))
    d_k = jnp.exp(lg * (c - 1.0 - row))
    d_c = math.exp(lg * c)
    return d_in, d_q, d_k, d_c


P_OFF_KA = QA_W
P_OFF_VA = P_OFF_KA + KA_W
P_OFF_QB = P_OFF_VA + VA_W
P_OFF_VB = P_OFF_QB + QB_W
P_OFF_GB = P_OFF_VB + VB_W
P_WIDTH = P_OFF_GB + GB_W
PAIR = 2 * A_HD


def _twice(t):
    low = lax.broadcasted_iota(jnp.int32, t.shape, 1) < A_HD
    swapped = pltpu.roll(t, A_HD, 1)
    return jnp.concatenate([jnp.where(low, t, swapped), jnp.where(low, swapped, t)], axis=1)


def _mix_even_prompt_body(n_cast, x_ref, mod_ref, gn_ref, win_ref, wkt_ref, qg_ref, kg_ref, sink_ref, rg_ref, wout_ref,
                          *refs):
    cast_in, refs = refs[:n_cast], refs[n_cast:]
    o_ref, pk_ref, pv_ref, ps_ref = refs[:4]
    cast_out = refs[4:4 + n_cast]
    mix_ref, carry_ref, bias_ref, dec_ref, ones_ref = refs[4 + n_cast:]
    _cast_blocks(cast_in, cast_out)
    step = pl.program_id(0)
    blk = WINDOW
    rows4 = A_GROUP * blk

    @pl.when(step == 0)
    def _():
        carry_ref[...] = jnp.zeros_like(carry_ref)
        ps_ref[...] = jnp.zeros_like(ps_ref)
        er = lax.broadcasted_iota(jnp.int32, ones_ref.shape, 0) // A_HD
        ec = lax.broadcasted_iota(jnp.int32, ones_ref.shape, 1) // A_HD
        ones_ref[...] = jnp.where(er == ec, 1.0 / A_HD, 0.0).astype(BF16)
        row = lax.broadcasted_iota(jnp.int32, (rows4, 2 * blk), 0)
        dist = row % blk + blk - lax.broadcasted_iota(jnp.int32, (rows4, 2 * blk), 1)
        in_window = (dist >= 0) & (dist < WINDOW)
        for kv in range(A_KV):
            slope = jnp.zeros((rows4, 2 * blk), F32)
            for g in range(A_GROUP):
                slope = jnp.where(row // blk == g, _alibi_slope(kv * A_GROUP + g), slope)
            bias_ref[kv] = jnp.where(in_window, slope * dist.astype(F32), -NEG_INF)
        for hb in range(B_HEADS):
            lg = _ret_log_gamma(hb)
            ii = lax.broadcasted_iota(jnp.int32, (blk, blk), 0).astype(F32)
            jj = lax.broadcasted_iota(jnp.int32, (blk, blk), 1).astype(F32)
            diff = ii - jj
            dec_ref[hb, 0] = jnp.where(diff >= 0, jnp.exp(lg * jnp.maximum(diff, 0.0)), 0.0)
            dec_ref[hb, 1] = jnp.exp(lg * (ii + 1.0))
            dec_ref[hb, 2] = jnp.exp(lg * (blk - 1.0 - jj))

    x = x_ref[...]
    sb, _, d = x.shape
    tb = sb * ROWS
    mod = _row_mod(mod_ref)
    h = _modulate(x, mod, gn_ref[...], 0).reshape(tb, d).astype(BF16)
    proj = _dot(h, win_ref[...])
    kt_all = _dot_nt(wkt_ref[...], h) * (B_KD ** -0.5)
    rg = rg_ref[...]

    qa = proj[:, 0:QA_W]
    ka = proj[:, P_OFF_KA:P_OFF_KA + KA_W]
    va = proj[:, P_OFF_VA:P_OFF_VA + VA_W]
    q_hat = (qa * lax.rsqrt(_dot((qa * qa).astype(BF16), ones_ref[...]) + EPS) * qg_ref[...]).astype(BF16)
    k_hat = ka * lax.rsqrt(_dot((ka * ka).astype(BF16), ones_ref[0:KA_W, 0:KA_W]) + EPS) * kg_ref[...]
    k_hat16 = _twice(k_hat).astype(BF16)
    va16 = _twice(va).astype(BF16)
    prev = carry_ref[step % 2]
    carry_ref[(step + 1) % 2] = jnp.concatenate([k_hat16[tb - blk:tb], va16[tb - blk:tb]], axis=1)

    row_g = lax.broadcasted_iota(jnp.int32, (rows4, 1), 0) // blk
    key_is_prev = lax.broadcasted_iota(jnp.int32, (rows4, 2 * blk), 1) < blk
    first_penalty = jnp.where(step == 0, -NEG_INF, 0.0)
    lane_low = lax.broadcasted_iota(jnp.int32, (blk, PAIR), 1) < A_HD
    ones_cols = jnp.ones((2 * blk, PAIR), BF16)

    n_blk = tb // blk
    att = [(j, kv) for j in range(n_blk) for kv in range(A_KV)]
    ret = [(j, hb) for j in range(n_blk) for hb in range(B_HEADS)]

    sinks, scores = {}, {}
    for j, kv in att:
        r0 = j * blk
        kcol = slice(kv * PAIR, (kv + 1) * PAIR)
        if j == 0:
            k2 = jnp.concatenate([prev[:, kcol], k_hat16[0:blk, kcol]], axis=0)
        else:
            k2 = k_hat16[r0 - blk:r0 + blk, kcol]
        q4 = jnp.concatenate(
            [jnp.where(lane_low == (g % 2 == 0),
                       q_hat[r0:r0 + blk, (kv * A_GROUP + g - g % 2) * A_HD:(kv * A_GROUP + g - g % 2 + 2) * A_HD],
                       jnp.zeros((), BF16))
             for g in range(A_GROUP)], axis=0)
        scores[j, kv] = _dot_nt(q4, k2)
        sink = jnp.zeros((rows4, 1), F32)
        for g in range(A_GROUP):
            sink = jnp.where(row_g == g, sink_ref[kv * A_GROUP + g], sink)
        sinks[j, kv] = sink
    qb, vb, kt, inner = {}, {}, {}, {}
    for j, hb in ret:
        r0 = j * blk
        qb[j, hb] = proj[r0:r0 + blk, P_OFF_QB + hb * B_KD:P_OFF_QB + (hb + 1) * B_KD].astype(BF16)
        vb[j, hb] = proj[r0:r0 + blk, P_OFF_VB + hb * B_VD:P_OFF_VB + (hb + 1) * B_VD].astype(BF16)
        kt[j, hb] = kt_all[hb * B_KD:(hb + 1) * B_KD, r0:r0 + blk]
        inner[j, hb] = _dot(qb[j, hb], kt[j, hb].astype(BF16))

    probs, maxes = {}, {}
    for j, kv in att:
        s = scores[j, kv] * (A_HD ** -0.5) - bias_ref[kv]
        if j == 0:
            s = s - jnp.where(key_is_prev, first_penalty, 0.0)
        mx = jnp.maximum(jnp.max(s, axis=-1, keepdims=True), sinks[j, kv])
        probs[j, kv] = jnp.exp(s - mx).astype(BF16)
        maxes[j, kv] = mx
    state = {}
    for hb in range(B_HEADS):
        d_c = math.exp(_ret_log_gamma(hb) * blk)
        state[0, hb] = ps_ref[hb]
        for j in range(n_blk):
            state[j + 1, hb] = state[j, hb] * d_c + _dot((kt[j, hb] * dec_ref[hb, 2]).astype(BF16), vb[j, hb])
        ps_ref[hb] = state[n_blk, hb]

    for j, kv in att:
        r0 = j * blk
        kcol = slice(kv * PAIR, (kv + 1) * PAIR)
        vcol = slice(2 * KA_W + kv * PAIR, 2 * KA_W + (kv + 1) * PAIR)
        if j == 0:
            v2 = jnp.concatenate([prev[:, vcol], va16[0:blk, kcol]], axis=0)
        else:
            v2 = va16[r0 - blk:r0 + blk, kcol]
        pv = _dot(probs[j, kv], jnp.concatenate([v2, ones_cols], axis=1))
        o4 = pv[:, 0:PAIR] / (pv[:, PAIR:2 * PAIR] + jnp.exp(sinks[j, kv] - maxes[j, kv]))
        for g in range(A_GROUP):
            hd = kv * A_GROUP + g
            half = slice((hd % 2) * A_HD, (hd % 2 + 1) * A_HD)
            mix_ref[r0:r0 + blk, hd * A_HD:(hd + 1) * A_HD] = o4[g * blk:(g + 1) * blk, half]
    o_ret = {}
    for j, hb in ret:
        o_ret[j, hb] = (_dot((inner[j, hb] * dec_ref[hb, 0]).astype(BF16), vb[j, hb])
                        + _dot(qb[j, hb], state[j, hb].astype(BF16)) * dec_ref[hb, 1])

    cen = {k: o_ret[k] - jnp.mean(o_ret[k], axis=-1, keepdims=True) for k in ret}
    var = {k: jnp.mean(cen[k] * cen[k], axis=-1, keepdims=True) for k in ret}
    for j, hb in ret:
        r0 = j * blk
        gb = proj[r0:r0 + blk, P_OFF_GB + hb * B_VD:P_OFF_GB + (hb + 1) * B_VD]
        mix_ref[r0:r0 + blk, QA_W + hb * B_VD:QA_W + (hb + 1) * B_VD] = (
            cen[j, hb] * lax.rsqrt(var[j, hb] + EPS) * rg[:, hb * B_VD:(hb + 1) * B_VD] * (gb * jax.nn.sigmoid(gb)))

    out = _dot(mix_ref[...].astype(BF16), wout_ref[...])
    o_ref[...] = x + _gate(mod, 0, d) * out.reshape(sb, ROWS, d)

    @pl.when(step == pl.num_programs(0) - 1)
    def _():
        pk_ref[...] = k_hat[tb - blk:tb, :]
        pv_ref[...] = va[tb - blk:tb, :]


def _mix_even_prompt(x3, mod, layer, prompt_row, gain, w_in, q_gain, k_gain, sinks, ret_gain, w_out, sb, cast=()):
    n8, _, d = x3.shape
    tb = sb * ROWS
    cast_specs = [_cast_specs(w, n8 // sb) for w in cast]
    w_main = jnp.concatenate([w_in[:, :OFF_KB], w_in[:, OFF_VB:]], axis=1)
    wk_t = w_in[:, OFF_KB:OFF_VB].T
    out = pl.pallas_call(
        functools.partial(_mix_even_prompt_body, len(cast)),
        grid=(n8 // sb,),
        in_specs=[pl.BlockSpec((sb, ROWS, d), lambda i: (i, 0, 0)),
                  _mod_row_spec(mod, layer, prompt_row),
                  _const_spec((1, d)), _const_spec((d, P_WIDTH)), _const_spec((KB_W, d)),
                  _const_spec((1, QA_W)), _const_spec((1, KA_W)),
                  pl.BlockSpec(memory_space=pltpu.SMEM),
                  _const_spec((1, VB_W)), _const_spec((QA_W + VB_W, d))] + [c[0] for c in cast_specs],
        out_specs=[pl.BlockSpec((sb, ROWS, d), lambda i: (i, 0, 0)),
                   pl.BlockSpec((WINDOW, KA_W), lambda i: (0, 0)),
                   pl.BlockSpec((WINDOW, VA_W), lambda i: (0, 0)),
                   pl.BlockSpec((B_HEADS, B_KD, B_VD), lambda i: (0, 0, 0))] + [c[1] for c in cast_specs],
        out_shape=[jax.ShapeDtypeStruct(x3.shape, F32),
                   jax.ShapeDtypeStruct((WINDOW, KA_W), F32),
                   jax.ShapeDtypeStruct((WINDOW, VA_W), F32),
                   jax.ShapeDtypeStruct((B_HEADS, B_KD, B_VD), F32)] + [c[2] for c in cast_specs],
        scratch_shapes=[pltpu.VMEM((tb, QA_W + VB_W), F32),
                        pltpu.VMEM((2, WINDOW, 2 * KA_W + 2 * VA_W), BF16),
                        pltpu.VMEM((A_KV, A_GROUP * WINDOW, 2 * WINDOW), F32),
                        pltpu.VMEM((B_HEADS, 3, WINDOW, WINDOW), F32),
                        pltpu.VMEM((QA_W, QA_W), BF16)],
        compiler_params=_params(),
        name="mix_even_prompt",
    )(x3, mod, gain.reshape(1, d), w_main, wk_t, jnp.tile(q_gain, A_HEADS).reshape(1, QA_W),
      jnp.tile(k_gain, A_KV).reshape(1, KA_W),
      sinks, ret_gain.reshape(1, VB_W), w_out, *[w for w, _ in cast])
    return out[:4], tuple(out[4:])


def _mix_even_sample_body(x_ref, mod_ref, gn_ref, win_ref, qg_ref, kg_ref, sink_ref, rg_ref, wout_ref,
                          ck_ref, cv_ref, s0_ref,
                          o_ref, nk_ref, nv_ref, ns_ref, mix_ref):
    x = x_ref[...]
    sb, length, d = x.shape
    tb = sb * length
    w = ck_ref.shape[1]
    mod = _seq_mod(mod_ref)
    h = _modulate(x, mod, gn_ref[...], 0).reshape(tb, d).astype(BF16)
    proj = _dot(h, win_ref[...])
    qg = qg_ref[...]
    kg = kg_ref[...]
    rg = rg_ref[...]

    rows = A_GROUP * length
    qpos_c = lax.broadcasted_iota(jnp.int32, (rows, w), 0) % length
    kpos_c = lax.broadcasted_iota(jnp.int32, (rows, w), 1)
    dist_c = w + qpos_c - kpos_c
    valid_c = (dist_c >= 0) & (dist_c < WINDOW)
    qpos_n = lax.broadcasted_iota(jnp.int32, (rows, length), 0) % length
    kpos_n = lax.broadcasted_iota(jnp.int32, (rows, length), 1)
    dist_n = qpos_n - kpos_n
    valid_n = (dist_n >= 0) & (dist_n < WINDOW)
    row_g = lax.broadcasted_iota(jnp.int32, (rows, 1), 0) // length

    for kv in range(A_KV):
        lanes = slice(kv * A_HD, (kv + 1) * A_HD)
        kn = _head_rms(proj[:, OFF_KA + kv * A_HD:OFF_KA + (kv + 1) * A_HD], kg).reshape(sb, length, A_HD)
        vn = proj[:, OFF_VA + kv * A_HD:OFF_VA + (kv + 1) * A_HD].reshape(sb, length, A_HD)
        nk_ref[:, 0:w - length, lanes] = ck_ref[:, length:w, lanes]
        nv_ref[:, 0:w - length, lanes] = cv_ref[:, length:w, lanes]
        nk_ref[:, w - length:w, lanes] = kn
        nv_ref[:, w - length:w, lanes] = vn
        kc = ck_ref[:, :, lanes].astype(BF16)
        vc = cv_ref[:, :, lanes].astype(BF16)
        q4 = jnp.concatenate(
            [_head_rms(proj[:, OFF_QA + (kv * A_GROUP + g) * A_HD:OFF_QA + (kv * A_GROUP + g + 1) * A_HD], qg)
             .reshape(sb, length, A_HD) for g in range(A_GROUP)], axis=1).astype(BF16)
        slope = jnp.zeros((rows, 1), F32)
        sink = jnp.zeros((rows, 1), F32)
        for g in range(A_GROUP):
            hd = kv * A_GROUP + g
            slope = jnp.where(row_g == g, _alibi_slope(hd), slope)
            sink = jnp.where(row_g == g, sink_ref[hd], sink)
        scale = A_HD ** -0.5
        s_c = _bmm_nt(q4, kc) * scale - slope * dist_c.astype(F32)
        s_c = jnp.where(valid_c, s_c, NEG_INF)
        s_n = _bmm_nt(q4, kn.astype(BF16)) * scale - slope * dist_n.astype(F32)
        s_n = jnp.where(valid_n, s_n, NEG_INF)
        mx = jnp.maximum(jnp.maximum(jnp.max(s_c, axis=-1, keepdims=True),
                                     jnp.max(s_n, axis=-1, keepdims=True)), sink)
        p_c = jnp.exp(s_c - mx)
        p_n = jnp.exp(s_n - mx)
        den = (jnp.sum(p_c, axis=-1, keepdims=True) + jnp.sum(p_n, axis=-1, keepdims=True)
               + jnp.exp(sink - mx))
        o4 = (_bmm(p_c.astype(BF16), vc) + _bmm(p_n.astype(BF16), vn.astype(BF16))) / den
        for g in range(A_GROUP):
            hd = kv * A_GROUP + g
            mix_ref[:, hd * A_HD:(hd + 1) * A_HD] = o4[:, g * length:(g + 1) * length, :].reshape(tb, A_HD)

    for hb in range(B_HEADS):
        d_in, d_q, d_k, d_c = _ret_decay(hb, length)
        qb = proj[:, OFF_QB + hb * B_KD:OFF_QB + (hb + 1) * B_KD].reshape(sb, length, B_KD).astype(BF16)
        kb = proj[:, OFF_KB + hb * B_KD:OFF_KB + (hb + 1) * B_KD].reshape(sb, length, B_KD) * (B_KD ** -0.5)
        vb = proj[:, OFF_VB + hb * B_VD:OFF_VB + (hb + 1) * B_VD].reshape(sb, length, B_VD).astype(BF16)
        gb = proj[:, OFF_GB + hb * B_VD:OFF_GB + (hb + 1) * B_VD]
        state = s0_ref[:, hb]
        inner = _bmm_nt(qb, kb.astype(BF16)) * d_in
        o = _bmm(inner.astype(BF16), vb) + _bmm(qb, state.astype(BF16)) * d_q
        ns_ref[:, hb] = state * d_c + _bmm_tn((kb * d_k).astype(BF16), vb)
        mix_ref[:, QA_W + hb * B_VD:QA_W + (hb + 1) * B_VD] = _group_norm_gate(
            o.reshape(tb, B_VD), rg[:, hb * B_VD:(hb + 1) * B_VD], gb)

    out = _dot(mix_ref[...].astype(BF16), wout_ref[...])
    o_ref[...] = x + _gate(mod, 0, d) * out.reshape(sb, length, d)


def _mix_even_sample(x3, mod, layer, gain, w_in, q_gain, k_gain, sinks, ret_gain, w_out, cache_k, cache_v, state, sb):
    n, length, d = x3.shape
    in_w = w_in.shape[1]
    w = cache_k.shape[1]
    seq_spec = lambda shape: pl.BlockSpec((sb,) + shape, lambda i, _n=len(shape): (i,) + (0,) * _n)
    return pl.pallas_call(
        _mix_even_sample_body,
        grid=(n // sb,),
        in_specs=[seq_spec((length, d)), _mod_seq_spec(mod, layer, sb),
                  _const_spec((1, d)), _const_spec((d, in_w)),
                  _const_spec((1, A_HD)), _const_spec((1, A_HD)),
                  pl.BlockSpec(memory_space=pltpu.SMEM),
                  _const_spec((1, VB_W)), _const_spec((QA_W + VB_W, d)),
                  seq_spec((w, KA_W)), seq_spec((w, VA_W)),
                  pl.BlockSpec((None, sb, B_HEADS, B_KD, B_VD), lambda i: (0, i, 0, 0, 0))],
        out_specs=[seq_spec((length, d)), seq_spec((w, KA_W)), seq_spec((w, VA_W)),
                   seq_spec((B_HEADS, B_KD, B_VD))],
        out_shape=[jax.ShapeDtypeStruct(x3.shape, F32),
                   jax.ShapeDtypeStruct(cache_k.shape, F32),
                   jax.ShapeDtypeStruct(cache_v.shape, F32),
                   jax.ShapeDtypeStruct(state.shape[1:], F32)],
        scratch_shapes=[pltpu.VMEM((sb * length, QA_W + VB_W), F32)],
        compiler_params=_params(),
        name="mix_even_sample",
    )(x3, mod, gain.reshape(1, d), w_in, q_gain.reshape(1, A_HD), k_gain.reshape(1, A_HD),
      sinks, ret_gain.reshape(1, VB_W), w_out, cache_k, cache_v, state)


def _cmul(ar, ai, br, bi):
    return ar * br - ai * bi, ar * bi + ai * br


def _s5_lambda(a_re, a_im, log_dt):
    dt = jnp.exp(log_dt)
    mag = jnp.exp(a_re * dt)
    return mag * jnp.cos(a_im * dt), mag * jnp.sin(a_im * dt)


def _s5_prep_body(t_len, n_chunks, are_ref, aim_ref, ldt_ref, btr_ref, bti_ref, ctr_ref, cti_ref,
                  ard_ref, aid_ref, ldd_ref,
                  bblk_ref, cre_ref, cim_ref, lam_ref, pwb_ref, ltp_ref, b32_ref, cr32_ref, ci32_ref, pw32_ref):
    groups = btr_ref.shape[0]
    k_in, p = btr_ref.shape[1:]
    half = S5_OCT * p
    a_re = are_ref[...]
    a_im = aim_ref[...]
    lam_re, lam_im = _s5_lambda(a_re, a_im, ldt_ref[...])
    den = a_re * a_re + a_im * a_im
    n_re = lam_re - 1.0
    n_im = lam_im
    f_re = (n_re * a_re + n_im * a_im) / den
    f_im = (n_im * a_re - n_re * a_im) / den
    br = btr_ref[...]
    bi = bti_ref[...]
    bb_re = f_re * br - f_im * bi
    bb_im = f_re * bi + f_im * br
    b32_ref[...] = jnp.zeros_like(b32_ref)
    cr32_ref[...] = jnp.zeros_like(cr32_ref)
    ci32_ref[...] = jnp.zeros_like(ci32_ref)
    for g in range(groups):
        s, gl = divmod(g, S5_OCT)
        rows = slice(gl * k_in, (gl + 1) * k_in)
        cols = slice(gl * p, (gl + 1) * p)
        b32_ref[s, rows, cols] = bb_re[g]
        b32_ref[s, rows, half + gl * p:half + (gl + 1) * p] = bb_im[g]
        cr32_ref[s, cols, rows] = ctr_ref[g]
        ci32_ref[s, cols, rows] = cti_ref[g]
    bblk_ref[...] = b32_ref[...].astype(BF16)
    cre_ref[...] = cr32_ref[...].astype(BF16)
    cim_ref[...] = ci32_ref[...].astype(BF16)
    lam_re, lam_im = _s5_lambda(ard_ref[...], aid_ref[...], ldd_ref[...])
    lam_ref[0] = lam_re
    lam_ref[1] = lam_im
    n_oct = lam_re.shape[0]

    def both(cr, ci, s, rows):
        return jnp.concatenate([jnp.broadcast_to(cr[s:s + 1, :], (rows, half)),
                                jnp.broadcast_to(ci[s:s + 1, :], (rows, half))], axis=1)

    cr, ci = lam_re, lam_im
    for t in range(t_len):
        for s in range(n_oct):
            pw32_ref[s, t * ROWS:(t + 1) * ROWS, :] = both(cr, ci, s, ROWS)
        if t + 1 < t_len:
            cr, ci = _cmul(cr, ci, lam_re, lam_im)
    pwb_ref[...] = pw32_ref[...].astype(BF16)
    tr, ti = cr, ci
    ltp_ref[...] = jnp.zeros_like(ltp_ref)
    cr, ci = jnp.ones_like(lam_re), jnp.zeros_like(lam_im)
    for m in range(n_chunks + 1):
        for s in range(n_oct):
            if m < n_chunks:
                ltp_ref[s, n_chunks + m:n_chunks + m + 1, :] = both(cr, ci, s, 1)
            if m in (1, 2, 4, 8):
                i = (1, 2, 4, 8).index(m)
                ltp_ref[s, i:i + 1, :] = both(cr, ci, s, 1)
        cr, ci = _cmul(cr, ci, tr, ti)


def _s5_prep(a_re, a_im, log_dt, b_re, b_im, c_re, c_im, t_len, n_chunks):
    g, p = a_re.shape
    k = b_re.shape[-1]
    assert n_chunks == ROWS
    n_oct = g // S5_OCT
    half = S5_OCT * p
    dense = (n_oct, half)
    return pl.pallas_call(
        functools.partial(_s5_prep_body, t_len, n_chunks),
        out_shape=[jax.ShapeDtypeStruct((n_oct, S5_OCT * k, 2 * half), BF16),
                   jax.ShapeDtypeStruct((n_oct, half, S5_OCT * k), BF16),
                   jax.ShapeDtypeStruct((n_oct, half, S5_OCT * k), BF16),
                   jax.ShapeDtypeStruct((2,) + dense, F32),
                   jax.ShapeDtypeStruct((n_oct, t_len * ROWS, 2 * half), BF16),
                   jax.ShapeDtypeStruct((n_oct, 2 * n_chunks, 2 * half), F32)],
        scratch_shapes=[pltpu.VMEM((n_oct, S5_OCT * k, 2 * half), F32),
                        pltpu.VMEM((n_oct, half, S5_OCT * k), F32),
                        pltpu.VMEM((n_oct, half, S5_OCT * k), F32),
                        pltpu.VMEM((n_oct, t_len * ROWS, 2 * half), F32)],
        compiler_params=pltpu.CompilerParams(vmem_limit_bytes=VMEM_LIMIT),
        name="s5_prep",
    )(a_re.reshape(g, 1, p), a_im.reshape(g, 1, p), log_dt.reshape(g, 1, 1),
      jnp.swapaxes(b_re, 1, 2), jnp.swapaxes(b_im, 1, 2), jnp.swapaxes(c_re, 1, 2), jnp.swapaxes(c_im, 1, 2),
      a_re.reshape(dense), a_im.reshape(dense), jnp.broadcast_to(log_dt[:, None], (g, p)).reshape(dense))


def _gelu_glu_out(x, mod, y, u, dskip, glua_ref, glub_ref):
    sb, rows, d = x.shape
    y = y + dskip * u
    yg = jax.nn.gelu(y, approximate=True).astype(BF16)
    out = _dot(yg, glua_ref[...]) * jax.nn.sigmoid(_dot(yg, glub_ref[...]))
    return x + _gate(mod, 0, d) * out.reshape(sb, rows, d)


def _mix_odd_prompt_body(x_ref, xn_ref, mod_ref, gn_ref, bblk_ref, lam_ref, pwb_ref, ltp_ref, cre_ref, cim_ref,
                         dskip_ref, glua_ref, glub_ref, o_ref, hre_ref, him_ref,
                         un_ref, l0_ref, l1_ref, up0_ref, up1_ref, pt_ref):
    step = pl.program_id(0)
    sb, _, d = x_ref.shape
    tm = sb * ROWS
    t_len = tm // ROWS
    n_oct = bblk_ref.shape[0]
    half = bblk_ref.shape[2] // 2
    mod = _row_mod(mod_ref)

    def permuted_input(src_ref, up_dst):
        u = _modulate(src_ref[...], mod, gn_ref[...], 0).reshape(tm, d)
        for k in range(d // 128):
            un_ref[k] = u[:, k * 128:(k + 1) * 128]
        up = jnp.concatenate(
            [jnp.concatenate([un_ref[k, pl.ds(t, ROWS, stride=t_len), :] for k in range(d // 128)], axis=1)
             for t in range(t_len)], axis=0)
        up_dst[...] = up
        return up.astype(BF16)

    @pl.when(step == 0)
    def _():
        hre_ref[...] = jnp.zeros_like(hre_ref)
        him_ref[...] = jnp.zeros_like(him_ref)
        nat = lax.broadcasted_iota(jnp.int32, (tm, tm), 0)
        prm = lax.broadcasted_iota(jnp.int32, (tm, tm), 1)
        pt_ref[...] = jnp.where(prm == (nat % t_len) * ROWS + nat // t_len, 1.0, 0.0).astype(BF16)
        up16 = permuted_input(x_ref, up0_ref)
        for s in range(n_oct):
            l0_ref[s] = _dot(up16[:, s * 128:(s + 1) * 128], bblk_ref[s])

    def run(l_ref, l_next, up_ref, up_next):
        for s0 in range(0, n_oct, 2):
            pair = (s0, s0 + 1)
            lam_re = [jnp.broadcast_to(lam_ref[0, s:s + 1, :], (ROWS, half)) for s in pair]
            lam_im = [jnp.broadcast_to(lam_ref[1, s:s + 1, :], (ROWS, half)) for s in pair]

            def local_step(t, carry):
                r = pl.multiple_of(t * ROWS, ROWS)
                out = []
                for i, s in enumerate(pair):
                    h_re, h_im = carry[2 * i], carry[2 * i + 1]
                    n_re = lam_re[i] * h_re - lam_im[i] * h_im + l_ref[s, pl.ds(r, ROWS), 0:half]
                    n_im = lam_re[i] * h_im + lam_im[i] * h_re + l_ref[s, pl.ds(r, ROWS), half:2 * half]
                    l_ref[s, pl.ds(r, ROWS), 0:half] = n_re
                    l_ref[s, pl.ds(r, ROWS), half:2 * half] = n_im
                    out += [n_re, n_im]
                return tuple(out)

            zero = jnp.zeros((ROWS, half), F32)
            lax.fori_loop(0, t_len, local_step, (zero, zero, zero, zero), unroll=2)

        next16 = permuted_input(xn_ref, up_next)
        chunk = lax.broadcasted_iota(jnp.int32, (ROWS, half), 0)
        y_parts = []
        for s in range(n_oct):
            l_next[s] = _dot(next16[:, s * 128:(s + 1) * 128], bblk_ref[s])
            p_re = l_ref[s, tm - ROWS:tm, 0:half]
            p_im = l_ref[s, tm - ROWS:tm, half:2 * half]
            for i, sh in enumerate((1, 2, 4)):
                m_re, m_im = _cmul(ltp_ref[s, i:i + 1, 0:half], ltp_ref[s, i:i + 1, half:2 * half],
                                   jnp.where(chunk >= sh, pltpu.roll(p_re, sh, 0), 0.0),
                                   jnp.where(chunk >= sh, pltpu.roll(p_im, sh, 0), 0.0))
                p_re, p_im = p_re + m_re, p_im + m_im
            hin_re = jnp.broadcast_to(hre_ref[s:s + 1, :], (ROWS, half))
            hin_im = jnp.broadcast_to(him_ref[s:s + 1, :], (ROWS, half))
            m_re, m_im = _cmul(ltp_ref[s, ROWS:2 * ROWS, 0:half], ltp_ref[s, ROWS:2 * ROWS, half:2 * half],
                               hin_re, hin_im)
            st_re = m_re + jnp.where(chunk >= 1, pltpu.roll(p_re, 1, 0), 0.0)
            st_im = m_im + jnp.where(chunk >= 1, pltpu.roll(p_im, 1, 0), 0.0)
            m_re, m_im = _cmul(ltp_ref[s, 3:4, 0:half], ltp_ref[s, 3:4, half:2 * half],
                               hre_ref[s:s + 1, :], him_ref[s:s + 1, :])
            hre_ref[s:s + 1, :] = m_re + p_re[ROWS - 1:ROWS, :]
            him_ref[s:s + 1, :] = m_im + p_im[ROWS - 1:ROWS, :]
            loc = l_ref[s].astype(BF16).reshape(tm // BF16_ROWS, BF16_ROWS, 2 * half)
            pw = pwb_ref[s].reshape(tm // BF16_ROWS, BF16_ROWS, 2 * half)
            pair_re = jnp.concatenate([st_re, st_re], axis=0).astype(BF16)[None]
            pair_im = jnp.concatenate([st_im, st_im], axis=0).astype(BF16)[None]
            f_re, f_im = _cmul(pw[:, :, 0:half], pw[:, :, half:2 * half], pair_re, pair_im)
            hs_re = (loc[:, :, 0:half] + f_re).reshape(tm, half)
            hs_im = (loc[:, :, half:2 * half] + f_im).reshape(tm, half)
            y_parts.append(_dot(hs_re, cre_ref[s]) - _dot(hs_im, cim_ref[s]))

        y = jnp.concatenate(y_parts, axis=1) + dskip_ref[...] * up_ref[...]
        yg = jax.nn.gelu(y, approximate=True).astype(BF16)
        yn = _dot(pt_ref[...], yg).astype(BF16)
        out = _dot(yn, glua_ref[...]) * jax.nn.sigmoid(_dot(yn, glub_ref[...]))
        o_ref[...] = x_ref[...] + _gate(mod, 0, d) * out.reshape(sb, ROWS, d)

    @pl.when(step % 2 == 0)
    def _():
        run(l0_ref, l1_ref, up0_ref, up1_ref)

    @pl.when(step % 2 == 1)
    def _():
        run(l1_ref, l0_ref, up1_ref, up0_ref)


def _mix_odd_prompt(x3, mod, layer, prompt_row, gain, bblk, lam_d, pwb, ltp, cre, cim, dskip, glu_a, glu_b, sb):
    n8, _, d = x3.shape
    n_oct, kin, wid = bblk.shape
    half = wid // 2
    tm = sb * ROWS
    n_tiles = n8 // sb
    proj_buf = pltpu.VMEM((n_oct, tm, wid), F32)
    perm_buf = pltpu.VMEM((tm, d), F32)
    return pl.pallas_call(
        _mix_odd_prompt_body,
        grid=(n_tiles,),
        in_specs=[pl.BlockSpec((sb, ROWS, d), lambda i: (i, 0, 0)),
                  pl.BlockSpec((sb, ROWS, d), lambda i: (jnp.minimum(i + 1, n_tiles - 1), 0, 0)),
                  _mod_row_spec(mod, layer, prompt_row),
                  _const_spec((1, d)), _const_spec(bblk.shape), _const_spec(lam_d.shape),
                  _const_spec(pwb.shape), _const_spec(ltp.shape),
                  _const_spec(cre.shape), _const_spec(cim.shape), _const_spec((1, d)),
                  _const_spec((d, d)), _const_spec((d, d))],
        out_specs=[pl.BlockSpec((sb, ROWS, d), lambda i: (i, 0, 0)),
                   pl.BlockSpec((n_oct, half), lambda i: (0, 0)),
                   pl.BlockSpec((n_oct, half), lambda i: (0, 0))],
        out_shape=[jax.ShapeDtypeStruct(x3.shape, F32),
                   jax.ShapeDtypeStruct((n_oct, half), F32),
                   jax.ShapeDtypeStruct((n_oct, half), F32)],
        scratch_shapes=[pltpu.VMEM((d // 128, tm, 128), F32), proj_buf, proj_buf, perm_buf, perm_buf,
                        pltpu.VMEM((tm, tm), BF16)],
        compiler_params=_params(),
        name="mix_odd_prompt",
    )(x3, x3, mod, gain.reshape(1, d), bblk, lam_d, pwb, ltp, cre, cim, dskip.reshape(1, d), glu_a, glu_b)


def _mix_odd_sample_body(x_ref, mod_ref, gn_ref, bblk_ref, lam_ref, cre_ref, cim_ref, dskip_ref,
                         glua_ref, glub_ref, sre_ref, sim_ref, o_ref, nre_ref, nim_ref, d_ref, y_ref):
    x = x_ref[...]
    sb, length, d = x.shape
    tm = sb * length
    n_oct = bblk_ref.shape[0]
    half = bblk_ref.shape[2] // 2
    mod = _seq_mod(mod_ref)
    u = _modulate(x, mod, gn_ref[...], 0).reshape(tm, d)
    u16 = u.astype(BF16)
    n_ch = bblk_ref.shape[2] // 128
    hc = n_ch // 2
    for s in range(n_oct):
        bu = _dot(u16[:, s * 128:(s + 1) * 128], bblk_ref[s])
        for c in range(n_ch):
            d_ref[c] = bu[:, c * 128:(c + 1) * 128]
        lam_re = lam_ref[0, s:s + 1, :]
        lam_im = lam_ref[1, s:s + 1, :]
        h_re = sre_ref[:, s * half:(s + 1) * half]
        h_im = sim_ref[:, s * half:(s + 1) * half]
        for t in range(length):
            b_re = jnp.concatenate([d_ref[c, pl.ds(t, sb, stride=length), :] for c in range(hc)], axis=1)
            b_im = jnp.concatenate([d_ref[hc + c, pl.ds(t, sb, stride=length), :] for c in range(hc)], axis=1)
            n_re = lam_re * h_re - lam_im * h_im + b_re
            n_im = lam_re * h_im + lam_im * h_re + b_im
            for c in range(hc):
                d_ref[c, pl.ds(t, sb, stride=length), :] = n_re[:, c * 128:(c + 1) * 128]
                d_ref[hc + c, pl.ds(t, sb, stride=length), :] = n_im[:, c * 128:(c + 1) * 128]
            h_re, h_im = n_re, n_im
        nre_ref[:, s * half:(s + 1) * half] = h_re
        nim_ref[:, s * half:(s + 1) * half] = h_im
        hs_re = jnp.concatenate([d_ref[c] for c in range(hc)], axis=1)
        hs_im = jnp.concatenate([d_ref[hc + c] for c in range(hc)], axis=1)
        y_ref[:, s * 128:(s + 1) * 128] = (_dot(hs_re.astype(BF16), cre_ref[s])
                                           - _dot(hs_im.astype(BF16), cim_ref[s]))
    o_ref[...] = _gelu_glu_out(x, mod, y_ref[...], u, dskip_ref[...], glua_ref, glub_ref)


def _mix_odd_sample(x3, mod, layer, gain, bblk, lam_d, cre, cim, dskip, glu_a, glu_b, s_re, s_im, sb):
    n, length, d = x3.shape
    n_oct, kin, wid = bblk.shape
    tm = sb * length
    nstate = s_re.shape[1]
    return pl.pallas_call(
        _mix_odd_sample_body,
        grid=(n // sb,),
        in_specs=[pl.BlockSpec((sb, length, d), lambda i: (i, 0, 0)),
                  _mod_seq_spec(mod, layer, sb),
                  _const_spec((1, d)), _const_spec(bblk.shape), _const_spec(lam_d.shape),
                  _const_spec(cre.shape), _const_spec(cim.shape), _const_spec((1, d)),
                  _const_spec((d, d)), _const_spec((d, d)),
                  pl.BlockSpec((sb, nstate), lambda i: (i, 0)),
                  pl.BlockSpec((sb, nstate), lambda i: (i, 0))],
        out_specs=[pl.BlockSpec((sb, length, d), lambda i: (i, 0, 0)),
                   pl.BlockSpec((sb, nstate), lambda i: (i, 0)),
                   pl.BlockSpec((sb, nstate), lambda i: (i, 0))],
        out_shape=[jax.ShapeDtypeStruct(x3.shape, F32),
                   jax.ShapeDtypeStruct(s_re.shape, F32),
                   jax.ShapeDtypeStruct(s_im.shape, F32)],
        scratch_shapes=[pltpu.VMEM((wid // 128, tm, 128), F32), pltpu.VMEM((tm, d), F32)],
        compiler_params=_params(),
        name="mix_odd_sample",
    )(x3, mod, gain.reshape(1, d), bblk, lam_d, cre, cim, dskip.reshape(1, d), glu_a, glu_b, s_re, s_im)


def _pick(n, want):
    while n % want:
        want //= 2
    return max(want, 1)


def kernel(x_prompt, x_sample, cache_win_k, cache_win_v, state_ret, state_s5_re, state_s5_im, c_prompt, c_sample, ada_w, ada_b, norm_mix, norm_ffn, ffn_wg, ffn_wu, ffn_wd, even_w_in, even_q_gain, even_k_gain, even_sinks, even_ret_gain, even_w_out, odd_A_re, odd_A_im, odd_log_dt, odd_B_re, odd_B_im, odd_C_re, odd_C_im, odd_D, odd_glu_a, odd_glu_b):
    bp, lp, d = x_prompt.shape
    ns, ls, _ = x_sample.shape
    assert bp == 1 and ls == ROWS and lp % WINDOW == 0
    w = cache_win_k.shape[2]
    groups, p_state = odd_A_re.shape[1:]

    n_c = bp + ns
    n_pad = -n_c % ROWS
    c_all = jnp.concatenate([c_sample, c_prompt, jnp.zeros((n_pad, d), F32)], axis=0)
    mod = _adaln(c_all, ada_w, ada_b)

    bf = lambda t: t.astype(BF16)
    w_in, w_out = bf(even_w_in[0]), bf(even_w_out[0])

    xp = x_prompt.reshape(lp // ROWS, ROWS, d)
    xs = x_sample

    sb_p = _pick(lp // ROWS, 32)
    sb_ffn = _pick(lp // ROWS, 64)
    sb_s = _pick(ns, 64)
    sb_s_even = _pick(ns, 16)

    (xp, p_k, p_v, p_ret), ffn0 = _mix_even_prompt(xp, mod, 0, ns, norm_mix[0], w_in, even_q_gain[0], even_k_gain[0],
                                                   even_sinks[0], even_ret_gain[0], w_out, sb_ffn,
                                                   cast=((ffn_wg, 0), (ffn_wu, 0), (ffn_wd, 0)))
    xs, s_k, s_v, s_ret = _mix_even_sample(xs, mod, 0, norm_mix[0], w_in, even_q_gain[0], even_k_gain[0],
                                           even_sinks[0], even_ret_gain[0], w_out,
                                           cache_win_k[0].reshape(ns, w, KA_W), cache_win_v[0].reshape(ns, w, VA_W),
                                           state_ret, sb_s_even)
    xp, xs, (wg1, wu1, wd1, glu_a, glu_b) = _ffn(
        xp, xs, mod, 0, norm_ffn[0], *ffn0, sb_ffn,
        cast=((ffn_wg, 1), (ffn_wu, 1), (ffn_wd, 1), (odd_glu_a, 0), (odd_glu_b, 0)))

    t_len = sb_p
    bblk, cre, cim, lam_d, pwb, ltp = _s5_prep(odd_A_re[0], odd_A_im[0], odd_log_dt[0], odd_B_re[0], odd_B_im[0],
                                               odd_C_re[0], odd_C_im[0], t_len, ROWS)
    xp, p_re, p_im = _mix_odd_prompt(xp, mod, 1, ns, norm_mix[1], bblk, lam_d, pwb, ltp, cre, cim, odd_D[0],
                                     glu_a, glu_b, sb_p)
    xs, s_re, s_im = _mix_odd_sample(xs, mod, 1, norm_mix[1], bblk, lam_d, cre, cim, odd_D[0], glu_a, glu_b,
                                     state_s5_re[0].reshape(ns, groups * p_state),
                                     state_s5_im[0].reshape(ns, groups * p_state), sb_s)
    xp, xs, _ = _ffn(xp, xs, mod, 1, norm_ffn[1], wg1, wu1, wd1, sb_ffn)

    y_prompt = xp.reshape(bp, lp, d)
    y_sample = xs
    return (y_prompt, y_sample,
            p_k.reshape(1, bp, WINDOW, A_KV, A_HD), p_v.reshape(1, bp, WINDOW, A_KV, A_HD),
            p_ret.reshape(1, bp, B_HEADS, B_KD, B_VD),
            p_re.reshape(1, bp, groups, p_state), p_im.reshape(1, bp, groups, p_state),
            s_k.reshape(1, ns, w, A_KV, A_HD), s_v.reshape(1, ns, w, A_KV, A_HD),
            s_ret.reshape(1, ns, B_HEADS, B_KD, B_VD),
            s_re.reshape(1, ns, groups, p_state), s_im.reshape(1, ns, groups, p_state))
```

```python
import functools
import math

import jax
import jax.numpy as jnp
from jax import lax
from jax.experimental import pallas as pl
from jax.experimental.pallas import tpu as pltpu

F32 = jnp.float32
BF16 = jnp.bfloat16

EPS = 1e-6
NEG_INF = -1e30
ROWS = 8

A_HEADS, A_KV, A_GROUP, A_HD = 8, 2, 4, 64
WINDOW = 128
B_HEADS, B_KD, B_VD = 4, 128, 128
S5_GROUP, S5_STATE = 16, 64
S5_OCT = 8

QA_W, KA_W, VA_W = A_HEADS * A_HD, A_KV * A_HD, A_KV * A_HD
QB_W, KB_W, VB_W, GB_W = B_HEADS * B_KD, B_HEADS * B_KD, B_HEADS * B_VD, B_HEADS * B_VD
OFF_QA = 0
OFF_KA = OFF_QA + QA_W
OFF_VA = OFF_KA + KA_W
OFF_QB = OFF_VA + VA_W
OFF_KB = OFF_QB + QB_W
OFF_VB = OFF_KB + KB_W
OFF_GB = OFF_VB + VB_W

VMEM_LIMIT = 56 * 1024 * 1024


def _ret_log_gamma(h):
    return math.log1p(-(2.0 ** (-5.0 - h)))


def _alibi_slope(h):
    return 2.0 ** (-8.0 * (h + 1) / A_HEADS)


def _const_spec(shape):
    nd = len(shape)
    return pl.BlockSpec(shape, lambda i, _n=nd: (0,) * _n, pipeline_mode=pl.Buffered(1))


def _params():
    return pltpu.CompilerParams(dimension_semantics=("arbitrary",), vmem_limit_bytes=VMEM_LIMIT)


def _dot(a, b):
    return jnp.dot(a, b, preferred_element_type=F32)


def _dot_nt(a, b):
    return lax.dot_general(a, b, (((1,), (1,)), ((), ())), preferred_element_type=F32)


def _bmm(a, b):
    return lax.dot_general(a, b, (((2,), (1,)), ((0,), (0,))), preferred_element_type=F32)


def _bmm_nt(a, b):
    return lax.dot_general(a, b, (((2,), (2,)), ((0,), (0,))), preferred_element_type=F32)


def _bmm_tn(a, b):
    return lax.dot_general(a, b, (((1,), (1,)), ((0,), (0,))), preferred_element_type=F32)


def _rms(x, g):
    return x * lax.rsqrt(jnp.mean(x * x, axis=-1, keepdims=True) + EPS) * g


def _modulate(x3, mod, gain, which):
    d = x3.shape[-1]
    sh = mod[:, :, (3 * which) * d:(3 * which + 1) * d]
    sc = mod[:, :, (3 * which + 1) * d:(3 * which + 2) * d]
    return _rms(x3, gain) * (1.0 + sc) + sh


def _mod_row_spec(mod, layer, row):
    assert row % ROWS == 0
    return pl.BlockSpec((None, ROWS, mod.shape[-1]), lambda i: (layer, row // ROWS, 0))


def _mod_seq_spec(mod, layer, sb, first_step=0):
    return pl.BlockSpec((None, sb, mod.shape[-1]), lambda i: (layer, jnp.maximum(i - first_step, 0), 0))


def _row_mod(mod_ref):
    return mod_ref[0:1, :][:, None, :]


def _seq_mod(mod_ref):
    return mod_ref[...][:, None, :]


def _gate(mod, which, d):
    return mod[:, :, (3 * which + 2) * d:(3 * which + 3) * d]


def _adaln_body(c_ref, w_ref, b_ref, o_ref):
    c = c_ref[...]
    a = (c * jax.nn.sigmoid(c)).astype(BF16)
    o_ref[0] = _dot(a, w_ref[0].astype(BF16)) + b_ref[0]


def _adaln(c_all, ada_w, ada_b):
    depth, d, n = ada_w.shape
    r = c_all.shape[0]
    tn = 1536
    return pl.pallas_call(
        _adaln_body,
        grid=(depth, n // tn),
        in_specs=[pl.BlockSpec((r, d), lambda l, j: (0, 0)),
                  pl.BlockSpec((1, d, tn), lambda l, j: (l, 0, j)),
                  pl.BlockSpec((1, 1, tn), lambda l, j: (l, 0, j))],
        out_specs=pl.BlockSpec((1, r, tn), lambda l, j: (l, 0, j)),
        out_shape=jax.ShapeDtypeStruct((depth, r, n), F32),
        compiler_params=pltpu.CompilerParams(dimension_semantics=("arbitrary", "arbitrary"),
                                             vmem_limit_bytes=VMEM_LIMIT),
        name="adaln",
    )(c_all, ada_w, ada_b.reshape(depth, 1, n))


BF16_ROWS = 16


def _cast_specs(job, n_steps):
    w, layer = job
    _, rows, cols = w.shape
    hold = 1
    while rows % (n_steps // hold) or (rows // (n_steps // hold)) % BF16_ROWS:
        hold *= 2
        assert hold <= n_steps and n_steps % hold == 0
    blk = rows // (n_steps // hold)
    src = pl.BlockSpec((None, blk, cols),
                       lambda i, _h=hold, _l=layer: (_l, jnp.minimum(i, n_steps - 1) // _h, 0))
    dst = pl.BlockSpec((blk, cols), lambda i, _h=hold: (jnp.minimum(i, n_steps - 1) // _h, 0))
    return src, dst, jax.ShapeDtypeStruct((rows, cols), BF16)


def _cast_blocks(in_refs, out_refs):
    for src, dst in zip(in_refs, out_refs):
        dst[...] = src[...].astype(BF16)


def _ffn_body(n_cast, n_prompt, xp_ref, mp_ref, xs_ref, ms_ref, gn_ref, wg_ref, wu_ref, wd_ref, *refs):
    cast_in, (op_ref, os_ref), cast_out = refs[:n_cast], refs[n_cast:n_cast + 2], refs[n_cast + 2:]
    step = pl.program_id(0)

    def tile(x_ref, mod, o_ref):
        x = x_ref[...]
        sb, _, d = x.shape
        h = _modulate(x, mod, gn_ref[...], 1).reshape(sb * ROWS, d).astype(BF16)
        a = _dot(h, wg_ref[...])
        b = _dot(h, wu_ref[...])
        act = (a * jax.nn.sigmoid(a) * b).astype(BF16)
        y = _dot(act, wd_ref[...])
        o_ref[...] = x + _gate(mod, 1, d) * y.reshape(sb, ROWS, d)

    @pl.when(step < n_prompt)
    def _():
        tile(xp_ref, _row_mod(mp_ref), op_ref)

    @pl.when(step >= n_prompt)
    def _():
        tile(xs_ref, _seq_mod(ms_ref), os_ref)

    _cast_blocks(cast_in, cast_out)


def _ffn(xp3, xs3, mod, layer, gain, wg, wu, wd, sb, cast=()):
    n8, _, d = xp3.shape
    f = wg.shape[1]
    n_prompt = n8 // sb
    n_sample = xs3.shape[0] // sb
    prompt_row = xs3.shape[0]
    assert n8 % sb == 0 and xs3.shape[0] % sb == 0
    cast_specs = [_cast_specs(w, n_prompt) for w in cast]
    p_idx = lambda i: (jnp.minimum(i, n_prompt - 1), 0, 0)
    s_idx = lambda i: (jnp.maximum(i - n_prompt, 0), 0, 0)
    xp_spec = pl.BlockSpec((sb, ROWS, d), p_idx)
    xs_spec = pl.BlockSpec((sb, ROWS, d), s_idx)
    out = pl.pallas_call(
        functools.partial(_ffn_body, len(cast), n_prompt),
        grid=(n_prompt + n_sample,),
        in_specs=[xp_spec, _mod_row_spec(mod, layer, prompt_row),
                  xs_spec, _mod_seq_spec(mod, layer, sb, n_prompt),
                  _const_spec((1, d)), _const_spec((d, f)), _const_spec((d, f)),
                  _const_spec((f, d))] + [c[0] for c in cast_specs],
        out_specs=[xp_spec, xs_spec] + [c[1] for c in cast_specs],
        out_shape=[jax.ShapeDtypeStruct(xp3.shape, F32), jax.ShapeDtypeStruct(xs3.shape, F32)]
        + [c[2] for c in cast_specs],
        compiler_params=_params(),
        name="ffn",
    )(xp3, mod, xs3, mod, gain.reshape(1, d), wg, wu, wd, *[w for w, _ in cast])
    return out[0], out[1], tuple(out[2:])


def _head_rms(t, g):
    return t * lax.rsqrt(jnp.mean(t * t, axis=-1, keepdims=True) + EPS) * g


def _group_norm_gate(o, gain, gate):
    mu = jnp.mean(o, axis=-1, keepdims=True)
    var = jnp.mean(jnp.square(o - mu), axis=-1, keepdims=True)
    return (o - mu) * lax.rsqrt(var + EPS) * gain * (gate * jax.nn.sigmoid(gate))


def _ret_decay(hb, c):
    lg = _ret_log_gamma(hb)
    ii = lax.broadcasted_iota(jnp.int32, (c, c), 0)
    jj = lax.broadcasted_iota(jnp.int32, (c, c), 1)
    diff = (ii - jj).astype(F32)
    d_in = jnp.where(diff >= 0, jnp.exp(lg * jnp.maximum(diff, 0.0)), 0.0)
    row = lax.broadcasted_iota(jnp.int32, (c, B_KD), 0).astype(F32)
    d_q = jnp.exp(lg * (row + 1.0))
    d_k = jnp.exp(lg * (c - 1.0 - row))
    d_c = math.exp(lg * c)
    return d_in, d_q, d_k, d_c


P_OFF_KA = QA_W
P_OFF_VA = P_OFF_KA + KA_W
P_OFF_QB = P_OFF_VA + VA_W
P_OFF_VB = P_OFF_QB + QB_W
P_OFF_GB = P_OFF_VB + VB_W
P_WIDTH = P_OFF_GB + GB_W
PAIR = 2 * A_HD


def _twice(t):
    low = lax.broadcasted_iota(jnp.int32, t.shape, 1) < A_HD
    swapped = pltpu.roll(t, A_HD, 1)
    return jnp.concatenate([jnp.where(low, t, swapped), jnp.where(low, swapped, t)], axis=1)


def _mix_even_prompt_body(n_cast, x_ref, mod_ref, gn_ref, win_ref, wkt_ref, qg_ref, kg_ref, sink_ref, rg_ref, wout_ref,
                          *refs):
    cast_in, refs = refs[:n_cast], refs[n_cast:]
    o_ref, pk_ref, pv_ref, ps_ref = refs[:4]
    cast_out = refs[4:4 + n_cast]
    mix_ref, carry_ref, bias_ref, dec_ref, ones_ref = refs[4 + n_cast:]
    _cast_blocks(cast_in, cast_out)
    step = pl.program_id(0)
    blk = WINDOW
    rows4 = A_GROUP * blk

    @pl.when(step == 0)
    def _():
        carry_ref[...] = jnp.zeros_like(carry_ref)
        ps_ref[...] = jnp.zeros_like(ps_ref)
        er = lax.broadcasted_iota(jnp.int32, ones_ref.shape, 0) // A_HD
        ec = lax.broadcasted_iota(jnp.int32, ones_ref.shape, 1) // A_HD
        ones_ref[...] = jnp.where(er == ec, 1.0 / A_HD, 0.0).astype(BF16)
        row = lax.broadcasted_iota(jnp.int32, (rows4, 2 * blk), 0)
        dist = row % blk + blk - lax.broadcasted_iota(jnp.int32, (rows4, 2 * blk), 1)
        in_window = (dist >= 0) & (dist < WINDOW)
        for kv in range(A_KV):
            slope = jnp.zeros((rows4, 2 * blk), F32)
            for g in range(A_GROUP):
                slope = jnp.where(row // blk == g, _alibi_slope(kv * A_GROUP + g), slope)
            bias_ref[kv] = jnp.where(in_window, slope * dist.astype(F32), -NEG_INF)
        for hb in range(B_HEADS):
            lg = _ret_log_gamma(hb)
            ii = lax.broadcasted_iota(jnp.int32, (blk, blk), 0).astype(F32)
            jj = lax.broadcasted_iota(jnp.int32, (blk, blk), 1).astype(F32)
            diff = ii - jj
            dec_ref[hb, 0] = jnp.where(diff >= 0, jnp.exp(lg * jnp.maximum(diff, 0.0)), 0.0)
            dec_ref[hb, 1] = jnp.exp(lg * (ii + 1.0))
            dec_ref[hb, 2] = jnp.exp(lg * (blk - 1.0 - jj))

    x = x_ref[...]
    sb, _, d = x.shape
    tb = sb * ROWS
    mod = _row_mod(mod_ref)
    h = _modulate(x, mod, gn_ref[...], 0).reshape(tb, d).astype(BF16)
    proj = _dot(h, win_ref[...])
    kt_all = _dot_nt(wkt_ref[...], h) * (B_KD ** -0.5)
    rg = rg_ref[...]

    qa = proj[:, 0:QA_W]
    ka = proj[:, P_OFF_KA:P_OFF_KA + KA_W]
    va = proj[:, P_OFF_VA:P_OFF_VA + VA_W]
    q_hat = (qa * lax.rsqrt(_dot((qa * qa).astype(BF16), ones_ref[...]) + EPS) * qg_ref[...]
             * (A_HD ** -0.5)).astype(BF16)
    k_hat = ka * lax.rsqrt(_dot((ka * ka).astype(BF16), ones_ref[0:KA_W, 0:KA_W]) + EPS) * kg_ref[...]
    k_hat16 = _twice(k_hat).astype(BF16)
    va16 = _twice(va).astype(BF16)
    prev = carry_ref[step % 2]
    carry_ref[(step + 1) % 2] = jnp.concatenate([k_hat16[tb - blk:tb], va16[tb - blk:tb]], axis=1)

    row_g = lax.broadcasted_iota(jnp.int32, (rows4, 1), 0) // blk
    key_is_prev = lax.broadcasted_iota(jnp.int32, (rows4, 2 * blk), 1) < blk
    first_penalty = jnp.where(step == 0, -NEG_INF, 0.0)
    lane_low = lax.broadcasted_iota(jnp.int32, (blk, PAIR), 1) < A_HD
    ones_cols = jnp.ones((2 * blk, PAIR), BF16)

    n_blk = tb // blk
    att = [(j, kv) for j in range(n_blk) for kv in range(A_KV)]
    ret = [(j, hb) for j in range(n_blk) for hb in range(B_HEADS)]

    sinks, scores = {}, {}
    for j, kv in att:
        r0 = j * blk
        kcol = slice(kv * PAIR, (kv + 1) * PAIR)
        if j == 0:
            k2 = jnp.concatenate([prev[:, kcol], k_hat16[0:blk, kcol]], axis=0)
        else:
            k2 = k_hat16[r0 - blk:r0 + blk, kcol]
        q4 = jnp.concatenate(
            [jnp.where(lane_low == (g % 2 == 0),
                       q_hat[r0:r0 + blk, (kv * A_GROUP + g - g % 2) * A_HD:(kv * A_GROUP + g - g % 2 + 2) * A_HD],
                       jnp.zeros((), BF16))
             for g in range(A_GROUP)], axis=0)
        scores[j, kv] = _dot_nt(q4, k2)
        sink = jnp.zeros((rows4, 1), F32)
        for g in range(A_GROUP):
            sink = jnp.where(row_g == g, sink_ref[kv * A_GROUP + g], sink)
        sinks[j, kv] = sink
    qb, vb, kt, inner = {}, {}, {}, {}
    for j, hb in ret:
        r0 = j * blk
        qb[j, hb] = proj[r0:r0 + blk, P_OFF_QB + hb * B_KD:P_OFF_QB + (hb + 1) * B_KD].astype(BF16)
        vb[j, hb] = proj[r0:r0 + blk, P_OFF_VB + hb * B_VD:P_OFF_VB + (hb + 1) * B_VD].astype(BF16)
        kt[j, hb] = kt_all[hb * B_KD:(hb + 1) * B_KD, r0:r0 + blk]
        inner[j, hb] = _dot(qb[j, hb], kt[j, hb].astype(BF16))

    probs, maxes = {}, {}
    for j, kv in att:
        s = scores[j, kv] - bias_ref[kv]
        if j == 0:
            s = s - jnp.where(key_is_prev, first_penalty, 0.0)
        mx = jnp.maximum(jnp.max(s, axis=-1, keepdims=True), sinks[j, kv])
        probs[j, kv] = jnp.exp(s - mx).astype(BF16)
        maxes[j, kv] = mx
    state = {}
    for hb in range(B_HEADS):
        d_c = math.exp(_ret_log_gamma(hb) * blk)
        state[0, hb] = ps_ref[hb]
        for j in range(n_blk):
            state[j + 1, hb] = state[j, hb] * d_c + _dot((kt[j, hb] * dec_ref[hb, 2]).astype(BF16), vb[j, hb])
        ps_ref[hb] = state[n_blk, hb]

    for j, kv in att:
        r0 = j * blk
        kcol = slice(kv * PAIR, (kv + 1) * PAIR)
        vcol = slice(2 * KA_W + kv * PAIR, 2 * KA_W + (kv + 1) * PAIR)
        if j == 0:
            v2 = jnp.concatenate([prev[:, vcol], va16[0:blk, kcol]], axis=0)
        else:
            v2 = va16[r0 - blk:r0 + blk, kcol]
        pv = _dot(probs[j, kv], jnp.concatenate([v2, ones_cols], axis=1))
        o4 = pv[:, 0:PAIR] / (pv[:, PAIR:2 * PAIR] + jnp.exp(sinks[j, kv] - maxes[j, kv]))
        for g in range(A_GROUP):
            hd = kv * A_GROUP + g
            half = slice((hd % 2) * A_HD, (hd % 2 + 1) * A_HD)
            mix_ref[r0:r0 + blk, hd * A_HD:(hd + 1) * A_HD] = o4[g * blk:(g + 1) * blk, half]
    o_ret = {}
    for j, hb in ret:
        o_ret[j, hb] = (_dot((inner[j, hb] * dec_ref[hb, 0]).astype(BF16), vb[j, hb])
                        + _dot(qb[j, hb], state[j, hb].astype(BF16)) * dec_ref[hb, 1])

    cen = {k: o_ret[k] - jnp.mean(o_ret[k], axis=-1, keepdims=True) for k in ret}
    var = {k: jnp.mean(cen[k] * cen[k], axis=-1, keepdims=True) for k in ret}
    for j, hb in ret:
        r0 = j * blk
        gb = proj[r0:r0 + blk, P_OFF_GB + hb * B_VD:P_OFF_GB + (hb + 1) * B_VD]
        mix_ref[r0:r0 + blk, QA_W + hb * B_VD:QA_W + (hb + 1) * B_VD] = (
            cen[j, hb] * lax.rsqrt(var[j, hb] + EPS) * rg[:, hb * B_VD:(hb + 1) * B_VD] * (gb * jax.nn.sigmoid(gb)))

    out = _dot(mix_ref[...].astype(BF16), wout_ref[...])
    o_ref[...] = x + _gate(mod, 0, d) * out.reshape(sb, ROWS, d)

    @pl.when(step == pl.num_programs(0) - 1)
    def _():
        pk_ref[...] = k_hat[tb - blk:tb, :]
        pv_ref[...] = va[tb - blk:tb, :]


def _mix_even_prompt(x3, mod, layer, prompt_row, gain, w_in, q_gain, k_gain, sinks, ret_gain, w_out, sb, cast=()):
    n8, _, d = x3.shape
    tb = sb * ROWS
    cast_specs = [_cast_specs(w, n8 // sb) for w in cast]
    w_main = jnp.concatenate([w_in[:, :OFF_KB], w_in[:, OFF_VB:]], axis=1)
    wk_t = w_in[:, OFF_KB:OFF_VB].T
    out = pl.pallas_call(
        functools.partial(_mix_even_prompt_body, len(cast)),
        grid=(n8 // sb,),
        in_specs=[pl.BlockSpec((sb, ROWS, d), lambda i: (i, 0, 0)),
                  _mod_row_spec(mod, layer, prompt_row),
                  _const_spec((1, d)), _const_spec((d, P_WIDTH)), _const_spec((KB_W, d)),
                  _const_spec((1, QA_W)), _const_spec((1, KA_W)),
                  pl.BlockSpec(memory_space=pltpu.SMEM),
                  _const_spec((1, VB_W)), _const_spec((QA_W + VB_W, d))] + [c[0] for c in cast_specs],
        out_specs=[pl.BlockSpec((sb, ROWS, d), lambda i: (i, 0, 0)),
                   pl.BlockSpec((WINDOW, KA_W), lambda i: (0, 0)),
                   pl.BlockSpec((WINDOW, VA_W), lambda i: (0, 0)),
                   pl.BlockSpec((B_HEADS, B_KD, B_VD), lambda i: (0, 0, 0))] + [c[1] for c in cast_specs],
        out_shape=[jax.ShapeDtypeStruct(x3.shape, F32),
                   jax.ShapeDtypeStruct((WINDOW, KA_W), F32),
                   jax.ShapeDtypeStruct((WINDOW, VA_W), F32),
                   jax.ShapeDtypeStruct((B_HEADS, B_KD, B_VD), F32)] + [c[2] for c in cast_specs],
        scratch_shapes=[pltpu.VMEM((tb, QA_W + VB_W), F32),
                        pltpu.VMEM((2, WINDOW, 2 * KA_W + 2 * VA_W), BF16),
                        pltpu.VMEM((A_KV, A_GROUP * WINDOW, 2 * WINDOW), F32),
                        pltpu.VMEM((B_HEADS, 3, WINDOW, WINDOW), F32),
                        pltpu.VMEM((QA_W, QA_W), BF16)],
        compiler_params=_params(),
        name="mix_even_prompt",
    )(x3, mod, gain.reshape(1, d), w_main, wk_t, jnp.tile(q_gain, A_HEADS).reshape(1, QA_W),
      jnp.tile(k_gain, A_KV).reshape(1, KA_W),
      sinks, ret_gain.reshape(1, VB_W), w_out, *[w for w, _ in cast])
    return out[:4], tuple(out[4:])


def _mix_even_sample_body(x_ref, mod_ref, gn_ref, win_ref, qg_ref, kg_ref, sink_ref, rg_ref, wout_ref,
                          ck_ref, cv_ref, s0_ref,
                          o_ref, nk_ref, nv_ref, ns_ref, mix_ref):
    x = x_ref[...]
    sb, length, d = x.shape
    tb = sb * length
    w = ck_ref.shape[1]
    mod = _seq_mod(mod_ref)
    h = _modulate(x, mod, gn_ref[...], 0).reshape(tb, d).astype(BF16)
    proj = _dot(h, win_ref[...])
    qg = qg_ref[...]
    kg = kg_ref[...]
    rg = rg_ref[...]

    rows = A_GROUP * length
    qpos_c = lax.broadcasted_iota(jnp.int32, (rows, w), 0) % length
    kpos_c = lax.broadcasted_iota(jnp.int32, (rows, w), 1)
    dist_c = w + qpos_c - kpos_c
    valid_c = (dist_c >= 0) & (dist_c < WINDOW)
    qpos_n = lax.broadcasted_iota(jnp.int32, (rows, length), 0) % length
    kpos_n = lax.broadcasted_iota(jnp.int32, (rows, length), 1)
    dist_n = qpos_n - kpos_n
    valid_n = (dist_n >= 0) & (dist_n < WINDOW)
    row_g = lax.broadcasted_iota(jnp.int32, (rows, 1), 0) // length

    scale = A_HD ** -0.5
    kn, vn, s_c, s_n, vc, slopes, sinks = {}, {}, {}, {}, {}, {}, {}
    for kv in range(A_KV):
        lanes = slice(kv * A_HD, (kv + 1) * A_HD)
        kn[kv] = _head_rms(proj[:, OFF_KA + kv * A_HD:OFF_KA + (kv + 1) * A_HD], kg).reshape(sb, length, A_HD)
        vn[kv] = proj[:, OFF_VA + kv * A_HD:OFF_VA + (kv + 1) * A_HD].reshape(sb, length, A_HD)
        nk_ref[:, 0:w - length, lanes] = ck_ref[:, length:w, lanes]
        nv_ref[:, 0:w - length, lanes] = cv_ref[:, length:w, lanes]
        nk_ref[:, w - length:w, lanes] = kn[kv]
        nv_ref[:, w - length:w, lanes] = vn[kv]
        kc = ck_ref[:, :, lanes].astype(BF16)
        vc[kv] = cv_ref[:, :, lanes].astype(BF16)
        q4 = jnp.concatenate(
            [_head_rms(proj[:, OFF_QA + (kv * A_GROUP + g) * A_HD:OFF_QA + (kv * A_GROUP + g + 1) * A_HD], qg)
             .reshape(sb, length, A_HD) for g in range(A_GROUP)], axis=1).astype(BF16)
        slope = jnp.zeros((rows, 1), F32)
        sink = jnp.zeros((rows, 1), F32)
        for g in range(A_GROUP):
            hd = kv * A_GROUP + g
            slope = jnp.where(row_g == g, _alibi_slope(hd), slope)
            sink = jnp.where(row_g == g, sink_ref[hd], sink)
        slopes[kv], sinks[kv] = slope, sink
        s_c[kv] = _bmm_nt(q4, kc)
        s_n[kv] = _bmm_nt(q4, kn[kv].astype(BF16))
    decay = {hb: _ret_decay(hb, length) for hb in range(B_HEADS)}
    qb, kb, vb, inner, q_state = {}, {}, {}, {}, {}
    for hb in range(B_HEADS):
        qb[hb] = proj[:, OFF_QB + hb * B_KD:OFF_QB + (hb + 1) * B_KD].reshape(sb, length, B_KD).astype(BF16)
        kb[hb] = proj[:, OFF_KB + hb * B_KD:OFF_KB + (hb + 1) * B_KD].reshape(sb, length, B_KD) * (B_KD ** -0.5)
        vb[hb] = proj[:, OFF_VB + hb * B_VD:OFF_VB + (hb + 1) * B_VD].reshape(sb, length, B_VD).astype(BF16)
        inner[hb] = _bmm_nt(qb[hb], kb[hb].astype(BF16))
        q_state[hb] = _bmm(qb[hb], s0_ref[:, hb].astype(BF16))

    p_c, p_n, den = {}, {}, {}
    for kv in range(A_KV):
        a_c = jnp.where(valid_c, s_c[kv] * scale - slopes[kv] * dist_c.astype(F32), NEG_INF)
        a_n = jnp.where(valid_n, s_n[kv] * scale - slopes[kv] * dist_n.astype(F32), NEG_INF)
        mx = jnp.maximum(jnp.maximum(jnp.max(a_c, axis=-1, keepdims=True),
                                     jnp.max(a_n, axis=-1, keepdims=True)), sinks[kv])
        p_c[kv] = jnp.exp(a_c - mx)
        p_n[kv] = jnp.exp(a_n - mx)
        den[kv] = (jnp.sum(p_c[kv], axis=-1, keepdims=True) + jnp.sum(p_n[kv], axis=-1, keepdims=True)
                   + jnp.exp(sinks[kv] - mx))

    for kv in range(A_KV):
        o4 = (_bmm(p_c[kv].astype(BF16), vc[kv]) + _bmm(p_n[kv].astype(BF16), vn[kv].astype(BF16))) / den[kv]
        for g in range(A_GROUP):
            hd = kv * A_GROUP + g
            mix_ref[:, hd * A_HD:(hd + 1) * A_HD] = o4[:, g * length:(g + 1) * length, :].reshape(tb, A_HD)
    o_ret = {}
    for hb in range(B_HEADS):
        d_in, d_q, d_k, d_c = decay[hb]
        o_ret[hb] = _bmm((inner[hb] * d_in).astype(BF16), vb[hb]) + q_state[hb] * d_q
        ns_ref[:, hb] = s0_ref[:, hb] * d_c + _bmm_tn((kb[hb] * d_k).astype(BF16), vb[hb])

    for hb in range(B_HEADS):
        gb = proj[:, OFF_GB + hb * B_VD:OFF_GB + (hb + 1) * B_VD]
        mix_ref[:, QA_W + hb * B_VD:QA_W + (hb + 1) * B_VD] = _group_norm_gate(
            o_ret[hb].reshape(tb, B_VD), rg[:, hb * B_VD:(hb + 1) * B_VD], gb)

    out = _dot(mix_ref[...].astype(BF16), wout_ref[...])
    o_ref[...] = x + _gate(mod, 0, d) * out.reshape(sb, length, d)


def _mix_even_sample(x3, mod, layer, gain, w_in, q_gain, k_gain, sinks, ret_gain, w_out, cache_k, cache_v, state, sb):
    n, length, d = x3.shape
    in_w = w_in.shape[1]
    w = cache_k.shape[1]
    seq_spec = lambda shape: pl.BlockSpec((sb,) + shape, lambda i, _n=len(shape): (i,) + (0,) * _n)
    return pl.pallas_call(
        _mix_even_sample_body,
        grid=(n // sb,),
        in_specs=[seq_spec((length, d)), _mod_seq_spec(mod, layer, sb),
                  _const_spec((1, d)), _const_spec((d, in_w)),
                  _const_spec((1, A_HD)), _const_spec((1, A_HD)),
                  pl.BlockSpec(memory_space=pltpu.SMEM),
                  _const_spec((1, VB_W)), _const_spec((QA_W + VB_W, d)),
                  seq_spec((w, KA_W)), seq_spec((w, VA_W)),
                  pl.BlockSpec((None, sb, B_HEADS, B_KD, B_VD), lambda i: (0, i, 0, 0, 0))],
        out_specs=[seq_spec((length, d)), seq_spec((w, KA_W)), seq_spec((w, VA_W)),
                   seq_spec((B_HEADS, B_KD, B_VD))],
        out_shape=[jax.ShapeDtypeStruct(x3.shape, F32),
                   jax.ShapeDtypeStruct(cache_k.shape, F32),
                   jax.ShapeDtypeStruct(cache_v.shape, F32),
                   jax.ShapeDtypeStruct(state.shape[1:], F32)],
        scratch_shapes=[pltpu.VMEM((sb * length, QA_W + VB_W), F32)],
        compiler_params=_params(),
        name="mix_even_sample",
    )(x3, mod, gain.reshape(1, d), w_in, q_gain.reshape(1, A_HD), k_gain.reshape(1, A_HD),
      sinks, ret_gain.reshape(1, VB_W), w_out, cache_k, cache_v, state)


def _cmul(ar, ai, br, bi):
    return ar * br - ai * bi, ar * bi + ai * br


def _s5_lambda(a_re, a_im, log_dt):
    dt = jnp.exp(log_dt)
    mag = jnp.exp(a_re * dt)
    return mag * jnp.cos(a_im * dt), mag * jnp.sin(a_im * dt)


def _s5_prep_body(t_len, n_chunks, are_ref, aim_ref, ldt_ref, btr_ref, bti_ref, ctr_ref, cti_ref,
                  ard_ref, aid_ref, ldd_ref,
                  bblk_ref, cre_ref, cim_ref, lam_ref, pwb_ref, ltp_ref, b32_ref, cr32_ref, ci32_ref, pw32_ref):
    groups = btr_ref.shape[0]
    k_in, p = btr_ref.shape[1:]
    half = S5_OCT * p
    a_re = are_ref[...]
    a_im = aim_ref[...]
    lam_re, lam_im = _s5_lambda(a_re, a_im, ldt_ref[...])
    den = a_re * a_re + a_im * a_im
    n_re = lam_re - 1.0
    n_im = lam_im
    f_re = (n_re * a_re + n_im * a_im) / den
    f_im = (n_im * a_re - n_re * a_im) / den
    br = btr_ref[...]
    bi = bti_ref[...]
    bb_re = f_re * br - f_im * bi
    bb_im = f_re * bi + f_im * br
    b32_ref[...] = jnp.zeros_like(b32_ref)
    cr32_ref[...] = jnp.zeros_like(cr32_ref)
    ci32_ref[...] = jnp.zeros_like(ci32_ref)
    for g in range(groups):
        s, gl = divmod(g, S5_OCT)
        rows = slice(gl * k_in, (gl + 1) * k_in)
        cols = slice(gl * p, (gl + 1) * p)
        b32_ref[s, rows, cols] = bb_re[g]
        b32_ref[s, rows, half + gl * p:half + (gl + 1) * p] = bb_im[g]
        cr32_ref[s, cols, rows] = ctr_ref[g]
        ci32_ref[s, cols, rows] = cti_ref[g]
    bblk_ref[...] = b32_ref[...].astype(BF16)
    cre_ref[...] = cr32_ref[...].astype(BF16)
    cim_ref[...] = ci32_ref[...].astype(BF16)
    lam_re, lam_im = _s5_lambda(ard_ref[...], aid_ref[...], ldd_ref[...])
    lam_ref[0] = lam_re
    lam_ref[1] = lam_im
    n_oct = lam_re.shape[0]

    def both(cr, ci, s, rows):
        return jnp.concatenate([jnp.broadcast_to(cr[s:s + 1, :], (rows, half)),
                                jnp.broadcast_to(ci[s:s + 1, :], (rows, half))], axis=1)

    cr, ci = lam_re, lam_im
    for t in range(t_len):
        for s in range(n_oct):
            pw32_ref[s, t * ROWS:(t + 1) * ROWS, :] = both(cr, ci, s, ROWS)
        if t + 1 < t_len:
            cr, ci = _cmul(cr, ci, lam_re, lam_im)
    pwb_ref[...] = pw32_ref[...].astype(BF16)
    tr, ti = cr, ci
    ltp_ref[...] = jnp.zeros_like(ltp_ref)
    cr, ci = jnp.ones_like(lam_re), jnp.zeros_like(lam_im)
    for m in range(n_chunks + 1):
        for s in range(n_oct):
            if m < n_chunks:
                ltp_ref[s, n_chunks + m:n_chunks + m + 1, :] = both(cr, ci, s, 1)
            if m in (1, 2, 4, 8):
                i = (1, 2, 4, 8).index(m)
                ltp_ref[s, i:i + 1, :] = both(cr, ci, s, 1)
        cr, ci = _cmul(cr, ci, tr, ti)


def _s5_prep(a_re, a_im, log_dt, b_re, b_im, c_re, c_im, t_len, n_chunks):
    g, p = a_re.shape
    k = b_re.shape[-1]
    assert n_chunks == ROWS
    n_oct = g // S5_OCT
    half = S5_OCT * p
    dense = (n_oct, half)
    return pl.pallas_call(
        functools.partial(_s5_prep_body, t_len, n_chunks),
        out_shape=[jax.ShapeDtypeStruct((n_oct, S5_OCT * k, 2 * half), BF16),
                   jax.ShapeDtypeStruct((n_oct, half, S5_OCT * k), BF16),
                   jax.ShapeDtypeStruct((n_oct, half, S5_OCT * k), BF16),
                   jax.ShapeDtypeStruct((2,) + dense, F32),
                   jax.ShapeDtypeStruct((n_oct, t_len * ROWS, 2 * half), BF16),
                   jax.ShapeDtypeStruct((n_oct, 2 * n_chunks, 2 * half), F32)],
        scratch_shapes=[pltpu.VMEM((n_oct, S5_OCT * k, 2 * half), F32),
                        pltpu.VMEM((n_oct, half, S5_OCT * k), F32),
                        pltpu.VMEM((n_oct, half, S5_OCT * k), F32),
                        pltpu.VMEM((n_oct, t_len * ROWS, 2 * half), F32)],
        compiler_params=pltpu.CompilerParams(vmem_limit_bytes=VMEM_LIMIT),
        name="s5_prep",
    )(a_re.reshape(g, 1, p), a_im.reshape(g, 1, p), log_dt.reshape(g, 1, 1),
      jnp.swapaxes(b_re, 1, 2), jnp.swapaxes(b_im, 1, 2), jnp.swapaxes(c_re, 1, 2), jnp.swapaxes(c_im, 1, 2),
      a_re.reshape(dense), a_im.reshape(dense), jnp.broadcast_to(log_dt[:, None], (g, p)).reshape(dense))


def _gelu_glu_out(x, mod, y, u, dskip, glua_ref, glub_ref):
    sb, rows, d = x.shape
    y = y + dskip * u
    yg = jax.nn.gelu(y, approximate=True).astype(BF16)
    out = _dot(yg, glua_ref[...]) * jax.nn.sigmoid(_dot(yg, glub_ref[...]))
    return x + _gate(mod, 0, d) * out.reshape(sb, rows, d)


def _mix_odd_prompt_body(x_ref, xn_ref, mod_ref, gn_ref, bblk_ref, lam_ref, pwb_ref, ltp_ref, cre_ref, cim_ref,
                         dskip_ref, glua_ref, glub_ref, o_ref, hre_ref, him_ref,
                         un_ref, l0_ref, l1_ref, up0_ref, up1_ref, pt_ref):
    step = pl.program_id(0)
    sb, _, d = x_ref.shape
    tm = sb * ROWS
    t_len = tm // ROWS
    n_oct = bblk_ref.shape[0]
    half = bblk_ref.shape[2] // 2
    mod = _row_mod(mod_ref)

    def permuted_input(src_ref, up_dst):
        u = _modulate(src_ref[...], mod, gn_ref[...], 0).reshape(tm, d)
        for k in range(d // 128):
            un_ref[k] = u[:, k * 128:(k + 1) * 128]
        up = jnp.concatenate(
            [jnp.concatenate([un_ref[k, pl.ds(t, ROWS, stride=t_len), :] for k in range(d // 128)], axis=1)
             for t in range(t_len)], axis=0)
        up_dst[...] = up
        return up.astype(BF16)

    @pl.when(step == 0)
    def _():
        hre_ref[...] = jnp.zeros_like(hre_ref)
        him_ref[...] = jnp.zeros_like(him_ref)
        nat = lax.broadcasted_iota(jnp.int32, (tm, tm), 0)
        prm = lax.broadcasted_iota(jnp.int32, (tm, tm), 1)
        pt_ref[...] = jnp.where(prm == (nat % t_len) * ROWS + nat // t_len, 1.0, 0.0).astype(BF16)
        up16 = permuted_input(x_ref, up0_ref)
        for s in range(n_oct):
            l0_ref[s] = _dot(up16[:, s * 128:(s + 1) * 128], bblk_ref[s])

    def run(l_ref, l_next, up_ref, up_next):
        for s0 in range(0, n_oct, 2):
            pair = (s0, s0 + 1)
            lam_re = [jnp.broadcast_to(lam_ref[0, s:s + 1, :], (ROWS, half)) for s in pair]
            lam_im = [jnp.broadcast_to(lam_ref[1, s:s + 1, :], (ROWS, half)) for s in pair]

            def local_step(t, carry):
                r = pl.multiple_of(t * ROWS, ROWS)
                out = []
                for i, s in enumerate(pair):
                    h_re, h_im = carry[2 * i], carry[2 * i + 1]
                    n_re = lam_re[i] * h_re - lam_im[i] * h_im + l_ref[s, pl.ds(r, ROWS), 0:half]
                    n_im = lam_re[i] * h_im + lam_im[i] * h_re + l_ref[s, pl.ds(r, ROWS), half:2 * half]
                    l_ref[s, pl.ds(r, ROWS), 0:half] = n_re
                    l_ref[s, pl.ds(r, ROWS), half:2 * half] = n_im
                    out += [n_re, n_im]
                return tuple(out)

            zero = jnp.zeros((ROWS, half), F32)
            lax.fori_loop(0, t_len, local_step, (zero, zero, zero, zero), unroll=2)

        next16 = permuted_input(xn_ref, up_next)
        chunk = lax.broadcasted_iota(jnp.int32, (ROWS, half), 0)
        y_parts = []
        for s in range(n_oct):
            l_next[s] = _dot(next16[:, s * 128:(s + 1) * 128], bblk_ref[s])
            p_re = l_ref[s, tm - ROWS:tm, 0:half]
            p_im = l_ref[s, tm - ROWS:tm, half:2 * half]
            for i, sh in enumerate((1, 2, 4)):
                m_re, m_im = _cmul(ltp_ref[s, i:i + 1, 0:half], ltp_ref[s, i:i + 1, half:2 * half],
                                   jnp.where(chunk >= sh, pltpu.roll(p_re, sh, 0), 0.0),
                                   jnp.where(chunk >= sh, pltpu.roll(p_im, sh, 0), 0.0))
                p_re, p_im = p_re + m_re, p_im + m_im
            hin_re = jnp.broadcast_to(hre_ref[s:s + 1, :], (ROWS, half))
            hin_im = jnp.broadcast_to(him_ref[s:s + 1, :], (ROWS, half))
            m_re, m_im = _cmul(ltp_ref[s, ROWS:2 * ROWS, 0:half], ltp_ref[s, ROWS:2 * ROWS, half:2 * half],
                               hin_re, hin_im)
            st_re = m_re + jnp.where(chunk >= 1, pltpu.roll(p_re, 1, 0), 0.0)
            st_im = m_im + jnp.where(chunk >= 1, pltpu.roll(p_im, 1, 0), 0.0)
            m_re, m_im = _cmul(ltp_ref[s, 3:4, 0:half], ltp_ref[s, 3:4, half:2 * half],
                               hre_ref[s:s + 1, :], him_ref[s:s + 1, :])
            hre_ref[s:s + 1, :] = m_re + p_re[ROWS - 1:ROWS, :]
            him_ref[s:s + 1, :] = m_im + p_im[ROWS - 1:ROWS, :]
            loc = l_ref[s].astype(BF16).reshape(tm // BF16_ROWS, BF16_ROWS, 2 * half)
            pw = pwb_ref[s].reshape(tm // BF16_ROWS, BF16_ROWS, 2 * half)
            pair_re = jnp.concatenate([st_re, st_re], axis=0).astype(BF16)[None]
            pair_im = jnp.concatenate([st_im, st_im], axis=0).astype(BF16)[None]
            f_re, f_im = _cmul(pw[:, :, 0:half], pw[:, :, half:2 * half], pair_re, pair_im)
            hs_re = (loc[:, :, 0:half] + f_re).reshape(tm, half)
            hs_im = (loc[:, :, half:2 * half] + f_im).reshape(tm, half)
            y_parts.append(_dot(hs_re, cre_ref[s]) - _dot(hs_im, cim_ref[s]))

        y = jnp.concatenate(y_parts, axis=1) + dskip_ref[...] * up_ref[...]
        yg = jax.nn.gelu(y, approximate=True).astype(BF16)
        yn = _dot(pt_ref[...], yg).astype(BF16)
        out = _dot(yn, glua_ref[...]) * jax.nn.sigmoid(_dot(yn, glub_ref[...]))
        o_ref[...] = x_ref[...] + _gate(mod, 0, d) * out.reshape(sb, ROWS, d)

    @pl.when(step % 2 == 0)
    def _():
        run(l0_ref, l1_ref, up0_ref, up1_ref)

    @pl.when(step % 2 == 1)
    def _():
        run(l1_ref, l0_ref, up1_ref, up0_ref)


def _mix_odd_prompt(x3, mod, layer, prompt_row, gain, bblk, lam_d, pwb, ltp, cre, cim, dskip, glu_a, glu_b, sb):
    n8, _, d = x3.shape
    n_oct, kin, wid = bblk.shape
    half = wid // 2
    tm = sb * ROWS
    n_tiles = n8 // sb
    proj_buf = pltpu.VMEM((n_oct, tm, wid), F32)
    perm_buf = pltpu.VMEM((tm, d), F32)
    return pl.pallas_call(
        _mix_odd_prompt_body,
        grid=(n_tiles,),
        in_specs=[pl.BlockSpec((sb, ROWS, d), lambda i: (i, 0, 0)),
                  pl.BlockSpec((sb, ROWS, d), lambda i: (jnp.minimum(i + 1, n_tiles - 1), 0, 0)),
                  _mod_row_spec(mod, layer, prompt_row),
                  _const_spec((1, d)), _const_spec(bblk.shape), _const_spec(lam_d.shape),
                  _const_spec(pwb.shape), _const_spec(ltp.shape),
                  _const_spec(cre.shape), _const_spec(cim.shape), _const_spec((1, d)),
                  _const_spec((d, d)), _const_spec((d, d))],
        out_specs=[pl.BlockSpec((sb, ROWS, d), lambda i: (i, 0, 0)),
                   pl.BlockSpec((n_oct, half), lambda i: (0, 0)),
                   pl.BlockSpec((n_oct, half), lambda i: (0, 0))],
        out_shape=[jax.ShapeDtypeStruct(x3.shape, F32),
                   jax.ShapeDtypeStruct((n_oct, half), F32),
                   jax.ShapeDtypeStruct((n_oct, half), F32)],
        scratch_shapes=[pltpu.VMEM((d // 128, tm, 128), F32), proj_buf, proj_buf, perm_buf, perm_buf,
                        pltpu.VMEM((tm, tm), BF16)],
        compiler_params=_params(),
        name="mix_odd_prompt",
    )(x3, x3, mod, gain.reshape(1, d), bblk, lam_d, pwb, ltp, cre, cim, dskip.reshape(1, d), glu_a, glu_b)


def _mix_odd_sample_body(x_ref, mod_ref, gn_ref, bblk_ref, lam_ref, cre_ref, cim_ref, dskip_ref,
                         glua_ref, glub_ref, sre_ref, sim_ref, o_ref, nre_ref, nim_ref, d_ref, y_ref):
    x = x_ref[...]
    sb, length, d = x.shape
    tm = sb * length
    n_oct = bblk_ref.shape[0]
    half = bblk_ref.shape[2] // 2
    mod = _seq_mod(mod_ref)
    u = _modulate(x, mod, gn_ref[...], 0).reshape(tm, d)
    u16 = u.astype(BF16)
    n_ch = bblk_ref.shape[2] // 128
    hc = n_ch // 2
    for s in range(n_oct):
        bu = _dot(u16[:, s * 128:(s + 1) * 128], bblk_ref[s])
        for c in range(n_ch):
            d_ref[c] = bu[:, c * 128:(c + 1) * 128]
        lam_re = lam_ref[0, s:s + 1, :]
        lam_im = lam_ref[1, s:s + 1, :]
        h_re = sre_ref[:, s * half:(s + 1) * half]
        h_im = sim_ref[:, s * half:(s + 1) * half]
        for t in range(length):
            b_re = jnp.concatenate([d_ref[c, pl.ds(t, sb, stride=length), :] for c in range(hc)], axis=1)
            b_im = jnp.concatenate([d_ref[hc + c, pl.ds(t, sb, stride=length), :] for c in range(hc)], axis=1)
            n_re = lam_re * h_re - lam_im * h_im + b_re
            n_im = lam_re * h_im + lam_im * h_re + b_im
            for c in range(hc):
                d_ref[c, pl.ds(t, sb, stride=length), :] = n_re[:, c * 128:(c + 1) * 128]
                d_ref[hc + c, pl.ds(t, sb, stride=length), :] = n_im[:, c * 128:(c + 1) * 128]
            h_re, h_im = n_re, n_im
        nre_ref[:, s * half:(s + 1) * half] = h_re
        nim_ref[:, s * half:(s + 1) * half] = h_im
        hs_re = jnp.concatenate([d_ref[c] for c in range(hc)], axis=1)
        hs_im = jnp.concatenate([d_ref[hc + c] for c in range(hc)], axis=1)
        y_ref[:, s * 128:(s + 1) * 128] = (_dot(hs_re.astype(BF16), cre_ref[s])
                                           - _dot(hs_im.astype(BF16), cim_ref[s]))
    o_ref[...] = _gelu_glu_out(x, mod, y_ref[...], u, dskip_ref[...], glua_ref, glub_ref)


def _mix_odd_sample(x3, mod, layer, gain, bblk, lam_d, cre, cim, dskip, glu_a, glu_b, s_re, s_im, sb):
    n, length, d = x3.shape
    n_oct, kin, wid = bblk.shape
    tm = sb * length
    nstate = s_re.shape[1]
    return pl.pallas_call(
        _mix_odd_sample_body,
        grid=(n // sb,),
        in_specs=[pl.BlockSpec((sb, length, d), lambda i: (i, 0, 0)),
                  _mod_seq_spec(mod, layer, sb),
                  _const_spec((1, d)), _const_spec(bblk.shape), _const_spec(lam_d.shape),
                  _const_spec(cre.shape), _const_spec(cim.shape), _const_spec((1, d)),
                  _const_spec((d, d)), _const_spec((d, d)),
                  pl.BlockSpec((sb, nstate), lambda i: (i, 0)),
                  pl.BlockSpec((sb, nstate), lambda i: (i, 0))],
        out_specs=[pl.BlockSpec((sb, length, d), lambda i: (i, 0, 0)),
                   pl.BlockSpec((sb, nstate), lambda i: (i, 0)),
                   pl.BlockSpec((sb, nstate), lambda i: (i, 0))],
        out_shape=[jax.ShapeDtypeStruct(x3.shape, F32),
                   jax.ShapeDtypeStruct(s_re.shape, F32),
                   jax.ShapeDtypeStruct(s_im.shape, F32)],
        scratch_shapes=[pltpu.VMEM((wid // 128, tm, 128), F32), pltpu.VMEM((tm, d), F32)],
        compiler_params=_params(),
        name="mix_odd_sample",
    )(x3, mod, gain.reshape(1, d), bblk, lam_d, cre, cim, dskip.reshape(1, d), glu_a, glu_b, s_re, s_im)


def _pick(n, want):
    while n % want:
        want //= 2
    return max(want, 1)


def kernel(x_prompt, x_sample, cache_win_k, cache_win_v, state_ret, state_s5_re, state_s5_im, c_prompt, c_sample, ada_w, ada_b, norm_mix, norm_ffn, ffn_wg, ffn_wu, ffn_wd, even_w_in, even_q_gain, even_k_gain, even_sinks, even_ret_gain, even_w_out, odd_A_re, odd_A_im, odd_log_dt, odd_B_re, odd_B_im, odd_C_re, odd_C_im, odd_D, odd_glu_a, odd_glu_b):
    bp, lp, d = x_prompt.shape
    ns, ls, _ = x_sample.shape
    assert bp == 1 and ls == ROWS and lp % WINDOW == 0
    w = cache_win_k.shape[2]
    groups, p_state = odd_A_re.shape[1:]

    n_c = bp + ns
    n_pad = -n_c % ROWS
    c_all = jnp.concatenate([c_sample, c_prompt, jnp.zeros((n_pad, d), F32)], axis=0)
    mod = _adaln(c_all, ada_w, ada_b)

    bf = lambda t: t.astype(BF16)
    w_in, w_out = bf(even_w_in[0]), bf(even_w_out[0])

    xp = x_prompt.reshape(lp // ROWS, ROWS, d)
    xs = x_sample

    sb_p = _pick(lp // ROWS, 32)
    sb_ffn = _pick(lp // ROWS, 64)
    sb_s = _pick(ns, 64)
    sb_s_even = _pick(ns, 16)

    (xp, p_k, p_v, p_ret), ffn0 = _mix_even_prompt(xp, mod, 0, ns, norm_mix[0], w_in, even_q_gain[0], even_k_gain[0],
                                                   even_sinks[0], even_ret_gain[0], w_out, sb_ffn,
                                                   cast=((ffn_wg, 0), (ffn_wu, 0), (ffn_wd, 0)))
    xs, s_k, s_v, s_ret = _mix_even_sample(xs, mod, 0, norm_mix[0], w_in, even_q_gain[0], even_k_gain[0],
                                           even_sinks[0], even_ret_gain[0], w_out,
                                           cache_win_k[0].reshape(ns, w, KA_W), cache_win_v[0].reshape(ns, w, VA_W),
                                           state_ret, sb_s_even)
    xp, xs, (wg1, wu1, wd1, glu_a, glu_b) = _ffn(
        xp, xs, mod, 0, norm_ffn[0], *ffn0, sb_ffn,
        cast=((ffn_wg, 1), (ffn_wu, 1), (ffn_wd, 1), (odd_glu_a, 0), (odd_glu_b, 0)))

    t_len = sb_p
    bblk, cre, cim, lam_d, pwb, ltp = _s5_prep(odd_A_re[0], odd_A_im[0], odd_log_dt[0], odd_B_re[0], odd_B_im[0],
                                               odd_C_re[0], odd_C_im[0], t_len, ROWS)
    xp, p_re, p_im = _mix_odd_prompt(xp, mod, 1, ns, norm_mix[1], bblk, lam_d, pwb, ltp, cre, cim, odd_D[0],
                                     glu_a, glu_b, sb_p)
    xs, s_re, s_im = _mix_odd_sample(xs, mod, 1, norm_mix[1], bblk, lam_d, cre, cim, odd_D[0], glu_a, glu_b,
                                     state_s5_re[0].reshape(ns, groups * p_state),
                                     state_s5_im[0].reshape(ns, groups * p_state), sb_s)
    xp, xs, _ = _ffn(xp, xs, mod, 1, norm_ffn[1], wg1, wu1, wd1, sb_ffn)

    y_prompt = xp.reshape(bp, lp, d)
    y_sample = xs
    return (y_prompt, y_sample,
            p_k.reshape(1, bp, WINDOW, A_KV, A_HD), p_v.reshape(1, bp, WINDOW, A_KV, A_HD),
            p_ret.reshape(1, bp, B_HEADS, B_KD, B_VD),
            p_re.reshape(1, bp, groups, p_state), p_im.reshape(1, bp, groups, p_state),
            s_k.reshape(1, ns, w, A_KV, A_HD), s_v.reshape(1, ns, w, A_KV, A_HD),
            s_ret.reshape(1, ns, B_HEADS, B_KD, B_VD),
            s_re.reshape(1, ns, groups, p_state), s_im.reshape(1, ns, groups, p_state))
```

```python
import functools
import math

import jax
import jax.numpy as jnp
from jax import lax
from jax.experimental import pallas as pl
from jax.experimental.pallas import tpu as pltpu

F32 = jnp.float32
BF16 = jnp.bfloat16

EPS = 1e-6
NEG_INF = -1e30
ROWS = 8

A_HEADS, A_KV, A_GROUP, A_HD = 8, 2, 4, 64
WINDOW = 128
B_HEADS, B_KD, B_VD = 4, 128, 128
S5_GROUP, S5_STATE = 16, 64
S5_OCT = 8

QA_W, KA_W, VA_W = A_HEADS * A_HD, A_KV * A_HD, A_KV * A_HD
QB_W, KB_W, VB_W, GB_W = B_HEADS * B_KD, B_HEADS * B_KD, B_HEADS * B_VD, B_HEADS * B_VD
OFF_QA = 0
OFF_KA = OFF_QA + QA_W
OFF_VA = OFF_KA + KA_W
OFF_QB = OFF_VA + VA_W
OFF_KB = OFF_QB + QB_W
OFF_VB = OFF_KB + KB_W
OFF_GB = OFF_VB + VB_W

VMEM_LIMIT = 56 * 1024 * 1024


def _ret_log_gamma(h):
    return math.log1p(-(2.0 ** (-5.0 - h)))


def _alibi_slope(h):
    return 2.0 ** (-8.0 * (h + 1) / A_HEADS)


def _const_spec(shape):
    nd = len(shape)
    return pl.BlockSpec(shape, lambda i, _n=nd: (0,) * _n, pipeline_mode=pl.Buffered(1))


def _params():
    return pltpu.CompilerParams(dimension_semantics=("arbitrary",), vmem_limit_bytes=VMEM_LIMIT)


def _dot(a, b):
    return jnp.dot(a, b, preferred_element_type=F32)


def _dot_nt(a, b):
    return lax.dot_general(a, b, (((1,), (1,)), ((), ())), preferred_element_type=F32)


def _bmm(a, b):
    return lax.dot_general(a, b, (((2,), (1,)), ((0,), (0,))), preferred_element_type=F32)


def _bmm_nt(a, b):
    return lax.dot_general(a, b, (((2,), (2,)), ((0,), (0,))), preferred_element_type=F32)


def _bmm_tn(a, b):
    return lax.dot_general(a, b, (((1,), (1,)), ((0,), (0,))), preferred_element_type=F32)


def _rms(x, g):
    return x * lax.rsqrt(jnp.mean(x * x, axis=-1, keepdims=True) + EPS) * g


def _modulate(x3, mod, gain, which):
    d = x3.shape[-1]
    sh = mod[:, :, (3 * which) * d:(3 * which + 1) * d]
    sc = mod[:, :, (3 * which + 1) * d:(3 * which + 2) * d]
    inv = lax.rsqrt(jnp.mean(x3 * x3, axis=-1, keepdims=True) + EPS)
    return x3 * inv * (gain * (1.0 + sc)) + sh


def _mod_row_spec(mod, layer, row):
    assert row % ROWS == 0
    return pl.BlockSpec((None, ROWS, mod.shape[-1]), lambda i: (layer, row // ROWS, 0))


def _mod_seq_spec(mod, layer, sb, first_step=0):
    return pl.BlockSpec((None, sb, mod.shape[-1]), lambda i: (layer, jnp.maximum(i - first_step, 0), 0))


def _row_mod(mod_ref):
    return mod_ref[0:1, :][:, None, :]


def _seq_mod(mod_ref):
    return mod_ref[...][:, None, :]


def _gate(mod, which, d):
    return mod[:, :, (3 * which + 2) * d:(3 * which + 3) * d]


def _adaln_body(c_ref, w_ref, b_ref, o_ref):
    c = c_ref[...]
    a = (c * jax.nn.sigmoid(c)).astype(BF16)
    o_ref[0] = _dot(a, w_ref[0].astype(BF16)) + b_ref[0]


def _adaln(c_all, ada_w, ada_b):
    depth, d, n = ada_w.shape
    r = c_all.shape[0]
    tn = 1536
    return pl.pallas_call(
        _adaln_body,
        grid=(depth, n // tn),
        in_specs=[pl.BlockSpec((r, d), lambda l, j: (0, 0)),
                  pl.BlockSpec((1, d, tn), lambda l, j: (l, 0, j)),
                  pl.BlockSpec((1, 1, tn), lambda l, j: (l, 0, j))],
        out_specs=pl.BlockSpec((1, r, tn), lambda l, j: (l, 0, j)),
        out_shape=jax.ShapeDtypeStruct((depth, r, n), F32),
        compiler_params=pltpu.CompilerParams(dimension_semantics=("arbitrary", "arbitrary"),
                                             vmem_limit_bytes=VMEM_LIMIT),
        name="adaln",
    )(c_all, ada_w, ada_b.reshape(depth, 1, n))


BF16_ROWS = 16


def _cast_specs(job, n_steps):
    w, layer = job
    _, rows, cols = w.shape
    hold = 1
    while rows % (n_steps // hold) or (rows // (n_steps // hold)) % BF16_ROWS:
        hold *= 2
        assert hold <= n_steps and n_steps % hold == 0
    blk = rows // (n_steps // hold)
    src = pl.BlockSpec((None, blk, cols),
                       lambda i, _h=hold, _l=layer: (_l, jnp.minimum(i, n_steps - 1) // _h, 0))
    dst = pl.BlockSpec((blk, cols), lambda i, _h=hold: (jnp.minimum(i, n_steps - 1) // _h, 0))
    return src, dst, jax.ShapeDtypeStruct((rows, cols), BF16)


def _cast_blocks(in_refs, out_refs):
    for src, dst in zip(in_refs, out_refs):
        dst[...] = src[...].astype(BF16)


def _ffn_body(n_cast, n_prompt, xp_ref, mp_ref, xs_ref, ms_ref, gn_ref, wg_ref, wu_ref, wd_ref, *refs):
    cast_in, (op_ref, os_ref), cast_out = refs[:n_cast], refs[n_cast:n_cast + 2], refs[n_cast + 2:]
    step = pl.program_id(0)

    def tile(x_ref, mod, o_ref):
        x = x_ref[...]
        sb, _, d = x.shape
        h = _modulate(x, mod, gn_ref[...], 1).reshape(sb * ROWS, d).astype(BF16)
        a = _dot(h, wg_ref[...])
        b = _dot(h, wu_ref[...])
        act = (a * jax.nn.sigmoid(a) * b).astype(BF16)
        y = _dot(act, wd_ref[...])
        o_ref[...] = x + _gate(mod, 1, d) * y.reshape(sb, ROWS, d)

    @pl.when(step < n_prompt)
    def _():
        tile(xp_ref, _row_mod(mp_ref), op_ref)

    @pl.when(step >= n_prompt)
    def _():
        tile(xs_ref, _seq_mod(ms_ref), os_ref)

    _cast_blocks(cast_in, cast_out)


def _ffn(xp3, xs3, mod, layer, gain, wg, wu, wd, sb, cast=()):
    n8, _, d = xp3.shape
    f = wg.shape[1]
    n_prompt = n8 // sb
    n_sample = xs3.shape[0] // sb
    prompt_row = xs3.shape[0]
    assert n8 % sb == 0 and xs3.shape[0] % sb == 0
    cast_specs = [_cast_specs(w, n_prompt) for w in cast]
    p_idx = lambda i: (jnp.minimum(i, n_prompt - 1), 0, 0)
    s_idx = lambda i: (jnp.maximum(i - n_prompt, 0), 0, 0)
    xp_spec = pl.BlockSpec((sb, ROWS, d), p_idx)
    xs_spec = pl.BlockSpec((sb, ROWS, d), s_idx)
    out = pl.pallas_call(
        functools.partial(_ffn_body, len(cast), n_prompt),
        grid=(n_prompt + n_sample,),
        in_specs=[xp_spec, _mod_row_spec(mod, layer, prompt_row),
                  xs_spec, _mod_seq_spec(mod, layer, sb, n_prompt),
                  _const_spec((1, d)), _const_spec((d, f)), _const_spec((d, f)),
                  _const_spec((f, d))] + [c[0] for c in cast_specs],
        out_specs=[xp_spec, xs_spec] + [c[1] for c in cast_specs],
        out_shape=[jax.ShapeDtypeStruct(xp3.shape, F32), jax.ShapeDtypeStruct(xs3.shape, F32)]
        + [c[2] for c in cast_specs],
        compiler_params=_params(),
        name="ffn",
    )(xp3, mod, xs3, mod, gain.reshape(1, d), wg, wu, wd, *[w for w, _ in cast])
    return out[0], out[1], tuple(out[2:])


def _head_rms(t, g):
    return t * lax.rsqrt(jnp.mean(t * t, axis=-1, keepdims=True) + EPS) * g


def _group_norm_gate(o, gain, gate):
    mu = jnp.mean(o, axis=-1, keepdims=True)
    var = jnp.mean(jnp.square(o - mu), axis=-1, keepdims=True)
    return (o - mu) * lax.rsqrt(var + EPS) * gain * (gate * jax.nn.sigmoid(gate))


def _ret_decay(hb, c):
    lg = _ret_log_gamma(hb)
    ii = lax.broadcasted_iota(jnp.int32, (c, c), 0)
    jj = lax.broadcasted_iota(jnp.int32, (c, c), 1)
    diff = (ii - jj).astype(F32)
    d_in = jnp.where(diff >= 0, jnp.exp(lg * jnp.maximum(diff, 0.0)), 0.0)
    row = lax.broadcasted_iota(jnp.int32, (c, B_KD), 0).astype(F32)
    d_q = jnp.exp(lg * (row + 1.0))
    d_k = jnp.exp(lg * (c - 1.0 - row))
    d_c = math.exp(lg * c)
    return d_in, d_q, d_k, d_c


P_OFF_KA = QA_W
P_OFF_VA = P_OFF_KA + KA_W
P_OFF_QB = P_OFF_VA + VA_W
P_OFF_VB = P_OFF_QB + QB_W
P_OFF_GB = P_OFF_VB + VB_W
P_WIDTH = P_OFF_GB + GB_W
PAIR = 2 * A_HD


def _twice(t):
    low = lax.broadcasted_iota(jnp.int32, t.shape, 1) < A_HD
    swapped = pltpu.roll(t, A_HD, 1)
    return jnp.concatenate([jnp.where(low, t, swapped), jnp.where(low, swapped, t)], axis=1)


def _mix_even_prompt_body(n_cast, x_ref, mod_ref, gn_ref, win_ref, wkt_ref, qg_ref, kg_ref, sink_ref, rg_ref, wout_ref,
                          *refs):
    cast_in, refs = refs[:n_cast], refs[n_cast:]
    o_ref, pk_ref, pv_ref, ps_ref = refs[:4]
    cast_out = refs[4:4 + n_cast]
    mix_ref, carry_ref, bias_ref, dec_ref, ones_ref = refs[4 + n_cast:]
    _cast_blocks(cast_in, cast_out)
    step = pl.program_id(0)
    blk = WINDOW
    rows4 = A_GROUP * blk

    @pl.when(step == 0)
    def _():
        carry_ref[...] = jnp.zeros_like(carry_ref)
        ps_ref[...] = jnp.zeros_like(ps_ref)
        er = lax.broadcasted_iota(jnp.int32, ones_ref.shape, 0) // A_HD
        ec = lax.broadcasted_iota(jnp.int32, ones_ref.shape, 1) // A_HD
        ones_ref[...] = jnp.where(er == ec, 1.0 / A_HD, 0.0).astype(BF16)
        row = lax.broadcasted_iota(jnp.int32, (rows4, 2 * blk), 0)
        dist = row % blk + blk - lax.broadcasted_iota(jnp.int32, (rows4, 2 * blk), 1)
        in_window = (dist >= 0) & (dist < WINDOW)
        for kv in range(A_KV):
            slope = jnp.zeros((rows4, 2 * blk), F32)
            for g in range(A_GROUP):
                slope = jnp.where(row // blk == g, _alibi_slope(kv * A_GROUP + g), slope)
            bias_ref[kv] = jnp.where(in_window, slope * dist.astype(F32), -NEG_INF)
        for hb in range(B_HEADS):
            lg = _ret_log_gamma(hb)
            ii = lax.broadcasted_iota(jnp.int32, (blk, blk), 0).astype(F32)
            jj = lax.broadcasted_iota(jnp.int32, (blk, blk), 1).astype(F32)
            diff = ii - jj
            dec_ref[hb, 0] = jnp.where(diff >= 0, jnp.exp(lg * jnp.maximum(diff, 0.0)), 0.0)
            dec_ref[hb, 1] = jnp.exp(lg * (ii + 1.0))
            dec_ref[hb, 2] = jnp.exp(lg * (blk - 1.0 - jj))

    x = x_ref[...]
    sb, _, d = x.shape
    tb = sb * ROWS
    mod = _row_mod(mod_ref)
    h = _modulate(x, mod, gn_ref[...], 0).reshape(tb, d).astype(BF16)
    proj = _dot(h, win_ref[...])
    kt_all = _dot_nt(wkt_ref[...], h) * (B_KD ** -0.5)
    rg = rg_ref[...]

    qa = proj[:, 0:QA_W]
    ka = proj[:, P_OFF_KA:P_OFF_KA + KA_W]
    va = proj[:, P_OFF_VA:P_OFF_VA + VA_W]
    qsq = (qa * qa).astype(BF16)
    tile_w = ones_ref.shape[0]
    q_ms = jnp.concatenate([_dot(qsq[:, c:c + tile_w], ones_ref[...]) for c in range(0, QA_W, tile_w)], axis=1)
    q_hat = (qa * lax.rsqrt(q_ms + EPS) * qg_ref[...] * (A_HD ** -0.5)).astype(BF16)
    k_hat = ka * lax.rsqrt(_dot((ka * ka).astype(BF16), ones_ref[0:KA_W, 0:KA_W]) + EPS) * kg_ref[...]
    k_hat16 = _twice(k_hat).astype(BF16)
    va16 = _twice(va).astype(BF16)
    prev = carry_ref[step % 2]
    carry_ref[(step + 1) % 2] = jnp.concatenate([k_hat16[tb - blk:tb], va16[tb - blk:tb]], axis=1)

    row_g = lax.broadcasted_iota(jnp.int32, (rows4, 1), 0) // blk
    key_is_prev = lax.broadcasted_iota(jnp.int32, (rows4, 2 * blk), 1) < blk
    first_penalty = jnp.where(step == 0, -NEG_INF, 0.0)
    lane_low = lax.broadcasted_iota(jnp.int32, (blk, PAIR), 1) < A_HD
    ones_cols = jnp.ones((2 * blk, PAIR), BF16)

    n_blk = tb // blk
    att = [(j, kv) for j in range(n_blk) for kv in range(A_KV)]
    ret = [(j, hb) for j in range(n_blk) for hb in range(B_HEADS)]

    sinks, scores = {}, {}
    for j, kv in att:
        r0 = j * blk
        kcol = slice(kv * PAIR, (kv + 1) * PAIR)
        if j == 0:
            k2 = jnp.concatenate([prev[:, kcol], k_hat16[0:blk, kcol]], axis=0)
        else:
            k2 = k_hat16[r0 - blk:r0 + blk, kcol]
        q4 = jnp.concatenate(
            [jnp.where(lane_low == (g % 2 == 0),
                       q_hat[r0:r0 + blk, (kv * A_GROUP + g - g % 2) * A_HD:(kv * A_GROUP + g - g % 2 + 2) * A_HD],
                       jnp.zeros((), BF16))
             for g in range(A_GROUP)], axis=0)
        scores[j, kv] = _dot_nt(q4, k2)
        sink = jnp.zeros((rows4, 1), F32)
        for g in range(A_GROUP):
            sink = jnp.where(row_g == g, sink_ref[kv * A_GROUP + g], sink)
        sinks[j, kv] = sink
    qb, vb, kt, inner = {}, {}, {}, {}
    for j, hb in ret:
        r0 = j * blk
        qb[j, hb] = proj[r0:r0 + blk, P_OFF_QB + hb * B_KD:P_OFF_QB + (hb + 1) * B_KD].astype(BF16)
        vb[j, hb] = proj[r0:r0 + blk, P_OFF_VB + hb * B_VD:P_OFF_VB + (hb + 1) * B_VD].astype(BF16)
        kt[j, hb] = kt_all[hb * B_KD:(hb + 1) * B_KD, r0:r0 + blk]
        inner[j, hb] = _dot(qb[j, hb], kt[j, hb].astype(BF16))

    probs, maxes = {}, {}
    for j, kv in att:
        s = scores[j, kv] - bias_ref[kv]
        if j == 0:
            s = s - jnp.where(key_is_prev, first_penalty, 0.0)
        mx = jnp.maximum(jnp.max(s, axis=-1, keepdims=True), sinks[j, kv])
        probs[j, kv] = jnp.exp(s - mx).astype(BF16)
        maxes[j, kv] = mx
    state = {}
    for hb in range(B_HEADS):
        d_c = math.exp(_ret_log_gamma(hb) * blk)
        state[0, hb] = ps_ref[hb]
        for j in range(n_blk):
            state[j + 1, hb] = state[j, hb] * d_c + _dot((kt[j, hb] * dec_ref[hb, 2]).astype(BF16), vb[j, hb])
        ps_ref[hb] = state[n_blk, hb]

    for j, kv in att:
        r0 = j * blk
        kcol = slice(kv * PAIR, (kv + 1) * PAIR)
        vcol = slice(2 * KA_W + kv * PAIR, 2 * KA_W + (kv + 1) * PAIR)
        if j == 0:
            v2 = jnp.concatenate([prev[:, vcol], va16[0:blk, kcol]], axis=0)
        else:
            v2 = va16[r0 - blk:r0 + blk, kcol]
        pv = _dot(probs[j, kv], jnp.concatenate([v2, ones_cols], axis=1))
        o4 = pv[:, 0:PAIR] / (pv[:, PAIR:2 * PAIR] + jnp.exp(sinks[j, kv] - maxes[j, kv]))
        for g in range(A_GROUP):
            hd = kv * A_GROUP + g
            half = slice((hd % 2) * A_HD, (hd % 2 + 1) * A_HD)
            mix_ref[r0:r0 + blk, hd * A_HD:(hd + 1) * A_HD] = o4[g * blk:(g + 1) * blk, half]
    o_ret = {}
    for j, hb in ret:
        o_ret[j, hb] = (_dot((inner[j, hb] * dec_ref[hb, 0]).astype(BF16), vb[j, hb])
                        + _dot(qb[j, hb], state[j, hb].astype(BF16)) * dec_ref[hb, 1])

    cen = {k: o_ret[k] - jnp.mean(o_ret[k], axis=-1, keepdims=True) for k in ret}
    var = {k: jnp.mean(cen[k] * cen[k], axis=-1, keepdims=True) for k in ret}
    for j, hb in ret:
        r0 = j * blk
        gb = proj[r0:r0 + blk, P_OFF_GB + hb * B_VD:P_OFF_GB + (hb + 1) * B_VD]
        mix_ref[r0:r0 + blk, QA_W + hb * B_VD:QA_W + (hb + 1) * B_VD] = (
            cen[j, hb] * lax.rsqrt(var[j, hb] + EPS) * rg[:, hb * B_VD:(hb + 1) * B_VD] * (gb * jax.nn.sigmoid(gb)))

    out = _dot(mix_ref[...].astype(BF16), wout_ref[...])
    o_ref[...] = x + _gate(mod, 0, d) * out.reshape(sb, ROWS, d)

    @pl.when(step == pl.num_programs(0) - 1)
    def _():
        pk_ref[...] = k_hat[tb - blk:tb, :]
        pv_ref[...] = va[tb - blk:tb, :]


def _mix_even_prompt(x3, mod, layer, prompt_row, gain, w_in, q_gain, k_gain, sinks, ret_gain, w_out, sb, cast=()):
    n8, _, d = x3.shape
    tb = sb * ROWS
    cast_specs = [_cast_specs(w, n8 // sb) for w in cast]
    w_main = jnp.concatenate([w_in[:, :OFF_KB], w_in[:, OFF_VB:]], axis=1)
    wk_t = w_in[:, OFF_KB:OFF_VB].T
    out = pl.pallas_call(
        functools.partial(_mix_even_prompt_body, len(cast)),
        grid=(n8 // sb,),
        in_specs=[pl.BlockSpec((sb, ROWS, d), lambda i: (i, 0, 0)),
                  _mod_row_spec(mod, layer, prompt_row),
                  _const_spec((1, d)), _const_spec((d, P_WIDTH)), _const_spec((KB_W, d)),
                  _const_spec((1, QA_W)), _const_spec((1, KA_W)),
                  pl.BlockSpec(memory_space=pltpu.SMEM),
                  _const_spec((1, VB_W)), _const_spec((QA_W + VB_W, d))] + [c[0] for c in cast_specs],
        out_specs=[pl.BlockSpec((sb, ROWS, d), lambda i: (i, 0, 0)),
                   pl.BlockSpec((WINDOW, KA_W), lambda i: (0, 0)),
                   pl.BlockSpec((WINDOW, VA_W), lambda i: (0, 0)),
                   pl.BlockSpec((B_HEADS, B_KD, B_VD), lambda i: (0, 0, 0))] + [c[1] for c in cast_specs],
        out_shape=[jax.ShapeDtypeStruct(x3.shape, F32),
                   jax.ShapeDtypeStruct((WINDOW, KA_W), F32),
                   jax.ShapeDtypeStruct((WINDOW, VA_W), F32),
                   jax.ShapeDtypeStruct((B_HEADS, B_KD, B_VD), F32)] + [c[2] for c in cast_specs],
        scratch_shapes=[pltpu.VMEM((tb, QA_W + VB_W), F32),
                        pltpu.VMEM((2, WINDOW, 2 * KA_W + 2 * VA_W), BF16),
                        pltpu.VMEM((A_KV, A_GROUP * WINDOW, 2 * WINDOW), F32),
                        pltpu.VMEM((B_HEADS, 3, WINDOW, WINDOW), F32),
                        pltpu.VMEM((2 * PAIR, 2 * PAIR), BF16)],
        compiler_params=_params(),
        name="mix_even_prompt",
    )(x3, mod, gain.reshape(1, d), w_main, wk_t, jnp.tile(q_gain, A_HEADS).reshape(1, QA_W),
      jnp.tile(k_gain, A_KV).reshape(1, KA_W),
      sinks, ret_gain.reshape(1, VB_W), w_out, *[w for w, _ in cast])
    return out[:4], tuple(out[4:])


def _mix_even_sample_body(x_ref, mod_ref, gn_ref, win_ref, qg_ref, kg_ref, sink_ref, rg_ref, wout_ref,
                          ck_ref, cv_ref, s0_ref,
                          o_ref, nk_ref, nv_ref, ns_ref, mix_ref):
    x = x_ref[...]
    sb, length, d = x.shape
    tb = sb * length
    w = ck_ref.shape[1]
    mod = _seq_mod(mod_ref)
    h = _modulate(x, mod, gn_ref[...], 0).reshape(tb, d).astype(BF16)
    proj = _dot(h, win_ref[...])
    qg = qg_ref[...]
    kg = kg_ref[...]
    rg = rg_ref[...]

    rows = A_GROUP * length
    qpos_c = lax.broadcasted_iota(jnp.int32, (rows, w), 0) % length
    kpos_c = lax.broadcasted_iota(jnp.int32, (rows, w), 1)
    dist_c = w + qpos_c - kpos_c
    valid_c = (dist_c >= 0) & (dist_c < WINDOW)
    qpos_n = lax.broadcasted_iota(jnp.int32, (rows, length), 0) % length
    kpos_n = lax.broadcasted_iota(jnp.int32, (rows, length), 1)
    dist_n = qpos_n - kpos_n
    valid_n = (dist_n >= 0) & (dist_n < WINDOW)
    row_g = lax.broadcasted_iota(jnp.int32, (rows, 1), 0) // length

    scale = A_HD ** -0.5
    kn, vn, s_c, s_n, vc, slopes, sinks = {}, {}, {}, {}, {}, {}, {}
    for kv in range(A_KV):
        lanes = slice(kv * A_HD, (kv + 1) * A_HD)
        kn[kv] = _head_rms(proj[:, OFF_KA + kv * A_HD:OFF_KA + (kv + 1) * A_HD], kg).reshape(sb, length, A_HD)
        vn[kv] = proj[:, OFF_VA + kv * A_HD:OFF_VA + (kv + 1) * A_HD].reshape(sb, length, A_HD)
        nk_ref[:, 0:w - length, lanes] = ck_ref[:, length:w, lanes]
        nv_ref[:, 0:w - length, lanes] = cv_ref[:, length:w, lanes]
        nk_ref[:, w - length:w, lanes] = kn[kv]
        nv_ref[:, w - length:w, lanes] = vn[kv]
        kc = ck_ref[:, :, lanes].astype(BF16)
        vc[kv] = cv_ref[:, :, lanes].astype(BF16)
        q4 = jnp.concatenate(
            [_head_rms(proj[:, OFF_QA + (kv * A_GROUP + g) * A_HD:OFF_QA + (kv * A_GROUP + g + 1) * A_HD], qg)
             .reshape(sb, length, A_HD) for g in range(A_GROUP)], axis=1).astype(BF16)
        slope = jnp.zeros((rows, 1), F32)
        sink = jnp.zeros((rows, 1), F32)
        for g in range(A_GROUP):
            hd = kv * A_GROUP + g
            slope = jnp.where(row_g == g, _alibi_slope(hd), slope)
            sink = jnp.where(row_g == g, sink_ref[hd], sink)
        slopes[kv], sinks[kv] = slope, sink
        s_c[kv] = _bmm_nt(q4, kc)
        s_n[kv] = _bmm_nt(q4, kn[kv].astype(BF16))
    decay = {hb: _ret_decay(hb, length) for hb in range(B_HEADS)}
    qb, kb, vb, inner, q_state = {}, {}, {}, {}, {}
    for hb in range(B_HEADS):
        qb[hb] = proj[:, OFF_QB + hb * B_KD:OFF_QB + (hb + 1) * B_KD].reshape(sb, length, B_KD).astype(BF16)
        kb[hb] = proj[:, OFF_KB + hb * B_KD:OFF_KB + (hb + 1) * B_KD].reshape(sb, length, B_KD) * (B_KD ** -0.5)
        vb[hb] = proj[:, OFF_VB + hb * B_VD:OFF_VB + (hb + 1) * B_VD].reshape(sb, length, B_VD).astype(BF16)
        inner[hb] = _bmm_nt(qb[hb], kb[hb].astype(BF16))
        q_state[hb] = _bmm(qb[hb], s0_ref[:, hb].astype(BF16))

    p_c, p_n, den = {}, {}, {}
    for kv in range(A_KV):
        a_c = jnp.where(valid_c, s_c[kv] * scale - slopes[kv] * dist_c.astype(F32), NEG_INF)
        a_n = jnp.where(valid_n, s_n[kv] * scale - slopes[kv] * dist_n.astype(F32), NEG_INF)
        mx = jnp.maximum(jnp.maximum(jnp.max(a_c, axis=-1, keepdims=True),
                                     jnp.max(a_n, axis=-1, keepdims=True)), sinks[kv])
        p_c[kv] = jnp.exp(a_c - mx)
        p_n[kv] = jnp.exp(a_n - mx)
        den[kv] = (jnp.sum(p_c[kv], axis=-1, keepdims=True) + jnp.sum(p_n[kv], axis=-1, keepdims=True)
                   + jnp.exp(sinks[kv] - mx))

    for kv in range(A_KV):
        o4 = (_bmm(p_c[kv].astype(BF16), vc[kv]) + _bmm(p_n[kv].astype(BF16), vn[kv].astype(BF16))) / den[kv]
        for g in range(A_GROUP):
            hd = kv * A_GROUP + g
            mix_ref[:, hd * A_HD:(hd + 1) * A_HD] = o4[:, g * length:(g + 1) * length, :].reshape(tb, A_HD)
    o_ret = {}
    for hb in range(B_HEADS):
        d_in, d_q, d_k, d_c = decay[hb]
        o_ret[hb] = _bmm((inner[hb] * d_in).astype(BF16), vb[hb]) + q_state[hb] * d_q
        ns_ref[:, hb] = s0_ref[:, hb] * d_c + _bmm_tn((kb[hb] * d_k).astype(BF16), vb[hb])

    for hb in range(B_HEADS):
        gb = proj[:, OFF_GB + hb * B_VD:OFF_GB + (hb + 1) * B_VD]
        mix_ref[:, QA_W + hb * B_VD:QA_W + (hb + 1) * B_VD] = _group_norm_gate(
            o_ret[hb].reshape(tb, B_VD), rg[:, hb * B_VD:(hb + 1) * B_VD], gb)

    out = _dot(mix_ref[...].astype(BF16), wout_ref[...])
    o_ref[...] = x + _gate(mod, 0, d) * out.reshape(sb, length, d)


def _mix_even_sample(x3, mod, layer, gain, w_in, q_gain, k_gain, sinks, ret_gain, w_out, cache_k, cache_v, state, sb):
    n, length, d = x3.shape
    in_w = w_in.shape[1]
    w = cache_k.shape[1]
    seq_spec = lambda shape: pl.BlockSpec((sb,) + shape, lambda i, _n=len(shape): (i,) + (0,) * _n)
    return pl.pallas_call(
        _mix_even_sample_body,
        grid=(n // sb,),
        in_specs=[seq_spec((length, d)), _mod_seq_spec(mod, layer, sb),
                  _const_spec((1, d)), _const_spec((d, in_w)),
                  _const_spec((1, A_HD)), _const_spec((1, A_HD)),
                  pl.BlockSpec(memory_space=pltpu.SMEM),
                  _const_spec((1, VB_W)), _const_spec((QA_W + VB_W, d)),
                  seq_spec((w, KA_W)), seq_spec((w, VA_W)),
                  pl.BlockSpec((None, sb, B_HEADS, B_KD, B_VD), lambda i: (0, i, 0, 0, 0))],
        out_specs=[seq_spec((length, d)), seq_spec((w, KA_W)), seq_spec((w, VA_W)),
                   seq_spec((B_HEADS, B_KD, B_VD))],
        out_shape=[jax.ShapeDtypeStruct(x3.shape, F32),
                   jax.ShapeDtypeStruct(cache_k.shape, F32),
                   jax.ShapeDtypeStruct(cache_v.shape, F32),
                   jax.ShapeDtypeStruct(state.shape[1:], F32)],
        scratch_shapes=[pltpu.VMEM((sb * length, QA_W + VB_W), F32)],
        compiler_params=_params(),
        name="mix_even_sample",
    )(x3, mod, gain.reshape(1, d), w_in, q_gain.reshape(1, A_HD), k_gain.reshape(1, A_HD),
      sinks, ret_gain.reshape(1, VB_W), w_out, cache_k, cache_v, state)


def _cmul(ar, ai, br, bi):
    return ar * br - ai * bi, ar * bi + ai * br


def _s5_lambda(a_re, a_im, log_dt):
    dt = jnp.exp(log_dt)
    mag = jnp.exp(a_re * dt)
    return mag * jnp.cos(a_im * dt), mag * jnp.sin(a_im * dt)


def _s5_prep_body(t_len, n_chunks, are_ref, aim_ref, ldt_ref, btr_ref, bti_ref, ctr_ref, cti_ref,
                  ard_ref, aid_ref, ldd_ref,
                  bblk_ref, cre_ref, cim_ref, lam_ref, pwb_ref, ltp_ref, b32_ref, cr32_ref, ci32_ref, pw32_ref):
    groups = btr_ref.shape[0]
    k_in, p = btr_ref.shape[1:]
    half = S5_OCT * p
    a_re = are_ref[...]
    a_im = aim_ref[...]
    lam_re, lam_im = _s5_lambda(a_re, a_im, ldt_ref[...])
    den = a_re * a_re + a_im * a_im
    n_re = lam_re - 1.0
    n_im = lam_im
    f_re = (n_re * a_re + n_im * a_im) / den
    f_im = (n_im * a_re - n_re * a_im) / den
    br = btr_ref[...]
    bi = bti_ref[...]
    bb_re = f_re * br - f_im * bi
    bb_im = f_re * bi + f_im * br
    b32_ref[...] = jnp.zeros_like(b32_ref)
    cr32_ref[...] = jnp.zeros_like(cr32_ref)
    ci32_ref[...] = jnp.zeros_like(ci32_ref)
    for g in range(groups):
        s, gl = divmod(g, S5_OCT)
        rows = slice(gl * k_in, (gl + 1) * k_in)
        cols = slice(gl * p, (gl + 1) * p)
        b32_ref[s, rows, cols] = bb_re[g]
        b32_ref[s, rows, half + gl * p:half + (gl + 1) * p] = bb_im[g]
        cr32_ref[s, cols, rows] = ctr_ref[g]
        ci32_ref[s, cols, rows] = cti_ref[g]
    bblk_ref[...] = b32_ref[...].astype(BF16)
    cre_ref[...] = cr32_ref[...].astype(BF16)
    cim_ref[...] = ci32_ref[...].astype(BF16)
    lam_re, lam_im = _s5_lambda(ard_ref[...], aid_ref[...], ldd_ref[...])
    lam_ref[0] = lam_re
    lam_ref[1] = lam_im
    n_oct = lam_re.shape[0]

    def both(cr, ci, s, rows):
        return jnp.concatenate([jnp.broadcast_to(cr[s:s + 1, :], (rows, half)),
                                jnp.broadcast_to(ci[s:s + 1, :], (rows, half))], axis=1)

    cr, ci = lam_re, lam_im
    for t in range(t_len):
        for s in range(n_oct):
            pw32_ref[s, t * ROWS:(t + 1) * ROWS, :] = both(cr, ci, s, ROWS)
        if t + 1 < t_len:
            cr, ci = _cmul(cr, ci, lam_re, lam_im)
    pwb_ref[...] = pw32_ref[...].astype(BF16)
    tr, ti = cr, ci
    ltp_ref[...] = jnp.zeros_like(ltp_ref)
    cr, ci = jnp.ones_like(lam_re), jnp.zeros_like(lam_im)
    for m in range(n_chunks + 1):
        for s in range(n_oct):
            if m < n_chunks:
                ltp_ref[s, n_chunks + m:n_chunks + m + 1, :] = both(cr, ci, s, 1)
            if m in (1, 2, 4, 8):
                i = (1, 2, 4, 8).index(m)
                ltp_ref[s, i:i + 1, :] = both(cr, ci, s, 1)
        cr, ci = _cmul(cr, ci, tr, ti)


def _s5_prep(a_re, a_im, log_dt, b_re, b_im, c_re, c_im, t_len, n_chunks):
    g, p = a_re.shape
    k = b_re.shape[-1]
    assert n_chunks == ROWS
    n_oct = g // S5_OCT
    half = S5_OCT * p
    dense = (n_oct, half)
    return pl.pallas_call(
        functools.partial(_s5_prep_body, t_len, n_chunks),
        out_shape=[jax.ShapeDtypeStruct((n_oct, S5_OCT * k, 2 * half), BF16),
                   jax.ShapeDtypeStruct((n_oct, half, S5_OCT * k), BF16),
                   jax.ShapeDtypeStruct((n_oct, half, S5_OCT * k), BF16),
                   jax.ShapeDtypeStruct((2,) + dense, F32),
                   jax.ShapeDtypeStruct((n_oct, t_len * ROWS, 2 * half), BF16),
                   jax.ShapeDtypeStruct((n_oct, 2 * n_chunks, 2 * half), F32)],
        scratch_shapes=[pltpu.VMEM((n_oct, S5_OCT * k, 2 * half), F32),
                        pltpu.VMEM((n_oct, half, S5_OCT * k), F32),
                        pltpu.VMEM((n_oct, half, S5_OCT * k), F32),
                        pltpu.VMEM((n_oct, t_len * ROWS, 2 * half), F32)],
        compiler_params=pltpu.CompilerParams(vmem_limit_bytes=VMEM_LIMIT),
        name="s5_prep",
    )(a_re.reshape(g, 1, p), a_im.reshape(g, 1, p), log_dt.reshape(g, 1, 1),
      jnp.swapaxes(b_re, 1, 2), jnp.swapaxes(b_im, 1, 2), jnp.swapaxes(c_re, 1, 2), jnp.swapaxes(c_im, 1, 2),
      a_re.reshape(dense), a_im.reshape(dense), jnp.broadcast_to(log_dt[:, None], (g, p)).reshape(dense))


def _gelu_glu_out(x, mod, y, u, dskip, glua_ref, glub_ref):
    sb, rows, d = x.shape
    y = y + dskip * u
    yg = jax.nn.gelu(y, approximate=True).astype(BF16)
    out = _dot(yg, glua_ref[...]) * jax.nn.sigmoid(_dot(yg, glub_ref[...]))
    return x + _gate(mod, 0, d) * out.reshape(sb, rows, d)


def _mix_odd_prompt_body(x_ref, xn_ref, mod_ref, gn_ref, bblk_ref, lam_ref, pwb_ref, ltp_ref, cre_ref, cim_ref,
                         dskip_ref, glua_ref, glub_ref, o_ref, hre_ref, him_ref,
                         un_ref, l0_ref, l1_ref, up0_ref, up1_ref, pt_ref):
    step = pl.program_id(0)
    sb, _, d = x_ref.shape
    tm = sb * ROWS
    t_len = tm // ROWS
    n_oct = bblk_ref.shape[0]
    half = bblk_ref.shape[2] // 2
    mod = _row_mod(mod_ref)

    def permuted_input(src_ref, up_dst):
        u = _modulate(src_ref[...], mod, gn_ref[...], 0).reshape(tm, d)
        for k in range(d // 128):
            un_ref[k] = u[:, k * 128:(k + 1) * 128]
        up = jnp.concatenate(
            [jnp.concatenate([un_ref[k, pl.ds(t, ROWS, stride=t_len), :] for k in range(d // 128)], axis=1)
             for t in range(t_len)], axis=0)
        up_dst[...] = up
        return up.astype(BF16)

    @pl.when(step == 0)
    def _():
        hre_ref[...] = jnp.zeros_like(hre_ref)
        him_ref[...] = jnp.zeros_like(him_ref)
        nat = lax.broadcasted_iota(jnp.int32, (tm, tm), 0)
        prm = lax.broadcasted_iota(jnp.int32, (tm, tm), 1)
        pt_ref[...] = jnp.where(prm == (nat % t_len) * ROWS + nat // t_len, 1.0, 0.0).astype(BF16)
        up16 = permuted_input(x_ref, up0_ref)
        for s in range(n_oct):
            l0_ref[s] = _dot(up16[:, s * 128:(s + 1) * 128], bblk_ref[s])

    def run(l_ref, l_next, up_ref, up_next):
        for s0 in range(0, n_oct, 2):
            pair = (s0, s0 + 1)
            lam_re = [jnp.broadcast_to(lam_ref[0, s:s + 1, :], (ROWS, half)) for s in pair]
            lam_im = [jnp.broadcast_to(lam_ref[1, s:s + 1, :], (ROWS, half)) for s in pair]

            def local_step(t, carry):
                r = pl.multiple_of(t * ROWS, ROWS)
                out = []
                for i, s in enumerate(pair):
                    h_re, h_im = carry[2 * i], carry[2 * i + 1]
                    n_re = lam_re[i] * h_re - lam_im[i] * h_im + l_ref[s, pl.ds(r, ROWS), 0:half]
                    n_im = lam_re[i] * h_im + lam_im[i] * h_re + l_ref[s, pl.ds(r, ROWS), half:2 * half]
                    l_ref[s, pl.ds(r, ROWS), 0:half] = n_re
                    l_ref[s, pl.ds(r, ROWS), half:2 * half] = n_im
                    out += [n_re, n_im]
                return tuple(out)

            zero = jnp.zeros((ROWS, half), F32)
            lax.fori_loop(0, t_len, local_step, (zero, zero, zero, zero), unroll=2)

        next16 = permuted_input(xn_ref, up_next)
        chunk = lax.broadcasted_iota(jnp.int32, (ROWS, half), 0)
        y_parts = []
        for s in range(n_oct):
            l_next[s] = _dot(next16[:, s * 128:(s + 1) * 128], bblk_ref[s])
            p_re = l_ref[s, tm - ROWS:tm, 0:half]
            p_im = l_ref[s, tm - ROWS:tm, half:2 * half]
            for i, sh in enumerate((1, 2, 4)):
                m_re, m_im = _cmul(ltp_ref[s, i:i + 1, 0:half], ltp_ref[s, i:i + 1, half:2 * half],
                                   jnp.where(chunk >= sh, pltpu.roll(p_re, sh, 0), 0.0),
                                   jnp.where(chunk >= sh, pltpu.roll(p_im, sh, 0), 0.0))
                p_re, p_im = p_re + m_re, p_im + m_im
            hin_re = jnp.broadcast_to(hre_ref[s:s + 1, :], (ROWS, half))
            hin_im = jnp.broadcast_to(him_ref[s:s + 1, :], (ROWS, half))
            m_re, m_im = _cmul(ltp_ref[s, ROWS:2 * ROWS, 0:half], ltp_ref[s, ROWS:2 * ROWS, half:2 * half],
                               hin_re, hin_im)
            st_re = m_re + jnp.where(chunk >= 1, pltpu.roll(p_re, 1, 0), 0.0)
            st_im = m_im + jnp.where(chunk >= 1, pltpu.roll(p_im, 1, 0), 0.0)
            m_re, m_im = _cmul(ltp_ref[s, 3:4, 0:half], ltp_ref[s, 3:4, half:2 * half],
                               hre_ref[s:s + 1, :], him_ref[s:s + 1, :])
            hre_ref[s:s + 1, :] = m_re + p_re[ROWS - 1:ROWS, :]
            him_ref[s:s + 1, :] = m_im + p_im[ROWS - 1:ROWS, :]
            loc = l_ref[s].astype(BF16).reshape(tm // BF16_ROWS, BF16_ROWS, 2 * half)
            pw = pwb_ref[s].reshape(tm // BF16_ROWS, BF16_ROWS, 2 * half)
            pair_re = jnp.concatenate([st_re, st_re], axis=0).astype(BF16)[None]
            pair_im = jnp.concatenate([st_im, st_im], axis=0).astype(BF16)[None]
            f_re, f_im = _cmul(pw[:, :, 0:half], pw[:, :, half:2 * half], pair_re, pair_im)
            hs_re = (loc[:, :, 0:half] + f_re).reshape(tm, half)
            hs_im = (loc[:, :, half:2 * half] + f_im).reshape(tm, half)
            y_parts.append(_dot(hs_re, cre_ref[s]) - _dot(hs_im, cim_ref[s]))

        y = jnp.concatenate(y_parts, axis=1) + dskip_ref[...] * up_ref[...]
        yg = jax.nn.gelu(y, approximate=True).astype(BF16)
        yn = _dot(pt_ref[...], yg).astype(BF16)
        out = _dot(yn, glua_ref[...]) * jax.nn.sigmoid(_dot(yn, glub_ref[...]))
        o_ref[...] = x_ref[...] + _gate(mod, 0, d) * out.reshape(sb, ROWS, d)

    @pl.when(step % 2 == 0)
    def _():
        run(l0_ref, l1_ref, up0_ref, up1_ref)

    @pl.when(step % 2 == 1)
    def _():
        run(l1_ref, l0_ref, up1_ref, up0_ref)


def _mix_odd_prompt(x3, mod, layer, prompt_row, gain, bblk, lam_d, pwb, ltp, cre, cim, dskip, glu_a, glu_b, sb):
    n8, _, d = x3.shape
    n_oct, kin, wid = bblk.shape
    half = wid // 2
    tm = sb * ROWS
    n_tiles = n8 // sb
    proj_buf = pltpu.VMEM((n_oct, tm, wid), F32)
    perm_buf = pltpu.VMEM((tm, d), F32)
    return pl.pallas_call(
        _mix_odd_prompt_body,
        grid=(n_tiles,),
        in_specs=[pl.BlockSpec((sb, ROWS, d), lambda i: (i, 0, 0)),
                  pl.BlockSpec((sb, ROWS, d), lambda i: (jnp.minimum(i + 1, n_tiles - 1), 0, 0)),
                  _mod_row_spec(mod, layer, prompt_row),
                  _const_spec((1, d)), _const_spec(bblk.shape), _const_spec(lam_d.shape),
                  _const_spec(pwb.shape), _const_spec(ltp.shape),
                  _const_spec(cre.shape), _const_spec(cim.shape), _const_spec((1, d)),
                  _const_spec((d, d)), _const_spec((d, d))],
        out_specs=[pl.BlockSpec((sb, ROWS, d), lambda i: (i, 0, 0)),
                   pl.BlockSpec((n_oct, half), lambda i: (0, 0)),
                   pl.BlockSpec((n_oct, half), lambda i: (0, 0))],
        out_shape=[jax.ShapeDtypeStruct(x3.shape, F32),
                   jax.ShapeDtypeStruct((n_oct, half), F32),
                   jax.ShapeDtypeStruct((n_oct, half), F32)],
        scratch_shapes=[pltpu.VMEM((d // 128, tm, 128), F32), proj_buf, proj_buf, perm_buf, perm_buf,
                        pltpu.VMEM((tm, tm), BF16)],
        compiler_params=_params(),
        name="mix_odd_prompt",
    )(x3, x3, mod, gain.reshape(1, d), bblk, lam_d, pwb, ltp, cre, cim, dskip.reshape(1, d), glu_a, glu_b)


def _mix_odd_sample_body(x_ref, mod_ref, gn_ref, bblk_ref, lam_ref, cre_ref, cim_ref, dskip_ref,
                         glua_ref, glub_ref, sre_ref, sim_ref, o_ref, nre_ref, nim_ref, d_ref, y_ref):
    x = x_ref[...]
    sb, length, d = x.shape
    tm = sb * length
    n_oct = bblk_ref.shape[0]
    half = bblk_ref.shape[2] // 2
    mod = _seq_mod(mod_ref)
    u = _modulate(x, mod, gn_ref[...], 0).reshape(tm, d)
    u16 = u.astype(BF16)
    n_ch = bblk_ref.shape[2] // 128
    hc = n_ch // 2
    for s in range(n_oct):
        bu = _dot(u16[:, s * 128:(s + 1) * 128], bblk_ref[s])
        for c in range(n_ch):
            d_ref[c] = bu[:, c * 128:(c + 1) * 128]
        lam_re = lam_ref[0, s:s + 1, :]
        lam_im = lam_ref[1, s:s + 1, :]
        h_re = sre_ref[:, s * half:(s + 1) * half]
        h_im = sim_ref[:, s * half:(s + 1) * half]
        for t in range(length):
            b_re = jnp.concatenate([d_ref[c, pl.ds(t, sb, stride=length), :] for c in range(hc)], axis=1)
            b_im = jnp.concatenate([d_ref[hc + c, pl.ds(t, sb, stride=length), :] for c in range(hc)], axis=1)
            n_re = lam_re * h_re - lam_im * h_im + b_re
            n_im = lam_re * h_im + lam_im * h_re + b_im
            for c in range(hc):
                d_ref[c, pl.ds(t, sb, stride=length), :] = n_re[:, c * 128:(c + 1) * 128]
                d_ref[hc + c, pl.ds(t, sb, stride=length), :] = n_im[:, c * 128:(c + 1) * 128]
            h_re, h_im = n_re, n_im
        nre_ref[:, s * half:(s + 1) * half] = h_re
        nim_ref[:, s * half:(s + 1) * half] = h_im
        hs_re = jnp.concatenate([d_ref[c] for c in range(hc)], axis=1)
        hs_im = jnp.concatenate([d_ref[hc + c] for c in range(hc)], axis=1)
        y_ref[:, s * 128:(s + 1) * 128] = (_dot(hs_re.astype(BF16), cre_ref[s])
                                           - _dot(hs_im.astype(BF16), cim_ref[s]))
    o_ref[...] = _gelu_glu_out(x, mod, y_ref[...], u, dskip_ref[...], glua_ref, glub_ref)


def _mix_odd_sample(x3, mod, layer, gain, bblk, lam_d, cre, cim, dskip, glu_a, glu_b, s_re, s_im, sb):
    n, length, d = x3.shape
    n_oct, kin, wid = bblk.shape
    tm = sb * length
    nstate = s_re.shape[1]
    return pl.pallas_call(
        _mix_odd_sample_body,
        grid=(n // sb,),
        in_specs=[pl.BlockSpec((sb, length, d), lambda i: (i, 0, 0)),
                  _mod_seq_spec(mod, layer, sb),
                  _const_spec((1, d)), _const_spec(bblk.shape), _const_spec(lam_d.shape),
                  _const_spec(cre.shape), _const_spec(cim.shape), _const_spec((1, d)),
                  _const_spec((d, d)), _const_spec((d, d)),
                  pl.BlockSpec((sb, nstate), lambda i: (i, 0)),
                  pl.BlockSpec((sb, nstate), lambda i: (i, 0))],
        out_specs=[pl.BlockSpec((sb, length, d), lambda i: (i, 0, 0)),
                   pl.BlockSpec((sb, nstate), lambda i: (i, 0)),
                   pl.BlockSpec((sb, nstate), lambda i: (i, 0))],
        out_shape=[jax.ShapeDtypeStruct(x3.shape, F32),
                   jax.ShapeDtypeStruct(s_re.shape, F32),
                   jax.ShapeDtypeStruct(s_im.shape, F32)],
        scratch_shapes=[pltpu.VMEM((wid // 128, tm, 128), F32), pltpu.VMEM((tm, d), F32)],
        compiler_params=_params(),
        name="mix_odd_sample",
    )(x3, mod, gain.reshape(1, d), bblk, lam_d, cre, cim, dskip.reshape(1, d), glu_a, glu_b, s_re, s_im)


def _pick(n, want):
    while n % want:
        want //= 2
    return max(want, 1)


def kernel(x_prompt, x_sample, cache_win_k, cache_win_v, state_ret, state_s5_re, state_s5_im, c_prompt, c_sample, ada_w, ada_b, norm_mix, norm_ffn, ffn_wg, ffn_wu, ffn_wd, even_w_in, even_q_gain, even_k_gain, even_sinks, even_ret_gain, even_w_out, odd_A_re, odd_A_im, odd_log_dt, odd_B_re, odd_B_im, odd_C_re, odd_C_im, odd_D, odd_glu_a, odd_glu_b):
    bp, lp, d = x_prompt.shape
    ns, ls, _ = x_sample.shape
    assert bp == 1 and ls == ROWS and lp % WINDOW == 0
    w = cache_win_k.shape[2]
    groups, p_state = odd_A_re.shape[1:]

    n_c = bp + ns
    n_pad = -n_c % ROWS
    c_all = jnp.concatenate([c_sample, c_prompt, jnp.zeros((n_pad, d), F32)], axis=0)
    mod = _adaln(c_all, ada_w, ada_b)

    bf = lambda t: t.astype(BF16)
    w_in, w_out = bf(even_w_in[0]), bf(even_w_out[0])

    xp = x_prompt.reshape(lp // ROWS, ROWS, d)
    xs = x_sample

    sb_p = _pick(lp // ROWS, 32)
    sb_ffn = _pick(lp // ROWS, 64)
    sb_s = _pick(ns, 64)
    sb_s_even = _pick(ns, 16)

    (xp, p_k, p_v, p_ret), ffn0 = _mix_even_prompt(xp, mod, 0, ns, norm_mix[0], w_in, even_q_gain[0], even_k_gain[0],
                                                   even_sinks[0], even_ret_gain[0], w_out, sb_ffn,
                                                   cast=((ffn_wg, 0), (ffn_wu, 0), (ffn_wd, 0)))
    xs, s_k, s_v, s_ret = _mix_even_sample(xs, mod, 0, norm_mix[0], w_in, even_q_gain[0], even_k_gain[0],
                                           even_sinks[0], even_ret_gain[0], w_out,
                                           cache_win_k[0].reshape(ns, w, KA_W), cache_win_v[0].reshape(ns, w, VA_W),
                                           state_ret, sb_s_even)
    xp, xs, (wg1, wu1, wd1, glu_a, glu_b) = _ffn(
        xp, xs, mod, 0, norm_ffn[0], *ffn0, sb_ffn,
        cast=((ffn_wg, 1), (ffn_wu, 1), (ffn_wd, 1), (odd_glu_a, 0), (odd_glu_b, 0)))

    t_len = sb_p
    bblk, cre, cim, lam_d, pwb, ltp = _s5_prep(odd_A_re[0], odd_A_im[0], odd_log_dt[0], odd_B_re[0], odd_B_im[0],
                                               odd_C_re[0], odd_C_im[0], t_len, ROWS)
    xp, p_re, p_im = _mix_odd_prompt(xp, mod, 1, ns, norm_mix[1], bblk, lam_d, pwb, ltp, cre, cim, odd_D[0],
                                     glu_a, glu_b, sb_p)
    xs, s_re, s_im = _mix_odd_sample(xs, mod, 1, norm_mix[1], bblk, lam_d, cre, cim, odd_D[0], glu_a, glu_b,
                                     state_s5_re[0].reshape(ns, groups * p_state),
                                     state_s5_im[0].reshape(ns, groups * p_state), sb_s)
    xp, xs, _ = _ffn(xp, xs, mod, 1, norm_ffn[1], wg1, wu1, wd1, sb_ffn)

    y_prompt = xp.reshape(bp, lp, d)
    y_sample = xs
    return (y_prompt, y_sample,
            p_k.reshape(1, bp, WINDOW, A_KV, A_HD), p_v.reshape(1, bp, WINDOW, A_KV, A_HD),
            p_ret.reshape(1, bp, B_HEADS, B_KD, B_VD),
            p_re.reshape(1, bp, groups, p_state), p_im.reshape(1, bp, groups, p_state),
            s_k.reshape(1, ns, w, A_KV, A_HD), s_v.reshape(1, ns, w, A_KV, A_HD),
            s_ret.reshape(1, ns, B_HEADS, B_KD, B_VD),
            s_re.reshape(1, ns, groups, p_state), s_im.reshape(1, ns, groups, p_state))
```

```python
import functools
import math

import jax
import jax.numpy as jnp
from jax import lax
from jax.experimental import pallas as pl
from jax.experimental.pallas import tpu as pltpu

F32 = jnp.float32
BF16 = jnp.bfloat16

EPS = 1e-6
NEG_INF = -1e30
ROWS = 8

A_HEADS, A_KV, A_GROUP, A_HD = 8, 2, 4, 64
WINDOW = 128
B_HEADS, B_KD, B_VD = 4, 128, 128
S5_GROUP, S5_STATE = 16, 64
S5_OCT = 8

QA_W, KA_W, VA_W = A_HEADS * A_HD, A_KV * A_HD, A_KV * A_HD
QB_W, KB_W, VB_W, GB_W = B_HEADS * B_KD, B_HEADS * B_KD, B_HEADS * B_VD, B_HEADS * B_VD
OFF_QA = 0
OFF_KA = OFF_QA + QA_W
OFF_VA = OFF_KA + KA_W
OFF_QB = OFF_VA + VA_W
OFF_KB = OFF_QB + QB_W
OFF_VB = OFF_KB + KB_W
OFF_GB = OFF_VB + VB_W

VMEM_LIMIT = 56 * 1024 * 1024


def _ret_log_gamma(h):
    return math.log1p(-(2.0 ** (-5.0 - h)))


def _alibi_slope(h):
    return 2.0 ** (-8.0 * (h + 1) / A_HEADS)


def _const_spec(shape):
    nd = len(shape)
    return pl.BlockSpec(shape, lambda i, _n=nd: (0,) * _n, pipeline_mode=pl.Buffered(1))


def _params():
    return pltpu.CompilerParams(dimension_semantics=("arbitrary",), vmem_limit_bytes=VMEM_LIMIT)


def _dot(a, b):
    return jnp.dot(a, b, preferred_element_type=F32)


def _dot_nt(a, b):
    return lax.dot_general(a, b, (((1,), (1,)), ((), ())), preferred_element_type=F32)


def _bmm(a, b):
    return lax.dot_general(a, b, (((2,), (1,)), ((0,), (0,))), preferred_element_type=F32)


def _bmm_nt(a, b):
    return lax.dot_general(a, b, (((2,), (2,)), ((0,), (0,))), preferred_element_type=F32)


def _bmm_tn(a, b):
    return lax.dot_general(a, b, (((1,), (1,)), ((0,), (0,))), preferred_element_type=F32)


def _rms(x, g):
    return x * lax.rsqrt(jnp.mean(x * x, axis=-1, keepdims=True) + EPS) * g


def _modulate(x3, mod, gain, which):
    d = x3.shape[-1]
    sh = mod[:, :, (3 * which) * d:(3 * which + 1) * d]
    sc = mod[:, :, (3 * which + 1) * d:(3 * which + 2) * d]
    inv = lax.rsqrt(jnp.mean(x3 * x3, axis=-1, keepdims=True) + EPS)
    return x3 * inv * (gain * (1.0 + sc)) + sh


def _mod_row_spec(mod, layer, row):
    assert row % ROWS == 0
    return pl.BlockSpec((None, ROWS, mod.shape[-1]), lambda i: (layer, row // ROWS, 0))


def _mod_seq_spec(mod, layer, sb, first_step=0):
    return pl.BlockSpec((None, sb, mod.shape[-1]), lambda i: (layer, jnp.maximum(i - first_step, 0), 0))


def _row_mod(mod_ref):
    return mod_ref[0:1, :][:, None, :]


def _seq_mod(mod_ref):
    return mod_ref[...][:, None, :]


def _gate(mod, which, d):
    return mod[:, :, (3 * which + 2) * d:(3 * which + 3) * d]


def _adaln_body(c_ref, w_ref, b_ref, o_ref):
    c = c_ref[...]
    a = (c * jax.nn.sigmoid(c)).astype(BF16)
    o_ref[0] = _dot(a, w_ref[0].astype(BF16)) + b_ref[0]


def _adaln(c_all, ada_w, ada_b):
    depth, d, n = ada_w.shape
    r = c_all.shape[0]
    tn = 1536
    return pl.pallas_call(
        _adaln_body,
        grid=(depth, n // tn),
        in_specs=[pl.BlockSpec((r, d), lambda l, j: (0, 0)),
                  pl.BlockSpec((1, d, tn), lambda l, j: (l, 0, j)),
                  pl.BlockSpec((1, 1, tn), lambda l, j: (l, 0, j))],
        out_specs=pl.BlockSpec((1, r, tn), lambda l, j: (l, 0, j)),
        out_shape=jax.ShapeDtypeStruct((depth, r, n), F32),
        compiler_params=pltpu.CompilerParams(dimension_semantics=("arbitrary", "arbitrary"),
                                             vmem_limit_bytes=VMEM_LIMIT),
        name="adaln",
    )(c_all, ada_w, ada_b.reshape(depth, 1, n))


BF16_ROWS = 16


def _cast_specs(job, n_steps):
    w, layer = job
    _, rows, cols = w.shape
    hold = 1
    while rows % (n_steps // hold) or (rows // (n_steps // hold)) % BF16_ROWS:
        hold *= 2
        assert hold <= n_steps and n_steps % hold == 0
    blk = rows // (n_steps // hold)
    src = pl.BlockSpec((None, blk, cols),
                       lambda i, _h=hold, _l=layer: (_l, jnp.minimum(i, n_steps - 1) // _h, 0))
    dst = pl.BlockSpec((blk, cols), lambda i, _h=hold: (jnp.minimum(i, n_steps - 1) // _h, 0))
    return src, dst, jax.ShapeDtypeStruct((rows, cols), BF16)


def _cast_blocks(in_refs, out_refs):
    for src, dst in zip(in_refs, out_refs):
        dst[...] = src[...].astype(BF16)


def _ffn_body(n_cast, n_prompt, xp_ref, mp_ref, xs_ref, ms_ref, gn_ref, wg_ref, wu_ref, wd_ref, *refs):
    cast_in, (op_ref, os_ref), cast_out = refs[:n_cast], refs[n_cast:n_cast + 2], refs[n_cast + 2:]
    step = pl.program_id(0)

    def tile(x_ref, mod, o_ref):
        x = x_ref[...]
        sb, _, d = x.shape
        h = _modulate(x, mod, gn_ref[...], 1).reshape(sb * ROWS, d).astype(BF16)
        a = _dot(h, wg_ref[...])
        b = _dot(h, wu_ref[...])
        act = (a * jax.nn.sigmoid(a) * b).astype(BF16)
        y = _dot(act, wd_ref[...])
        o_ref[...] = x + _gate(mod, 1, d) * y.reshape(sb, ROWS, d)

    @pl.when(step < n_prompt)
    def _():
        tile(xp_ref, _row_mod(mp_ref), op_ref)

    @pl.when(step >= n_prompt)
    def _():
        tile(xs_ref, _seq_mod(ms_ref), os_ref)

    _cast_blocks(cast_in, cast_out)


def _ffn(xp3, xs3, mod, layer, gain, wg, wu, wd, sb, cast=()):
    n8, _, d = xp3.shape
    f = wg.shape[1]
    n_prompt = n8 // sb
    n_sample = xs3.shape[0] // sb
    prompt_row = xs3.shape[0]
    assert n8 % sb == 0 and xs3.shape[0] % sb == 0
    cast_specs = [_cast_specs(w, n_prompt) for w in cast]
    p_idx = lambda i: (jnp.minimum(i, n_prompt - 1), 0, 0)
    s_idx = lambda i: (jnp.maximum(i - n_prompt, 0), 0, 0)
    xp_spec = pl.BlockSpec((sb, ROWS, d), p_idx)
    xs_spec = pl.BlockSpec((sb, ROWS, d), s_idx)
    out = pl.pallas_call(
        functools.partial(_ffn_body, len(cast), n_prompt),
        grid=(n_prompt + n_sample,),
        in_specs=[xp_spec, _mod_row_spec(mod, layer, prompt_row),
                  xs_spec, _mod_seq_spec(mod, layer, sb, n_prompt),
                  _const_spec((1, d)), _const_spec((d, f)), _const_spec((d, f)),
                  _const_spec((f, d))] + [c[0] for c in cast_specs],
        out_specs=[xp_spec, xs_spec] + [c[1] for c in cast_specs],
        out_shape=[jax.ShapeDtypeStruct(xp3.shape, F32), jax.ShapeDtypeStruct(xs3.shape, F32)]
        + [c[2] for c in cast_specs],
        compiler_params=_params(),
        name="ffn",
    )(xp3, mod, xs3, mod, gain.reshape(1, d), wg, wu, wd, *[w for w, _ in cast])
    return out[0], out[1], tuple(out[2:])


def _head_rms(t, g):
    return t * lax.rsqrt(jnp.mean(t * t, axis=-1, keepdims=True) + EPS) * g


def _group_norm_gate(o, gain, gate):
    mu = jnp.mean(o, axis=-1, keepdims=True)
    var = jnp.mean(jnp.square(o - mu), axis=-1, keepdims=True)
    return (o - mu) * lax.rsqrt(var + EPS) * gain * (gate * jax.nn.sigmoid(gate))


def _ret_decay(hb, c):
    lg = _ret_log_gamma(hb)
    ii = lax.broadcasted_iota(jnp.int32, (c, c), 0)
    jj = lax.broadcasted_iota(jnp.int32, (c, c), 1)
    diff = (ii - jj).astype(F32)
    d_in = jnp.where(diff >= 0, jnp.exp(lg * jnp.maximum(diff, 0.0)), 0.0)
    row = lax.broadcasted_iota(jnp.int32, (c, B_KD), 0).astype(F32)
    d_q = jnp.exp(lg * (row + 1.0))
    d_k = jnp.exp(lg * (c - 1.0 - row))
    d_c = math.exp(lg * c)
    return d_in, d_q, d_k, d_c


P_OFF_KA = QA_W
P_OFF_VA = P_OFF_KA + KA_W
P_OFF_QB = P_OFF_VA + VA_W
P_OFF_VB = P_OFF_QB + QB_W
P_OFF_GB = P_OFF_VB + VB_W
P_WIDTH = P_OFF_GB + GB_W
PAIR = 2 * A_HD


def _twice(t):
    low = lax.broadcasted_iota(jnp.int32, t.shape, 1) < A_HD
    swapped = pltpu.roll(t, A_HD, 1)
    return jnp.concatenate([jnp.where(low, t, swapped), jnp.where(low, swapped, t)], axis=1)


def _mix_even_prompt_body(n_cast, x_ref, mod_ref, gn_ref, win_ref, wkt_ref, qg_ref, kg_ref, sink_ref, rg_ref, wout_ref,
                          *refs):
    cast_in, refs = refs[:n_cast], refs[n_cast:]
    o_ref, pk_ref, pv_ref, ps_ref = refs[:4]
    cast_out = refs[4:4 + n_cast]
    mix_ref, carry_ref, bias_ref, dec_ref, ones_ref = refs[4 + n_cast:]
    _cast_blocks(cast_in, cast_out)
    step = pl.program_id(0)
    blk = WINDOW
    rows4 = A_GROUP * blk

    @pl.when(step == 0)
    def _():
        carry_ref[...] = jnp.zeros_like(carry_ref)
        ps_ref[...] = jnp.zeros_like(ps_ref)
        er = lax.broadcasted_iota(jnp.int32, ones_ref.shape, 0) // A_HD
        ec = lax.broadcasted_iota(jnp.int32, ones_ref.shape, 1) // A_HD
        ones_ref[...] = jnp.where(er == ec, 1.0 / A_HD, 0.0).astype(BF16)
        row = lax.broadcasted_iota(jnp.int32, (rows4, 2 * blk), 0)
        dist = row % blk + blk - lax.broadcasted_iota(jnp.int32, (rows4, 2 * blk), 1)
        in_window = (dist >= 0) & (dist < WINDOW)
        for kv in range(A_KV):
            slope = jnp.zeros((rows4, 2 * blk), F32)
            for g in range(A_GROUP):
                slope = jnp.where(row // blk == g, _alibi_slope(kv * A_GROUP + g), slope)
            bias_ref[kv] = jnp.where(in_window, slope * dist.astype(F32), -NEG_INF)
        for hb in range(B_HEADS):
            lg = _ret_log_gamma(hb)
            ii = lax.broadcasted_iota(jnp.int32, (blk, blk), 0).astype(F32)
            jj = lax.broadcasted_iota(jnp.int32, (blk, blk), 1).astype(F32)
            diff = ii - jj
            dec_ref[hb, 0] = jnp.where(diff >= 0, jnp.exp(lg * jnp.maximum(diff, 0.0)), 0.0)
            dec_ref[hb, 1] = jnp.exp(lg * (ii + 1.0))
            dec_ref[hb, 2] = jnp.exp(lg * (blk - 1.0 - jj))

    x = x_ref[...]
    sb, _, d = x.shape
    tb = sb * ROWS
    mod = _row_mod(mod_ref)
    h = _modulate(x, mod, gn_ref[...], 0).reshape(tb, d).astype(BF16)
    proj = _dot(h, win_ref[...])
    kt_all = _dot_nt(wkt_ref[...], h) * (B_KD ** -0.5)
    rg = rg_ref[...]

    qa = proj[:, 0:QA_W]
    ka = proj[:, P_OFF_KA:P_OFF_KA + KA_W]
    va = proj[:, P_OFF_VA:P_OFF_VA + VA_W]
    qsq = (qa * qa).astype(BF16)
    tile_w = ones_ref.shape[0]
    q_ms = jnp.concatenate([_dot(qsq[:, c:c + tile_w], ones_ref[...]) for c in range(0, QA_W, tile_w)], axis=1)
    q_hat = (qa * lax.rsqrt(q_ms + EPS) * qg_ref[...] * (A_HD ** -0.5)).astype(BF16)
    k_hat = ka * lax.rsqrt(_dot((ka * ka).astype(BF16), ones_ref[0:KA_W, 0:KA_W]) + EPS) * kg_ref[...]
    k_hat16 = _twice(k_hat).astype(BF16)
    va16 = _twice(va).astype(BF16)
    prev = carry_ref[step % 2]
    carry_ref[(step + 1) % 2] = jnp.concatenate([k_hat16[tb - blk:tb], va16[tb - blk:tb]], axis=1)

    row_g = lax.broadcasted_iota(jnp.int32, (rows4, 1), 0) // blk
    key_is_prev = lax.broadcasted_iota(jnp.int32, (rows4, 2 * blk), 1) < blk
    first_penalty = jnp.where(step == 0, -NEG_INF, 0.0)
    lane_low = lax.broadcasted_iota(jnp.int32, (blk, PAIR), 1) < A_HD
    ones_cols = jnp.ones((2 * blk, PAIR), BF16)

    n_blk = tb // blk
    att = [(j, kv) for j in range(n_blk) for kv in range(A_KV)]
    ret = [(j, hb) for j in range(n_blk) for hb in range(B_HEADS)]

    sinks, scores = {}, {}
    for j, kv in att:
        r0 = j * blk
        kcol = slice(kv * PAIR, (kv + 1) * PAIR)
        if j == 0:
            k2 = jnp.concatenate([prev[:, kcol], k_hat16[0:blk, kcol]], axis=0)
        else:
            k2 = k_hat16[r0 - blk:r0 + blk, kcol]
        q4 = jnp.concatenate(
            [jnp.where(lane_low == (g % 2 == 0),
                       q_hat[r0:r0 + blk, (kv * A_GROUP + g - g % 2) * A_HD:(kv * A_GROUP + g - g % 2 + 2) * A_HD],
                       jnp.zeros((), BF16))
             for g in range(A_GROUP)], axis=0)
        scores[j, kv] = _dot_nt(q4, k2)
        sink = jnp.zeros((rows4, 1), F32)
        for g in range(A_GROUP):
            sink = jnp.where(row_g == g, sink_ref[kv * A_GROUP + g], sink)
        sinks[j, kv] = sink
    qb, vb, kt, inner = {}, {}, {}, {}
    for j, hb in ret:
        r0 = j * blk
        qb[j, hb] = proj[r0:r0 + blk, P_OFF_QB + hb * B_KD:P_OFF_QB + (hb + 1) * B_KD].astype(BF16)
        vb[j, hb] = proj[r0:r0 + blk, P_OFF_VB + hb * B_VD:P_OFF_VB + (hb + 1) * B_VD].astype(BF16)
        kt[j, hb] = kt_all[hb * B_KD:(hb + 1) * B_KD, r0:r0 + blk]
        inner[j, hb] = _dot(qb[j, hb], kt[j, hb].astype(BF16))

    probs, maxes = {}, {}
    for j, kv in att:
        s = scores[j, kv] - bias_ref[kv]
        if j == 0:
            s = s - jnp.where(key_is_prev, first_penalty, 0.0)
        mx = jnp.maximum(jnp.max(s, axis=-1, keepdims=True), sinks[j, kv])
        probs[j, kv] = jnp.exp(s - mx).astype(BF16)
        maxes[j, kv] = mx
    state = {}
    for hb in range(B_HEADS):
        d_c = math.exp(_ret_log_gamma(hb) * blk)
        state[0, hb] = ps_ref[hb]
        for j in range(n_blk):
            state[j + 1, hb] = state[j, hb] * d_c + _dot((kt[j, hb] * dec_ref[hb, 2]).astype(BF16), vb[j, hb])
        ps_ref[hb] = state[n_blk, hb]

    for j, kv in att:
        r0 = j * blk
        kcol = slice(kv * PAIR, (kv + 1) * PAIR)
        vcol = slice(2 * KA_W + kv * PAIR, 2 * KA_W + (kv + 1) * PAIR)
        if j == 0:
            v2 = jnp.concatenate([prev[:, vcol], va16[0:blk, kcol]], axis=0)
        else:
            v2 = va16[r0 - blk:r0 + blk, kcol]
        pv = _dot(probs[j, kv], jnp.concatenate([v2, ones_cols], axis=1))
        o4 = pv[:, 0:PAIR] / (pv[:, PAIR:2 * PAIR] + jnp.exp(sinks[j, kv] - maxes[j, kv]))
        for g in range(A_GROUP):
            hd = kv * A_GROUP + g
            half = slice((hd % 2) * A_HD, (hd % 2 + 1) * A_HD)
            mix_ref[r0:r0 + blk, hd * A_HD:(hd + 1) * A_HD] = o4[g * blk:(g + 1) * blk, half]
    o_ret = {}
    for j, hb in ret:
        o_ret[j, hb] = (_dot((inner[j, hb] * dec_ref[hb, 0]).astype(BF16), vb[j, hb])
                        + _dot(qb[j, hb], state[j, hb].astype(BF16)) * dec_ref[hb, 1])

    cen = {k: o_ret[k] - jnp.mean(o_ret[k], axis=-1, keepdims=True) for k in ret}
    var = {k: jnp.mean(cen[k] * cen[k], axis=-1, keepdims=True) for k in ret}
    for j, hb in ret:
        r0 = j * blk
        gb = proj[r0:r0 + blk, P_OFF_GB + hb * B_VD:P_OFF_GB + (hb + 1) * B_VD]
        mix_ref[r0:r0 + blk, QA_W + hb * B_VD:QA_W + (hb + 1) * B_VD] = (
            cen[j, hb] * lax.rsqrt(var[j, hb] + EPS) * rg[:, hb * B_VD:(hb + 1) * B_VD] * (gb * jax.nn.sigmoid(gb)))

    out = _dot(mix_ref[...].astype(BF16), wout_ref[...])
    o_ref[...] = x + _gate(mod, 0, d) * out.reshape(sb, ROWS, d)

    @pl.when(step == pl.num_programs(0) - 1)
    def _():
        pk_ref[...] = k_hat[tb - blk:tb, :]
        pv_ref[...] = va[tb - blk:tb, :]


def _mix_even_prompt(x3, mod, layer, prompt_row, gain, w_in, q_gain, k_gain, sinks, ret_gain, w_out, sb, cast=()):
    n8, _, d = x3.shape
    tb = sb * ROWS
    cast_specs = [_cast_specs(w, n8 // sb) for w in cast]
    w_main = jnp.concatenate([w_in[:, :OFF_KB], w_in[:, OFF_VB:]], axis=1)
    wk_t = w_in[:, OFF_KB:OFF_VB].T
    out = pl.pallas_call(
        functools.partial(_mix_even_prompt_body, len(cast)),
        grid=(n8 // sb,),
        in_specs=[pl.BlockSpec((sb, ROWS, d), lambda i: (i, 0, 0)),
                  _mod_row_spec(mod, layer, prompt_row),
                  _const_spec((1, d)), _const_spec((d, P_WIDTH)), _const_spec((KB_W, d)),
                  _const_spec((1, QA_W)), _const_spec((1, KA_W)),
                  pl.BlockSpec(memory_space=pltpu.SMEM),
                  _const_spec((1, VB_W)), _const_spec((QA_W + VB_W, d))] + [c[0] for c in cast_specs],
        out_specs=[pl.BlockSpec((sb, ROWS, d), lambda i: (i, 0, 0)),
                   pl.BlockSpec((WINDOW, KA_W), lambda i: (0, 0)),
                   pl.BlockSpec((WINDOW, VA_W), lambda i: (0, 0)),
                   pl.BlockSpec((B_HEADS, B_KD, B_VD), lambda i: (0, 0, 0))] + [c[1] for c in cast_specs],
        out_shape=[jax.ShapeDtypeStruct(x3.shape, F32),
                   jax.ShapeDtypeStruct((WINDOW, KA_W), F32),
                   jax.ShapeDtypeStruct((WINDOW, VA_W), F32),
                   jax.ShapeDtypeStruct((B_HEADS, B_KD, B_VD), F32)] + [c[2] for c in cast_specs],
        scratch_shapes=[pltpu.VMEM((tb, QA_W + VB_W), F32),
                        pltpu.VMEM((2, WINDOW, 2 * KA_W + 2 * VA_W), BF16),
                        pltpu.VMEM((A_KV, A_GROUP * WINDOW, 2 * WINDOW), F32),
                        pltpu.VMEM((B_HEADS, 3, WINDOW, WINDOW), F32),
                        pltpu.VMEM((2 * PAIR, 2 * PAIR), BF16)],
        compiler_params=_params(),
        name="mix_even_prompt",
    )(x3, mod, gain.reshape(1, d), w_main, wk_t, jnp.tile(q_gain, A_HEADS).reshape(1, QA_W),
      jnp.tile(k_gain, A_KV).reshape(1, KA_W),
      sinks, ret_gain.reshape(1, VB_W), w_out, *[w for w, _ in cast])
    return out[:4], tuple(out[4:])


def _mix_even_sample_body(x_ref, mod_ref, gn_ref, win_ref, qg_ref, kg_ref, sink_ref, rg_ref, wout_ref,
                          ck_ref, cv_ref, s0_ref,
                          o_ref, nk_ref, nv_ref, ns_ref, mix_ref):
    x = x_ref[...]
    sb, length, d = x.shape
    tb = sb * length
    w = ck_ref.shape[1]
    mod = _seq_mod(mod_ref)
    h = _modulate(x, mod, gn_ref[...], 0).reshape(tb, d).astype(BF16)
    proj = _dot(h, win_ref[...])
    qg = qg_ref[...]
    kg = kg_ref[...]
    rg = rg_ref[...]

    rows = A_GROUP * length
    qpos_c = lax.broadcasted_iota(jnp.int32, (rows, w), 0) % length
    kpos_c = lax.broadcasted_iota(jnp.int32, (rows, w), 1)
    dist_c = w + qpos_c - kpos_c
    valid_c = (dist_c >= 0) & (dist_c < WINDOW)
    qpos_n = lax.broadcasted_iota(jnp.int32, (rows, length), 0) % length
    kpos_n = lax.broadcasted_iota(jnp.int32, (rows, length), 1)
    dist_n = qpos_n - kpos_n
    valid_n = (dist_n >= 0) & (dist_n < WINDOW)
    row_g = lax.broadcasted_iota(jnp.int32, (rows, 1), 0) // length

    scale = A_HD ** -0.5
    kn, vn, s_c, s_n, vc, slopes, sinks = {}, {}, {}, {}, {}, {}, {}
    for kv in range(A_KV):
        lanes = slice(kv * A_HD, (kv + 1) * A_HD)
        kn[kv] = _head_rms(proj[:, OFF_KA + kv * A_HD:OFF_KA + (kv + 1) * A_HD], kg).reshape(sb, length, A_HD)
        vn[kv] = proj[:, OFF_VA + kv * A_HD:OFF_VA + (kv + 1) * A_HD].reshape(sb, length, A_HD)
        nk_ref[:, 0:w - length, lanes] = ck_ref[:, length:w, lanes]
        nv_ref[:, 0:w - length, lanes] = cv_ref[:, length:w, lanes]
        nk_ref[:, w - length:w, lanes] = kn[kv]
        nv_ref[:, w - length:w, lanes] = vn[kv]
        kc = ck_ref[:, :, lanes].astype(BF16)
        vc[kv] = cv_ref[:, :, lanes].astype(BF16)
        q4 = jnp.concatenate(
            [_head_rms(proj[:, OFF_QA + (kv * A_GROUP + g) * A_HD:OFF_QA + (kv * A_GROUP + g + 1) * A_HD], qg)
             .reshape(sb, length, A_HD) for g in range(A_GROUP)], axis=1).astype(BF16)
        slope = jnp.zeros((rows, 1), F32)
        sink = jnp.zeros((rows, 1), F32)
        for g in range(A_GROUP):
            hd = kv * A_GROUP + g
            slope = jnp.where(row_g == g, _alibi_slope(hd), slope)
            sink = jnp.where(row_g == g, sink_ref[hd], sink)
        slopes[kv], sinks[kv] = slope, sink
        s_c[kv] = _bmm_nt(q4, kc)
        s_n[kv] = _bmm_nt(q4, kn[kv].astype(BF16))
    decay = {hb: _ret_decay(hb, length) for hb in range(B_HEADS)}
    qb, kb, vb, inner, q_state = {}, {}, {}, {}, {}
    for hb in range(B_HEADS):
        qb[hb] = proj[:, OFF_QB + hb * B_KD:OFF_QB + (hb + 1) * B_KD].reshape(sb, length, B_KD).astype(BF16)
        kb[hb] = proj[:, OFF_KB + hb * B_KD:OFF_KB + (hb + 1) * B_KD].reshape(sb, length, B_KD) * (B_KD ** -0.5)
        vb[hb] = proj[:, OFF_VB + hb * B_VD:OFF_VB + (hb + 1) * B_VD].reshape(sb, length, B_VD).astype(BF16)
        inner[hb] = _bmm_nt(qb[hb], kb[hb].astype(BF16))
        q_state[hb] = _bmm(qb[hb], s0_ref[:, hb].astype(BF16))

    p_c, p_n, den = {}, {}, {}
    for kv in range(A_KV):
        a_c = jnp.where(valid_c, s_c[kv] * scale - slopes[kv] * dist_c.astype(F32), NEG_INF)
        a_n = jnp.where(valid_n, s_n[kv] * scale - slopes[kv] * dist_n.astype(F32), NEG_INF)
        mx = jnp.maximum(jnp.maximum(jnp.max(a_c, axis=-1, keepdims=True),
                                     jnp.max(a_n, axis=-1, keepdims=True)), sinks[kv])
        p_c[kv] = jnp.exp(a_c - mx)
        p_n[kv] = jnp.exp(a_n - mx)
        den[kv] = (jnp.sum(p_c[kv], axis=-1, keepdims=True) + jnp.sum(p_n[kv], axis=-1, keepdims=True)
                   + jnp.exp(sinks[kv] - mx))

    for kv in range(A_KV):
        o4 = (_bmm(p_c[kv].astype(BF16), vc[kv]) + _bmm(p_n[kv].astype(BF16), vn[kv].astype(BF16))) / den[kv]
        for g in range(A_GROUP):
            hd = kv * A_GROUP + g
            mix_ref[:, hd * A_HD:(hd + 1) * A_HD] = o4[:, g * length:(g + 1) * length, :].reshape(tb, A_HD)
    o_ret = {}
    for hb in range(B_HEADS):
        d_in, d_q, d_k, d_c = decay[hb]
        o_ret[hb] = _bmm((inner[hb] * d_in).astype(BF16), vb[hb]) + q_state[hb] * d_q
        ns_ref[:, hb] = s0_ref[:, hb] * d_c + _bmm_tn((kb[hb] * d_k).astype(BF16), vb[hb])

    for hb in range(B_HEADS):
        gb = proj[:, OFF_GB + hb * B_VD:OFF_GB + (hb + 1) * B_VD]
        mix_ref[:, QA_W + hb * B_VD:QA_W + (hb + 1) * B_VD] = _group_norm_gate(
            o_ret[hb].reshape(tb, B_VD), rg[:, hb * B_VD:(hb + 1) * B_VD], gb)

    out = _dot(mix_ref[...].astype(BF16), wout_ref[...])
    o_ref[...] = x + _gate(mod, 0, d) * out.reshape(sb, length, d)


def _mix_even_sample(x3, mod, layer, gain, w_in, q_gain, k_gain, sinks, ret_gain, w_out, cache_k, cache_v, state, sb):
    n, length, d = x3.shape
    in_w = w_in.shape[1]
    w = cache_k.shape[1]
    seq_spec = lambda shape: pl.BlockSpec((sb,) + shape, lambda i, _n=len(shape): (i,) + (0,) * _n)
    return pl.pallas_call(
        _mix_even_sample_body,
        grid=(n // sb,),
        in_specs=[seq_spec((length, d)), _mod_seq_spec(mod, layer, sb),
                  _const_spec((1, d)), _const_spec((d, in_w)),
                  _const_spec((1, A_HD)), _const_spec((1, A_HD)),
                  pl.BlockSpec(memory_space=pltpu.SMEM),
                  _const_spec((1, VB_W)), _const_spec((QA_W + VB_W, d)),
                  seq_spec((w, KA_W)), seq_spec((w, VA_W)),
                  pl.BlockSpec((None, sb, B_HEADS, B_KD, B_VD), lambda i: (0, i, 0, 0, 0))],
        out_specs=[seq_spec((length, d)), seq_spec((w, KA_W)), seq_spec((w, VA_W)),
                   seq_spec((B_HEADS, B_KD, B_VD))],
        out_shape=[jax.ShapeDtypeStruct(x3.shape, F32),
                   jax.ShapeDtypeStruct(cache_k.shape, F32),
                   jax.ShapeDtypeStruct(cache_v.shape, F32),
                   jax.ShapeDtypeStruct(state.shape[1:], F32)],
        scratch_shapes=[pltpu.VMEM((sb * length, QA_W + VB_W), F32)],
        compiler_params=_params(),
        name="mix_even_sample",
    )(x3, mod, gain.reshape(1, d), w_in, q_gain.reshape(1, A_HD), k_gain.reshape(1, A_HD),
      sinks, ret_gain.reshape(1, VB_W), w_out, cache_k, cache_v, state)


def _cmul(ar, ai, br, bi):
    return ar * br - ai * bi, ar * bi + ai * br


def _s5_lambda(a_re, a_im, log_dt):
    dt = jnp.exp(log_dt)
    mag = jnp.exp(a_re * dt)
    return mag * jnp.cos(a_im * dt), mag * jnp.sin(a_im * dt)


def _s5_prep_body(t_len, n_chunks, are_ref, aim_ref, ldt_ref, btr_ref, bti_ref, ctr_ref, cti_ref,
                  ard_ref, aid_ref, ldd_ref,
                  bblk_ref, cre_ref, cim_ref, lam_ref, pwb_ref, ltp_ref, b32_ref, cr32_ref, ci32_ref, pw32_ref):
    groups = btr_ref.shape[0]
    k_in, p = btr_ref.shape[1:]
    half = S5_OCT * p
    a_re = are_ref[...]
    a_im = aim_ref[...]
    lam_re, lam_im = _s5_lambda(a_re, a_im, ldt_ref[...])
    den = a_re * a_re + a_im * a_im
    n_re = lam_re - 1.0
    n_im = lam_im
    f_re = (n_re * a_re + n_im * a_im) / den
    f_im = (n_im * a_re - n_re * a_im) / den
    br = btr_ref[...]
    bi = bti_ref[...]
    bb_re = f_re * br - f_im * bi
    bb_im = f_re * bi + f_im * br
    b32_ref[...] = jnp.zeros_like(b32_ref)
    cr32_ref[...] = jnp.zeros_like(cr32_ref)
    ci32_ref[...] = jnp.zeros_like(ci32_ref)
    for g in range(groups):
        s, gl = divmod(g, S5_OCT)
        rows = slice(gl * k_in, (gl + 1) * k_in)
        cols = slice(gl * p, (gl + 1) * p)
        b32_ref[s, rows, cols] = bb_re[g]
        b32_ref[s, rows, half + gl * p:half + (gl + 1) * p] = bb_im[g]
        cr32_ref[s, cols, rows] = ctr_ref[g]
        ci32_ref[s, cols, rows] = cti_ref[g]
    bblk_ref[...] = b32_ref[...].astype(BF16)
    cre_ref[...] = cr32_ref[...].astype(BF16)
    cim_ref[...] = ci32_ref[...].astype(BF16)
    lam_re, lam_im = _s5_lambda(ard_ref[...], aid_ref[...], ldd_ref[...])
    lam_ref[0] = lam_re
    lam_ref[1] = lam_im
    n_oct = lam_re.shape[0]

    def both(cr, ci, s, rows):
        return jnp.concatenate([jnp.broadcast_to(cr[s:s + 1, :], (rows, half)),
                                jnp.broadcast_to(ci[s:s + 1, :], (rows, half))], axis=1)

    cr, ci = lam_re, lam_im
    for t in range(t_len):
        for s in range(n_oct):
            pw32_ref[s, t * ROWS:(t + 1) * ROWS, :] = both(cr, ci, s, ROWS)
        if t + 1 < t_len:
            cr, ci = _cmul(cr, ci, lam_re, lam_im)
    pwb_ref[...] = pw32_ref[...].astype(BF16)
    tr, ti = cr, ci
    ltp_ref[...] = jnp.zeros_like(ltp_ref)
    cr, ci = jnp.ones_like(lam_re), jnp.zeros_like(lam_im)
    for m in range(n_chunks + 1):
        for s in range(n_oct):
            if m < n_chunks:
                ltp_ref[s, n_chunks + m:n_chunks + m + 1, :] = both(cr, ci, s, 1)
            if m in (1, 2, 4, 8):
                i = (1, 2, 4, 8).index(m)
                ltp_ref[s, i:i + 1, :] = both(cr, ci, s, 1)
        cr, ci = _cmul(cr, ci, tr, ti)


def _s5_prep(a_re, a_im, log_dt, b_re, b_im, c_re, c_im, t_len, n_chunks):
    g, p = a_re.shape
    k = b_re.shape[-1]
    assert n_chunks == ROWS
    n_oct = g // S5_OCT
    half = S5_OCT * p
    dense = (n_oct, half)
    return pl.pallas_call(
        functools.partial(_s5_prep_body, t_len, n_chunks),
        out_shape=[jax.ShapeDtypeStruct((n_oct, S5_OCT * k, 2 * half), BF16),
                   jax.ShapeDtypeStruct((n_oct, half, S5_OCT * k), BF16),
                   jax.ShapeDtypeStruct((n_oct, half, S5_OCT * k), BF16),
                   jax.ShapeDtypeStruct((2,) + dense, F32),
                   jax.ShapeDtypeStruct((n_oct, t_len * ROWS, 2 * half), BF16),
                   jax.ShapeDtypeStruct((n_oct, 2 * n_chunks, 2 * half), F32)],
        scratch_shapes=[pltpu.VMEM((n_oct, S5_OCT * k, 2 * half), F32),
                        pltpu.VMEM((n_oct, half, S5_OCT * k), F32),
                        pltpu.VMEM((n_oct, half, S5_OCT * k), F32),
                        pltpu.VMEM((n_oct, t_len * ROWS, 2 * half), F32)],
        compiler_params=pltpu.CompilerParams(vmem_limit_bytes=VMEM_LIMIT),
        name="s5_prep",
    )(a_re.reshape(g, 1, p), a_im.reshape(g, 1, p), log_dt.reshape(g, 1, 1),
      jnp.swapaxes(b_re, 1, 2), jnp.swapaxes(b_im, 1, 2), jnp.swapaxes(c_re, 1, 2), jnp.swapaxes(c_im, 1, 2),
      a_re.reshape(dense), a_im.reshape(dense), jnp.broadcast_to(log_dt[:, None], (g, p)).reshape(dense))


def _gelu_glu_out(x, mod, y, u, dskip, glua_ref, glub_ref):
    sb, rows, d = x.shape
    y = y + dskip * u
    yg = jax.nn.gelu(y, approximate=True).astype(BF16)
    out = _dot(yg, glua_ref[...]) * jax.nn.sigmoid(_dot(yg, glub_ref[...]))
    return x + _gate(mod, 0, d) * out.reshape(sb, rows, d)


def _mix_odd_prompt_body(x_ref, xn_ref, mod_ref, gn_ref, bblk_ref, lam_ref, pwb_ref, ltp_ref, cre_ref, cim_ref,
                         dskip_ref, glua_ref, glub_ref, o_ref, hre_ref, him_ref,
                         un_ref, l0_ref, l1_ref, up0_ref, up1_ref, pt_ref):
    step = pl.program_id(0)
    sb, _, d = x_ref.shape
    tm = sb * ROWS
    t_len = tm // ROWS
    n_oct = bblk_ref.shape[0]
    half = bblk_ref.shape[2] // 2
    mod = _row_mod(mod_ref)

    def permuted_input(src_ref, up_dst):
        u = _modulate(src_ref[...], mod, gn_ref[...], 0).reshape(tm, d)
        pitch = un_ref.shape[1] // ROWS
        for k in range(d // 128):
            for c in range(ROWS):
                un_ref[k, c * pitch:c * pitch + t_len, :] = u[c * t_len:(c + 1) * t_len, k * 128:(k + 1) * 128]
        up = jnp.concatenate(
            [jnp.concatenate([un_ref[k, pl.ds(t, ROWS, stride=pitch), :] for k in range(d // 128)], axis=1)
             for t in range(t_len)], axis=0)
        up_dst[...] = up
        return up.astype(BF16)

    @pl.when(step == 0)
    def _():
        hre_ref[...] = jnp.zeros_like(hre_ref)
        him_ref[...] = jnp.zeros_like(him_ref)
        nat = lax.broadcasted_iota(jnp.int32, (tm, tm), 0)
        prm = lax.broadcasted_iota(jnp.int32, (tm, tm), 1)
        pt_ref[...] = jnp.where(prm == (nat % t_len) * ROWS + nat // t_len, 1.0, 0.0).astype(BF16)
        up16 = permuted_input(x_ref, up0_ref)
        for s in range(n_oct):
            l0_ref[s] = _dot(up16[:, s * 128:(s + 1) * 128], bblk_ref[s])

    def run(l_ref, l_next, up_ref, up_next):
        for s0 in range(0, n_oct, 2):
            pair = (s0, s0 + 1)
            lam_re = [jnp.broadcast_to(lam_ref[0, s:s + 1, :], (ROWS, half)) for s in pair]
            lam_im = [jnp.broadcast_to(lam_ref[1, s:s + 1, :], (ROWS, half)) for s in pair]

            def local_step(t, carry):
                r = pl.multiple_of(t * ROWS, ROWS)
                out = []
                for i, s in enumerate(pair):
                    h_re, h_im = carry[2 * i], carry[2 * i + 1]
                    n_re = lam_re[i] * h_re - lam_im[i] * h_im + l_ref[s, pl.ds(r, ROWS), 0:half]
                    n_im = lam_re[i] * h_im + lam_im[i] * h_re + l_ref[s, pl.ds(r, ROWS), half:2 * half]
                    l_ref[s, pl.ds(r, ROWS), 0:half] = n_re
                    l_ref[s, pl.ds(r, ROWS), half:2 * half] = n_im
                    out += [n_re, n_im]
                return tuple(out)

            zero = jnp.zeros((ROWS, half), F32)
            lax.fori_loop(0, t_len, local_step, (zero, zero, zero, zero), unroll=2)

        next16 = permuted_input(xn_ref, up_next)
        chunk = lax.broadcasted_iota(jnp.int32, (ROWS, half), 0)
        y_parts = []
        for s in range(n_oct):
            l_next[s] = _dot(next16[:, s * 128:(s + 1) * 128], bblk_ref[s])
            p_re = l_ref[s, tm - ROWS:tm, 0:half]
            p_im = l_ref[s, tm - ROWS:tm, half:2 * half]
            for i, sh in enumerate((1, 2, 4)):
                m_re, m_im = _cmul(ltp_ref[s, i:i + 1, 0:half], ltp_ref[s, i:i + 1, half:2 * half],
                                   jnp.where(chunk >= sh, pltpu.roll(p_re, sh, 0), 0.0),
                                   jnp.where(chunk >= sh, pltpu.roll(p_im, sh, 0), 0.0))
                p_re, p_im = p_re + m_re, p_im + m_im
            hin_re = jnp.broadcast_to(hre_ref[s:s + 1, :], (ROWS, half))
            hin_im = jnp.broadcast_to(him_ref[s:s + 1, :], (ROWS, half))
            m_re, m_im = _cmul(ltp_ref[s, ROWS:2 * ROWS, 0:half], ltp_ref[s, ROWS:2 * ROWS, half:2 * half],
                               hin_re, hin_im)
            st_re = m_re + jnp.where(chunk >= 1, pltpu.roll(p_re, 1, 0), 0.0)
            st_im = m_im + jnp.where(chunk >= 1, pltpu.roll(p_im, 1, 0), 0.0)
            m_re, m_im = _cmul(ltp_ref[s, 3:4, 0:half], ltp_ref[s, 3:4, half:2 * half],
                               hre_ref[s:s + 1, :], him_ref[s:s + 1, :])
            hre_ref[s:s + 1, :] = m_re + p_re[ROWS - 1:ROWS, :]
            him_ref[s:s + 1, :] = m_im + p_im[ROWS - 1:ROWS, :]
            loc = l_ref[s].astype(BF16).reshape(tm // BF16_ROWS, BF16_ROWS, 2 * half)
            pw = pwb_ref[s].reshape(tm // BF16_ROWS, BF16_ROWS, 2 * half)
            pair_re = jnp.concatenate([st_re, st_re], axis=0).astype(BF16)[None]
            pair_im = jnp.concatenate([st_im, st_im], axis=0).astype(BF16)[None]
            f_re, f_im = _cmul(pw[:, :, 0:half], pw[:, :, half:2 * half], pair_re, pair_im)
            hs_re = (loc[:, :, 0:half] + f_re).reshape(tm, half)
            hs_im = (loc[:, :, half:2 * half] + f_im).reshape(tm, half)
            y_parts.append(_dot(hs_re, cre_ref[s]) - _dot(hs_im, cim_ref[s]))

        y = jnp.concatenate(y_parts, axis=1) + dskip_ref[...] * up_ref[...]
        yg = jax.nn.gelu(y, approximate=True).astype(BF16)
        yn = _dot(pt_ref[...], yg).astype(BF16)
        out = _dot(yn, glua_ref[...]) * jax.nn.sigmoid(_dot(yn, glub_ref[...]))
        o_ref[...] = x_ref[...] + _gate(mod, 0, d) * out.reshape(sb, ROWS, d)

    @pl.when(step % 2 == 0)
    def _():
        run(l0_ref, l1_ref, up0_ref, up1_ref)

    @pl.when(step % 2 == 1)
    def _():
        run(l1_ref, l0_ref, up1_ref, up0_ref)


def _mix_odd_prompt(x3, mod, layer, prompt_row, gain, bblk, lam_d, pwb, ltp, cre, cim, dskip, glu_a, glu_b, sb):
    n8, _, d = x3.shape
    n_oct, kin, wid = bblk.shape
    half = wid // 2
    tm = sb * ROWS
    n_tiles = n8 // sb
    proj_buf = pltpu.VMEM((n_oct, tm, wid), F32)
    perm_buf = pltpu.VMEM((tm, d), F32)
    return pl.pallas_call(
        _mix_odd_prompt_body,
        grid=(n_tiles,),
        in_specs=[pl.BlockSpec((sb, ROWS, d), lambda i: (i, 0, 0)),
                  pl.BlockSpec((sb, ROWS, d), lambda i: (jnp.minimum(i + 1, n_tiles - 1), 0, 0)),
                  _mod_row_spec(mod, layer, prompt_row),
                  _const_spec((1, d)), _const_spec(bblk.shape), _const_spec(lam_d.shape),
                  _const_spec(pwb.shape), _const_spec(ltp.shape),
                  _const_spec(cre.shape), _const_spec(cim.shape), _const_spec((1, d)),
                  _const_spec((d, d)), _const_spec((d, d))],
        out_specs=[pl.BlockSpec((sb, ROWS, d), lambda i: (i, 0, 0)),
                   pl.BlockSpec((n_oct, half), lambda i: (0, 0)),
                   pl.BlockSpec((n_oct, half), lambda i: (0, 0))],
        out_shape=[jax.ShapeDtypeStruct(x3.shape, F32),
                   jax.ShapeDtypeStruct((n_oct, half), F32),
                   jax.ShapeDtypeStruct((n_oct, half), F32)],
        scratch_shapes=[pltpu.VMEM((d // 128, tm + ROWS * ROWS, 128), F32), proj_buf, proj_buf, perm_buf, perm_buf,
                        pltpu.VMEM((tm, tm), BF16)],
        compiler_params=_params(),
        name="mix_odd_prompt",
    )(x3, x3, mod, gain.reshape(1, d), bblk, lam_d, pwb, ltp, cre, cim, dskip.reshape(1, d), glu_a, glu_b)


def _mix_odd_sample_body(x_ref, mod_ref, gn_ref, bblk_ref, lam_ref, cre_ref, cim_ref, dskip_ref,
                         glua_ref, glub_ref, sre_ref, sim_ref, o_ref, nre_ref, nim_ref, d_ref, y_ref):
    x = x_ref[...]
    sb, length, d = x.shape
    tm = sb * length
    n_oct = bblk_ref.shape[0]
    half = bblk_ref.shape[2] // 2
    mod = _seq_mod(mod_ref)
    u = _modulate(x, mod, gn_ref[...], 0).reshape(tm, d)
    u16 = u.astype(BF16)
    n_ch = bblk_ref.shape[2] // 128
    hc = n_ch // 2
    for s in range(n_oct):
        bu = _dot(u16[:, s * 128:(s + 1) * 128], bblk_ref[s])
        for c in range(n_ch):
            d_ref[c] = bu[:, c * 128:(c + 1) * 128]
        lam_re = lam_ref[0, s:s + 1, :]
        lam_im = lam_ref[1, s:s + 1, :]
        h_re = sre_ref[:, s * half:(s + 1) * half]
        h_im = sim_ref[:, s * half:(s + 1) * half]
        for t in range(length):
            b_re = jnp.concatenate([d_ref[c, pl.ds(t, sb, stride=length), :] for c in range(hc)], axis=1)
            b_im = jnp.concatenate([d_ref[hc + c, pl.ds(t, sb, stride=length), :] for c in range(hc)], axis=1)
            n_re = lam_re * h_re - lam_im * h_im + b_re
            n_im = lam_re * h_im + lam_im * h_re + b_im
            for c in range(hc):
                d_ref[c, pl.ds(t, sb, stride=length), :] = n_re[:, c * 128:(c + 1) * 128]
                d_ref[hc + c, pl.ds(t, sb, stride=length), :] = n_im[:, c * 128:(c + 1) * 128]
            h_re, h_im = n_re, n_im
        nre_ref[:, s * half:(s + 1) * half] = h_re
        nim_ref[:, s * half:(s + 1) * half] = h_im
        hs_re = jnp.concatenate([d_ref[c] for c in range(hc)], axis=1)
        hs_im = jnp.concatenate([d_ref[hc + c] for c in range(hc)], axis=1)
        y_ref[:, s * 128:(s + 1) * 128] = (_dot(hs_re.astype(BF16), cre_ref[s])
                                           - _dot(hs_im.astype(BF16), cim_ref[s]))
    o_ref[...] = _gelu_glu_out(x, mod, y_ref[...], u, dskip_ref[...], glua_ref, glub_ref)


def _mix_odd_sample(x3, mod, layer, gain, bblk, lam_d, cre, cim, dskip, glu_a, glu_b, s_re, s_im, sb):
    n, length, d = x3.shape
    n_oct, kin, wid = bblk.shape
    tm = sb * length
    nstate = s_re.shape[1]
    return pl.pallas_call(
        _mix_odd_sample_body,
        grid=(n // sb,),
        in_specs=[pl.BlockSpec((sb, length, d), lambda i: (i, 0, 0)),
                  _mod_seq_spec(mod, layer, sb),
                  _const_spec((1, d)), _const_spec(bblk.shape), _const_spec(lam_d.shape),
                  _const_spec(cre.shape), _const_spec(cim.shape), _const_spec((1, d)),
                  _const_spec((d, d)), _const_spec((d, d)),
                  pl.BlockSpec((sb, nstate), lambda i: (i, 0)),
                  pl.BlockSpec((sb, nstate), lambda i: (i, 0))],
        out_specs=[pl.BlockSpec((sb, length, d), lambda i: (i, 0, 0)),
                   pl.BlockSpec((sb, nstate), lambda i: (i, 0)),
                   pl.BlockSpec((sb, nstate), lambda i: (i, 0))],
        out_shape=[jax.ShapeDtypeStruct(x3.shape, F32),
                   jax.ShapeDtypeStruct(s_re.shape, F32),
                   jax.ShapeDtypeStruct(s_im.shape, F32)],
        scratch_shapes=[pltpu.VMEM((wid // 128, tm, 128), F32), pltpu.VMEM((tm, d), F32)],
        compiler_params=_params(),
        name="mix_odd_sample",
    )(x3, mod, gain.reshape(1, d), bblk, lam_d, cre, cim, dskip.reshape(1, d), glu_a, glu_b, s_re, s_im)


def _pick(n, want):
    while n % want:
        want //= 2
    return max(want, 1)


def kernel(x_prompt, x_sample, cache_win_k, cache_win_v, state_ret, state_s5_re, state_s5_im, c_prompt, c_sample, ada_w, ada_b, norm_mix, norm_ffn, ffn_wg, ffn_wu, ffn_wd, even_w_in, even_q_gain, even_k_gain, even_sinks, even_ret_gain, even_w_out, odd_A_re, odd_A_im, odd_log_dt, odd_B_re, odd_B_im, odd_C_re, odd_C_im, odd_D, odd_glu_a, odd_glu_b):
    bp, lp, d = x_prompt.shape
    ns, ls, _ = x_sample.shape
    assert bp == 1 and ls == ROWS and lp % WINDOW == 0
    w = cache_win_k.shape[2]
    groups, p_state = odd_A_re.shape[1:]

    n_c = bp + ns
    n_pad = -n_c % ROWS
    c_all = jnp.concatenate([c_sample, c_prompt, jnp.zeros((n_pad, d), F32)], axis=0)
    mod = _adaln(c_all, ada_w, ada_b)

    bf = lambda t: t.astype(BF16)
    w_in, w_out = bf(even_w_in[0]), bf(even_w_out[0])

    xp = x_prompt.reshape(lp // ROWS, ROWS, d)
    xs = x_sample

    sb_p = _pick(lp // ROWS, 32)
    sb_ffn = _pick(lp // ROWS, 64)
    sb_s = _pick(ns, 64)
    sb_s_even = _pick(ns, 16)

    (xp, p_k, p_v, p_ret), ffn0 = _mix_even_prompt(xp, mod, 0, ns, norm_mix[0], w_in, even_q_gain[0], even_k_gain[0],
                                                   even_sinks[0], even_ret_gain[0], w_out, sb_ffn,
                                                   cast=((ffn_wg, 0), (ffn_wu, 0), (ffn_wd, 0)))
    xs, s_k, s_v, s_ret = _mix_even_sample(xs, mod, 0, norm_mix[0], w_in, even_q_gain[0], even_k_gain[0],
                                           even_sinks[0], even_ret_gain[0], w_out,
                                           cache_win_k[0].reshape(ns, w, KA_W), cache_win_v[0].reshape(ns, w, VA_W),
                                           state_ret, sb_s_even)
    xp, xs, (wg1, wu1, wd1, glu_a, glu_b) = _ffn(
        xp, xs, mod, 0, norm_ffn[0], *ffn0, sb_ffn,
        cast=((ffn_wg, 1), (ffn_wu, 1), (ffn_wd, 1), (odd_glu_a, 0), (odd_glu_b, 0)))

    t_len = sb_p
    bblk, cre, cim, lam_d, pwb, ltp = _s5_prep(odd_A_re[0], odd_A_im[0], odd_log_dt[0], odd_B_re[0], odd_B_im[0],
                                               odd_C_re[0], odd_C_im[0], t_len, ROWS)
    xp, p_re, p_im = _mix_odd_prompt(xp, mod, 1, ns, norm_mix[1], bblk, lam_d, pwb, ltp, cre, cim, odd_D[0],
                                     glu_a, glu_b, sb_p)
    xs, s_re, s_im = _mix_odd_sample(xs, mod, 1, norm_mix[1], bblk, lam_d, cre, cim, odd_D[0], glu_a, glu_b,
                                     state_s5_re[0].reshape(ns, groups * p_state),
                                     state_s5_im[0].reshape(ns, groups * p_state), sb_s)
    xp, xs, _ = _ffn(xp, xs, mod, 1, norm_ffn[1], wg1, wu1, wd1, sb_ffn)

    y_prompt = xp.reshape(bp, lp, d)
    y_sample = xs
    return (y_prompt, y_sample,
            p_k.reshape(1, bp, WINDOW, A_KV, A_HD), p_v.reshape(1, bp, WINDOW, A_KV, A_HD),
            p_ret.reshape(1, bp, B_HEADS, B_KD, B_VD),
            p_re.reshape(1, bp, groups, p_state), p_im.reshape(1, bp, groups, p_state),
            s_k.reshape(1, ns, w, A_KV, A_HD), s_v.reshape(1, ns, w, A_KV, A_HD),
            s_ret.reshape(1, ns, B_HEADS, B_KD, B_VD),
            s_re.reshape(1, ns, groups, p_state), s_im.reshape(1, ns, groups, p_state))
```

```python
import functools
import math

import jax
import jax.numpy as jnp
from jax import lax
from jax.experimental import pallas as pl
from jax.experimental.pallas import tpu as pltpu

F32 = jnp.float32
BF16 = jnp.bfloat16

EPS = 1e-6
NEG_INF = -1e30
ROWS = 8

A_HEADS, A_KV, A_GROUP, A_HD = 8, 2, 4, 64
WINDOW = 128
B_HEADS, B_KD, B_VD = 4, 128, 128
S5_GROUP, S5_STATE = 16, 64
S5_OCT = 8

QA_W, KA_W, VA_W = A_HEADS * A_HD, A_KV * A_HD, A_KV * A_HD
QB_W, KB_W, VB_W, GB_W = B_HEADS * B_KD, B_HEADS * B_KD, B_HEADS * B_VD, B_HEADS * B_VD
OFF_QA = 0
OFF_KA = OFF_QA + QA_W
OFF_VA = OFF_KA + KA_W
OFF_QB = OFF_VA + VA_W
OFF_KB = OFF_QB + QB_W
OFF_VB = OFF_KB + KB_W
OFF_GB = OFF_VB + VB_W

VMEM_LIMIT = 56 * 1024 * 1024


def _ret_log_gamma(h):
    return math.log1p(-(2.0 ** (-5.0 - h)))


def _alibi_slope(h):
    return 2.0 ** (-8.0 * (h + 1) / A_HEADS)


def _const_spec(shape):
    nd = len(shape)
    return pl.BlockSpec(shape, lambda i, _n=nd: (0,) * _n, pipeline_mode=pl.Buffered(1))


def _params():
    return pltpu.CompilerParams(dimension_semantics=("arbitrary",), vmem_limit_bytes=VMEM_LIMIT)


def _dot(a, b):
    return jnp.dot(a, b, preferred_element_type=F32)


def _dot_nt(a, b):
    return lax.dot_general(a, b, (((1,), (1,)), ((), ())), preferred_element_type=F32)


def _bmm(a, b):
    return lax.dot_general(a, b, (((2,), (1,)), ((0,), (0,))), preferred_element_type=F32)


def _bmm_nt(a, b):
    return lax.dot_general(a, b, (((2,), (2,)), ((0,), (0,))), preferred_element_type=F32)


def _bmm_tn(a, b):
    return lax.dot_general(a, b, (((1,), (1,)), ((0,), (0,))), preferred_element_type=F32)


def _rms(x, g):
    return x * lax.rsqrt(jnp.mean(x * x, axis=-1, keepdims=True) + EPS) * g


def _modulate(x3, mod, gain, which):
    d = x3.shape[-1]
    sh = mod[:, :, (3 * which) * d:(3 * which + 1) * d]
    sc = mod[:, :, (3 * which + 1) * d:(3 * which + 2) * d]
    inv = lax.rsqrt(jnp.mean(x3 * x3, axis=-1, keepdims=True) + EPS)
    return x3 * inv * (gain * (1.0 + sc)) + sh


def _mod_row_spec(mod, layer, row):
    assert row % ROWS == 0
    return pl.BlockSpec((None, ROWS, mod.shape[-1]), lambda i: (layer, row // ROWS, 0))


def _mod_seq_spec(mod, layer, sb, first_step=0):
    return pl.BlockSpec((None, sb, mod.shape[-1]), lambda i: (layer, jnp.maximum(i - first_step, 0), 0))


def _row_mod(mod_ref):
    return mod_ref[0:1, :][:, None, :]


def _seq_mod(mod_ref):
    return mod_ref[...][:, None, :]


def _gate(mod, which, d):
    return mod[:, :, (3 * which + 2) * d:(3 * which + 3) * d]


def _adaln_body(c_ref, w_ref, b_ref, o_ref):
    c = c_ref[...]
    a = (c * jax.nn.sigmoid(c)).astype(BF16)
    o_ref[0] = _dot(a, w_ref[0].astype(BF16)) + b_ref[0]


def _adaln(c_all, ada_w, ada_b):
    depth, d, n = ada_w.shape
    r = c_all.shape[0]
    tn = 3072
    return pl.pallas_call(
        _adaln_body,
        grid=(depth, n // tn),
        in_specs=[pl.BlockSpec((r, d), lambda l, j: (0, 0)),
                  pl.BlockSpec((1, d, tn), lambda l, j: (l, 0, j)),
                  pl.BlockSpec((1, 1, tn), lambda l, j: (l, 0, j))],
        out_specs=pl.BlockSpec((1, r, tn), lambda l, j: (l, 0, j)),
        out_shape=jax.ShapeDtypeStruct((depth, r, n), F32),
        compiler_params=pltpu.CompilerParams(dimension_semantics=("arbitrary", "arbitrary"),
                                             vmem_limit_bytes=VMEM_LIMIT),
        name="adaln",
    )(c_all, ada_w, ada_b.reshape(depth, 1, n))


BF16_ROWS = 16


def _cast_specs(job, n_steps):
    w, layer = job
    _, rows, cols = w.shape
    hold = 1
    while rows % (n_steps // hold) or (rows // (n_steps // hold)) % BF16_ROWS:
        hold *= 2
        assert hold <= n_steps and n_steps % hold == 0
    blk = rows // (n_steps // hold)
    src = pl.BlockSpec((None, blk, cols),
                       lambda i, _h=hold, _l=layer: (_l, jnp.minimum(i, n_steps - 1) // _h, 0))
    dst = pl.BlockSpec((blk, cols), lambda i, _h=hold: (jnp.minimum(i, n_steps - 1) // _h, 0))
    return src, dst, jax.ShapeDtypeStruct((rows, cols), BF16)


def _cast_blocks(in_refs, out_refs):
    for src, dst in zip(in_refs, out_refs):
        dst[...] = src[...].astype(BF16)


def _ffn_body(n_cast, n_prompt, xp_ref, mp_ref, xs_ref, ms_ref, gn_ref, wg_ref, wu_ref, wd_ref, *refs):
    cast_in, (op_ref, os_ref), cast_out = refs[:n_cast], refs[n_cast:n_cast + 2], refs[n_cast + 2:]
    step = pl.program_id(0)

    def tile(x_ref, mod, o_ref):
        x = x_ref[...]
        sb, _, d = x.shape
        h = _modulate(x, mod, gn_ref[...], 1).reshape(sb * ROWS, d).astype(BF16)
        a = _dot(h, wg_ref[...])
        b = _dot(h, wu_ref[...])
        act = (a * jax.nn.sigmoid(a) * b).astype(BF16)
        y = _dot(act, wd_ref[...])
        o_ref[...] = x + _gate(mod, 1, d) * y.reshape(sb, ROWS, d)

    @pl.when(step < n_prompt)
    def _():
        tile(xp_ref, _row_mod(mp_ref), op_ref)

    @pl.when(step >= n_prompt)
    def _():
        tile(xs_ref, _seq_mod(ms_ref), os_ref)

    _cast_blocks(cast_in, cast_out)


def _ffn(xp3, xs3, mod, layer, gain, wg, wu, wd, sb, cast=()):
    n8, _, d = xp3.shape
    f = wg.shape[1]
    n_prompt = n8 // sb
    n_sample = xs3.shape[0] // sb
    prompt_row = xs3.shape[0]
    assert n8 % sb == 0 and xs3.shape[0] % sb == 0
    cast_specs = [_cast_specs(w, n_prompt) for w in cast]
    p_idx = lambda i: (jnp.minimum(i, n_prompt - 1), 0, 0)
    s_idx = lambda i: (jnp.maximum(i - n_prompt, 0), 0, 0)
    xp_spec = pl.BlockSpec((sb, ROWS, d), p_idx)
    xs_spec = pl.BlockSpec((sb, ROWS, d), s_idx)
    out = pl.pallas_call(
        functools.partial(_ffn_body, len(cast), n_prompt),
        grid=(n_prompt + n_sample,),
        in_specs=[xp_spec, _mod_row_spec(mod, layer, prompt_row),
                  xs_spec, _mod_seq_spec(mod, layer, sb, n_prompt),
                  _const_spec((1, d)), _const_spec((d, f)), _const_spec((d, f)),
                  _const_spec((f, d))] + [c[0] for c in cast_specs],
        out_specs=[xp_spec, xs_spec] + [c[1] for c in cast_specs],
        out_shape=[jax.ShapeDtypeStruct(xp3.shape, F32), jax.ShapeDtypeStruct(xs3.shape, F32)]
        + [c[2] for c in cast_specs],
        compiler_params=_params(),
        name="ffn",
    )(xp3, mod, xs3, mod, gain.reshape(1, d), wg, wu, wd, *[w for w, _ in cast])
    return out[0], out[1], tuple(out[2:])


def _head_rms(t, g):
    return t * lax.rsqrt(jnp.mean(t * t, axis=-1, keepdims=True) + EPS) * g


def _group_norm_gate(o, gain, gate):
    mu = jnp.mean(o, axis=-1, keepdims=True)
    var = jnp.mean(jnp.square(o - mu), axis=-1, keepdims=True)
    return (o - mu) * lax.rsqrt(var + EPS) * gain * (gate * jax.nn.sigmoid(gate))


def _ret_decay(hb, c):
    lg = _ret_log_gamma(hb)
    ii = lax.broadcasted_iota(jnp.int32, (c, c), 0)
    jj = lax.broadcasted_iota(jnp.int32, (c, c), 1)
    diff = (ii - jj).astype(F32)
    d_in = jnp.where(diff >= 0, jnp.exp(lg * jnp.maximum(diff, 0.0)), 0.0)
    row = lax.broadcasted_iota(jnp.int32, (c, B_KD), 0).astype(F32)
    d_q = jnp.exp(lg * (row + 1.0))
    d_k = jnp.exp(lg * (c - 1.0 - row))
    d_c = math.exp(lg * c)
    return d_in, d_q, d_k, d_c


P_OFF_KA = QA_W
P_OFF_VA = P_OFF_KA + KA_W
P_OFF_QB = P_OFF_VA + VA_W
P_OFF_VB = P_OFF_QB + QB_W
P_OFF_GB = P_OFF_VB + VB_W
P_WIDTH = P_OFF_GB + GB_W
PAIR = 2 * A_HD


def _twice(t):
    low = lax.broadcasted_iota(jnp.int32, t.shape, 1) < A_HD
    swapped = pltpu.roll(t, A_HD, 1)
    return jnp.concatenate([jnp.where(low, t, swapped), jnp.where(low, swapped, t)], axis=1)


def _mix_even_prompt_body(n_cast, x_ref, mod_ref, gn_ref, win_ref, wkt_ref, qg_ref, kg_ref, sink_ref, rg_ref, wout_ref,
                          *refs):
    cast_in, refs = refs[:n_cast], refs[n_cast:]
    o_ref, pk_ref, pv_ref, ps_ref = refs[:4]
    cast_out = refs[4:4 + n_cast]
    mix_ref, carry_ref, bias_ref, dec_ref, ones_ref = refs[4 + n_cast:]
    _cast_blocks(cast_in, cast_out)
    step = pl.program_id(0)
    blk = WINDOW
    rows4 = A_GROUP * blk

    @pl.when(step == 0)
    def _():
        carry_ref[...] = jnp.zeros_like(carry_ref)
        ps_ref[...] = jnp.zeros_like(ps_ref)
        er = lax.broadcasted_iota(jnp.int32, ones_ref.shape, 0) // A_HD
        ec = lax.broadcasted_iota(jnp.int32, ones_ref.shape, 1) // A_HD
        ones_ref[...] = jnp.where(er == ec, 1.0 / A_HD, 0.0).astype(BF16)
        row = lax.broadcasted_iota(jnp.int32, (rows4, 2 * blk), 0)
        dist = row % blk + blk - lax.broadcasted_iota(jnp.int32, (rows4, 2 * blk), 1)
        in_window = (dist >= 0) & (dist < WINDOW)
        for kv in range(A_KV):
            slope = jnp.zeros((rows4, 2 * blk), F32)
            for g in range(A_GROUP):
                slope = jnp.where(row // blk == g, _alibi_slope(kv * A_GROUP + g), slope)
            bias_ref[kv] = jnp.where(in_window, slope * dist.astype(F32), -NEG_INF)
        for hb in range(B_HEADS):
            lg = _ret_log_gamma(hb)
            ii = lax.broadcasted_iota(jnp.int32, (blk, blk), 0).astype(F32)
            jj = lax.broadcasted_iota(jnp.int32, (blk, blk), 1).astype(F32)
            diff = ii - jj
            dec_ref[hb, 0] = jnp.where(diff >= 0, jnp.exp(lg * jnp.maximum(diff, 0.0)), 0.0)
            dec_ref[hb, 1] = jnp.exp(lg * (ii + 1.0))
            dec_ref[hb, 2] = jnp.exp(lg * (blk - 1.0 - jj))

    x = x_ref[...]
    sb, _, d = x.shape
    tb = sb * ROWS
    mod = _row_mod(mod_ref)
    h = _modulate(x, mod, gn_ref[...], 0).reshape(tb, d).astype(BF16)
    proj = _dot(h, win_ref[...])
    kt_all = _dot_nt(wkt_ref[...], h) * (B_KD ** -0.5)
    rg = rg_ref[...]

    qa = proj[:, 0:QA_W]
    ka = proj[:, P_OFF_KA:P_OFF_KA + KA_W]
    va = proj[:, P_OFF_VA:P_OFF_VA + VA_W]
    qsq = (qa * qa).astype(BF16)
    tile_w = ones_ref.shape[0]
    q_ms = jnp.concatenate([_dot(qsq[:, c:c + tile_w], ones_ref[...]) for c in range(0, QA_W, tile_w)], axis=1)
    q_hat = (qa * lax.rsqrt(q_ms + EPS) * qg_ref[...] * (A_HD ** -0.5)).astype(BF16)
    k_hat = ka * lax.rsqrt(_dot((ka * ka).astype(BF16), ones_ref[0:KA_W, 0:KA_W]) + EPS) * kg_ref[...]
    k_hat16 = _twice(k_hat).astype(BF16)
    va16 = _twice(va).astype(BF16)
    prev = carry_ref[step % 2]
    carry_ref[(step + 1) % 2] = jnp.concatenate([k_hat16[tb - blk:tb], va16[tb - blk:tb]], axis=1)

    row_g = lax.broadcasted_iota(jnp.int32, (rows4, 1), 0) // blk
    key_is_prev = lax.broadcasted_iota(jnp.int32, (rows4, 2 * blk), 1) < blk
    first_penalty = jnp.where(step == 0, -NEG_INF, 0.0)
    lane_low = lax.broadcasted_iota(jnp.int32, (blk, PAIR), 1) < A_HD
    ones_cols = jnp.ones((2 * blk, PAIR), BF16)

    n_blk = tb // blk
    att = [(j, kv) for j in range(n_blk) for kv in range(A_KV)]
    ret = [(j, hb) for j in range(n_blk) for hb in range(B_HEADS)]

    sinks, scores = {}, {}
    for j, kv in att:
        r0 = j * blk
        kcol = slice(kv * PAIR, (kv + 1) * PAIR)
        if j == 0:
            k2 = jnp.concatenate([prev[:, kcol], k_hat16[0:blk, kcol]], axis=0)
        else:
            k2 = k_hat16[r0 - blk:r0 + blk, kcol]
        q4 = jnp.concatenate(
            [jnp.where(lane_low == (g % 2 == 0),
                       q_hat[r0:r0 + blk, (kv * A_GROUP + g - g % 2) * A_HD:(kv * A_GROUP + g - g % 2 + 2) * A_HD],
                       jnp.zeros((), BF16))
             for g in range(A_GROUP)], axis=0)
        scores[j, kv] = _dot_nt(q4, k2)
        sink = jnp.zeros((rows4, 1), F32)
        for g in range(A_GROUP):
            sink = jnp.where(row_g == g, sink_ref[kv * A_GROUP + g], sink)
        sinks[j, kv] = sink
    qb, vb, kt, inner = {}, {}, {}, {}
    for j, hb in ret:
        r0 = j * blk
        qb[j, hb] = proj[r0:r0 + blk, P_OFF_QB + hb * B_KD:P_OFF_QB + (hb + 1) * B_KD].astype(BF16)
        vb[j, hb] = proj[r0:r0 + blk, P_OFF_VB + hb * B_VD:P_OFF_VB + (hb + 1) * B_VD].astype(BF16)
        kt[j, hb] = kt_all[hb * B_KD:(hb + 1) * B_KD, r0:r0 + blk]
        inner[j, hb] = _dot(qb[j, hb], kt[j, hb].astype(BF16))

    probs, maxes = {}, {}
    for j, kv in att:
        s = scores[j, kv] - bias_ref[kv]
        if j == 0:
            s = s - jnp.where(key_is_prev, first_penalty, 0.0)
        mx = jnp.maximum(jnp.max(s, axis=-1, keepdims=True), sinks[j, kv])
        probs[j, kv] = jnp.exp(s - mx).astype(BF16)
        maxes[j, kv] = mx
    state = {}
    for hb in range(B_HEADS):
        d_c = math.exp(_ret_log_gamma(hb) * blk)
        state[0, hb] = ps_ref[hb]
        for j in range(n_blk):
            state[j + 1, hb] = state[j, hb] * d_c + _dot((kt[j, hb] * dec_ref[hb, 2]).astype(BF16), vb[j, hb])
        ps_ref[hb] = state[n_blk, hb]

    for j, kv in att:
        r0 = j * blk
        kcol = slice(kv * PAIR, (kv + 1) * PAIR)
        vcol = slice(2 * KA_W + kv * PAIR, 2 * KA_W + (kv + 1) * PAIR)
        if j == 0:
            v2 = jnp.concatenate([prev[:, vcol], va16[0:blk, kcol]], axis=0)
        else:
            v2 = va16[r0 - blk:r0 + blk, kcol]
        pv = _dot(probs[j, kv], jnp.concatenate([v2, ones_cols], axis=1))
        o4 = pv[:, 0:PAIR] / (pv[:, PAIR:2 * PAIR] + jnp.exp(sinks[j, kv] - maxes[j, kv]))
        for g in range(A_GROUP):
            hd = kv * A_GROUP + g
            half = slice((hd % 2) * A_HD, (hd % 2 + 1) * A_HD)
            mix_ref[r0:r0 + blk, hd * A_HD:(hd + 1) * A_HD] = o4[g * blk:(g + 1) * blk, half]
    o_ret = {}
    for j, hb in ret:
        o_ret[j, hb] = (_dot((inner[j, hb] * dec_ref[hb, 0]).astype(BF16), vb[j, hb])
                        + _dot(qb[j, hb], state[j, hb].astype(BF16)) * dec_ref[hb, 1])

    cen = {k: o_ret[k] - jnp.mean(o_ret[k], axis=-1, keepdims=True) for k in ret}
    var = {k: jnp.mean(cen[k] * cen[k], axis=-1, keepdims=True) for k in ret}
    for j, hb in ret:
        r0 = j * blk
        gb = proj[r0:r0 + blk, P_OFF_GB + hb * B_VD:P_OFF_GB + (hb + 1) * B_VD]
        mix_ref[r0:r0 + blk, QA_W + hb * B_VD:QA_W + (hb + 1) * B_VD] = (
            cen[j, hb] * lax.rsqrt(var[j, hb] + EPS) * rg[:, hb * B_VD:(hb + 1) * B_VD] * (gb * jax.nn.sigmoid(gb)))

    out = _dot(mix_ref[...].astype(BF16), wout_ref[...])
    o_ref[...] = x + _gate(mod, 0, d) * out.reshape(sb, ROWS, d)

    @pl.when(step == pl.num_programs(0) - 1)
    def _():
        pk_ref[...] = k_hat[tb - blk:tb, :]
        pv_ref[...] = va[tb - blk:tb, :]


def _mix_even_prompt(x3, mod, layer, prompt_row, gain, w_in, q_gain, k_gain, sinks, ret_gain, w_out, sb, cast=()):
    n8, _, d = x3.shape
    tb = sb * ROWS
    cast_specs = [_cast_specs(w, n8 // sb) for w in cast]
    w_main = jnp.concatenate([w_in[:, :OFF_KB], w_in[:, OFF_VB:]], axis=1)
    wk_t = w_in[:, OFF_KB:OFF_VB].T
    out = pl.pallas_call(
        functools.partial(_mix_even_prompt_body, len(cast)),
        grid=(n8 // sb,),
        in_specs=[pl.BlockSpec((sb, ROWS, d), lambda i: (i, 0, 0)),
                  _mod_row_spec(mod, layer, prompt_row),
                  _const_spec((1, d)), _const_spec((d, P_WIDTH)), _const_spec((KB_W, d)),
                  _const_spec((1, QA_W)), _const_spec((1, KA_W)),
                  pl.BlockSpec(memory_space=pltpu.SMEM),
                  _const_spec((1, VB_W)), _const_spec((QA_W + VB_W, d))] + [c[0] for c in cast_specs],
        out_specs=[pl.BlockSpec((sb, ROWS, d), lambda i: (i, 0, 0)),
                   pl.BlockSpec((WINDOW, KA_W), lambda i: (0, 0)),
                   pl.BlockSpec((WINDOW, VA_W), lambda i: (0, 0)),
                   pl.BlockSpec((B_HEADS, B_KD, B_VD), lambda i: (0, 0, 0))] + [c[1] for c in cast_specs],
        out_shape=[jax.ShapeDtypeStruct(x3.shape, F32),
                   jax.ShapeDtypeStruct((WINDOW, KA_W), F32),
                   jax.ShapeDtypeStruct((WINDOW, VA_W), F32),
                   jax.ShapeDtypeStruct((B_HEADS, B_KD, B_VD), F32)] + [c[2] for c in cast_specs],
        scratch_shapes=[pltpu.VMEM((tb, QA_W + VB_W), F32),
                        pltpu.VMEM((2, WINDOW, 2 * KA_W + 2 * VA_W), BF16),
                        pltpu.VMEM((A_KV, A_GROUP * WINDOW, 2 * WINDOW), F32),
                        pltpu.VMEM((B_HEADS, 3, WINDOW, WINDOW), F32),
                        pltpu.VMEM((2 * PAIR, 2 * PAIR), BF16)],
        compiler_params=_params(),
        name="mix_even_prompt",
    )(x3, mod, gain.reshape(1, d), w_main, wk_t, jnp.tile(q_gain, A_HEADS).reshape(1, QA_W),
      jnp.tile(k_gain, A_KV).reshape(1, KA_W),
      sinks, ret_gain.reshape(1, VB_W), w_out, *[w for w, _ in cast])
    return out[:4], tuple(out[4:])


def _mix_even_sample_body(x_ref, mod_ref, gn_ref, win_ref, qg_ref, kg_ref, sink_ref, rg_ref, wout_ref,
                          ck_ref, cv_ref, s0_ref,
                          o_ref, nk_ref, nv_ref, ns_ref, mix_ref):
    x = x_ref[...]
    sb, length, d = x.shape
    tb = sb * length
    w = ck_ref.shape[1]
    mod = _seq_mod(mod_ref)
    h = _modulate(x, mod, gn_ref[...], 0).reshape(tb, d).astype(BF16)
    proj = _dot(h, win_ref[...])
    qg = qg_ref[...]
    kg = kg_ref[...]
    rg = rg_ref[...]

    rows = A_GROUP * length
    qpos_c = lax.broadcasted_iota(jnp.int32, (rows, w), 0) % length
    kpos_c = lax.broadcasted_iota(jnp.int32, (rows, w), 1)
    dist_c = w + qpos_c - kpos_c
    valid_c = (dist_c >= 0) & (dist_c < WINDOW)
    qpos_n = lax.broadcasted_iota(jnp.int32, (rows, length), 0) % length
    kpos_n = lax.broadcasted_iota(jnp.int32, (rows, length), 1)
    dist_n = qpos_n - kpos_n
    valid_n = (dist_n >= 0) & (dist_n < WINDOW)
    row_g = lax.broadcasted_iota(jnp.int32, (rows, 1), 0) // length

    scale = A_HD ** -0.5
    kn, vn, s_c, s_n, vc, slopes, sinks = {}, {}, {}, {}, {}, {}, {}
    for kv in range(A_KV):
        lanes = slice(kv * A_HD, (kv + 1) * A_HD)
        kn[kv] = _head_rms(proj[:, OFF_KA + kv * A_HD:OFF_KA + (kv + 1) * A_HD], kg).reshape(sb, length, A_HD)
        vn[kv] = proj[:, OFF_VA + kv * A_HD:OFF_VA + (kv + 1) * A_HD].reshape(sb, length, A_HD)
        nk_ref[:, 0:w - length, lanes] = ck_ref[:, length:w, lanes]
        nv_ref[:, 0:w - length, lanes] = cv_ref[:, length:w, lanes]
        nk_ref[:, w - length:w, lanes] = kn[kv]
        nv_ref[:, w - length:w, lanes] = vn[kv]
        kc = ck_ref[:, :, lanes].astype(BF16)
        vc[kv] = cv_ref[:, :, lanes].astype(BF16)
        q4 = jnp.concatenate(
            [_head_rms(proj[:, OFF_QA + (kv * A_GROUP + g) * A_HD:OFF_QA + (kv * A_GROUP + g + 1) * A_HD], qg)
             .reshape(sb, length, A_HD) for g in range(A_GROUP)], axis=1).astype(BF16)
        slope = jnp.zeros((rows, 1), F32)
        sink = jnp.zeros((rows, 1), F32)
        for g in range(A_GROUP):
            hd = kv * A_GROUP + g
            slope = jnp.where(row_g == g, _alibi_slope(hd), slope)
            sink = jnp.where(row_g == g, sink_ref[hd], sink)
        slopes[kv], sinks[kv] = slope, sink
        s_c[kv] = _bmm_nt(q4, kc)
        s_n[kv] = _bmm_nt(q4, kn[kv].astype(BF16))
    decay = {hb: _ret_decay(hb, length) for hb in range(B_HEADS)}
    qb, kb, vb, inner, q_state = {}, {}, {}, {}, {}
    for hb in range(B_HEADS):
        qb[hb] = proj[:, OFF_QB + hb * B_KD:OFF_QB + (hb + 1) * B_KD].reshape(sb, length, B_KD).astype(BF16)
        kb[hb] = proj[:, OFF_KB + hb * B_KD:OFF_KB + (hb + 1) * B_KD].reshape(sb, length, B_KD) * (B_KD ** -0.5)
        vb[hb] = proj[:, OFF_VB + hb * B_VD:OFF_VB + (hb + 1) * B_VD].reshape(sb, length, B_VD).astype(BF16)
        inner[hb] = _bmm_nt(qb[hb], kb[hb].astype(BF16))
        q_state[hb] = _bmm(qb[hb], s0_ref[:, hb].astype(BF16))

    p_c, p_n, den = {}, {}, {}
    for kv in range(A_KV):
        a_c = jnp.where(valid_c, s_c[kv] * scale - slopes[kv] * dist_c.astype(F32), NEG_INF)
        a_n = jnp.where(valid_n, s_n[kv] * scale - slopes[kv] * dist_n.astype(F32), NEG_INF)
        mx = jnp.maximum(jnp.maximum(jnp.max(a_c, axis=-1, keepdims=True),
                                     jnp.max(a_n, axis=-1, keepdims=True)), sinks[kv])
        p_c[kv] = jnp.exp(a_c - mx)
        p_n[kv] = jnp.exp(a_n - mx)
        den[kv] = (jnp.sum(p_c[kv], axis=-1, keepdims=True) + jnp.sum(p_n[kv], axis=-1, keepdims=True)
                   + jnp.exp(sinks[kv] - mx))

    for kv in range(A_KV):
        o4 = (_bmm(p_c[kv].astype(BF16), vc[kv]) + _bmm(p_n[kv].astype(BF16), vn[kv].astype(BF16))) / den[kv]
        for g in range(A_GROUP):
            hd = kv * A_GROUP + g
            mix_ref[:, hd * A_HD:(hd + 1) * A_HD] = o4[:, g * length:(g + 1) * length, :].reshape(tb, A_HD)
    o_ret = {}
    for hb in range(B_HEADS):
        d_in, d_q, d_k, d_c = decay[hb]
        o_ret[hb] = _bmm((inner[hb] * d_in).astype(BF16), vb[hb]) + q_state[hb] * d_q
        ns_ref[:, hb] = s0_ref[:, hb] * d_c + _bmm_tn((kb[hb] * d_k).astype(BF16), vb[hb])

    for hb in range(B_HEADS):
        gb = proj[:, OFF_GB + hb * B_VD:OFF_GB + (hb + 1) * B_VD]
        mix_ref[:, QA_W + hb * B_VD:QA_W + (hb + 1) * B_VD] = _group_norm_gate(
            o_ret[hb].reshape(tb, B_VD), rg[:, hb * B_VD:(hb + 1) * B_VD], gb)

    out = _dot(mix_ref[...].astype(BF16), wout_ref[...])
    o_ref[...] = x + _gate(mod, 0, d) * out.reshape(sb, length, d)


def _mix_even_sample(x3, mod, layer, gain, w_in, q_gain, k_gain, sinks, ret_gain, w_out, cache_k, cache_v, state, sb):
    n, length, d = x3.shape
    in_w = w_in.shape[1]
    w = cache_k.shape[1]
    seq_spec = lambda shape: pl.BlockSpec((sb,) + shape, lambda i, _n=len(shape): (i,) + (0,) * _n)
    return pl.pallas_call(
        _mix_even_sample_body,
        grid=(n // sb,),
        in_specs=[seq_spec((length, d)), _mod_seq_spec(mod, layer, sb),
                  _const_spec((1, d)), _const_spec((d, in_w)),
                  _const_spec((1, A_HD)), _const_spec((1, A_HD)),
                  pl.BlockSpec(memory_space=pltpu.SMEM),
                  _const_spec((1, VB_W)), _const_spec((QA_W + VB_W, d)),
                  seq_spec((w, KA_W)), seq_spec((w, VA_W)),
                  pl.BlockSpec((None, sb, B_HEADS, B_KD, B_VD), lambda i: (0, i, 0, 0, 0))],
        out_specs=[seq_spec((length, d)), seq_spec((w, KA_W)), seq_spec((w, VA_W)),
                   seq_spec((B_HEADS, B_KD, B_VD))],
        out_shape=[jax.ShapeDtypeStruct(x3.shape, F32),
                   jax.ShapeDtypeStruct(cache_k.shape, F32),
                   jax.ShapeDtypeStruct(cache_v.shape, F32),
                   jax.ShapeDtypeStruct(state.shape[1:], F32)],
        scratch_shapes=[pltpu.VMEM((sb * length, QA_W + VB_W), F32)],
        compiler_params=_params(),
        name="mix_even_sample",
    )(x3, mod, gain.reshape(1, d), w_in, q_gain.reshape(1, A_HD), k_gain.reshape(1, A_HD),
      sinks, ret_gain.reshape(1, VB_W), w_out, cache_k, cache_v, state)


def _cmul(ar, ai, br, bi):
    return ar * br - ai * bi, ar * bi + ai * br


def _s5_lambda(a_re, a_im, log_dt):
    dt = jnp.exp(log_dt)
    mag = jnp.exp(a_re * dt)
    return mag * jnp.cos(a_im * dt), mag * jnp.sin(a_im * dt)


def _s5_prep_body(t_len, n_chunks, are_ref, aim_ref, ldt_ref, btr_ref, bti_ref, ctr_ref, cti_ref,
                  ard_ref, aid_ref, ldd_ref,
                  bblk_ref, cre_ref, cim_ref, lam_ref, pwb_ref, ltp_ref, b32_ref, cr32_ref, ci32_ref, pw32_ref):
    groups = btr_ref.shape[0]
    k_in, p = btr_ref.shape[1:]
    half = S5_OCT * p
    a_re = are_ref[...]
    a_im = aim_ref[...]
    lam_re, lam_im = _s5_lambda(a_re, a_im, ldt_ref[...])
    den = a_re * a_re + a_im * a_im
    n_re = lam_re - 1.0
    n_im = lam_im
    f_re = (n_re * a_re + n_im * a_im) / den
    f_im = (n_im * a_re - n_re * a_im) / den
    br = btr_ref[...]
    bi = bti_ref[...]
    bb_re = f_re * br - f_im * bi
    bb_im = f_re * bi + f_im * br
    b32_ref[...] = jnp.zeros_like(b32_ref)
    cr32_ref[...] = jnp.zeros_like(cr32_ref)
    ci32_ref[...] = jnp.zeros_like(ci32_ref)
    for g in range(groups):
        s, gl = divmod(g, S5_OCT)
        rows = slice(gl * k_in, (gl + 1) * k_in)
        cols = slice(gl * p, (gl + 1) * p)
        b32_ref[s, rows, cols] = bb_re[g]
        b32_ref[s, rows, half + gl * p:half + (gl + 1) * p] = bb_im[g]
        cr32_ref[s, cols, rows] = ctr_ref[g]
        ci32_ref[s, cols, rows] = cti_ref[g]
    bblk_ref[...] = b32_ref[...].astype(BF16)
    cre_ref[...] = cr32_ref[...].astype(BF16)
    cim_ref[...] = ci32_ref[...].astype(BF16)
    lam_re, lam_im = _s5_lambda(ard_ref[...], aid_ref[...], ldd_ref[...])
    lam_ref[0] = lam_re
    lam_ref[1] = lam_im
    n_oct = lam_re.shape[0]

    def both(cr, ci, s, rows):
        return jnp.concatenate([jnp.broadcast_to(cr[s:s + 1, :], (rows, half)),
                                jnp.broadcast_to(ci[s:s + 1, :], (rows, half))], axis=1)

    cr, ci = lam_re, lam_im
    for t in range(t_len):
        for s in range(n_oct):
            pw32_ref[s, t * ROWS:(t + 1) * ROWS, :] = both(cr, ci, s, ROWS)
        if t + 1 < t_len:
            cr, ci = _cmul(cr, ci, lam_re, lam_im)
    pwb_ref[...] = pw32_ref[...].astype(BF16)
    tr, ti = cr, ci
    ltp_ref[...] = jnp.zeros_like(ltp_ref)
    cr, ci = jnp.ones_like(lam_re), jnp.zeros_like(lam_im)
    for m in range(n_chunks + 1):
        for s in range(n_oct):
            if m < n_chunks:
                ltp_ref[s, n_chunks + m:n_chunks + m + 1, :] = both(cr, ci, s, 1)
            if m in (1, 2, 4, 8):
                i = (1, 2, 4, 8).index(m)
                ltp_ref[s, i:i + 1, :] = both(cr, ci, s, 1)
        cr, ci = _cmul(cr, ci, tr, ti)


def _s5_prep(a_re, a_im, log_dt, b_re, b_im, c_re, c_im, t_len, n_chunks):
    g, p = a_re.shape
    k = b_re.shape[-1]
    assert n_chunks == ROWS
    n_oct = g // S5_OCT
    half = S5_OCT * p
    dense = (n_oct, half)
    return pl.pallas_call(
        functools.partial(_s5_prep_body, t_len, n_chunks),
        out_shape=[jax.ShapeDtypeStruct((n_oct, S5_OCT * k, 2 * half), BF16),
                   jax.ShapeDtypeStruct((n_oct, half, S5_OCT * k), BF16),
                   jax.ShapeDtypeStruct((n_oct, half, S5_OCT * k), BF16),
                   jax.ShapeDtypeStruct((2,) + dense, F32),
                   jax.ShapeDtypeStruct((n_oct, t_len * ROWS, 2 * half), BF16),
                   jax.ShapeDtypeStruct((n_oct, 2 * n_chunks, 2 * half), F32)],
        scratch_shapes=[pltpu.VMEM((n_oct, S5_OCT * k, 2 * half), F32),
                        pltpu.VMEM((n_oct, half, S5_OCT * k), F32),
                        pltpu.VMEM((n_oct, half, S5_OCT * k), F32),
                        pltpu.VMEM((n_oct, t_len * ROWS, 2 * half), F32)],
        compiler_params=pltpu.CompilerParams(vmem_limit_bytes=VMEM_LIMIT),
        name="s5_prep",
    )(a_re.reshape(g, 1, p), a_im.reshape(g, 1, p), log_dt.reshape(g, 1, 1),
      jnp.swapaxes(b_re, 1, 2), jnp.swapaxes(b_im, 1, 2), jnp.swapaxes(c_re, 1, 2), jnp.swapaxes(c_im, 1, 2),
      a_re.reshape(dense), a_im.reshape(dense), jnp.broadcast_to(log_dt[:, None], (g, p)).reshape(dense))


def _gelu_glu_out(x, mod, y, u, dskip, glua_ref, glub_ref):
    sb, rows, d = x.shape
    y = y + dskip * u
    yg = jax.nn.gelu(y, approximate=True).astype(BF16)
    out = _dot(yg, glua_ref[...]) * jax.nn.sigmoid(_dot(yg, glub_ref[...]))
    return x + _gate(mod, 0, d) * out.reshape(sb, rows, d)


def _mix_odd_prompt_body(x_ref, xn_ref, mod_ref, gn_ref, bblk_ref, lam_ref, pwb_ref, ltp_ref, cre_ref, cim_ref,
                         dskip_ref, glua_ref, glub_ref, o_ref, hre_ref, him_ref,
                         un_ref, l0_ref, l1_ref, up0_ref, up1_ref, pt_ref):
    step = pl.program_id(0)
    sb, _, d = x_ref.shape
    tm = sb * ROWS
    t_len = tm // ROWS
    n_oct = bblk_ref.shape[0]
    half = bblk_ref.shape[2] // 2
    mod = _row_mod(mod_ref)

    def permuted_input(src_ref, up_dst):
        u = _modulate(src_ref[...], mod, gn_ref[...], 0).reshape(tm, d)
        pitch = un_ref.shape[1] // ROWS
        for k in range(d // 128):
            for c in range(ROWS):
                un_ref[k, c * pitch:c * pitch + t_len, :] = u[c * t_len:(c + 1) * t_len, k * 128:(k + 1) * 128]
        up = jnp.concatenate(
            [jnp.concatenate([un_ref[k, pl.ds(t, ROWS, stride=pitch), :] for k in range(d // 128)], axis=1)
             for t in range(t_len)], axis=0)
        up_dst[...] = up
        return up.astype(BF16)

    @pl.when(step == 0)
    def _():
        hre_ref[...] = jnp.zeros_like(hre_ref)
        him_ref[...] = jnp.zeros_like(him_ref)
        nat = lax.broadcasted_iota(jnp.int32, (tm, tm), 0)
        prm = lax.broadcasted_iota(jnp.int32, (tm, tm), 1)
        pt_ref[...] = jnp.where(prm == (nat % t_len) * ROWS + nat // t_len, 1.0, 0.0).astype(BF16)
        up16 = permuted_input(x_ref, up0_ref)
        for s in range(n_oct):
            l0_ref[s] = _dot(up16[:, s * 128:(s + 1) * 128], bblk_ref[s])

    def run(l_ref, l_next, up_ref, up_next):
        for s0 in range(0, n_oct, 2):
            pair = (s0, s0 + 1)
            lam_re = [jnp.broadcast_to(lam_ref[0, s:s + 1, :], (ROWS, half)) for s in pair]
            lam_im = [jnp.broadcast_to(lam_ref[1, s:s + 1, :], (ROWS, half)) for s in pair]

            def local_step(t, carry):
                r = pl.multiple_of(t * ROWS, ROWS)
                out = []
                for i, s in enumerate(pair):
                    h_re, h_im = carry[2 * i], carry[2 * i + 1]
                    n_re = lam_re[i] * h_re - lam_im[i] * h_im + l_ref[s, pl.ds(r, ROWS), 0:half]
                    n_im = lam_re[i] * h_im + lam_im[i] * h_re + l_ref[s, pl.ds(r, ROWS), half:2 * half]
                    l_ref[s, pl.ds(r, ROWS), 0:half] = n_re
                    l_ref[s, pl.ds(r, ROWS), half:2 * half] = n_im
                    out += [n_re, n_im]
                return tuple(out)

            zero = jnp.zeros((ROWS, half), F32)
            lax.fori_loop(0, t_len, local_step, (zero, zero, zero, zero), unroll=4)

        next16 = permuted_input(xn_ref, up_next)
        chunk = lax.broadcasted_iota(jnp.int32, (ROWS, half), 0)
        y_parts = []
        for s in range(n_oct):
            l_next[s] = _dot(next16[:, s * 128:(s + 1) * 128], bblk_ref[s])
            p_re = l_ref[s, tm - ROWS:tm, 0:half]
            p_im = l_ref[s, tm - ROWS:tm, half:2 * half]
            for i, sh in enumerate((1, 2, 4)):
                m_re, m_im = _cmul(ltp_ref[s, i:i + 1, 0:half], ltp_ref[s, i:i + 1, half:2 * half],
                                   jnp.where(chunk >= sh, pltpu.roll(p_re, sh, 0), 0.0),
                                   jnp.where(chunk >= sh, pltpu.roll(p_im, sh, 0), 0.0))
                p_re, p_im = p_re + m_re, p_im + m_im
            hin_re = jnp.broadcast_to(hre_ref[s:s + 1, :], (ROWS, half))
            hin_im = jnp.broadcast_to(him_ref[s:s + 1, :], (ROWS, half))
            m_re, m_im = _cmul(ltp_ref[s, ROWS:2 * ROWS, 0:half], ltp_ref[s, ROWS:2 * ROWS, half:2 * half],
                               hin_re, hin_im)
            st_re = m_re + jnp.where(chunk >= 1, pltpu.roll(p_re, 1, 0), 0.0)
            st_im = m_im + jnp.where(chunk >= 1, pltpu.roll(p_im, 1, 0), 0.0)
            m_re, m_im = _cmul(ltp_ref[s, 3:4, 0:half], ltp_ref[s, 3:4, half:2 * half],
                               hre_ref[s:s + 1, :], him_ref[s:s + 1, :])
            hre_ref[s:s + 1, :] = m_re + p_re[ROWS - 1:ROWS, :]
            him_ref[s:s + 1, :] = m_im + p_im[ROWS - 1:ROWS, :]
            loc = l_ref[s].astype(BF16).reshape(tm // BF16_ROWS, BF16_ROWS, 2 * half)
            pw = pwb_ref[s].reshape(tm // BF16_ROWS, BF16_ROWS, 2 * half)
            pair_re = jnp.concatenate([st_re, st_re], axis=0).astype(BF16)[None]
            pair_im = jnp.concatenate([st_im, st_im], axis=0).astype(BF16)[None]
            f_re, f_im = _cmul(pw[:, :, 0:half], pw[:, :, half:2 * half], pair_re, pair_im)
            hs_re = (loc[:, :, 0:half] + f_re).reshape(tm, half)
            hs_im = (loc[:, :, half:2 * half] + f_im).reshape(tm, half)
            y_parts.append(_dot(hs_re, cre_ref[s]) - _dot(hs_im, cim_ref[s]))

        y = jnp.concatenate(y_parts, axis=1) + dskip_ref[...] * up_ref[...]
        yg = jax.nn.gelu(y, approximate=True).astype(BF16)
        yn = _dot(pt_ref[...], yg).astype(BF16)
        out = _dot(yn, glua_ref[...]) * jax.nn.sigmoid(_dot(yn, glub_ref[...]))
        o_ref[...] = x_ref[...] + _gate(mod, 0, d) * out.reshape(sb, ROWS, d)

    @pl.when(step % 2 == 0)
    def _():
        run(l0_ref, l1_ref, up0_ref, up1_ref)

    @pl.when(step % 2 == 1)
    def _():
        run(l1_ref, l0_ref, up1_ref, up0_ref)


def _mix_odd_prompt(x3, mod, layer, prompt_row, gain, bblk, lam_d, pwb, ltp, cre, cim, dskip, glu_a, glu_b, sb):
    n8, _, d = x3.shape
    n_oct, kin, wid = bblk.shape
    half = wid // 2
    tm = sb * ROWS
    n_tiles = n8 // sb
    proj_buf = pltpu.VMEM((n_oct, tm, wid), F32)
    perm_buf = pltpu.VMEM((tm, d), F32)
    return pl.pallas_call(
        _mix_odd_prompt_body,
        grid=(n_tiles,),
        in_specs=[pl.BlockSpec((sb, ROWS, d), lambda i: (i, 0, 0)),
                  pl.BlockSpec((sb, ROWS, d), lambda i: (jnp.minimum(i + 1, n_tiles - 1), 0, 0)),
                  _mod_row_spec(mod, layer, prompt_row),
                  _const_spec((1, d)), _const_spec(bblk.shape), _const_spec(lam_d.shape),
                  _const_spec(pwb.shape), _const_spec(ltp.shape),
                  _const_spec(cre.shape), _const_spec(cim.shape), _const_spec((1, d)),
                  _const_spec((d, d)), _const_spec((d, d))],
        out_specs=[pl.BlockSpec((sb, ROWS, d), lambda i: (i, 0, 0)),
                   pl.BlockSpec((n_oct, half), lambda i: (0, 0)),
                   pl.BlockSpec((n_oct, half), lambda i: (0, 0))],
        out_shape=[jax.ShapeDtypeStruct(x3.shape, F32),
                   jax.ShapeDtypeStruct((n_oct, half), F32),
                   jax.ShapeDtypeStruct((n_oct, half), F32)],
        scratch_shapes=[pltpu.VMEM((d // 128, tm + ROWS * ROWS, 128), F32), proj_buf, proj_buf, perm_buf, perm_buf,
                        pltpu.VMEM((tm, tm), BF16)],
        compiler_params=_params(),
        name="mix_odd_prompt",
    )(x3, x3, mod, gain.reshape(1, d), bblk, lam_d, pwb, ltp, cre, cim, dskip.reshape(1, d), glu_a, glu_b)


def _mix_odd_sample_body(x_ref, mod_ref, gn_ref, bblk_ref, lam_ref, cre_ref, cim_ref, dskip_ref,
                         glua_ref, glub_ref, sre_ref, sim_ref, o_ref, nre_ref, nim_ref, d_ref, y_ref):
    x = x_ref[...]
    sb, length, d = x.shape
    tm = sb * length
    n_oct = bblk_ref.shape[0]
    half = bblk_ref.shape[2] // 2
    mod = _seq_mod(mod_ref)
    u = _modulate(x, mod, gn_ref[...], 0).reshape(tm, d)
    u16 = u.astype(BF16)
    n_ch = bblk_ref.shape[2] // 128
    hc = n_ch // 2
    for s in range(n_oct):
        bu = _dot(u16[:, s * 128:(s + 1) * 128], bblk_ref[s])
        for c in range(n_ch):
            d_ref[c] = bu[:, c * 128:(c + 1) * 128]
        lam_re = lam_ref[0, s:s + 1, :]
        lam_im = lam_ref[1, s:s + 1, :]
        h_re = sre_ref[:, s * half:(s + 1) * half]
        h_im = sim_ref[:, s * half:(s + 1) * half]
        for t in range(length):
            b_re = jnp.concatenate([d_ref[c, pl.ds(t, sb, stride=length), :] for c in range(hc)], axis=1)
            b_im = jnp.concatenate([d_ref[hc + c, pl.ds(t, sb, stride=length), :] for c in range(hc)], axis=1)
            n_re = lam_re * h_re - lam_im * h_im + b_re
            n_im = lam_re * h_im + lam_im * h_re + b_im
            for c in range(hc):
                d_ref[c, pl.ds(t, sb, stride=length), :] = n_re[:, c * 128:(c + 1) * 128]
                d_ref[hc + c, pl.ds(t, sb, stride=length), :] = n_im[:, c * 128:(c + 1) * 128]
            h_re, h_im = n_re, n_im
        nre_ref[:, s * half:(s + 1) * half] = h_re
        nim_ref[:, s * half:(s + 1) * half] = h_im
        hs_re = jnp.concatenate([d_ref[c] for c in range(hc)], axis=1)
        hs_im = jnp.concatenate([d_ref[hc + c] for c in range(hc)], axis=1)
        y_ref[:, s * 128:(s + 1) * 128] = (_dot(hs_re.astype(BF16), cre_ref[s])
                                           - _dot(hs_im.astype(BF16), cim_ref[s]))
    o_ref[...] = _gelu_glu_out(x, mod, y_ref[...], u, dskip_ref[...], glua_ref, glub_ref)


def _mix_odd_sample(x3, mod, layer, gain, bblk, lam_d, cre, cim, dskip, glu_a, glu_b, s_re, s_im, sb):
    n, length, d = x3.shape
    n_oct, kin, wid = bblk.shape
    tm = sb * length
    nstate = s_re.shape[1]
    return pl.pallas_call(
        _mix_odd_sample_body,
        grid=(n // sb,),
        in_specs=[pl.BlockSpec((sb, length, d), lambda i: (i, 0, 0)),
                  _mod_seq_spec(mod, layer, sb),
                  _const_spec((1, d)), _const_spec(bblk.shape), _const_spec(lam_d.shape),
                  _const_spec(cre.shape), _const_spec(cim.shape), _const_spec((1, d)),
                  _const_spec((d, d)), _const_spec((d, d)),
                  pl.BlockSpec((sb, nstate), lambda i: (i, 0)),
                  pl.BlockSpec((sb, nstate), lambda i: (i, 0))],
        out_specs=[pl.BlockSpec((sb, length, d), lambda i: (i, 0, 0)),
                   pl.BlockSpec((sb, nstate), lambda i: (i, 0)),
                   pl.BlockSpec((sb, nstate), lambda i: (i, 0))],
        out_shape=[jax.ShapeDtypeStruct(x3.shape, F32),
                   jax.ShapeDtypeStruct(s_re.shape, F32),
                   jax.ShapeDtypeStruct(s_im.shape, F32)],
        scratch_shapes=[pltpu.VMEM((wid // 128, tm, 128), F32), pltpu.VMEM((tm, d), F32)],
        compiler_params=_params(),
        name="mix_odd_sample",
    )(x3, mod, gain.reshape(1, d), bblk, lam_d, cre, cim, dskip.reshape(1, d), glu_a, glu_b, s_re, s_im)


def _pick(n, want):
    while n % want:
        want //= 2
    return max(want, 1)


def kernel(x_prompt, x_sample, cache_win_k, cache_win_v, state_ret, state_s5_re, state_s5_im, c_prompt, c_sample, ada_w, ada_b, norm_mix, norm_ffn, ffn_wg, ffn_wu, ffn_wd, even_w_in, even_q_gain, even_k_gain, even_sinks, even_ret_gain, even_w_out, odd_A_re, odd_A_im, odd_log_dt, odd_B_re, odd_B_im, odd_C_re, odd_C_im, odd_D, odd_glu_a, odd_glu_b):
    bp, lp, d = x_prompt.shape
    ns, ls, _ = x_sample.shape
    assert bp == 1 and ls == ROWS and lp % WINDOW == 0
    w = cache_win_k.shape[2]
    groups, p_state = odd_A_re.shape[1:]

    n_c = bp + ns
    n_pad = -n_c % ROWS
    c_all = jnp.concatenate([c_sample, c_prompt, jnp.zeros((n_pad, d), F32)], axis=0)
    mod = _adaln(c_all, ada_w, ada_b)

    bf = lambda t: t.astype(BF16)
    w_in, w_out = bf(even_w_in[0]), bf(even_w_out[0])

    xp = x_prompt.reshape(lp // ROWS, ROWS, d)
    xs = x_sample

    sb_p = _pick(lp // ROWS, 32)
    sb_ffn = _pick(lp // ROWS, 64)
    sb_s = _pick(ns, 64)
    sb_s_even = _pick(ns, 16)

    (xp, p_k, p_v, p_ret), ffn0 = _mix_even_prompt(xp, mod, 0, ns, norm_mix[0], w_in, even_q_gain[0], even_k_gain[0],
                                                   even_sinks[0], even_ret_gain[0], w_out, sb_ffn,
                                                   cast=((ffn_wg, 0), (ffn_wu, 0), (ffn_wd, 0)))
    xs, s_k, s_v, s_ret = _mix_even_sample(xs, mod, 0, norm_mix[0], w_in, even_q_gain[0], even_k_gain[0],
                                           even_sinks[0], even_ret_gain[0], w_out,
                                           cache_win_k[0].reshape(ns, w, KA_W), cache_win_v[0].reshape(ns, w, VA_W),
                                           state_ret, sb_s_even)
    xp, xs, (wg1, wu1, wd1, glu_a, glu_b) = _ffn(
        xp, xs, mod, 0, norm_ffn[0], *ffn0, sb_ffn,
        cast=((ffn_wg, 1), (ffn_wu, 1), (ffn_wd, 1), (odd_glu_a, 0), (odd_glu_b, 0)))

    t_len = sb_p
    bblk, cre, cim, lam_d, pwb, ltp = _s5_prep(odd_A_re[0], odd_A_im[0], odd_log_dt[0], odd_B_re[0], odd_B_im[0],
                                               odd_C_re[0], odd_C_im[0], t_len, ROWS)
    xp, p_re, p_im = _mix_odd_prompt(xp, mod, 1, ns, norm_mix[1], bblk, lam_d, pwb, ltp, cre, cim, odd_D[0],
                                     glu_a, glu_b, sb_p)
    xs, s_re, s_im = _mix_odd_sample(xs, mod, 1, norm_mix[1], bblk, lam_d, cre, cim, odd_D[0], glu_a, glu_b,
                                     state_s5_re[0].reshape(ns, groups * p_state),
                                     state_s5_im[0].reshape(ns, groups * p_state), sb_s)
    xp, xs, _ = _ffn(xp, xs, mod, 1, norm_ffn[1], wg1, wu1, wd1, sb_ffn)

    y_prompt = xp.reshape(bp, lp, d)
    y_sample = xs
    return (y_prompt, y_sample,
            p_k.reshape(1, bp, WINDOW, A_KV, A_HD), p_v.reshape(1, bp, WINDOW, A_KV, A_HD),
            p_ret.reshape(1, bp, B_HEADS, B_KD, B_VD),
            p_re.reshape(1, bp, groups, p_state), p_im.reshape(1, bp, groups, p_state),
            s_k.reshape(1, ns, w, A_KV, A_HD), s_v.reshape(1, ns, w, A_KV, A_HD),
            s_ret.reshape(1, ns, B_HEADS, B_KD, B_VD),
            s_re.reshape(1, ns, groups, p_state), s_im.reshape(1, ns, groups, p_state))
```
